```python
import jax, jax.numpy as jnp
from jax import lax
import numpy as np

D_MODEL = 2048
BATCH = 8
SEQ = 4096
DEPTH = 4

BRANCH_WIDTH = D_MODEL // 2
N_BRANCHES = 3
RET_HEADS = 4
RET_WIDTH = BRANCH_WIDTH
RET_HEAD_DIM = RET_WIDTH // RET_HEADS
RET_CHUNK = 128
RET_ROPE_BASE = 10000.0
POOL_WINDOWS = (2, 4, 8, 16)
POOL_GROUPS = len(POOL_WINDOWS)
POOL_WIDTH = BRANCH_WIDTH
POOL_GROUP_DIM = POOL_WIDTH // POOL_GROUPS
ATT_HEAD_DIM = 128
ATT_WIDTH = BRANCH_WIDTH
ATT_Q_HEADS = ATT_WIDTH // ATT_HEAD_DIM
ATT_KV_HEADS = ATT_Q_HEADS // 4
ATT_KV_WIDTH = ATT_KV_HEADS * ATT_HEAD_DIM
ATT_WINDOW = 128
ATT_BLOCK = 128
ROPE_THETA = 500000.0
ROPE_DIMS = ATT_HEAD_DIM // 4
RMS_EPS = 1e-6
NEG_BIG = -1e30

IN_SIZES = (RET_WIDTH, RET_WIDTH, RET_WIDTH, RET_WIDTH,
            POOL_WIDTH, POOL_WIDTH,
            ATT_WIDTH, ATT_KV_WIDTH, ATT_KV_WIDTH, ATT_WIDTH,
            N_BRANCHES * D_MODEL)
IN_WIDTH = sum(IN_SIZES)

kernel_name = "hybrid_retention_pool_swa_encoder"


def rms_norm(x, gain=None):
    x32 = x.astype(jnp.float32)
    y = x32 * lax.rsqrt(jnp.mean(x32 * x32, axis=-1, keepdims=True) + RMS_EPS)
    if gain is not None:
        y = y * gain.astype(jnp.float32)
    return y.astype(x.dtype)


def rotary(x, inv_freq):
    S = x.shape[1]
    half = inv_freq.shape[0]
    ang = jnp.arange(S, dtype=jnp.float32)[:, None] * inv_freq[None, :]
    cos = jnp.cos(ang)[None, :, None, :]
    sin = jnp.sin(ang)[None, :, None, :]
    xr = x[..., :2 * half].astype(jnp.float32)
    x1, x2 = xr[..., :half], xr[..., half:]
    rot = jnp.concatenate([x1 * cos - x2 * sin, x2 * cos + x1 * sin], axis=-1).astype(x.dtype)
    return jnp.concatenate([rot, x[..., 2 * half:]], axis=-1)


def retention(q, k, v, a_fwd, a_bwd):
    B, S, H, Dh = q.shape
    C = RET_CHUNK
    nC = S // C
    dt = q.dtype
    inv = 1.0 / (RET_ROPE_BASE ** jnp.linspace(0.0, 1.0, Dh // 2, dtype=jnp.float32))
    q = rotary(q, inv)
    k = rotary(k, inv) * jnp.asarray(Dh ** -0.5, dt)
    lg_f = -jnp.exp(a_fwd.astype(jnp.float32))
    lg_b = -jnp.exp(a_bwd.astype(jnp.float32))
    j = jnp.arange(C, dtype=jnp.float32)
    lag = j[:, None] - j[None, :]
    alag = jnp.abs(lag)[None]
    dmask = jnp.where(lag[None] >= 0,
                      jnp.exp(lg_f[:, None, None] * alag),
                      jnp.exp(lg_b[:, None, None] * alag)).astype(dt)
    qc = q.reshape(B, nC, C, H, Dh)
    kc = k.reshape(B, nC, C, H, Dh)
    vc = v.reshape(B, nC, C, H, Dh)
    scores = jnp.einsum('bnjhd,bnlhd->bnhjl', qc, kc) * dmask[None, None]
    out = jnp.einsum('bnhjl,bnlhe->bnjhe', scores, vc)
    w_f = jnp.exp(lg_f[None, :] * (C - 1 - j)[:, None]).astype(dt)
    w_b = jnp.exp(lg_b[None, :] * j[:, None]).astype(dt)
    kv_f = jnp.einsum('bnlhd,lh,bnlhe->nbhde', kc, w_f, vc)
    kv_b = jnp.einsum('bnlhd,lh,bnlhe->nbhde', kc, w_b, vc)
    dec_f = jnp.exp(lg_f * C).astype(dt)[None, :, None, None]
    dec_b = jnp.exp(lg_b * C).astype(dt)[None, :, None, None]

    def step_f(state, kv):
        return state * dec_f + kv, state

    def step_b(state, kv):
        return state * dec_b + kv, state

    init = jnp.zeros((B, H, Dh, Dh), dt)
    _, s_f = lax.scan(step_f, init, kv_f)
    _, s_b = lax.scan(step_b, init, kv_b, reverse=True)
    q_f = jnp.exp(lg_f[None, :] * (j + 1.0)[:, None]).astype(dt)
    q_b = jnp.exp(lg_b[None, :] * (C - j)[:, None]).astype(dt)
    out = (out
           + jnp.einsum('bnjhd,nbhde->bnjhe', qc * q_f[None, None, :, :, None], s_f)
           + jnp.einsum('bnjhd,nbhde->bnjhe', qc * q_b[None, None, :, :, None], s_b))
    out = rms_norm(out.reshape(B, S, H, Dh))
    return out.reshape(B, S, H * Dh)


def multiscale_pool(u, pool_w, pool_scale):
    B, S, _ = u.shape
    ug = u.reshape(B, S, POOL_GROUPS, POOL_GROUP_DIM).astype(jnp.float32)
    cs = jnp.pad(jnp.cumsum(ug, axis=1), ((0, 0), (1, 0), (0, 0), (0, 0)))
    pos = jnp.arange(S)
    groups = []
    for g, w in enumerate(POOL_WINDOWS):
        lo = jnp.clip(pos - w // 2, 0, S)
        hi = jnp.clip(pos + w // 2, 0, S)
        cnt = (hi - lo).astype(jnp.float32)[None, :, None]
        csg = cs[:, :, g]
        mean = (csg[:, hi] - csg[:, lo]) / cnt
        groups.append(mean - ug[:, :, g])
    p = jnp.stack(groups, axis=2).astype(u.dtype)
    y = jnp.einsum('bsgd,gde->bsge', p, pool_w).reshape(B, S, POOL_WIDTH)
    return y * pool_scale


def windowed_gqa(q, k, v, q_gain, k_gain, sink):
    B, S, Hq, Dh = q.shape
    Hkv = k.shape[2]
    G = Hq // Hkv
    nB = S // ATT_BLOCK
    L = ATT_BLOCK
    inv = ROPE_THETA ** (-jnp.arange(ROPE_DIMS // 2, dtype=jnp.float32) / (ROPE_DIMS // 2))
    q = rotary(rms_norm(q, q_gain), inv)
    k = rotary(rms_norm(k, k_gain), inv)
    qb = q.reshape(B, nB, L, Hkv, G, Dh)
    pad = ((0, 0), (1, 1), (0, 0), (0, 0), (0, 0))
    kp = jnp.pad(k.reshape(B, nB, L, Hkv, Dh), pad)
    vp = jnp.pad(v.reshape(B, nB, L, Hkv, Dh), pad)
    kw = jnp.concatenate([kp[:, :-2], kp[:, 1:-1], kp[:, 2:]], axis=2)
    vw = jnp.concatenate([vp[:, :-2], vp[:, 1:-1], vp[:, 2:]], axis=2)
    s = jnp.einsum('bnqkgd,bnskd->bnkgqs', qb, kw).astype(jnp.float32) * (Dh ** -0.5)
    blk = jnp.arange(nB)[:, None]
    qpos = blk * L + jnp.arange(L)[None, :]
    kpos = (blk - 1) * L + jnp.arange(3 * L)[None, :]
    diff = kpos[:, None, :] - qpos[:, :, None]
    valid = ((jnp.abs(diff) <= ATT_WINDOW)
             & (kpos >= 0)[:, None, :] & (kpos < S)[:, None, :])
    s = jnp.where(valid[None, :, None, None], s, NEG_BIG)
    sk = sink.astype(jnp.float32).reshape(Hkv, G)[None, None, :, :, None, None]
    m = jnp.maximum(jnp.max(s, axis=-1, keepdims=True), sk)
    p = jnp.exp(s - m)
    p = p / (jnp.sum(p, axis=-1, keepdims=True) + jnp.exp(sk - m))
    o = jnp.einsum('bnkgqs,bnskd->bnqkgd', p.astype(v.dtype), vw)
    return o.reshape(B, S, Hq * Dh)


def hybrid_layer(x, norm_g, w_in, a_fwd, a_bwd, pool_w, pool_scale,
                 q_gain, k_gain, sink, w_ret, w_pool, w_att, w_out):
    B, S, D = x.shape
    h = rms_norm(x, norm_g)
    z = jnp.einsum('bsd,de->bse', h, w_in)
    (rq, rk, rv, rg, pv, pg, aq, ak, av, ag, mg) = jnp.split(
        z, list(np.cumsum(IN_SIZES)[:-1]), axis=-1)
    ya = retention(rq.reshape(B, S, RET_HEADS, RET_HEAD_DIM),
                   rk.reshape(B, S, RET_HEADS, RET_HEAD_DIM),
                   rv.reshape(B, S, RET_HEADS, RET_HEAD_DIM), a_fwd, a_bwd)
    ya = jnp.einsum('bse,ed->bsd', ya * jax.nn.silu(rg), w_ret)
    yb = multiscale_pool(pv, pool_w, pool_scale)
    yb = jnp.einsum('bse,ed->bsd', yb * jax.nn.silu(pg), w_pool)
    yc = windowed_gqa(aq.reshape(B, S, ATT_Q_HEADS, ATT_HEAD_DIM),
                      ak.reshape(B, S, ATT_KV_HEADS, ATT_HEAD_DIM),
                      av.reshape(B, S, ATT_KV_HEADS, ATT_HEAD_DIM), q_gain, k_gain, sink)
    yc = jnp.einsum('bse,ed->bsd', yc * jax.nn.silu(ag), w_att)
    gates = jax.nn.sigmoid(mg.astype(jnp.float32)).astype(x.dtype).reshape(B, S, N_BRANCHES, D)
    merged = gates[:, :, 0] * ya + gates[:, :, 1] * yb + gates[:, :, 2] * yc
    return x + jnp.einsum('bsd,de->bse', merged, w_out)


def _fwd_setup_inputs(seed: int = 0) -> dict:
    key = jax.random.key(seed)
    ks = jax.random.split(key, 14)
    f32 = jnp.float32
    D = D_MODEL
    nrm = jax.random.normal
    base = np.log(-np.log1p(-(2.0 ** (-5.0 - np.arange(RET_HEADS))))).astype(np.float32)
    base = jnp.asarray(base)[None, :]
    return {
        "x": nrm(ks[0], (BATCH, SEQ, D), f32),
        "norm_g": 1.0 + 0.02 * nrm(ks[1], (DEPTH, D), f32),
        "w_in": nrm(ks[2], (DEPTH, D, IN_WIDTH), f32) * (D ** -0.5),
        "ret_decay_fwd": base + 0.1 * nrm(ks[3], (DEPTH, RET_HEADS), f32),
        "ret_decay_bwd": base + 0.1 * nrm(ks[4], (DEPTH, RET_HEADS), f32),
        "pool_w": nrm(ks[5], (DEPTH, POOL_GROUPS, POOL_GROUP_DIM, POOL_GROUP_DIM), f32) * (POOL_GROUP_DIM ** -0.5),
        "pool_scale": 1.0 + 0.02 * nrm(ks[6], (DEPTH, POOL_WIDTH), f32),
        "attn_q_gain": 1.0 + 0.02 * nrm(ks[7], (DEPTH, ATT_HEAD_DIM), f32),
        "attn_k_gain": 1.0 + 0.02 * nrm(ks[8], (DEPTH, ATT_HEAD_DIM), f32),
        "attn_sink": 0.5 * nrm(ks[9], (DEPTH, ATT_Q_HEADS), f32),
        "w_ret": nrm(ks[10], (DEPTH, RET_WIDTH, D), f32) * (RET_WIDTH ** -0.5),
        "w_pool": nrm(ks[11], (DEPTH, POOL_WIDTH, D), f32) * (POOL_WIDTH ** -0.5),
        "w_att": nrm(ks[12], (DEPTH, ATT_WIDTH, D), f32) * (ATT_WIDTH ** -0.5),
        "w_out": nrm(ks[13], (DEPTH, D, D), f32) * (D ** -0.5),
    }


def _fwd_reference(x, norm_g, w_in, ret_decay_fwd, ret_decay_bwd, pool_w, pool_scale,
              attn_q_gain, attn_k_gain, attn_sink, w_ret, w_pool, w_att, w_out):
    for l in range(DEPTH):
        x = hybrid_layer(x, norm_g[l], w_in[l], ret_decay_fwd[l], ret_decay_bwd[l],
                         pool_w[l], pool_scale[l], attn_q_gain[l], attn_k_gain[l],
                         attn_sink[l], w_ret[l], w_pool[l], w_att[l], w_out[l])
    return x


import jax as _jax
import jax.numpy as _jnp

TWIN_FORMAT = 'train_step'
FWD_PARAMS = ['x', 'norm_g', 'w_in', 'ret_decay_fwd', 'ret_decay_bwd', 'pool_w', 'pool_scale', 'attn_q_gain', 'attn_k_gain', 'attn_sink', 'w_ret', 'w_pool', 'w_att', 'w_out']
TWIN_WEIGHTS = ['norm_g', 'w_in', 'ret_decay_fwd', 'ret_decay_bwd', 'pool_w', 'pool_scale', 'attn_q_gain', 'attn_k_gain', 'attn_sink', 'w_ret', 'w_pool', 'w_att', 'w_out']
TWIN_DIFF_INPUT = 'x'
TWIN_INPUTS = ['x', 'norm_g', 'w_in', 'ret_decay_fwd', 'ret_decay_bwd', 'pool_w', 'pool_scale', 'attn_q_gain', 'attn_k_gain', 'attn_sink', 'w_ret', 'w_pool', 'w_att', 'w_out', 'loss_target', 'm_norm_g', 'm_w_in', 'm_ret_decay_fwd', 'm_ret_decay_bwd', 'm_pool_w', 'm_pool_scale', 'm_attn_q_gain', 'm_attn_k_gain', 'm_attn_sink', 'm_w_ret', 'm_w_pool', 'm_w_att', 'm_w_out', 'v_norm_g', 'v_w_in', 'v_ret_decay_fwd', 'v_ret_decay_bwd', 'v_pool_w', 'v_pool_scale', 'v_attn_q_gain', 'v_attn_k_gain', 'v_attn_sink', 'v_w_ret', 'v_w_pool', 'v_w_att', 'v_w_out']
TWIN_OUTPUTS = ['loss', 'grad_x', 'grad_norm_g', 'grad_w_in', 'grad_ret_decay_fwd', 'grad_ret_decay_bwd', 'grad_pool_w', 'grad_pool_scale', 'grad_attn_q_gain', 'grad_attn_k_gain', 'grad_attn_sink', 'grad_w_ret', 'grad_w_pool', 'grad_w_att', 'grad_w_out', 'delta_norm_g', 'delta_w_in', 'delta_ret_decay_fwd', 'delta_ret_decay_bwd', 'delta_pool_w', 'delta_pool_scale', 'delta_attn_q_gain', 'delta_attn_k_gain', 'delta_attn_sink', 'delta_w_ret', 'delta_w_pool', 'delta_w_att', 'delta_w_out', 'new_m_norm_g', 'new_m_w_in', 'new_m_ret_decay_fwd', 'new_m_ret_decay_bwd', 'new_m_pool_w', 'new_m_pool_scale', 'new_m_attn_q_gain', 'new_m_attn_k_gain', 'new_m_attn_sink', 'new_m_w_ret', 'new_m_w_pool', 'new_m_w_att', 'new_m_w_out', 'new_v_norm_g', 'new_v_w_in', 'new_v_ret_decay_fwd', 'new_v_ret_decay_bwd', 'new_v_pool_w', 'new_v_pool_scale', 'new_v_attn_q_gain', 'new_v_attn_k_gain', 'new_v_attn_sink', 'new_v_w_ret', 'new_v_w_pool', 'new_v_w_att', 'new_v_w_out']
TWIN_LEAF_KINDS = {'loss': 'loss', 'grad_x': 'grad_x', 'grad_norm_g': 'grad_w', 'grad_w_in': 'grad_w', 'grad_ret_decay_fwd': 'grad_w', 'grad_ret_decay_bwd': 'grad_w', 'grad_pool_w': 'grad_w', 'grad_pool_scale': 'grad_w', 'grad_attn_q_gain': 'grad_w', 'grad_attn_k_gain': 'grad_w', 'grad_attn_sink': 'grad_w', 'grad_w_ret': 'grad_w', 'grad_w_pool': 'grad_w', 'grad_w_att': 'grad_w', 'grad_w_out': 'grad_w', 'delta_norm_g': 'delta_w', 'delta_w_in': 'delta_w', 'delta_ret_decay_fwd': 'delta_w', 'delta_ret_decay_bwd': 'delta_w', 'delta_pool_w': 'delta_w', 'delta_pool_scale': 'delta_w', 'delta_attn_q_gain': 'delta_w', 'delta_attn_k_gain': 'delta_w', 'delta_attn_sink': 'delta_w', 'delta_w_ret': 'delta_w', 'delta_w_pool': 'delta_w', 'delta_w_att': 'delta_w', 'delta_w_out': 'delta_w', 'new_m_norm_g': 'new_m', 'new_m_w_in': 'new_m', 'new_m_ret_decay_fwd': 'new_m', 'new_m_ret_decay_bwd': 'new_m', 'new_m_pool_w': 'new_m', 'new_m_pool_scale': 'new_m', 'new_m_attn_q_gain': 'new_m', 'new_m_attn_k_gain': 'new_m', 'new_m_attn_sink': 'new_m', 'new_m_w_ret': 'new_m', 'new_m_w_pool': 'new_m', 'new_m_w_att': 'new_m', 'new_m_w_out': 'new_m', 'new_v_norm_g': 'new_v', 'new_v_w_in': 'new_v', 'new_v_ret_decay_fwd': 'new_v', 'new_v_ret_decay_bwd': 'new_v', 'new_v_pool_w': 'new_v', 'new_v_pool_scale': 'new_v', 'new_v_attn_q_gain': 'new_v', 'new_v_attn_k_gain': 'new_v', 'new_v_attn_sink': 'new_v', 'new_v_w_ret': 'new_v', 'new_v_w_pool': 'new_v', 'new_v_w_att': 'new_v', 'new_v_w_out': 'new_v'}


def _forward(args):
    return _fwd_reference(*[args[k] for k in FWD_PARAMS])


def _output_shape():
    def fwd():
        inp = _fwd_setup_inputs(0)
        return _fwd_reference(*[inp[k] for k in FWD_PARAMS])
    out = _jax.eval_shape(fwd)
    return out.shape, out.dtype

N_MICROBATCH = 1
ADAM_LR = 0.001
ADAM_B1 = 0.9
ADAM_B2 = 0.999
ADAM_EPS = 1e-08
ADAM_WD = 0.01
ADAM_STEP = 10
PER_EXAMPLE_BATCH_AXIS = {'x': 0, 'loss_target': 0}
SHARED_INPUTS = []
_WEIGHT_DTYPES = {'norm_g': _jnp.float32, 'w_in': _jnp.float32, 'ret_decay_fwd': _jnp.float32, 'ret_decay_bwd': _jnp.float32, 'pool_w': _jnp.float32, 'pool_scale': _jnp.float32, 'attn_q_gain': _jnp.float32, 'attn_k_gain': _jnp.float32, 'attn_sink': _jnp.float32, 'w_ret': _jnp.float32, 'w_pool': _jnp.float32, 'w_att': _jnp.float32, 'w_out': _jnp.float32}
MOMENT_SCALE = {'norm_g': 5.048142e+00, 'w_in': 7.836351e-02, 'ret_decay_fwd': 7.780706e-01, 'ret_decay_bwd': 8.463105e-01, 'pool_w': 1.661069e-01, 'pool_scale': 2.601706e+00, 'attn_q_gain': 2.149240e-01, 'attn_k_gain': 2.147187e-01, 'attn_sink': 1.124545e-02, 'w_ret': 8.835268e-02, 'w_pool': 7.906437e-02, 'w_att': 8.915359e-03, 'w_out': 1.177283e-01}


def _to_microbatches(a, axis):
    t = _jnp.moveaxis(a, axis, 0)
    t = t.reshape((N_MICROBATCH, t.shape[0] // N_MICROBATCH) + t.shape[1:])
    return _jnp.moveaxis(t, 1, axis + 1)


def setup_inputs(seed: int = 0) -> dict:
    inp = _fwd_setup_inputs(seed)
    key = _jax.random.fold_in(_jax.random.key(seed), 7919)
    shape, _ = _output_shape()
    out = dict(inp)
    out["loss_target"] = _jax.random.normal(_jax.random.fold_in(key, 0), shape, _jnp.float32)
    for i, name in enumerate(TWIN_WEIGHTS):
        w = inp[name].astype(_jnp.float32)
        if MOMENT_SCALE is None:
            s = _jnp.sqrt(_jnp.mean(_jnp.square(w)) + 1e-30)
        else:
            s = MOMENT_SCALE[name]
        km, kv = _jax.random.split(_jax.random.fold_in(key, i + 1))
        out[name] = w
        out["m_" + name] = s * _jax.random.normal(km, w.shape, _jnp.float32)
        out["v_" + name] = (s * s) * _jax.random.uniform(kv, w.shape, _jnp.float32, 0.5, 1.5)
    if N_MICROBATCH > 1:
        for name, axis in PER_EXAMPLE_BATCH_AXIS.items():
            out[name] = _to_microbatches(out[name], axis)
    return {'x': out['x'], 'norm_g': out['norm_g'], 'w_in': out['w_in'], 'ret_decay_fwd': out['ret_decay_fwd'], 'ret_decay_bwd': out['ret_decay_bwd'], 'pool_w': out['pool_w'], 'pool_scale': out['pool_scale'], 'attn_q_gain': out['attn_q_gain'], 'attn_k_gain': out['attn_k_gain'], 'attn_sink': out['attn_sink'], 'w_ret': out['w_ret'], 'w_pool': out['w_pool'], 'w_att': out['w_att'], 'w_out': out['w_out'], 'loss_target': out['loss_target'], 'm_norm_g': out['m_norm_g'], 'm_w_in': out['m_w_in'], 'm_ret_decay_fwd': out['m_ret_decay_fwd'], 'm_ret_decay_bwd': out['m_ret_decay_bwd'], 'm_pool_w': out['m_pool_w'], 'm_pool_scale': out['m_pool_scale'], 'm_attn_q_gain': out['m_attn_q_gain'], 'm_attn_k_gain': out['m_attn_k_gain'], 'm_attn_sink': out['m_attn_sink'], 'm_w_ret': out['m_w_ret'], 'm_w_pool': out['m_w_pool'], 'm_w_att': out['m_w_att'], 'm_w_out': out['m_w_out'], 'v_norm_g': out['v_norm_g'], 'v_w_in': out['v_w_in'], 'v_ret_decay_fwd': out['v_ret_decay_fwd'], 'v_ret_decay_bwd': out['v_ret_decay_bwd'], 'v_pool_w': out['v_pool_w'], 'v_pool_scale': out['v_pool_scale'], 'v_attn_q_gain': out['v_attn_q_gain'], 'v_attn_k_gain': out['v_attn_k_gain'], 'v_attn_sink': out['v_attn_sink'], 'v_w_ret': out['v_w_ret'], 'v_w_pool': out['v_w_pool'], 'v_w_att': out['v_w_att'], 'v_w_out': out['v_w_out']}


def _loss(weights, diff, rest, loss_target):
    with _jax.named_scope("forward"):
        args = {**rest, TWIN_DIFF_INPUT: diff, **{k: w.astype(_WEIGHT_DTYPES[k]) for k, w in weights.items()}}
        y = _forward(args)
    with _jax.named_scope("loss_head"):
        err = _jnp.square(y.astype(_jnp.float32) - loss_target)
        return 0.5 * _jnp.sum(_jnp.mean(err, axis=-1)) if err.ndim else 0.5 * err


def _adamw(w, g, m, v):
    m = ADAM_B1 * m + (1.0 - ADAM_B1) * g
    v = ADAM_B2 * v + (1.0 - ADAM_B2) * _jnp.square(g)
    m_hat = m / (1.0 - ADAM_B1 ** ADAM_STEP)
    v_hat = v / (1.0 - ADAM_B2 ** ADAM_STEP)
    delta = -ADAM_LR * (m_hat / (_jnp.sqrt(v_hat) + ADAM_EPS) + ADAM_WD * w)
    return delta, m, v


def reference(x, norm_g, w_in, ret_decay_fwd, ret_decay_bwd, pool_w, pool_scale, attn_q_gain, attn_k_gain, attn_sink, w_ret, w_pool, w_att, w_out, loss_target, m_norm_g, m_w_in, m_ret_decay_fwd, m_ret_decay_bwd, m_pool_w, m_pool_scale, m_attn_q_gain, m_attn_k_gain, m_attn_sink, m_w_ret, m_w_pool, m_w_att, m_w_out, v_norm_g, v_w_in, v_ret_decay_fwd, v_ret_decay_bwd, v_pool_w, v_pool_scale, v_attn_q_gain, v_attn_k_gain, v_attn_sink, v_w_ret, v_w_pool, v_w_att, v_w_out):
    given = dict(x=x, norm_g=norm_g, w_in=w_in, ret_decay_fwd=ret_decay_fwd, ret_decay_bwd=ret_decay_bwd, pool_w=pool_w, pool_scale=pool_scale, attn_q_gain=attn_q_gain, attn_k_gain=attn_k_gain, attn_sink=attn_sink, w_ret=w_ret, w_pool=w_pool, w_att=w_att, w_out=w_out, loss_target=loss_target, m_norm_g=m_norm_g, m_w_in=m_w_in, m_ret_decay_fwd=m_ret_decay_fwd, m_ret_decay_bwd=m_ret_decay_bwd, m_pool_w=m_pool_w, m_pool_scale=m_pool_scale, m_attn_q_gain=m_attn_q_gain, m_attn_k_gain=m_attn_k_gain, m_attn_sink=m_attn_sink, m_w_ret=m_w_ret, m_w_pool=m_w_pool, m_w_att=m_w_att, m_w_out=m_w_out, v_norm_g=v_norm_g, v_w_in=v_w_in, v_ret_decay_fwd=v_ret_decay_fwd, v_ret_decay_bwd=v_ret_decay_bwd, v_pool_w=v_pool_w, v_pool_scale=v_pool_scale, v_attn_q_gain=v_attn_q_gain, v_attn_k_gain=v_attn_k_gain, v_attn_sink=v_attn_sink, v_w_ret=v_w_ret, v_w_pool=v_w_pool, v_w_att=v_w_att, v_w_out=v_w_out)
    weights = {n: given[n] for n in TWIN_WEIGHTS}
    shared = {n: given[n] for n in SHARED_INPUTS}
    per_example = {n: given[n] for n in ['x']}
    grad_fn = _jax.value_and_grad(_loss, argnums=(0, 1))

    def one_microbatch(ex, loss_target):
        ex = dict(ex)
        diff = ex.pop(TWIN_DIFF_INPUT)
        return grad_fn(weights, diff, {**shared, **ex}, loss_target)

    if N_MICROBATCH == 1:
        loss, (grad_w, grad_x) = one_microbatch(per_example, given["loss_target"])
    else:
        def body(carry, xs):
            loss_sum, grad_sum = carry
            l_k, (gw_k, gx_k) = one_microbatch(xs[0], xs[1])
            with _jax.named_scope("update"):
                return (loss_sum + l_k, _jax.tree.map(_jnp.add, grad_sum, gw_k)), gx_k

        init = (_jnp.zeros((), _jnp.float32), _jax.tree.map(_jnp.zeros_like, weights))
        (loss, grad_w), grad_x = _jax.lax.scan(body, init, (per_example, given["loss_target"]))
    with _jax.named_scope("update"):
        delta_w, new_m, new_v = {}, {}, {}
        for n in TWIN_WEIGHTS:
            delta_w[n], new_m[n], new_v[n] = _adamw(weights[n], grad_w[n], given["m_" + n], given["v_" + n])
    return (loss, grad_x, *[grad_w[n] for n in TWIN_WEIGHTS], *[delta_w[n] for n in TWIN_WEIGHTS],
            *[new_m[n] for n in TWIN_WEIGHTS], *[new_v[n] for n in TWIN_WEIGHTS])
```

```python
import functools

import numpy as np
import jax
import jax.numpy as jnp
from jax import lax
from jax.experimental import pallas as pl
from jax.experimental.pallas import tpu as pltpu

F32 = jnp.float32
MXU = jnp.bfloat16
ACT = jnp.bfloat16

N_DEV = 8
RMS_EPS = 1e-6
NEG_BIG = -1e30
RET_HEADS = 4
RET_HD = 256
CH = 128
RET_ROPE_BASE = 10000.0
POOL_WINDOWS = (2, 4, 8, 16)
POOL_GD = 256
POOL_PAD = 8
ATT_HD = 128
ATT_Q = 8
ATT_KV = 2
ATT_G = ATT_Q // ATT_KV
ATT_WIN = 128
ATT_BLK = 128
ATT_SPAN = 3 * ATT_BLK
ROPE_THETA = 500000.0
ROPE_HALF = 16

ADAM_LR = 0.001
ADAM_B1 = 0.9
ADAM_B2 = 0.999
ADAM_EPS = 1e-08
ADAM_WD = 0.01
ADAM_STEP = 10

VMEM_LIMIT = 48 * 1024 * 1024

NN = ((1,), (0,))
NT = ((1,), (1,))
TN = ((0,), (0,))


def _dot(a, b, dims):
    return lax.dot_general(a.astype(MXU), b.astype(MXU), (dims, ((), ())),
                           preferred_element_type=F32)


def _sigmoid(x):
    return 1.0 / (1.0 + jnp.exp(-x))


def _params(*sem):
    return pltpu.CompilerParams(dimension_semantics=sem, vmem_limit_bytes=VMEM_LIMIT)


def _sum_all(x):
    return jnp.sum(jnp.sum(x, axis=1, keepdims=True), axis=0, keepdims=True)


def _fiota(shape, dim):
    return lax.broadcasted_iota(jnp.int32, shape, dim).astype(F32)


SMEM_SPEC = pl.BlockSpec(memory_space=pltpu.SMEM)


def _matmul(a, b, mode, out_dtype, tm, tn, tk, name, res=None):
    if mode == "tn":
        K, M = a.shape
    else:
        M, K = a.shape
    N = b.shape[0] if mode == "nt" else b.shape[1]
    tm, tn, tk = min(tm, M), min(tn, N), min(tk, K)
    assert M % tm == 0 and N % tn == 0 and K % tk == 0, (name, M, N, K, tm, tn, tk)
    nk = K // tk
    dims = {"nn": NN, "nt": NT, "tn": TN}[mode]
    a_spec = (pl.BlockSpec((tk, tm), lambda i, j, k: (k, i)) if mode == "tn"
              else pl.BlockSpec((tm, tk), lambda i, j, k: (i, k)))
    b_spec = (pl.BlockSpec((tn, tk), lambda i, j, k: (j, k)) if mode == "nt"
              else pl.BlockSpec((tk, tn), lambda i, j, k: (k, j)))
    o_spec = pl.BlockSpec((tm, tn), lambda i, j, k: (i, j))
    has_res = res is not None

    def body(*refs):
        if has_res:
            a_ref, b_ref, r_ref, o_ref, acc = refs
        else:
            a_ref, b_ref, o_ref, acc = refs
        k = pl.program_id(2)

        @pl.when(k == 0)
        def _():
            acc[...] = jnp.zeros_like(acc)

        acc[...] += _dot(a_ref[...], b_ref[...], dims)

        @pl.when(k == nk - 1)
        def _():
            out = acc[...]
            if has_res:
                out = out + r_ref[...]
            o_ref[...] = out.astype(out_dtype)

    ins = [a, b] + ([res] if has_res else [])
    in_specs = [a_spec, b_spec] + ([o_spec] if has_res else [])
    return pl.pallas_call(
        body, grid=(M // tm, N // tn, nk), in_specs=in_specs, out_specs=o_spec,
        out_shape=jax.ShapeDtypeStruct((M, N), out_dtype),
        scratch_shapes=[pltpu.VMEM((tm, tn), F32)], name=name,
        compiler_params=_params("parallel", "parallel", "arbitrary"))(*ins)


def _rmsnorm_fwd(x, g):
    S, D = x.shape
    tm = min(512, S)

    def body(x_ref, g_ref, h_ref):
        xv = x_ref[...]
        r = lax.rsqrt(jnp.mean(xv * xv, axis=-1, keepdims=True) + RMS_EPS)
        h_ref[...] = (xv * r * g_ref[...]).astype(ACT)

    row = pl.BlockSpec((tm, D), lambda i: (i, 0))
    return pl.pallas_call(
        body, grid=(S // tm,), in_specs=[row, pl.BlockSpec((1, D), lambda i: (0, 0))],
        out_specs=row, out_shape=jax.ShapeDtypeStruct((S, D), ACT), name="rmsnorm_fwd",
        compiler_params=_params("parallel"))(x, g)


def _rmsnorm_bwd(x, g, dh, dres):
    S, D = x.shape
    tm = min(256, S)

    def body(x_ref, g_ref, dh_ref, dr_ref, dx_ref, dxb_ref, dg_ref):
        xv = x_ref[...]
        r = lax.rsqrt(jnp.mean(xv * xv, axis=-1, keepdims=True) + RMS_EPS)
        xh = xv * r
        dhv = dh_ref[...]
        dxh = dhv * g_ref[...]
        dx = r * (dxh - xh * jnp.mean(dxh * xh, axis=-1, keepdims=True)) + dr_ref[...]
        dx_ref[...] = dx
        dxb_ref[...] = dx.astype(ACT)

        @pl.when(pl.program_id(0) == 0)
        def _():
            dg_ref[...] = jnp.zeros_like(dg_ref)

        dg_ref[...] += jnp.sum(dhv * xh, axis=0, keepdims=True)

    row = pl.BlockSpec((tm, D), lambda i: (i, 0))
    vec = pl.BlockSpec((1, D), lambda i: (0, 0))
    return pl.pallas_call(
        body, grid=(S // tm,), in_specs=[row, vec, row, row], out_specs=[row, row, vec],
        out_shape=[jax.ShapeDtypeStruct((S, D), F32), jax.ShapeDtypeStruct((S, D), ACT),
                   jax.ShapeDtypeStruct((1, D), F32)],
        name="rmsnorm_bwd", compiler_params=_params("arbitrary"))(x, g, dh, dres)


def _rot256(x, c, s):
    x1, x2 = x[:, :128], x[:, 128:]
    return jnp.concatenate([x1 * c - x2 * s, x2 * c + x1 * s], axis=1)


def _rot256_t(g, c, s):
    g1, g2 = g[:, :128], g[:, 128:]
    return jnp.concatenate([g1 * c + g2 * s, g2 * c - g1 * s], axis=1)


def _log_decay(a_ref, h, shape):
    return -jnp.exp(jnp.full(shape, a_ref[h], F32))


def _ret_state(xsrc, xbase, xscale, ysrc, ybase, cos, sin, af, ab, mode, name):
    S = xsrc.shape[0]
    nC = S // CH
    H = RET_HEADS

    def body(x1_ref, y1_ref, c1_ref, s1_ref, x2_ref, y2_ref, c2_ref, s2_ref, af_ref, ab_ref,
             st1_ref, st2_ref, acc1, acc2):
        h = pl.program_id(0)

        @pl.when(pl.program_id(1) == 0)
        def _():
            acc1[...] = jnp.zeros_like(acc1)
            acc2[...] = jnp.zeros_like(acc2)

        lgf = _log_decay(af_ref, h, (CH, 1))
        lgb = _log_decay(ab_ref, h, (CH, 1))
        j = _fiota((CH, 1), 0)
        if mode == "fwd":
            w1, d1 = jnp.exp(lgf * (CH - 1.0 - j)), jnp.exp(lgf[:1] * CH)
            w2, d2 = jnp.exp(lgb * j), jnp.exp(lgb[:1] * CH)
        else:
            w1, d1 = jnp.exp(lgb * (CH - j)), jnp.exp(lgb[:1] * CH)
            w2, d2 = jnp.exp(lgf * (j + 1.0)), jnp.exp(lgf[:1] * CH)
        xa = _rot256(x1_ref[...], c1_ref[...], s1_ref[...]) * xscale
        st1_ref[...] = acc1[...]
        acc1[...] = d1 * acc1[...] + _dot(xa * w1, y1_ref[...], TN)
        xb = _rot256(x2_ref[...], c2_ref[...], s2_ref[...]) * xscale
        st2_ref[...] = acc2[...]
        acc2[...] = d2 * acc2[...] + _dot(xb * w2, y2_ref[...], TN)

    def fw(col):
        return lambda h, c: (c, col + h)

    def rv(col):
        return lambda h, c: (nC - 1 - c, col + h)

    in_specs = [
        pl.BlockSpec((CH, RET_HD), fw(xbase)), pl.BlockSpec((CH, RET_HD), fw(ybase)),
        pl.BlockSpec((CH, 128), lambda h, c: (c, 0)), pl.BlockSpec((CH, 128), lambda h, c: (c, 0)),
        pl.BlockSpec((CH, RET_HD), rv(xbase)), pl.BlockSpec((CH, RET_HD), rv(ybase)),
        pl.BlockSpec((CH, 128), lambda h, c: (nC - 1 - c, 0)),
        pl.BlockSpec((CH, 128), lambda h, c: (nC - 1 - c, 0)),
        SMEM_SPEC, SMEM_SPEC]
    out_specs = [pl.BlockSpec((None, None, RET_HD, RET_HD), lambda h, c: (h, c, 0, 0)),
                 pl.BlockSpec((None, None, RET_HD, RET_HD), lambda h, c: (h, nC - 1 - c, 0, 0))]
    st = jax.ShapeDtypeStruct((H, nC, RET_HD, RET_HD), F32)
    return pl.pallas_call(
        body, grid=(H, nC), in_specs=in_specs, out_specs=out_specs, out_shape=[st, st],
        scratch_shapes=[pltpu.VMEM((RET_HD, RET_HD), F32), pltpu.VMEM((RET_HD, RET_HD), F32)],
        name=name, compiler_params=_params("parallel", "arbitrary"))(
            xsrc, ysrc, cos, sin, xsrc, ysrc, cos, sin, af, ab)


def _decay_mask(lgf1, lgb1):
    lag = _fiota((CH, CH), 0) - _fiota((CH, CH), 1)
    alag = jnp.abs(lag)
    return lag, jnp.where(lag >= 0, jnp.exp(lgf1 * alag), jnp.exp(lgb1 * alag))


def _ret_fwd(z, cos, sin, sf, sb, af, ab, cols):
    S = z.shape[0]
    nC = S // CH
    H = RET_HEADS
    rq, rk, rv, rg = cols
    W = H * RET_HD

    def body(q_ref, k_ref, v_ref, g_ref, c_ref, s_ref, sf_ref, sb_ref, af_ref, ab_ref,
             o_ref, u_ref):
        h = pl.program_id(0)
        lgf = _log_decay(af_ref, h, (CH, 1))
        lgb = _log_decay(ab_ref, h, (CH, 1))
        j = _fiota((CH, 1), 0)
        c, s = c_ref[...], s_ref[...]
        q = _rot256(q_ref[...], c, s)
        k = _rot256(k_ref[...], c, s) * (RET_HD ** -0.5)
        _, dm = _decay_mask(lgf[:1], lgb[:1])
        p = _dot(q, k, NT) * dm
        o = (_dot(p, v_ref[...], NN)
             + _dot(q * jnp.exp(lgf * (j + 1.0)), sf_ref[...], NN)
             + _dot(q * jnp.exp(lgb * (CH - j)), sb_ref[...], NN))
        o_ref[...] = o
        on = o * lax.rsqrt(jnp.mean(o * o, axis=-1, keepdims=True) + RMS_EPS)
        g = g_ref[...]
        u_ref[...] = (on * (g * _sigmoid(g))).astype(ACT)

    def zc(col):
        return pl.BlockSpec((CH, RET_HD), lambda h, c: (c, col + h))

    tab = pl.BlockSpec((CH, 128), lambda h, c: (c, 0))
    stt = pl.BlockSpec((None, None, RET_HD, RET_HD), lambda h, c: (h, c, 0, 0))
    out = pl.BlockSpec((CH, RET_HD), lambda h, c: (c, h))
    return pl.pallas_call(
        body, grid=(H, nC),
        in_specs=[zc(rq), zc(rk), zc(rv), zc(rg), tab, tab, stt, stt, SMEM_SPEC, SMEM_SPEC],
        out_specs=[out, out],
        out_shape=[jax.ShapeDtypeStruct((S, W), F32), jax.ShapeDtypeStruct((S, W), ACT)],
        name="ret_fwd", compiler_params=_params("parallel", "parallel"))(
            z, z, z, z, cos, sin, sf, sb, af, ab)


def _ret_gate_bwd(du, o_pre, z, rg):
    S, W = du.shape
    H = RET_HEADS
    tm = min(512, S)

    def body(du_ref, o_ref, g_ref, do_ref, dg_ref):
        o = o_ref[...]
        r = lax.rsqrt(jnp.mean(o * o, axis=-1, keepdims=True) + RMS_EPS)
        on = o * r
        g = g_ref[...]
        sg = _sigmoid(g)
        duv = du_ref[...]
        don = duv * (g * sg)
        dg_ref[...] = (duv * on * (sg * (1.0 + g * (1.0 - sg)))).astype(ACT)
        do_ref[...] = r * (don - on * jnp.mean(don * on, axis=-1, keepdims=True))

    blk = pl.BlockSpec((tm, RET_HD), lambda i, h: (i, h))
    return pl.pallas_call(
        body, grid=(S // tm, H),
        in_specs=[blk, blk, pl.BlockSpec((tm, RET_HD), lambda i, h: (i, rg + h))],
        out_specs=[blk, blk],
        out_shape=[jax.ShapeDtypeStruct((S, W), F32), jax.ShapeDtypeStruct((S, W), ACT)],
        name="ret_gate_bwd", compiler_params=_params("parallel", "parallel"))(du, o_pre, z)


def _ret_bwd(z, do, cos, sin, sf, sb, ef, eb, af, ab, cols):
    S = z.shape[0]
    nC = S // CH
    H = RET_HEADS
    rq, rk, rv, _ = cols
    W = H * RET_HD

    def body(q_ref, k_ref, v_ref, do_ref, c_ref, s_ref, sf_ref, sb_ref, ef_ref, eb_ref,
             af_ref, ab_ref, dq_ref, dk_ref, dv_ref, da_ref):
        h = pl.program_id(0)
        lgf = _log_decay(af_ref, h, (CH, 1))
        lgb = _log_decay(ab_ref, h, (CH, 1))
        j = _fiota((CH, 1), 0)
        c, s = c_ref[...], s_ref[...]
        scale = RET_HD ** -0.5
        q = _rot256(q_ref[...], c, s)
        k = _rot256(k_ref[...], c, s) * scale
        v = v_ref[...]
        do = do_ref[...]
        sf_, sb_, ef_, eb_ = sf_ref[...], sb_ref[...], ef_ref[...], eb_ref[...]
        a_w = jnp.exp(lgf * (j + 1.0))
        b_w = jnp.exp(lgb * (CH - j))
        wf = jnp.exp(lgf * (CH - 1.0 - j))
        wb = jnp.exp(lgb * j)
        lag, dm = _decay_mask(lgf[:1], lgb[:1])
        sc = _dot(q, k, NT)
        gg = _dot(do, v, NT)
        dg = gg * dm
        x1 = _dot(do, sf_, NT) * a_w
        x2 = _dot(do, sb_, NT) * b_w
        y1 = _dot(v, ef_, NT) * wf
        y2 = _dot(v, eb_, NT) * wb
        dq = _dot(dg, k, NN) + x1 + x2
        dk = _dot(dg, q, TN) + y1 + y2
        dv = _dot(sc * dm, do, TN) + _dot(k * wf, ef_, NN) + _dot(k * wb, eb_, NN)
        dq_ref[...] = _rot256_t(dq, c, s).astype(ACT)
        dk_ref[...] = (_rot256_t(dk, c, s) * scale).astype(ACT)
        dv_ref[...] = dv.astype(ACT)
        t = dm * gg * sc
        qx1 = jnp.sum(q * x1, axis=-1, keepdims=True)
        qx2 = jnp.sum(q * x2, axis=-1, keepdims=True)
        ky1 = jnp.sum(k * y1, axis=-1, keepdims=True)
        ky2 = jnp.sum(k * y2, axis=-1, keepdims=True)
        dlf = (_sum_all(jnp.where(lag > 0, lag * t, 0.0))
               + _sum_all((j + 1.0) * qx1 + (CH - 1.0 - j) * ky1)
               + CH * jnp.exp(lgf[:1] * CH) * _sum_all(ef_ * sf_))
        dlb = (_sum_all(jnp.where(lag < 0, -lag * t, 0.0))
               + _sum_all((CH - j) * qx2 + j * ky2)
               + CH * jnp.exp(lgb[:1] * CH) * _sum_all(eb_ * sb_))
        row = lax.broadcasted_iota(jnp.int32, (8, 128), 0)
        lane = lax.broadcasted_iota(jnp.int32, (8, 128), 1)
        tile = jnp.where((row == 0) & (lane == 0), dlf * lgf[:1],
                         jnp.where((row == 0) & (lane == 1), dlb * lgb[:1], 0.0))

        @pl.when(pl.program_id(1) == 0)
        def _():
            da_ref[...] = jnp.zeros_like(da_ref)

        da_ref[...] += tile

    def zc(col):
        return pl.BlockSpec((CH, RET_HD), lambda h, c: (c, col + h))

    tab = pl.BlockSpec((CH, 128), lambda h, c: (c, 0))
    stt = pl.BlockSpec((None, None, RET_HD, RET_HD), lambda h, c: (h, c, 0, 0))
    out = pl.BlockSpec((CH, RET_HD), lambda h, c: (c, h))
    dz = jax.ShapeDtypeStruct((S, W), ACT)
    return pl.pallas_call(
        body, grid=(H, nC),
        in_specs=[zc(rq), zc(rk), zc(rv), out, tab, tab, stt, stt, stt, stt, SMEM_SPEC, SMEM_SPEC],
        out_specs=[out, out, out, pl.BlockSpec((None, 8, 128), lambda h, c: (h, 0, 0))],
        out_shape=[dz, dz, dz, jax.ShapeDtypeStruct((H, 8, 128), F32)],
        name="ret_bwd", compiler_params=_params("parallel", "arbitrary"))(
            z, z, z, do, cos, sin, sf, sb, ef, eb, af, ab)


def _fill_padded(pad_ref, src_ref, S):
    zeros = jnp.zeros((POOL_PAD, POOL_GD), F32)
    pad_ref[pl.ds(0, POOL_PAD), :] = zeros
    pad_ref[pl.ds(POOL_PAD, S), :] = src_ref[...]
    pad_ref[pl.ds(S + POOL_PAD, POOL_PAD), :] = zeros


def _window_sum(ext, T, lo, hi):
    n = T + 2 * POOL_PAD
    acc = None
    for k in range(lo, hi):
        sh = ext if k == 0 else pltpu.roll(ext, (-k) % n, 0)
        piece = sh[POOL_PAD:POOL_PAD + T]
        acc = piece if acc is None else acc + piece
    return acc


def _window_count(pos, w, S):
    lo = jnp.maximum(pos - w // 2, 0)
    hi = jnp.minimum(pos + w // 2, S)
    return (hi - lo).astype(F32)


def _pool_fwd(z, pw, scale, pv, pg):
    S = z.shape[0]
    G = len(POOL_WINDOWS)
    T = min(512, S)
    W = G * POOL_GD

    def body(x_ref, g_ref, pw_ref, sc_ref, y_ref, u_ref, pad, p_scr):
        grp = pl.program_id(0)
        i = pl.program_id(1)

        @pl.when(i == 0)
        def _():
            _fill_padded(pad, x_ref, S)

        r0 = pl.multiple_of(i * T, T)
        ext = pad[pl.ds(r0, T + 2 * POOL_PAD), :]
        pos = r0 + lax.broadcasted_iota(jnp.int32, (T, 1), 0)
        for gi, w in enumerate(POOL_WINDOWS):
            @pl.when(grp == gi)
            def _(w=w):
                acc = _window_sum(ext, T, -(w // 2), w // 2)
                p_scr[...] = acc / _window_count(pos, w, S) - ext[POOL_PAD:POOL_PAD + T]

        y = _dot(p_scr[...], pw_ref[...], NN)
        y_ref[...] = y
        g = g_ref[...]
        u_ref[...] = (y * sc_ref[...] * (g * _sigmoid(g))).astype(ACT)

    blk = pl.BlockSpec((T, POOL_GD), lambda g, i: (i, g))
    return pl.pallas_call(
        body, grid=(G, S // T),
        in_specs=[pl.BlockSpec((S, POOL_GD), lambda g, i: (0, pv + g)),
                  pl.BlockSpec((T, POOL_GD), lambda g, i: (i, pg + g)),
                  pl.BlockSpec((None, POOL_GD, POOL_GD), lambda g, i: (g, 0, 0)),
                  pl.BlockSpec((1, POOL_GD), lambda g, i: (0, g))],
        out_specs=[blk, blk],
        out_shape=[jax.ShapeDtypeStruct((S, W), F32), jax.ShapeDtypeStruct((S, W), ACT)],
        scratch_shapes=[pltpu.VMEM((S + 2 * POOL_PAD, POOL_GD), F32), pltpu.VMEM((T, POOL_GD), F32)],
        name="pool_fwd", compiler_params=_params("parallel", "arbitrary"))(z, z, pw, scale)


def _pool_bwd_a(z, pw, scale, y_raw, du, pv, pg):
    S = z.shape[0]
    G = len(POOL_WINDOWS)
    T = min(512, S)
    W = G * POOL_GD

    def body(x_ref, g_ref, pw_ref, sc_ref, y_ref, du_ref, dpc_ref, dg_ref, dsc_ref, dpw_ref,
             pad, p_scr, c_scr):
        grp = pl.program_id(0)
        i = pl.program_id(1)

        @pl.when(i == 0)
        def _():
            _fill_padded(pad, x_ref, S)
            dsc_ref[...] = jnp.zeros_like(dsc_ref)
            dpw_ref[...] = jnp.zeros_like(dpw_ref)

        r0 = pl.multiple_of(i * T, T)
        ext = pad[pl.ds(r0, T + 2 * POOL_PAD), :]
        pos = r0 + lax.broadcasted_iota(jnp.int32, (T, 1), 0)
        for gi, w in enumerate(POOL_WINDOWS):
            @pl.when(grp == gi)
            def _(w=w):
                cnt = _window_count(pos, w, S)
                acc = _window_sum(ext, T, -(w // 2), w // 2)
                p_scr[...] = acc / cnt - ext[POOL_PAD:POOL_PAD + T]
                c_scr[...] = jnp.broadcast_to(cnt, (T, 128))

        g = g_ref[...]
        sg = _sigmoid(g)
        duv = du_ref[...]
        y = y_ref[...]
        scl = sc_ref[...]
        dy = duv * (scl * (g * sg))
        dg_ref[...] = (duv * y * scl * (sg * (1.0 + g * (1.0 - sg)))).astype(ACT)
        dsc_ref[...] += jnp.sum(duv * y * (g * sg), axis=0, keepdims=True)
        dpw_ref[...] += _dot(p_scr[...], dy, TN)
        dpc_ref[...] = _dot(dy, pw_ref[...], NT) / c_scr[:, :1]

    blk = pl.BlockSpec((T, POOL_GD), lambda g, i: (i, g))
    return pl.pallas_call(
        body, grid=(G, S // T),
        in_specs=[pl.BlockSpec((S, POOL_GD), lambda g, i: (0, pv + g)),
                  pl.BlockSpec((T, POOL_GD), lambda g, i: (i, pg + g)),
                  pl.BlockSpec((None, POOL_GD, POOL_GD), lambda g, i: (g, 0, 0)),
                  pl.BlockSpec((1, POOL_GD), lambda g, i: (0, g)), blk, blk],
        out_specs=[blk, blk, pl.BlockSpec((1, POOL_GD), lambda g, i: (0, g)),
                   pl.BlockSpec((None, POOL_GD, POOL_GD), lambda g, i: (g, 0, 0))],
        out_shape=[jax.ShapeDtypeStruct((S, W), F32), jax.ShapeDtypeStruct((S, W), ACT),
                   jax.ShapeDtypeStruct((1, W), F32), jax.ShapeDtypeStruct((G, POOL_GD, POOL_GD), F32)],
        scratch_shapes=[pltpu.VMEM((S + 2 * POOL_PAD, POOL_GD), F32), pltpu.VMEM((T, POOL_GD), F32),
                        pltpu.VMEM((T, 128), F32)],
        name="pool_bwd_a", compiler_params=_params("parallel", "arbitrary"))(z, z, pw, scale, y_raw, du)


def _pool_bwd_b(dpc):
    S, W = dpc.shape
    G = len(POOL_WINDOWS)
    T = min(512, S)

    def body(x_ref, o_ref, pad, acc_scr):
        grp = pl.program_id(0)
        i = pl.program_id(1)

        @pl.when(i == 0)
        def _():
            _fill_padded(pad, x_ref, S)

        r0 = pl.multiple_of(i * T, T)
        ext = pad[pl.ds(r0, T + 2 * POOL_PAD), :]
        pos = r0 + lax.broadcasted_iota(jnp.int32, (T, 1), 0)
        for gi, w in enumerate(POOL_WINDOWS):
            @pl.when(grp == gi)
            def _(w=w):
                acc = _window_sum(ext, T, -(w // 2) + 1, w // 2 + 1)
                acc_scr[...] = acc - ext[POOL_PAD:POOL_PAD + T] * _window_count(pos, w, S)

        o_ref[...] = acc_scr[...].astype(ACT)

    blk = pl.BlockSpec((T, POOL_GD), lambda g, i: (i, g))
    return pl.pallas_call(
        body, grid=(G, S // T),
        in_specs=[pl.BlockSpec((S, POOL_GD), lambda g, i: (0, g))], out_specs=blk,
        out_shape=jax.ShapeDtypeStruct((S, W), ACT),
        scratch_shapes=[pltpu.VMEM((S + 2 * POOL_PAD, POOL_GD), F32), pltpu.VMEM((T, POOL_GD), F32)],
        name="pool_bwd_b", compiler_params=_params("parallel", "arbitrary"))(dpc)


def _rope128(x, cf, sa, sb):
    return x * cf + pltpu.roll(x, ROPE_HALF, 1) * sa + pltpu.roll(x, ATT_HD - ROPE_HALF, 1) * sb


def _rope128_t(g, cf, sa, sb):
    return g * cf + pltpu.roll(g * sa, ATT_HD - ROPE_HALF, 1) + pltpu.roll(g * sb, ROPE_HALF, 1)


def _attn_prep(z, qgain, kgain, cf, sa, sb, aq, ak, av):
    S = z.shape[0]
    T = min(512, S)
    QW, KW = ATT_Q * ATT_HD, ATT_KV * ATT_HD

    def body(q_ref, k_ref, v_ref, qg_ref, kg_ref, cf_ref, sa_ref, sb_ref, qn_ref, kn_ref, vb_ref):
        cfv, sav, sbv = cf_ref[...], sa_ref[...], sb_ref[...]

        def prep(x, gain):
            r = lax.rsqrt(jnp.mean(x * x, axis=-1, keepdims=True) + RMS_EPS)
            return _rope128(x * r * gain, cfv, sav, sbv)

        for hh in range(ATT_Q):
            sl = slice(hh * ATT_HD, (hh + 1) * ATT_HD)
            qn_ref[:, sl] = prep(q_ref[:, sl], qg_ref[...]).astype(ACT)
        for hh in range(ATT_KV):
            sl = slice(hh * ATT_HD, (hh + 1) * ATT_HD)
            kn_ref[:, sl] = prep(k_ref[:, sl], kg_ref[...]).astype(ACT)
        vb_ref[...] = v_ref[...].astype(ACT)

    tab = pl.BlockSpec((T, ATT_HD), lambda i: (i, 0))
    gain = pl.BlockSpec((1, ATT_HD), lambda i: (0, 0))
    return pl.pallas_call(
        body, grid=(S // T,),
        in_specs=[pl.BlockSpec((T, QW), lambda i: (i, aq)), pl.BlockSpec((T, KW), lambda i: (i, ak)),
                  pl.BlockSpec((T, KW), lambda i: (i, av)), gain, gain, tab, tab, tab],
        out_specs=[pl.BlockSpec((T, QW), lambda i: (i, 0)), pl.BlockSpec((T, KW), lambda i: (i, 0)),
                   pl.BlockSpec((T, KW), lambda i: (i, 0))],
        out_shape=[jax.ShapeDtypeStruct((S, QW), ACT), jax.ShapeDtypeStruct((S, KW), ACT),
                   jax.ShapeDtypeStruct((S, KW), ACT)],
        name="attn_prep", compiler_params=_params("parallel"))(z, z, z, qgain, kgain, cf, sa, sb)


def _attn_window(i, S):
    start = jnp.clip(i * ATT_BLK - ATT_BLK, 0, S - ATT_SPAN)
    start = pl.multiple_of(start, ATT_BLK)
    qpos = i * ATT_BLK + lax.broadcasted_iota(jnp.int32, (ATT_BLK, ATT_SPAN), 0)
    kpos = start + lax.broadcasted_iota(jnp.int32, (ATT_BLK, ATT_SPAN), 1)
    return start, jnp.abs(kpos - qpos) <= ATT_WIN


def _attn_probs(q, kw, valid, sink):
    s = _dot(q, kw, NT) * (ATT_HD ** -0.5)
    s = jnp.where(valid, s, NEG_BIG)
    m = jnp.maximum(jnp.max(s, axis=-1, keepdims=True), sink)
    p = jnp.exp(s - m)
    es = jnp.exp(sink - m)
    den = jnp.sum(p, axis=-1, keepdims=True) + es
    return p / den, es / den


def _attn_fwd(qn, kn, vb, z, sink, ag):
    S = qn.shape[0]
    nB = S // ATT_BLK
    assert S >= ATT_SPAN
    QW = ATT_Q * ATT_HD

    def body(q_ref, k_ref, v_ref, g_ref, sink_ref, o_ref, u_ref):
        h = pl.program_id(0)
        i = pl.program_id(1)
        start, valid = _attn_window(i, S)
        kw = k_ref[pl.ds(start, ATT_SPAN), :]
        vw = v_ref[pl.ds(start, ATT_SPAN), :]
        sk = jnp.full((ATT_BLK, 1), sink_ref[h], F32)
        pn, _ = _attn_probs(q_ref[...], kw, valid, sk)
        o = _dot(pn, vw, NN)
        o_ref[...] = o
        g = g_ref[...]
        u_ref[...] = (o * (g * _sigmoid(g))).astype(ACT)

    blk = pl.BlockSpec((ATT_BLK, ATT_HD), lambda h, i: (i, h))
    kv = pl.BlockSpec((S, ATT_HD), lambda h, i: (0, h // ATT_G))
    return pl.pallas_call(
        body, grid=(ATT_Q, nB),
        in_specs=[blk, kv, kv, pl.BlockSpec((ATT_BLK, ATT_HD), lambda h, i: (i, ag + h)), SMEM_SPEC],
        out_specs=[blk, blk],
        out_shape=[jax.ShapeDtypeStruct((S, QW), F32), jax.ShapeDtypeStruct((S, QW), ACT)],
        name="attn_fwd", compiler_params=_params("parallel", "parallel"))(qn, kn, vb, z, sink)


def _attn_bwd(qn, kn, vb, o, du, z, sink, ag):
    S = qn.shape[0]
    nB = S // ATT_BLK
    QW, KW = ATT_Q * ATT_HD, ATT_KV * ATT_HD
    GW = ATT_G * ATT_HD

    def body(q_ref, k_ref, v_ref, o_ref, du_ref, g_ref, sink_ref,
             dq_ref, dk_ref, dv_ref, dg_ref, ds_ref):
        kvh = pl.program_id(0)
        i = pl.program_id(1)

        @pl.when(i == 0)
        def _():
            dk_ref[...] = jnp.zeros_like(dk_ref)
            dv_ref[...] = jnp.zeros_like(dv_ref)
            ds_ref[...] = jnp.zeros_like(ds_ref)

        start, valid = _attn_window(i, S)
        kw = k_ref[pl.ds(start, ATT_SPAN), :]
        vw = v_ref[pl.ds(start, ATT_SPAN), :]
        row = lax.broadcasted_iota(jnp.int32, (8, 128), 0)
        lane = lax.broadcasted_iota(jnp.int32, (8, 128), 1)
        dsink = jnp.zeros((8, 128), F32)
        dk_acc = jnp.zeros((ATT_SPAN, ATT_HD), F32)
        dv_acc = jnp.zeros((ATT_SPAN, ATT_HD), F32)
        for gi in range(ATT_G):
            sl = slice(gi * ATT_HD, (gi + 1) * ATT_HD)
            q = q_ref[:, sl]
            ov = o_ref[:, sl]
            g = g_ref[:, sl]
            duv = du_ref[:, sl]
            sg = _sigmoid(g)
            do = duv * (g * sg)
            dg_ref[:, sl] = (duv * ov * (sg * (1.0 + g * (1.0 - sg)))).astype(ACT)
            sk = jnp.full((ATT_BLK, 1), sink_ref[kvh * ATT_G + gi], F32)
            pn, psink = _attn_probs(q, kw, valid, sk)
            delta = jnp.sum(do * ov, axis=-1, keepdims=True)
            dsc = pn * (_dot(do, vw, NT) - delta) * (ATT_HD ** -0.5)
            dq_ref[:, sl] = _dot(dsc, kw, NN)
            dk_acc = dk_acc + _dot(dsc, q, TN)
            dv_acc = dv_acc + _dot(pn, do, TN)
            dsink = dsink + jnp.where((row == 0) & (lane == gi), -_sum_all(psink * delta), 0.0)
        dk_ref[pl.ds(start, ATT_SPAN), :] += dk_acc
        dv_ref[pl.ds(start, ATT_SPAN), :] += dv_acc
        ds_ref[...] += dsink

    grp = pl.BlockSpec((ATT_BLK, GW), lambda k, i: (i, k))
    kv = pl.BlockSpec((S, ATT_HD), lambda k, i: (0, k))
    return pl.pallas_call(
        body, grid=(ATT_KV, nB),
        in_specs=[grp, kv, kv, grp, grp,
                  pl.BlockSpec((ATT_BLK, GW), lambda k, i: (i, ag // ATT_G + k)), SMEM_SPEC],
        out_specs=[grp, kv, kv, grp, pl.BlockSpec((None, 8, 128), lambda k, i: (k, 0, 0))],
        out_shape=[jax.ShapeDtypeStruct((S, QW), F32), jax.ShapeDtypeStruct((S, KW), F32),
                   jax.ShapeDtypeStruct((S, KW), F32), jax.ShapeDtypeStruct((S, QW), ACT),
                   jax.ShapeDtypeStruct((ATT_KV, 8, 128), F32)],
        name="attn_bwd", compiler_params=_params("parallel", "arbitrary"))(qn, kn, vb, o, du, z, sink)


def _attn_prep_bwd(z, dqn, dkn, dv, qgain, kgain, cf, sa, sb, aq, ak):
    S = z.shape[0]
    T = min(512, S)
    QW, KW = ATT_Q * ATT_HD, ATT_KV * ATT_HD

    def body(q_ref, k_ref, dqn_ref, dkn_ref, dv_ref, qg_ref, kg_ref, cf_ref, sa_ref, sb_ref,
             dq_ref, dk_ref, dvb_ref, dqg_ref, dkg_ref):
        cfv, sav, sbv = cf_ref[...], sa_ref[...], sb_ref[...]

        @pl.when(pl.program_id(0) == 0)
        def _():
            dqg_ref[...] = jnp.zeros_like(dqg_ref)
            dkg_ref[...] = jnp.zeros_like(dkg_ref)

        def back(x, gn, gain):
            r = lax.rsqrt(jnp.mean(x * x, axis=-1, keepdims=True) + RMS_EPS)
            xh = x * r
            dy = _rope128_t(gn, cfv, sav, sbv)
            dxh = dy * gain
            dx = r * (dxh - xh * jnp.mean(dxh * xh, axis=-1, keepdims=True))
            return dx, jnp.sum(dy * xh, axis=0, keepdims=True)

        dqg = jnp.zeros((1, ATT_HD), F32)
        for hh in range(ATT_Q):
            sl = slice(hh * ATT_HD, (hh + 1) * ATT_HD)
            dx, dgn = back(q_ref[:, sl], dqn_ref[:, sl], qg_ref[...])
            dq_ref[:, sl] = dx.astype(ACT)
            dqg = dqg + dgn
        dkg = jnp.zeros((1, ATT_HD), F32)
        for hh in range(ATT_KV):
            sl = slice(hh * ATT_HD, (hh + 1) * ATT_HD)
            dx, dgn = back(k_ref[:, sl], dkn_ref[:, sl], kg_ref[...])
            dk_ref[:, sl] = dx.astype(ACT)
            dkg = dkg + dgn
        dvb_ref[...] = dv_ref[...].astype(ACT)
        dqg_ref[...] += dqg
        dkg_ref[...] += dkg

    tab = pl.BlockSpec((T, ATT_HD), lambda i: (i, 0))
    gain = pl.BlockSpec((1, ATT_HD), lambda i: (0, 0))
    qb = pl.BlockSpec((T, QW), lambda i: (i, 0))
    kb = pl.BlockSpec((T, KW), lambda i: (i, 0))
    return pl.pallas_call(
        body, grid=(S // T,),
        in_specs=[pl.BlockSpec((T, QW), lambda i: (i, aq)), pl.BlockSpec((T, KW), lambda i: (i, ak)),
                  qb, kb, kb, gain, gain, tab, tab, tab],
        out_specs=[qb, kb, kb, gain, gain],
        out_shape=[jax.ShapeDtypeStruct((S, QW), ACT), jax.ShapeDtypeStruct((S, KW), ACT),
                   jax.ShapeDtypeStruct((S, KW), ACT), jax.ShapeDtypeStruct((1, ATT_HD), F32),
                   jax.ShapeDtypeStruct((1, ATT_HD), F32)],
        name="attn_prep_bwd", compiler_params=_params("arbitrary"))(
            z, z, dqn, dkn, dv, qgain, kgain, cf, sa, sb)


def _branch_merge(ua, ub, uc, wr, wp, wa, z, mg):
    S, W = ua.shape
    D = wr.shape[1]
    tm, tn = min(512, S), min(512, D)
    nb = D // tn

    def body(ua_ref, ub_ref, uc_ref, wr_ref, wp_ref, wa_ref, g0_ref, g1_ref, g2_ref,
             ya_ref, yb_ref, yc_ref, m_ref):
        ya = _dot(ua_ref[...], wr_ref[...], NN)
        yb = _dot(ub_ref[...], wp_ref[...], NN)
        yc = _dot(uc_ref[...], wa_ref[...], NN)
        ya_ref[...] = ya.astype(ACT)
        yb_ref[...] = yb.astype(ACT)
        yc_ref[...] = yc.astype(ACT)
        m_ref[...] = (_sigmoid(g0_ref[...]) * ya + _sigmoid(g1_ref[...]) * yb
                      + _sigmoid(g2_ref[...]) * yc).astype(ACT)

    u = pl.BlockSpec((tm, W), lambda i, j: (i, 0))
    w = pl.BlockSpec((W, tn), lambda i, j: (0, j))
    o = pl.BlockSpec((tm, tn), lambda i, j: (i, j))

    assert (mg * POOL_GD) % tn == 0
    base = (mg * POOL_GD) // tn

    def gate(k):
        return pl.BlockSpec((tm, tn), lambda i, j: (i, base + k * nb + j))

    sd = jax.ShapeDtypeStruct((S, D), ACT)
    return pl.pallas_call(
        body, grid=(S // tm, nb), in_specs=[u, u, u, w, w, w, gate(0), gate(1), gate(2)],
        out_specs=[o, o, o, o], out_shape=[sd, sd, sd, sd], name="branch_merge",
        compiler_params=_params("parallel", "parallel"))(ua, ub, uc, wr, wp, wa, z, z, z)


def _merge_bwd(dxb, wo, ya, yb, yc, z, mg):
    S, D = dxb.shape
    tm, tn = min(512, S), min(512, D)
    nb = D // tn
    base = (mg * POOL_GD) // tn

    def body(dx_ref, wo_ref, ya_ref, yb_ref, yc_ref, g0_ref, g1_ref, g2_ref,
             da_ref, db_ref, dc_ref, dg0_ref, dg1_ref, dg2_ref):
        dm = _dot(dx_ref[...], wo_ref[...], NT)
        for y_ref, g_ref, dy_ref, dg_ref in ((ya_ref, g0_ref, da_ref, dg0_ref),
                                             (yb_ref, g1_ref, db_ref, dg1_ref),
                                             (yc_ref, g2_ref, dc_ref, dg2_ref)):
            sg = _sigmoid(g_ref[...])
            dy_ref[...] = (sg * dm).astype(ACT)
            dg_ref[...] = (dm * y_ref[...].astype(F32) * (sg * (1.0 - sg))).astype(ACT)

    o = pl.BlockSpec((tm, tn), lambda i, j: (i, j))

    def gate(k):
        return pl.BlockSpec((tm, tn), lambda i, j: (i, base + k * nb + j))

    sd = jax.ShapeDtypeStruct((S, D), ACT)
    return pl.pallas_call(
        body, grid=(S // tm, nb),
        in_specs=[pl.BlockSpec((tm, D), lambda i, j: (i, 0)), pl.BlockSpec((tn, D), lambda i, j: (j, 0)),
                  o, o, o, gate(0), gate(1), gate(2)],
        out_specs=[o] * 6, out_shape=[sd] * 6, name="merge_bwd",
        compiler_params=_params("parallel", "parallel"))(dxb, wo, ya, yb, yc, z, z, z)


def _loss_head(y, t):
    S, D = y.shape
    tm = min(256, S)

    def body(y_ref, t_ref, dy_ref, dyb_ref, l_ref):
        e = y_ref[...] - t_ref[...]
        dy = e * (1.0 / D)
        dy_ref[...] = dy
        dyb_ref[...] = dy.astype(ACT)

        @pl.when(pl.program_id(0) == 0)
        def _():
            l_ref[...] = jnp.zeros_like(l_ref)

        l_ref[...] += jnp.sum(jnp.mean(e * e, axis=-1, keepdims=True), axis=0, keepdims=True)

    row = pl.BlockSpec((tm, D), lambda i: (i, 0))
    return pl.pallas_call(
        body, grid=(S // tm,), in_specs=[row, row],
        out_specs=[row, row, pl.BlockSpec((1, 1), lambda i: (0, 0))],
        out_shape=[jax.ShapeDtypeStruct((S, D), F32), jax.ShapeDtypeStruct((S, D), ACT),
                   jax.ShapeDtypeStruct((1, 1), F32)],
        name="loss_head", compiler_params=_params("arbitrary"))(y, t)


def _adamw(w, parts, m, v, name):
    shape = w.shape
    C = shape[-1]
    R = int(np.prod(shape[:-1]))
    w2, m2, v2 = (a.reshape(R, C) for a in (w, m, v))
    p2 = parts.reshape(N_DEV, R, C)
    tr = min(64, R)
    assert R % tr == 0
    c1 = 1.0 / (1.0 - ADAM_B1 ** ADAM_STEP)
    c2 = 1.0 / (1.0 - ADAM_B2 ** ADAM_STEP)

    def body(w_ref, p_ref, m_ref, v_ref, g_ref, d_ref, nm_ref, nv_ref):
        g = p_ref[0]
        for k in range(1, N_DEV):
            g = g + p_ref[k]
        nm = ADAM_B1 * m_ref[...] + (1.0 - ADAM_B1) * g
        nv = ADAM_B2 * v_ref[...] + (1.0 - ADAM_B2) * (g * g)
        g_ref[...] = g
        nm_ref[...] = nm
        nv_ref[...] = nv
        d_ref[...] = -ADAM_LR * ((nm * c1) / (jnp.sqrt(nv * c2) + ADAM_EPS) + ADAM_WD * w_ref[...])

    blk = pl.BlockSpec((tr, C), lambda i: (i, 0))
    sd = jax.ShapeDtypeStruct((R, C), F32)
    outs = pl.pallas_call(
        body, grid=(R // tr,),
        in_specs=[blk, pl.BlockSpec((N_DEV, tr, C), lambda i: (0, i, 0)), blk, blk],
        out_specs=[blk] * 4, out_shape=[sd] * 4, name=name,
        compiler_params=_params("parallel"))(w2, p2, m2, v2)
    return [a.reshape(shape) for a in outs]


def _exchange(arrs, scatter, name):
    n = len(arrs)
    out_shape = [jax.ShapeDtypeStruct(a.shape if scatter else (N_DEV,) + a.shape, a.dtype) for a in arrs]

    def body(*refs):
        ins, outs = refs[:n], refs[n:2 * n]
        send_sems, recv_sems, local_sems = refs[2 * n:]
        x, y, c = lax.axis_index("x"), lax.axis_index("y"), lax.axis_index("c")
        me = 4 * x + 2 * y + c
        copies = []
        for a in range(n):
            src = ins[a].at[me] if scatter else ins[a]
            own = pltpu.make_async_copy(src, outs[a].at[me], local_sems.at[a])
            own.start()
            copies.append(own)
        sends, recvs = [], []
        for k in range(1, N_DEV):
            px, py, pc = x ^ (k >> 2), y ^ ((k >> 1) & 1), c ^ (k & 1)
            peer = 4 * px + 2 * py + pc
            for a in range(n):
                src = ins[a].at[peer] if scatter else ins[a]
                cp = pltpu.make_async_remote_copy(
                    src_ref=src, dst_ref=outs[a].at[me],
                    send_sem=send_sems.at[a, k - 1], recv_sem=recv_sems.at[a, k - 1],
                    device_id=(px, py, pc), device_id_type=pl.DeviceIdType.MESH)
                cp.start()
                sends.append(cp)
                recvs.append(pltpu.make_async_remote_copy(
                    src_ref=src, dst_ref=outs[a].at[peer],
                    send_sem=send_sems.at[a, k - 1], recv_sem=recv_sems.at[a, k - 1],
                    device_id=(px, py, pc), device_id_type=pl.DeviceIdType.MESH))
        for cp in recvs:
            cp.wait_recv()
        for cp in sends:
            cp.wait_send()
        for cp in copies:
            cp.wait()

    any_spec = pl.BlockSpec(memory_space=pl.ANY)
    return pl.pallas_call(
        body, in_specs=[any_spec] * n, out_specs=[any_spec] * n, out_shape=out_shape,
        scratch_shapes=[pltpu.SemaphoreType.DMA((n, N_DEV - 1)), pltpu.SemaphoreType.DMA((n, N_DEV - 1)),
                        pltpu.SemaphoreType.DMA((n,))],
        name=name)(*arrs)


class _Cols:
    def __init__(self, D):
        Wb = D // 2
        sizes = (Wb, Wb, Wb, Wb, Wb, Wb, Wb, ATT_KV * ATT_HD, ATT_KV * ATT_HD, Wb, 3 * D)
        offs = np.concatenate([[0], np.cumsum(sizes)])
        assert all(int(o) % 256 == 0 for o in offs)
        (self.rq, self.rk, self.rv, self.rg, self.pv, self.pg,
         self.aq, self.ak, self.av, self.ag, self.mg) = (int(o) // 256 for o in offs[:-1])
        self.width = int(offs[-1])
        self.sizes = sizes


def _rope_tables(S):
    pos = jnp.arange(S, dtype=F32)[:, None]
    inv_r = 1.0 / (RET_ROPE_BASE ** jnp.linspace(0.0, 1.0, RET_HD // 2, dtype=F32))
    ang_r = pos * inv_r[None, :]
    inv_a = ROPE_THETA ** (-jnp.arange(ROPE_HALF, dtype=F32) / ROPE_HALF)
    ang_a = pos * inv_a[None, :]
    ca, sa = jnp.cos(ang_a), jnp.sin(ang_a)
    z16 = jnp.zeros((S, ROPE_HALF), F32)
    rest = ATT_HD - 2 * ROPE_HALF
    cf = jnp.concatenate([ca, ca, jnp.ones((S, rest), F32)], axis=1)
    s_up = jnp.concatenate([z16, sa, jnp.zeros((S, rest), F32)], axis=1)
    s_dn = jnp.concatenate([-sa, z16, jnp.zeros((S, rest), F32)], axis=1)
    return jnp.cos(ang_r), jnp.sin(ang_r), cf, s_up, s_dn


def _layer_fwd(x, p, tabs, cols):
    cos_r, sin_r, cf, s_up, s_dn = tabs
    S, D = x.shape
    h = _rmsnorm_fwd(x, p["norm_g"])
    z = _matmul(h, p["w_in"], "nn", F32, 1024, 512, D, "in_proj")
    rcols = (cols.rq, cols.rk, cols.rv, cols.rg)
    sf, sb = _ret_state(z, cols.rk, RET_HD ** -0.5, z, cols.rv, cos_r, sin_r, p["af"], p["ab"], "fwd",
                        "ret_state_fwd")
    o_ret, ua = _ret_fwd(z, cos_r, sin_r, sf, sb, p["af"], p["ab"], rcols)
    y_pool, ub = _pool_fwd(z, p["pool_w"], p["pool_scale"], cols.pv, cols.pg)
    qn, kn, vb = _attn_prep(z, p["q_gain"], p["k_gain"], cf, s_up, s_dn,
                            cols.aq // 4, cols.ak, cols.av)
    o_att, uc = _attn_fwd(qn, kn, vb, z, p["sink"], cols.ag * 2)
    ya, yb, yc, merged = _branch_merge(ua, ub, uc, p["w_ret"], p["w_pool"], p["w_att"], z, cols.mg)
    out = _matmul(merged, p["w_out"], "nn", F32, 1024, 512, D, "out_proj", res=x)
    saved = dict(x=x, h=h, z=z, sf=sf, sb=sb, o_ret=o_ret, ua=ua, y_pool=y_pool, ub=ub,
                 qn=qn, kn=kn, vb=vb, o_att=o_att, uc=uc, ya=ya, yb=yb, yc=yc, merged=merged)
    return out, saved


def _layer_bwd(dx, dxb, p, sv, tabs, cols):
    cos_r, sin_r, cf, s_up, s_dn = tabs
    z = sv["z"]
    S, D = dx.shape
    Wb = D // 2
    dya, dyb, dyc, dmg0, dmg1, dmg2 = _merge_bwd(dxb, p["w_out"], sv["ya"], sv["yb"], sv["yc"], z, cols.mg)
    g_w_out = _matmul(sv["merged"], dxb, "tn", F32, 512, 512, 1024, "dw_out")
    grads = {"w_out": g_w_out}
    dus = {}
    for nm, u, dy in (("ret", sv["ua"], dya), ("pool", sv["ub"], dyb), ("att", sv["uc"], dyc)):
        dus[nm] = _matmul(dy, p["w_" + nm], "nt", F32, 1024, 512, D, "du_" + nm)
        grads["w_" + nm] = _matmul(u, dy, "tn", F32, 512, 512, 1024, "dw_" + nm)
    rcols = (cols.rq, cols.rk, cols.rv, cols.rg)
    do_ret, d_rg = _ret_gate_bwd(dus["ret"], sv["o_ret"], z, cols.rg)
    eb, ef = _ret_state(z, cols.rq, 1.0, do_ret, 0, cos_r, sin_r, p["af"], p["ab"], "bwd", "ret_state_bwd")
    d_rq, d_rk, d_rv, d_decay = _ret_bwd(z, do_ret, cos_r, sin_r, sv["sf"], sv["sb"], ef, eb,
                                         p["af"], p["ab"], rcols)
    dpc, d_pg, d_pscale, g_pool_w = _pool_bwd_a(z, p["pool_w"], p["pool_scale"], sv["y_pool"], dus["pool"],
                                                cols.pv, cols.pg)
    d_pv = _pool_bwd_b(dpc)
    grads["pool_w"] = g_pool_w
    dqn, dkn, dvv, d_ag, d_sink = _attn_bwd(sv["qn"], sv["kn"], sv["vb"], sv["o_att"], dus["att"], z,
                                            p["sink"], cols.ag * 2)
    d_aq, d_ak, d_av, d_qg, d_kg = _attn_prep_bwd(z, dqn, dkn, dvv, p["q_gain"], p["k_gain"], cf, s_up, s_dn,
                                                  cols.aq // 4, cols.ak)
    dz = jnp.concatenate([d_rq, d_rk, d_rv, d_rg, d_pv, d_pg, d_aq, d_ak, d_av, d_ag, dmg0, dmg1, dmg2],
                         axis=1)
    grads["w_in"] = _matmul(sv["h"], dz, "tn", F32, 512, 512, 1024, "dw_in")
    dh = _matmul(dz, p["w_in"], "nt", F32, 1024, 512, 512, "dh")
    dx_in, dxb_in, d_norm_g = _rmsnorm_bwd(sv["x"], p["norm_g"], dh, dx)
    misc = jnp.concatenate([d_decay[:, 0, 0], d_decay[:, 0, 1], d_sink[:, 0, :ATT_G].reshape(-1)])
    misc = jnp.pad(misc, (0, 128 - misc.shape[0]))[None, :]
    small = jnp.concatenate([d_norm_g, d_pscale, d_qg, d_kg, misc], axis=1)
    return dx_in, dxb_in, grads, small


def _pack_small(norm_g, pool_scale, q_gain, k_gain, af, ab, sink):
    L = norm_g.shape[0]
    misc = jnp.concatenate([af, ab, sink], axis=1)
    misc = jnp.pad(misc, ((0, 0), (0, 128 - misc.shape[1])))
    return jnp.concatenate([norm_g, pool_scale, q_gain, k_gain, misc], axis=1)


def _unpack_small(a, D):
    Wb = D // 2
    o = np.cumsum([0, D, Wb, ATT_HD, ATT_HD])
    misc = a[:, o[4]:]
    return (a[:, o[0]:o[1]], a[:, o[1]:o[2]], a[:, o[2]:o[3]], a[:, o[3]:o[4]],
            misc[:, :RET_HEADS], misc[:, RET_HEADS:2 * RET_HEADS],
            misc[:, 2 * RET_HEADS:2 * RET_HEADS + ATT_Q])


def _local_step(x, t, layers, tabs, cols):
    saved = []
    for p in layers:
        x, sv = _layer_fwd(x, p, tabs, cols)
        saved.append(sv)
    dx, dxb, lsum = _loss_head(x, t)
    grads, smalls = [], []
    for p, sv in zip(reversed(layers), reversed(saved)):
        dx, dxb, g, sm = _layer_bwd(dx, dxb, p, sv, tabs, cols)
        grads.append(g)
        smalls.append(sm)
    return 0.5 * lsum[0, 0], dx, grads[::-1], jnp.concatenate(smalls[::-1], axis=0)


def kernel(x, norm_g, w_in, ret_decay_fwd, ret_decay_bwd, pool_w, pool_scale, attn_q_gain, attn_k_gain, attn_sink, w_ret, w_pool, w_att, w_out, loss_target, m_norm_g, m_w_in, m_ret_decay_fwd, m_ret_decay_bwd, m_pool_w, m_pool_scale, m_attn_q_gain, m_attn_k_gain, m_attn_sink, m_w_ret, m_w_pool, m_w_att, m_w_out, v_norm_g, v_w_in, v_ret_decay_fwd, v_ret_decay_bwd, v_pool_w, v_pool_scale, v_attn_q_gain, v_attn_k_gain, v_attn_sink, v_w_ret, v_w_pool, v_w_att, v_w_out):
    L = norm_g.shape[0]
    _, S, D = x.shape
    Wb = D // 2
    cols = _Cols(D)
    NZ = cols.width
    tabs = _rope_tables(S)

    g_in, g_ret, g_pool, g_att, g_out, g_pw = _exchange(
        [w_in.astype(MXU), w_ret.astype(MXU), w_pool.astype(MXU), w_att.astype(MXU),
         w_out.astype(MXU), pool_w.astype(MXU)], False, "gather_weights")

    def cols_full(g, l, rows):
        return jnp.transpose(g[:, l], (1, 0, 2)).reshape(rows, -1)

    layers = []
    for l in range(L):
        layers.append(dict(
            norm_g=norm_g[l][None, :], w_in=cols_full(g_in, l, D),
            w_ret=cols_full(g_ret, l, Wb), w_pool=cols_full(g_pool, l, Wb), w_att=cols_full(g_att, l, Wb),
            w_out=g_out[:, l].reshape(D, D),
            pool_w=jnp.transpose(g_pw[:, l], (1, 0, 2, 3)).reshape(len(POOL_WINDOWS), POOL_GD, POOL_GD),
            pool_scale=pool_scale[l][None, :], q_gain=attn_q_gain[l][None, :], k_gain=attn_k_gain[l][None, :],
            af=ret_decay_fwd[l], ab=ret_decay_bwd[l], sink=attn_sink[l]))

    loss_local, grad_x, grads, small = _local_step(x[0], loss_target[0], layers, tabs, cols)
    loss = lax.psum(loss_local, ("x", "y", "c"))

    def col_slots(g, rows):
        return jnp.transpose(g.reshape(rows, N_DEV, -1), (1, 0, 2))

    parts = {k: [] for k in ("w_in", "w_ret", "w_pool", "w_att", "w_out", "pool_w")}
    for l in range(L):
        g = grads[l]
        outs = _exchange(
            [col_slots(g["w_in"], D), col_slots(g["w_ret"], Wb), col_slots(g["w_pool"], Wb),
             col_slots(g["w_att"], Wb), g["w_out"].reshape(N_DEV, D // N_DEV, D),
             jnp.transpose(g["pool_w"].reshape(len(POOL_WINDOWS), N_DEV, POOL_GD // N_DEV, POOL_GD),
                           (1, 0, 2, 3))],
            True, "scatter_grads")
        for k, o in zip(parts, outs):
            parts[k].append(o)
    small_all, = _exchange([small], False, "gather_small_grads")

    res = {}
    shard = dict(w_in=(w_in, m_w_in, v_w_in), w_ret=(w_ret, m_w_ret, v_w_ret),
                 w_pool=(w_pool, m_w_pool, v_w_pool), w_att=(w_att, m_w_att, v_w_att),
                 w_out=(w_out, m_w_out, v_w_out), pool_w=(pool_w, m_pool_w, v_pool_w))
    for k, (w, m, v) in shard.items():
        per_layer = [_adamw(w[l], parts[k][l], m[l], v[l], "adamw_" + k) for l in range(L)]
        res[k] = [jnp.stack([pl_[i] for pl_ in per_layer]) for i in range(4)]
    sw = _pack_small(norm_g, pool_scale, attn_q_gain, attn_k_gain, ret_decay_fwd, ret_decay_bwd, attn_sink)
    sm = _pack_small(m_norm_g, m_pool_scale, m_attn_q_gain, m_attn_k_gain, m_ret_decay_fwd, m_ret_decay_bwd,
                     m_attn_sink)
    sv_ = _pack_small(v_norm_g, v_pool_scale, v_attn_q_gain, v_attn_k_gain, v_ret_decay_fwd, v_ret_decay_bwd,
                      v_attn_sink)
    small_res = [_unpack_small(a, D) for a in _adamw(sw, small_all, sm, sv_, "adamw_small")]

    def ordered(i):
        ng, ps, qg, kg, af, ab, sk = small_res[i]
        return (ng, res["w_in"][i], af, ab, res["pool_w"][i], ps, qg, kg, sk,
                res["w_ret"][i], res["w_pool"][i], res["w_att"][i], res["w_out"][i])

    return (loss, grad_x[None], *ordered(0), *ordered(1), *ordered(2), *ordered(3))
```

```python
import functools

import numpy as np
import jax
import jax.numpy as jnp
from jax import lax
from jax.experimental import pallas as pl
from jax.experimental.pallas import tpu as pltpu

F32 = jnp.float32
MXU = jnp.bfloat16
ACT = jnp.bfloat16

N_DEV = 8
RMS_EPS = 1e-6
NEG_BIG = -1e30
RET_HEADS = 4
RET_HD = 256
CH = 128
RET_ROPE_BASE = 10000.0
POOL_WINDOWS = (2, 4, 8, 16)
POOL_GD = 256
POOL_PAD = 8
ATT_HD = 128
ATT_Q = 8
ATT_KV = 2
ATT_G = ATT_Q // ATT_KV
ATT_WIN = 128
ATT_BLK = 128
ATT_SPAN = 3 * ATT_BLK
ROPE_THETA = 500000.0
ROPE_HALF = 16

ADAM_LR = 0.001
ADAM_B1 = 0.9
ADAM_B2 = 0.999
ADAM_EPS = 1e-08
ADAM_WD = 0.01
ADAM_STEP = 10

VMEM_LIMIT = 48 * 1024 * 1024

NN = ((1,), (0,))
NT = ((1,), (1,))
TN = ((0,), (0,))


def _dot(a, b, dims):
    return lax.dot_general(a.astype(MXU), b.astype(MXU), (dims, ((), ())),
                           preferred_element_type=F32)


def _sigmoid(x):
    return 1.0 / (1.0 + jnp.exp(-x))


def _params(*sem):
    return pltpu.CompilerParams(dimension_semantics=sem, vmem_limit_bytes=VMEM_LIMIT)


def _sum_all(x):
    return jnp.sum(jnp.sum(x, axis=1, keepdims=True), axis=0, keepdims=True)


def _fiota(shape, dim):
    return lax.broadcasted_iota(jnp.int32, shape, dim).astype(F32)


SMEM_SPEC = pl.BlockSpec(memory_space=pltpu.SMEM)


def _matmul(a, b, mode, out_dtype, tm, tn, tk, name, res=None):
    if mode == "tn":
        K, M = a.shape
    else:
        M, K = a.shape
    N = b.shape[0] if mode == "nt" else b.shape[1]
    tm, tn, tk = min(tm, M), min(tn, N), min(tk, K)
    assert M % tm == 0 and N % tn == 0 and K % tk == 0, (name, M, N, K, tm, tn, tk)
    nk = K // tk
    dims = {"nn": NN, "nt": NT, "tn": TN}[mode]
    a_spec = (pl.BlockSpec((tk, tm), lambda i, j, k: (k, i)) if mode == "tn"
              else pl.BlockSpec((tm, tk), lambda i, j, k: (i, k)))
    b_spec = (pl.BlockSpec((tn, tk), lambda i, j, k: (j, k)) if mode == "nt"
              else pl.BlockSpec((tk, tn), lambda i, j, k: (k, j)))
    o_spec = pl.BlockSpec((tm, tn), lambda i, j, k: (i, j))
    has_res = res is not None

    def body(*refs):
        if has_res:
            a_ref, b_ref, r_ref, o_ref, acc = refs
        else:
            a_ref, b_ref, o_ref, acc = refs
        k = pl.program_id(2)

        @pl.when(k == 0)
        def _():
            acc[...] = jnp.zeros_like(acc)

        acc[...] += _dot(a_ref[...], b_ref[...], dims)

        @pl.when(k == nk - 1)
        def _():
            out = acc[...]
            if has_res:
                out = out + r_ref[...]
            o_ref[...] = out.astype(out_dtype)

    ins = [a, b] + ([res] if has_res else [])
    in_specs = [a_spec, b_spec] + ([o_spec] if has_res else [])
    return pl.pallas_call(
        body, grid=(M // tm, N // tn, nk), in_specs=in_specs, out_specs=o_spec,
        out_shape=jax.ShapeDtypeStruct((M, N), out_dtype),
        scratch_shapes=[pltpu.VMEM((tm, tn), F32)], name=name,
        compiler_params=_params("parallel", "parallel", "arbitrary"))(*ins)


DEP_SPEC1 = pl.BlockSpec((8, 128), lambda i: (0, 0))
DEP_SPEC2 = pl.BlockSpec((8, 128), lambda i, j: (0, 0))


def _rmsnorm_fwd(x, g, deps=()):
    S, D = x.shape
    tm = min(512, S)

    def body(x_ref, g_ref, *rest):
        h_ref = rest[-1]
        xv = x_ref[...]
        r = lax.rsqrt(jnp.mean(xv * xv, axis=-1, keepdims=True) + RMS_EPS)
        h_ref[...] = (xv * r * g_ref[...]).astype(ACT)

    row = pl.BlockSpec((tm, D), lambda i: (i, 0))
    return pl.pallas_call(
        body, grid=(S // tm,), in_specs=[row, pl.BlockSpec((1, D), lambda i: (0, 0))] + [DEP_SPEC1] * len(deps),
        out_specs=row, out_shape=jax.ShapeDtypeStruct((S, D), ACT), name="rmsnorm_fwd",
        compiler_params=_params("parallel"))(x, g, *deps)


def _rmsnorm_bwd(x, g, dh, dres):
    S, D = x.shape
    tm = min(256, S)

    def body(x_ref, g_ref, dh_ref, dr_ref, dx_ref, dxb_ref, dg_ref):
        xv = x_ref[...]
        r = lax.rsqrt(jnp.mean(xv * xv, axis=-1, keepdims=True) + RMS_EPS)
        xh = xv * r
        dhv = dh_ref[...]
        dxh = dhv * g_ref[...]
        dx = r * (dxh - xh * jnp.mean(dxh * xh, axis=-1, keepdims=True)) + dr_ref[...]
        dx_ref[...] = dx
        dxb_ref[...] = dx.astype(ACT)

        @pl.when(pl.program_id(0) == 0)
        def _():
            dg_ref[...] = jnp.zeros_like(dg_ref)

        dg_ref[...] += jnp.sum(dhv * xh, axis=0, keepdims=True)

    row = pl.BlockSpec((tm, D), lambda i: (i, 0))
    vec = pl.BlockSpec((1, D), lambda i: (0, 0))
    return pl.pallas_call(
        body, grid=(S // tm,), in_specs=[row, vec, row, row], out_specs=[row, row, vec],
        out_shape=[jax.ShapeDtypeStruct((S, D), F32), jax.ShapeDtypeStruct((S, D), ACT),
                   jax.ShapeDtypeStruct((1, D), F32)],
        name="rmsnorm_bwd", compiler_params=_params("arbitrary"))(x, g, dh, dres)


def _rot256(x, c, s):
    x1, x2 = x[:, :128], x[:, 128:]
    return jnp.concatenate([x1 * c - x2 * s, x2 * c + x1 * s], axis=1)


def _rot256_t(g, c, s):
    g1, g2 = g[:, :128], g[:, 128:]
    return jnp.concatenate([g1 * c + g2 * s, g2 * c - g1 * s], axis=1)


def _log_decay(a_ref, h, shape):
    return -jnp.exp(jnp.full(shape, a_ref[h], F32))


def _ret_state(xsrc, xbase, xscale, ysrc, ybase, cos, sin, af, ab, mode, name):
    S = xsrc.shape[0]
    nC = S // CH
    H = RET_HEADS

    def body(x1_ref, y1_ref, c1_ref, s1_ref, x2_ref, y2_ref, c2_ref, s2_ref, af_ref, ab_ref,
             st1_ref, st2_ref, acc1, acc2):
        h = pl.program_id(0)

        @pl.when(pl.program_id(1) == 0)
        def _():
            acc1[...] = jnp.zeros_like(acc1)
            acc2[...] = jnp.zeros_like(acc2)

        lgf = _log_decay(af_ref, h, (CH, 1))
        lgb = _log_decay(ab_ref, h, (CH, 1))
        j = _fiota((CH, 1), 0)
        if mode == "fwd":
            w1, d1 = jnp.exp(lgf * (CH - 1.0 - j)), jnp.exp(lgf[:1] * CH)
            w2, d2 = jnp.exp(lgb * j), jnp.exp(lgb[:1] * CH)
        else:
            w1, d1 = jnp.exp(lgb * (CH - j)), jnp.exp(lgb[:1] * CH)
            w2, d2 = jnp.exp(lgf * (j + 1.0)), jnp.exp(lgf[:1] * CH)
        xa = _rot256(x1_ref[...], c1_ref[...], s1_ref[...]) * xscale
        st1_ref[...] = acc1[...]
        acc1[...] = d1 * acc1[...] + _dot(xa * w1, y1_ref[...], TN)
        xb = _rot256(x2_ref[...], c2_ref[...], s2_ref[...]) * xscale
        st2_ref[...] = acc2[...]
        acc2[...] = d2 * acc2[...] + _dot(xb * w2, y2_ref[...], TN)

    def fw(col):
        return lambda h, c: (c, col + h)

    def rv(col):
        return lambda h, c: (nC - 1 - c, col + h)

    in_specs = [
        pl.BlockSpec((CH, RET_HD), fw(xbase)), pl.BlockSpec((CH, RET_HD), fw(ybase)),
        pl.BlockSpec((CH, 128), lambda h, c: (c, 0)), pl.BlockSpec((CH, 128), lambda h, c: (c, 0)),
        pl.BlockSpec((CH, RET_HD), rv(xbase)), pl.BlockSpec((CH, RET_HD), rv(ybase)),
        pl.BlockSpec((CH, 128), lambda h, c: (nC - 1 - c, 0)),
        pl.BlockSpec((CH, 128), lambda h, c: (nC - 1 - c, 0)),
        SMEM_SPEC, SMEM_SPEC]
    out_specs = [pl.BlockSpec((None, None, RET_HD, RET_HD), lambda h, c: (h, c, 0, 0)),
                 pl.BlockSpec((None, None, RET_HD, RET_HD), lambda h, c: (h, nC - 1 - c, 0, 0))]
    st = jax.ShapeDtypeStruct((H, nC, RET_HD, RET_HD), F32)
    return pl.pallas_call(
        body, grid=(H, nC), in_specs=in_specs, out_specs=out_specs, out_shape=[st, st],
        scratch_shapes=[pltpu.VMEM((RET_HD, RET_HD), F32), pltpu.VMEM((RET_HD, RET_HD), F32)],
        name=name, compiler_params=_params("parallel", "arbitrary"))(
            xsrc, ysrc, cos, sin, xsrc, ysrc, cos, sin, af, ab)


def _decay_mask(lgf1, lgb1):
    lag = _fiota((CH, CH), 0) - _fiota((CH, CH), 1)
    alag = jnp.abs(lag)
    return lag, jnp.where(lag >= 0, jnp.exp(lgf1 * alag), jnp.exp(lgb1 * alag))


def _ret_fwd(z, cos, sin, sf, sb, af, ab, cols):
    S = z.shape[0]
    nC = S // CH
    H = RET_HEADS
    rq, rk, rv, rg = cols
    W = H * RET_HD

    def body(q_ref, k_ref, v_ref, g_ref, c_ref, s_ref, sf_ref, sb_ref, af_ref, ab_ref,
             o_ref, u_ref):
        h = pl.program_id(0)
        lgf = _log_decay(af_ref, h, (CH, 1))
        lgb = _log_decay(ab_ref, h, (CH, 1))
        j = _fiota((CH, 1), 0)
        c, s = c_ref[...], s_ref[...]
        q = _rot256(q_ref[...], c, s)
        k = _rot256(k_ref[...], c, s) * (RET_HD ** -0.5)
        _, dm = _decay_mask(lgf[:1], lgb[:1])
        p = _dot(q, k, NT) * dm
        o = (_dot(p, v_ref[...], NN)
             + _dot(q * jnp.exp(lgf * (j + 1.0)), sf_ref[...], NN)
             + _dot(q * jnp.exp(lgb * (CH - j)), sb_ref[...], NN))
        o_ref[...] = o
        on = o * lax.rsqrt(jnp.mean(o * o, axis=-1, keepdims=True) + RMS_EPS)
        g = g_ref[...]
        u_ref[...] = (on * (g * _sigmoid(g))).astype(ACT)

    def zc(col):
        return pl.BlockSpec((CH, RET_HD), lambda h, c: (c, col + h))

    tab = pl.BlockSpec((CH, 128), lambda h, c: (c, 0))
    stt = pl.BlockSpec((None, None, RET_HD, RET_HD), lambda h, c: (h, c, 0, 0))
    out = pl.BlockSpec((CH, RET_HD), lambda h, c: (c, h))
    return pl.pallas_call(
        body, grid=(H, nC),
        in_specs=[zc(rq), zc(rk), zc(rv), zc(rg), tab, tab, stt, stt, SMEM_SPEC, SMEM_SPEC],
        out_specs=[out, out],
        out_shape=[jax.ShapeDtypeStruct((S, W), F32), jax.ShapeDtypeStruct((S, W), ACT)],
        name="ret_fwd", compiler_params=_params("parallel", "parallel"))(
            z, z, z, z, cos, sin, sf, sb, af, ab)


def _ret_gate_bwd(du, o_pre, z, rg, deps=()):
    S, W = du.shape
    H = RET_HEADS
    tm = min(512, S)

    def body(du_ref, o_ref, g_ref, *rest):
        do_ref, dg_ref = rest[-2:]
        o = o_ref[...]
        r = lax.rsqrt(jnp.mean(o * o, axis=-1, keepdims=True) + RMS_EPS)
        on = o * r
        g = g_ref[...]
        sg = _sigmoid(g)
        duv = du_ref[...]
        don = duv * (g * sg)
        dg_ref[...] = (duv * on * (sg * (1.0 + g * (1.0 - sg)))).astype(ACT)
        do_ref[...] = r * (don - on * jnp.mean(don * on, axis=-1, keepdims=True))

    blk = pl.BlockSpec((tm, RET_HD), lambda i, h: (i, h))
    return pl.pallas_call(
        body, grid=(S // tm, H),
        in_specs=[blk, blk, pl.BlockSpec((tm, RET_HD), lambda i, h: (i, rg + h))] + [DEP_SPEC2] * len(deps),
        out_specs=[blk, blk],
        out_shape=[jax.ShapeDtypeStruct((S, W), F32), jax.ShapeDtypeStruct((S, W), ACT)],
        name="ret_gate_bwd", compiler_params=_params("parallel", "parallel"))(du, o_pre, z, *deps)


def _ret_bwd(z, do, cos, sin, sf, sb, ef, eb, af, ab, cols):
    S = z.shape[0]
    nC = S // CH
    H = RET_HEADS
    rq, rk, rv, _ = cols
    W = H * RET_HD

    def body(q_ref, k_ref, v_ref, do_ref, c_ref, s_ref, sf_ref, sb_ref, ef_ref, eb_ref,
             af_ref, ab_ref, dq_ref, dk_ref, dv_ref, da_ref):
        h = pl.program_id(0)
        lgf = _log_decay(af_ref, h, (CH, 1))
        lgb = _log_decay(ab_ref, h, (CH, 1))
        j = _fiota((CH, 1), 0)
        c, s = c_ref[...], s_ref[...]
        scale = RET_HD ** -0.5
        q = _rot256(q_ref[...], c, s)
        k = _rot256(k_ref[...], c, s) * scale
        v = v_ref[...]
        do = do_ref[...]
        sf_, sb_, ef_, eb_ = sf_ref[...], sb_ref[...], ef_ref[...], eb_ref[...]
        a_w = jnp.exp(lgf * (j + 1.0))
        b_w = jnp.exp(lgb * (CH - j))
        wf = jnp.exp(lgf * (CH - 1.0 - j))
        wb = jnp.exp(lgb * j)
        lag, dm = _decay_mask(lgf[:1], lgb[:1])
        sc = _dot(q, k, NT)
        gg = _dot(do, v, NT)
        dg = gg * dm
        x1 = _dot(do, sf_, NT) * a_w
        x2 = _dot(do, sb_, NT) * b_w
        y1 = _dot(v, ef_, NT) * wf
        y2 = _dot(v, eb_, NT) * wb
        dq = _dot(dg, k, NN) + x1 + x2
        dk = _dot(dg, q, TN) + y1 + y2
        dv = _dot(sc * dm, do, TN) + _dot(k * wf, ef_, NN) + _dot(k * wb, eb_, NN)
        dq_ref[...] = _rot256_t(dq, c, s).astype(ACT)
        dk_ref[...] = (_rot256_t(dk, c, s) * scale).astype(ACT)
        dv_ref[...] = dv.astype(ACT)
        t = dm * gg * sc
        qx1 = jnp.sum(q * x1, axis=-1, keepdims=True)
        qx2 = jnp.sum(q * x2, axis=-1, keepdims=True)
        ky1 = jnp.sum(k * y1, axis=-1, keepdims=True)
        ky2 = jnp.sum(k * y2, axis=-1, keepdims=True)
        dlf = (_sum_all(jnp.where(lag > 0, lag * t, 0.0))
               + _sum_all((j + 1.0) * qx1 + (CH - 1.0 - j) * ky1)
               + CH * jnp.exp(lgf[:1] * CH) * _sum_all(ef_ * sf_))
        dlb = (_sum_all(jnp.where(lag < 0, -lag * t, 0.0))
               + _sum_all((CH - j) * qx2 + j * ky2)
               + CH * jnp.exp(lgb[:1] * CH) * _sum_all(eb_ * sb_))
        row = lax.broadcasted_iota(jnp.int32, (8, 128), 0)
        lane = lax.broadcasted_iota(jnp.int32, (8, 128), 1)
        tile = jnp.where((row == 0) & (lane == 0), dlf * lgf[:1],
                         jnp.where((row == 0) & (lane == 1), dlb * lgb[:1], 0.0))

        @pl.when(pl.program_id(1) == 0)
        def _():
            da_ref[...] = jnp.zeros_like(da_ref)

        da_ref[...] += tile

    def zc(col):
        return pl.BlockSpec((CH, RET_HD), lambda h, c: (c, col + h))

    tab = pl.BlockSpec((CH, 128), lambda h, c: (c, 0))
    stt = pl.BlockSpec((None, None, RET_HD, RET_HD), lambda h, c: (h, c, 0, 0))
    out = pl.BlockSpec((CH, RET_HD), lambda h, c: (c, h))
    dz = jax.ShapeDtypeStruct((S, W), ACT)
    return pl.pallas_call(
        body, grid=(H, nC),
        in_specs=[zc(rq), zc(rk), zc(rv), out, tab, tab, stt, stt, stt, stt, SMEM_SPEC, SMEM_SPEC],
        out_specs=[out, out, out, pl.BlockSpec((None, 8, 128), lambda h, c: (h, 0, 0))],
        out_shape=[dz, dz, dz, jax.ShapeDtypeStruct((H, 8, 128), F32)],
        name="ret_bwd", compiler_params=_params("parallel", "arbitrary"))(
            z, z, z, do, cos, sin, sf, sb, ef, eb, af, ab)


def _fill_padded(pad_ref, src_ref, S):
    zeros = jnp.zeros((POOL_PAD, POOL_GD), F32)
    pad_ref[pl.ds(0, POOL_PAD), :] = zeros
    pad_ref[pl.ds(POOL_PAD, S), :] = src_ref[...]
    pad_ref[pl.ds(S + POOL_PAD, POOL_PAD), :] = zeros


def _window_sum(ext, T, lo, hi):
    n = T + 2 * POOL_PAD
    acc = None
    for k in range(lo, hi):
        sh = ext if k == 0 else pltpu.roll(ext, (-k) % n, 0)
        piece = sh[POOL_PAD:POOL_PAD + T]
        acc = piece if acc is None else acc + piece
    return acc


def _window_count(pos, w, S):
    lo = jnp.maximum(pos - w // 2, 0)
    hi = jnp.minimum(pos + w // 2, S)
    return (hi - lo).astype(F32)


def _pool_fwd(z, pw, scale, pv, pg):
    S = z.shape[0]
    G = len(POOL_WINDOWS)
    T = min(512, S)
    W = G * POOL_GD

    def body(x_ref, g_ref, pw_ref, sc_ref, y_ref, u_ref, pad, p_scr):
        grp = pl.program_id(0)
        i = pl.program_id(1)

        @pl.when(i == 0)
        def _():
            _fill_padded(pad, x_ref, S)

        r0 = pl.multiple_of(i * T, T)
        ext = pad[pl.ds(r0, T + 2 * POOL_PAD), :]
        pos = r0 + lax.broadcasted_iota(jnp.int32, (T, 1), 0)
        for gi, w in enumerate(POOL_WINDOWS):
            @pl.when(grp == gi)
            def _(w=w):
                acc = _window_sum(ext, T, -(w // 2), w // 2)
                p_scr[...] = acc / _window_count(pos, w, S) - ext[POOL_PAD:POOL_PAD + T]

        y = _dot(p_scr[...], pw_ref[...], NN)
        y_ref[...] = y
        g = g_ref[...]
        u_ref[...] = (y * sc_ref[...] * (g * _sigmoid(g))).astype(ACT)

    blk = pl.BlockSpec((T, POOL_GD), lambda g, i: (i, g))
    return pl.pallas_call(
        body, grid=(G, S // T),
        in_specs=[pl.BlockSpec((S, POOL_GD), lambda g, i: (0, pv + g)),
                  pl.BlockSpec((T, POOL_GD), lambda g, i: (i, pg + g)),
                  pl.BlockSpec((None, POOL_GD, POOL_GD), lambda g, i: (g, 0, 0)),
                  pl.BlockSpec((1, POOL_GD), lambda g, i: (0, g))],
        out_specs=[blk, blk],
        out_shape=[jax.ShapeDtypeStruct((S, W), F32), jax.ShapeDtypeStruct((S, W), ACT)],
        scratch_shapes=[pltpu.VMEM((S + 2 * POOL_PAD, POOL_GD), F32), pltpu.VMEM((T, POOL_GD), F32)],
        name="pool_fwd", compiler_params=_params("parallel", "arbitrary"))(z, z, pw, scale)


def _pool_bwd_a(z, pw, scale, y_raw, du, pv, pg):
    S = z.shape[0]
    G = len(POOL_WINDOWS)
    T = min(512, S)
    W = G * POOL_GD

    def body(x_ref, g_ref, pw_ref, sc_ref, y_ref, du_ref, dpc_ref, dg_ref, dsc_ref, dpw_ref,
             pad, p_scr, c_scr):
        grp = pl.program_id(0)
        i = pl.program_id(1)

        @pl.when(i == 0)
        def _():
            _fill_padded(pad, x_ref, S)
            dsc_ref[...] = jnp.zeros_like(dsc_ref)
            dpw_ref[...] = jnp.zeros_like(dpw_ref)

        r0 = pl.multiple_of(i * T, T)
        ext = pad[pl.ds(r0, T + 2 * POOL_PAD), :]
        pos = r0 + lax.broadcasted_iota(jnp.int32, (T, 1), 0)
        for gi, w in enumerate(POOL_WINDOWS):
            @pl.when(grp == gi)
            def _(w=w):
                cnt = _window_count(pos, w, S)
                acc = _window_sum(ext, T, -(w // 2), w // 2)
                p_scr[...] = acc / cnt - ext[POOL_PAD:POOL_PAD + T]
                c_scr[...] = jnp.broadcast_to(cnt, (T, 128))

        g = g_ref[...]
        sg = _sigmoid(g)
        duv = du_ref[...]
        y = y_ref[...]
        scl = sc_ref[...]
        dy = duv * (scl * (g * sg))
        dg_ref[...] = (duv * y * scl * (sg * (1.0 + g * (1.0 - sg)))).astype(ACT)
        dsc_ref[...] += jnp.sum(duv * y * (g * sg), axis=0, keepdims=True)
        dpw_ref[...] += _dot(p_scr[...], dy, TN)
        dpc_ref[...] = _dot(dy, pw_ref[...], NT) / c_scr[:, :1]

    blk = pl.BlockSpec((T, POOL_GD), lambda g, i: (i, g))
    return pl.pallas_call(
        body, grid=(G, S // T),
        in_specs=[pl.BlockSpec((S, POOL_GD), lambda g, i: (0, pv + g)),
                  pl.BlockSpec((T, POOL_GD), lambda g, i: (i, pg + g)),
                  pl.BlockSpec((None, POOL_GD, POOL_GD), lambda g, i: (g, 0, 0)),
                  pl.BlockSpec((1, POOL_GD), lambda g, i: (0, g)), blk, blk],
        out_specs=[blk, blk, pl.BlockSpec((1, POOL_GD), lambda g, i: (0, g)),
                   pl.BlockSpec((None, POOL_GD, POOL_GD), lambda g, i: (g, 0, 0))],
        out_shape=[jax.ShapeDtypeStruct((S, W), F32), jax.ShapeDtypeStruct((S, W), ACT),
                   jax.ShapeDtypeStruct((1, W), F32), jax.ShapeDtypeStruct((G, POOL_GD, POOL_GD), F32)],
        scratch_shapes=[pltpu.VMEM((S + 2 * POOL_PAD, POOL_GD), F32), pltpu.VMEM((T, POOL_GD), F32),
                        pltpu.VMEM((T, 128), F32)],
        name="pool_bwd_a", compiler_params=_params("parallel", "arbitrary"))(z, z, pw, scale, y_raw, du)


def _pool_bwd_b(dpc):
    S, W = dpc.shape
    G = len(POOL_WINDOWS)
    T = min(512, S)

    def body(x_ref, o_ref, pad, acc_scr):
        grp = pl.program_id(0)
        i = pl.program_id(1)

        @pl.when(i == 0)
        def _():
            _fill_padded(pad, x_ref, S)

        r0 = pl.multiple_of(i * T, T)
        ext = pad[pl.ds(r0, T + 2 * POOL_PAD), :]
        pos = r0 + lax.broadcasted_iota(jnp.int32, (T, 1), 0)
        for gi, w in enumerate(POOL_WINDOWS):
            @pl.when(grp == gi)
            def _(w=w):
                acc = _window_sum(ext, T, -(w // 2) + 1, w // 2 + 1)
                acc_scr[...] = acc - ext[POOL_PAD:POOL_PAD + T] * _window_count(pos, w, S)

        o_ref[...] = acc_scr[...].astype(ACT)

    blk = pl.BlockSpec((T, POOL_GD), lambda g, i: (i, g))
    return pl.pallas_call(
        body, grid=(G, S // T),
        in_specs=[pl.BlockSpec((S, POOL_GD), lambda g, i: (0, g))], out_specs=blk,
        out_shape=jax.ShapeDtypeStruct((S, W), ACT),
        scratch_shapes=[pltpu.VMEM((S + 2 * POOL_PAD, POOL_GD), F32), pltpu.VMEM((T, POOL_GD), F32)],
        name="pool_bwd_b", compiler_params=_params("parallel", "arbitrary"))(dpc)


def _rope128(x, cf, sa, sb):
    return x * cf + pltpu.roll(x, ROPE_HALF, 1) * sa + pltpu.roll(x, ATT_HD - ROPE_HALF, 1) * sb


def _rope128_t(g, cf, sa, sb):
    return g * cf + pltpu.roll(g * sa, ATT_HD - ROPE_HALF, 1) + pltpu.roll(g * sb, ROPE_HALF, 1)


def _attn_prep(z, qgain, kgain, cf, sa, sb, aq, ak, av):
    S = z.shape[0]
    T = min(512, S)
    QW, KW = ATT_Q * ATT_HD, ATT_KV * ATT_HD

    def body(q_ref, k_ref, v_ref, qg_ref, kg_ref, cf_ref, sa_ref, sb_ref, qn_ref, kn_ref, vb_ref):
        cfv, sav, sbv = cf_ref[...], sa_ref[...], sb_ref[...]

        def prep(x, gain):
            r = lax.rsqrt(jnp.mean(x * x, axis=-1, keepdims=True) + RMS_EPS)
            return _rope128(x * r * gain, cfv, sav, sbv)

        for hh in range(ATT_Q):
            sl = slice(hh * ATT_HD, (hh + 1) * ATT_HD)
            qn_ref[:, sl] = prep(q_ref[:, sl], qg_ref[...]).astype(ACT)
        for hh in range(ATT_KV):
            sl = slice(hh * ATT_HD, (hh + 1) * ATT_HD)
            kn_ref[:, sl] = prep(k_ref[:, sl], kg_ref[...]).astype(ACT)
        vb_ref[...] = v_ref[...].astype(ACT)

    tab = pl.BlockSpec((T, ATT_HD), lambda i: (i, 0))
    gain = pl.BlockSpec((1, ATT_HD), lambda i: (0, 0))
    return pl.pallas_call(
        body, grid=(S // T,),
        in_specs=[pl.BlockSpec((T, QW), lambda i: (i, aq)), pl.BlockSpec((T, KW), lambda i: (i, ak)),
                  pl.BlockSpec((T, KW), lambda i: (i, av)), gain, gain, tab, tab, tab],
        out_specs=[pl.BlockSpec((T, QW), lambda i: (i, 0)), pl.BlockSpec((T, KW), lambda i: (i, 0)),
                   pl.BlockSpec((T, KW), lambda i: (i, 0))],
        out_shape=[jax.ShapeDtypeStruct((S, QW), ACT), jax.ShapeDtypeStruct((S, KW), ACT),
                   jax.ShapeDtypeStruct((S, KW), ACT)],
        name="attn_prep", compiler_params=_params("parallel"))(z, z, z, qgain, kgain, cf, sa, sb)


def _attn_window(i, S):
    start = jnp.clip(i * ATT_BLK - ATT_BLK, 0, S - ATT_SPAN)
    start = pl.multiple_of(start, ATT_BLK)
    qpos = i * ATT_BLK + lax.broadcasted_iota(jnp.int32, (ATT_BLK, ATT_SPAN), 0)
    kpos = start + lax.broadcasted_iota(jnp.int32, (ATT_BLK, ATT_SPAN), 1)
    return start, jnp.abs(kpos - qpos) <= ATT_WIN


def _attn_probs(q, kw, valid, sink):
    s = _dot(q, kw, NT) * (ATT_HD ** -0.5)
    s = jnp.where(valid, s, NEG_BIG)
    m = jnp.maximum(jnp.max(s, axis=-1, keepdims=True), sink)
    p = jnp.exp(s - m)
    es = jnp.exp(sink - m)
    den = jnp.sum(p, axis=-1, keepdims=True) + es
    return p / den, es / den


def _attn_fwd(qn, kn, vb, z, sink, ag):
    S = qn.shape[0]
    nB = S // ATT_BLK
    assert S >= ATT_SPAN
    QW = ATT_Q * ATT_HD

    def body(q_ref, k_ref, v_ref, g_ref, sink_ref, o_ref, u_ref):
        h = pl.program_id(0)
        i = pl.program_id(1)
        start, valid = _attn_window(i, S)
        kw = k_ref[pl.ds(start, ATT_SPAN), :]
        vw = v_ref[pl.ds(start, ATT_SPAN), :]
        sk = jnp.full((ATT_BLK, 1), sink_ref[h], F32)
        pn, _ = _attn_probs(q_ref[...], kw, valid, sk)
        o = _dot(pn, vw, NN)
        o_ref[...] = o
        g = g_ref[...]
        u_ref[...] = (o * (g * _sigmoid(g))).astype(ACT)

    blk = pl.BlockSpec((ATT_BLK, ATT_HD), lambda h, i: (i, h))
    kv = pl.BlockSpec((S, ATT_HD), lambda h, i: (0, h // ATT_G))
    return pl.pallas_call(
        body, grid=(ATT_Q, nB),
        in_specs=[blk, kv, kv, pl.BlockSpec((ATT_BLK, ATT_HD), lambda h, i: (i, ag + h)), SMEM_SPEC],
        out_specs=[blk, blk],
        out_shape=[jax.ShapeDtypeStruct((S, QW), F32), jax.ShapeDtypeStruct((S, QW), ACT)],
        name="attn_fwd", compiler_params=_params("parallel", "parallel"))(qn, kn, vb, z, sink)


def _attn_bwd(qn, kn, vb, o, du, z, sink, ag):
    S = qn.shape[0]
    nB = S // ATT_BLK
    QW, KW = ATT_Q * ATT_HD, ATT_KV * ATT_HD
    GW = ATT_G * ATT_HD

    def body(q_ref, k_ref, v_ref, o_ref, du_ref, g_ref, sink_ref,
             dq_ref, dk_ref, dv_ref, dg_ref, ds_ref):
        kvh = pl.program_id(0)
        i = pl.program_id(1)

        @pl.when(i == 0)
        def _():
            dk_ref[...] = jnp.zeros_like(dk_ref)
            dv_ref[...] = jnp.zeros_like(dv_ref)
            ds_ref[...] = jnp.zeros_like(ds_ref)

        start, valid = _attn_window(i, S)
        kw = k_ref[pl.ds(start, ATT_SPAN), :]
        vw = v_ref[pl.ds(start, ATT_SPAN), :]
        row = lax.broadcasted_iota(jnp.int32, (8, 128), 0)
        lane = lax.broadcasted_iota(jnp.int32, (8, 128), 1)
        dsink = jnp.zeros((8, 128), F32)
        dk_acc = jnp.zeros((ATT_SPAN, ATT_HD), F32)
        dv_acc = jnp.zeros((ATT_SPAN, ATT_HD), F32)
        for gi in range(ATT_G):
            sl = slice(gi * ATT_HD, (gi + 1) * ATT_HD)
            q = q_ref[:, sl]
            ov = o_ref[:, sl]
            g = g_ref[:, sl]
            duv = du_ref[:, sl]
            sg = _sigmoid(g)
            do = duv * (g * sg)
            dg_ref[:, sl] = (duv * ov * (sg * (1.0 + g * (1.0 - sg)))).astype(ACT)
            sk = jnp.full((ATT_BLK, 1), sink_ref[kvh * ATT_G + gi], F32)
            pn, psink = _attn_probs(q, kw, valid, sk)
            delta = jnp.sum(do * ov, axis=-1, keepdims=True)
            dsc = pn * (_dot(do, vw, NT) - delta) * (ATT_HD ** -0.5)
            dq_ref[:, sl] = _dot(dsc, kw, NN)
            dk_acc = dk_acc + _dot(dsc, q, TN)
            dv_acc = dv_acc + _dot(pn, do, TN)
            dsink = dsink + jnp.where((row == 0) & (lane == gi), -_sum_all(psink * delta), 0.0)
        dk_ref[pl.ds(start, ATT_SPAN), :] += dk_acc
        dv_ref[pl.ds(start, ATT_SPAN), :] += dv_acc
        ds_ref[...] += dsink

    grp = pl.BlockSpec((ATT_BLK, GW), lambda k, i: (i, k))
    kv = pl.BlockSpec((S, ATT_HD), lambda k, i: (0, k))
    return pl.pallas_call(
        body, grid=(ATT_KV, nB),
        in_specs=[grp, kv, kv, grp, grp,
                  pl.BlockSpec((ATT_BLK, GW), lambda k, i: (i, ag // ATT_G + k)), SMEM_SPEC],
        out_specs=[grp, kv, kv, grp, pl.BlockSpec((None, 8, 128), lambda k, i: (k, 0, 0))],
        out_shape=[jax.ShapeDtypeStruct((S, QW), F32), jax.ShapeDtypeStruct((S, KW), F32),
                   jax.ShapeDtypeStruct((S, KW), F32), jax.ShapeDtypeStruct((S, QW), ACT),
                   jax.ShapeDtypeStruct((ATT_KV, 8, 128), F32)],
        name="attn_bwd", compiler_params=_params("parallel", "arbitrary"))(qn, kn, vb, o, du, z, sink)


def _attn_prep_bwd(z, dqn, dkn, dv, qgain, kgain, cf, sa, sb, aq, ak):
    S = z.shape[0]
    T = min(512, S)
    QW, KW = ATT_Q * ATT_HD, ATT_KV * ATT_HD

    def body(q_ref, k_ref, dqn_ref, dkn_ref, dv_ref, qg_ref, kg_ref, cf_ref, sa_ref, sb_ref,
             dq_ref, dk_ref, dvb_ref, dqg_ref, dkg_ref):
        cfv, sav, sbv = cf_ref[...], sa_ref[...], sb_ref[...]

        @pl.when(pl.program_id(0) == 0)
        def _():
            dqg_ref[...] = jnp.zeros_like(dqg_ref)
            dkg_ref[...] = jnp.zeros_like(dkg_ref)

        def back(x, gn, gain):
            r = lax.rsqrt(jnp.mean(x * x, axis=-1, keepdims=True) + RMS_EPS)
            xh = x * r
            dy = _rope128_t(gn, cfv, sav, sbv)
            dxh = dy * gain
            dx = r * (dxh - xh * jnp.mean(dxh * xh, axis=-1, keepdims=True))
            return dx, jnp.sum(dy * xh, axis=0, keepdims=True)

        dqg = jnp.zeros((1, ATT_HD), F32)
        for hh in range(ATT_Q):
            sl = slice(hh * ATT_HD, (hh + 1) * ATT_HD)
            dx, dgn = back(q_ref[:, sl], dqn_ref[:, sl], qg_ref[...])
            dq_ref[:, sl] = dx.astype(ACT)
            dqg = dqg + dgn
        dkg = jnp.zeros((1, ATT_HD), F32)
        for hh in range(ATT_KV):
            sl = slice(hh * ATT_HD, (hh + 1) * ATT_HD)
            dx, dgn = back(k_ref[:, sl], dkn_ref[:, sl], kg_ref[...])
            dk_ref[:, sl] = dx.astype(ACT)
            dkg = dkg + dgn
        dvb_ref[...] = dv_ref[...].astype(ACT)
        dqg_ref[...] += dqg
        dkg_ref[...] += dkg

    tab = pl.BlockSpec((T, ATT_HD), lambda i: (i, 0))
    gain = pl.BlockSpec((1, ATT_HD), lambda i: (0, 0))
    qb = pl.BlockSpec((T, QW), lambda i: (i, 0))
    kb = pl.BlockSpec((T, KW), lambda i: (i, 0))
    return pl.pallas_call(
        body, grid=(S // T,),
        in_specs=[pl.BlockSpec((T, QW), lambda i: (i, aq)), pl.BlockSpec((T, KW), lambda i: (i, ak)),
                  qb, kb, kb, gain, gain, tab, tab, tab],
        out_specs=[qb, kb, kb, gain, gain],
        out_shape=[jax.ShapeDtypeStruct((S, QW), ACT), jax.ShapeDtypeStruct((S, KW), ACT),
                   jax.ShapeDtypeStruct((S, KW), ACT), jax.ShapeDtypeStruct((1, ATT_HD), F32),
                   jax.ShapeDtypeStruct((1, ATT_HD), F32)],
        name="attn_prep_bwd", compiler_params=_params("arbitrary"))(
            z, z, dqn, dkn, dv, qgain, kgain, cf, sa, sb)


def _branch_merge(ua, ub, uc, wr, wp, wa, z, mg):
    S, W = ua.shape
    D = wr.shape[1]
    tm, tn = min(512, S), min(512, D)
    nb = D // tn

    def body(ua_ref, ub_ref, uc_ref, wr_ref, wp_ref, wa_ref, g0_ref, g1_ref, g2_ref,
             ya_ref, yb_ref, yc_ref, m_ref):
        ya = _dot(ua_ref[...], wr_ref[...], NN)
        yb = _dot(ub_ref[...], wp_ref[...], NN)
        yc = _dot(uc_ref[...], wa_ref[...], NN)
        ya_ref[...] = ya.astype(ACT)
        yb_ref[...] = yb.astype(ACT)
        yc_ref[...] = yc.astype(ACT)
        m_ref[...] = (_sigmoid(g0_ref[...]) * ya + _sigmoid(g1_ref[...]) * yb
                      + _sigmoid(g2_ref[...]) * yc).astype(ACT)

    u = pl.BlockSpec((tm, W), lambda i, j: (i, 0))
    w = pl.BlockSpec((W, tn), lambda i, j: (0, j))
    o = pl.BlockSpec((tm, tn), lambda i, j: (i, j))

    assert (mg * POOL_GD) % tn == 0
    base = (mg * POOL_GD) // tn

    def gate(k):
        return pl.BlockSpec((tm, tn), lambda i, j: (i, base + k * nb + j))

    sd = jax.ShapeDtypeStruct((S, D), ACT)
    return pl.pallas_call(
        body, grid=(S // tm, nb), in_specs=[u, u, u, w, w, w, gate(0), gate(1), gate(2)],
        out_specs=[o, o, o, o], out_shape=[sd, sd, sd, sd], name="branch_merge",
        compiler_params=_params("parallel", "parallel"))(ua, ub, uc, wr, wp, wa, z, z, z)


def _merge_bwd(dxb, wo, ya, yb, yc, z, mg):
    S, D = dxb.shape
    tm, tn = min(512, S), min(512, D)
    nb = D // tn
    base = (mg * POOL_GD) // tn

    def body(dx_ref, wo_ref, ya_ref, yb_ref, yc_ref, g0_ref, g1_ref, g2_ref,
             da_ref, db_ref, dc_ref, dg0_ref, dg1_ref, dg2_ref):
        dm = _dot(dx_ref[...], wo_ref[...], NT)
        for y_ref, g_ref, dy_ref, dg_ref in ((ya_ref, g0_ref, da_ref, dg0_ref),
                                             (yb_ref, g1_ref, db_ref, dg1_ref),
                                             (yc_ref, g2_ref, dc_ref, dg2_ref)):
            sg = _sigmoid(g_ref[...])
            dy_ref[...] = (sg * dm).astype(ACT)
            dg_ref[...] = (dm * y_ref[...].astype(F32) * (sg * (1.0 - sg))).astype(ACT)

    o = pl.BlockSpec((tm, tn), lambda i, j: (i, j))

    def gate(k):
        return pl.BlockSpec((tm, tn), lambda i, j: (i, base + k * nb + j))

    sd = jax.ShapeDtypeStruct((S, D), ACT)
    return pl.pallas_call(
        body, grid=(S // tm, nb),
        in_specs=[pl.BlockSpec((tm, D), lambda i, j: (i, 0)), pl.BlockSpec((tn, D), lambda i, j: (j, 0)),
                  o, o, o, gate(0), gate(1), gate(2)],
        out_specs=[o] * 6, out_shape=[sd] * 6, name="merge_bwd",
        compiler_params=_params("parallel", "parallel"))(dxb, wo, ya, yb, yc, z, z, z)


def _loss_head(y, t):
    S, D = y.shape
    tm = min(256, S)

    def body(y_ref, t_ref, dy_ref, dyb_ref, l_ref):
        e = y_ref[...] - t_ref[...]
        dy = e * (1.0 / D)
        dy_ref[...] = dy
        dyb_ref[...] = dy.astype(ACT)

        @pl.when(pl.program_id(0) == 0)
        def _():
            l_ref[...] = jnp.zeros_like(l_ref)

        l_ref[...] += jnp.sum(jnp.mean(e * e, axis=-1, keepdims=True), axis=0, keepdims=True)

    row = pl.BlockSpec((tm, D), lambda i: (i, 0))
    return pl.pallas_call(
        body, grid=(S // tm,), in_specs=[row, row],
        out_specs=[row, row, pl.BlockSpec((1, 1), lambda i: (0, 0))],
        out_shape=[jax.ShapeDtypeStruct((S, D), F32), jax.ShapeDtypeStruct((S, D), ACT),
                   jax.ShapeDtypeStruct((1, 1), F32)],
        name="loss_head", compiler_params=_params("arbitrary"))(y, t)


def _adamw(w, parts, m, v, name):
    shape = w.shape
    C = shape[-1]
    R = int(np.prod(shape[:-1]))
    w2, m2, v2 = (a.reshape(R, C) for a in (w, m, v))
    p2 = parts.reshape(N_DEV, R, C)
    tr = min(64, R)
    assert R % tr == 0
    c1 = 1.0 / (1.0 - ADAM_B1 ** ADAM_STEP)
    c2 = 1.0 / (1.0 - ADAM_B2 ** ADAM_STEP)

    def body(w_ref, p_ref, m_ref, v_ref, g_ref, d_ref, nm_ref, nv_ref):
        g = p_ref[0].astype(F32)
        for k in range(1, N_DEV):
            g = g + p_ref[k].astype(F32)
        nm = ADAM_B1 * m_ref[...] + (1.0 - ADAM_B1) * g
        nv = ADAM_B2 * v_ref[...] + (1.0 - ADAM_B2) * (g * g)
        g_ref[...] = g
        nm_ref[...] = nm
        nv_ref[...] = nv
        d_ref[...] = -ADAM_LR * ((nm * c1) / (jnp.sqrt(nv * c2) + ADAM_EPS) + ADAM_WD * w_ref[...])

    blk = pl.BlockSpec((tr, C), lambda i: (i, 0))
    sd = jax.ShapeDtypeStruct((R, C), F32)
    outs = pl.pallas_call(
        body, grid=(R // tr,),
        in_specs=[blk, pl.BlockSpec((N_DEV, tr, C), lambda i: (0, i, 0)), blk, blk],
        out_specs=[blk] * 4, out_shape=[sd] * 4, name=name,
        compiler_params=_params("parallel"))(w2, p2, m2, v2)
    return [a.reshape(shape) for a in outs]


def _exchange(arrs, scatter, name):
    n = len(arrs)
    out_shape = [jax.ShapeDtypeStruct(a.shape if scatter else (N_DEV,) + a.shape, a.dtype) for a in arrs]

    def body(*refs):
        ins, outs = refs[:n], refs[n:2 * n]
        send_sems, recv_sems, local_sems = refs[2 * n:]
        x, y, c = lax.axis_index("x"), lax.axis_index("y"), lax.axis_index("c")
        me = 4 * x + 2 * y + c
        copies = []
        for a in range(n):
            src = ins[a].at[me] if scatter else ins[a]
            own = pltpu.make_async_copy(src, outs[a].at[me], local_sems.at[a])
            own.start()
            copies.append(own)
        sends, recvs = [], []
        for k in range(1, N_DEV):
            px, py, pc = x ^ (k >> 2), y ^ ((k >> 1) & 1), c ^ (k & 1)
            peer = 4 * px + 2 * py + pc
            for a in range(n):
                src = ins[a].at[peer] if scatter else ins[a]
                cp = pltpu.make_async_remote_copy(
                    src_ref=src, dst_ref=outs[a].at[me],
                    send_sem=send_sems.at[a, k - 1], recv_sem=recv_sems.at[a, k - 1],
                    device_id=(px, py, pc), device_id_type=pl.DeviceIdType.MESH)
                cp.start()
                sends.append(cp)
                recvs.append(pltpu.make_async_remote_copy(
                    src_ref=src, dst_ref=outs[a].at[peer],
                    send_sem=send_sems.at[a, k - 1], recv_sem=recv_sems.at[a, k - 1],
                    device_id=(px, py, pc), device_id_type=pl.DeviceIdType.MESH))
        for cp in recvs:
            cp.wait_recv()
        for cp in sends:
            cp.wait_send()
        for cp in copies:
            cp.wait()

    any_spec = pl.BlockSpec(memory_space=pl.ANY)
    return pl.pallas_call(
        body, in_specs=[any_spec] * n, out_specs=[any_spec] * n, out_shape=out_shape,
        scratch_shapes=[pltpu.SemaphoreType.DMA((n, N_DEV - 1)), pltpu.SemaphoreType.DMA((n, N_DEV - 1)),
                        pltpu.SemaphoreType.DMA((n,))],
        name=name)(*arrs)


HBM_SPEC = pl.BlockSpec(memory_space=pltpu.HBM)
SEM_SPEC = pl.BlockSpec(memory_space=pltpu.SEMAPHORE)
DATAFLOW = pltpu.SideEffectType.DATAFLOW_SIDE_EFFECTING


def _peer_of(k):
    x, y, c = lax.axis_index("x"), lax.axis_index("y"), lax.axis_index("c")
    return x ^ (k >> 2), y ^ ((k >> 1) & 1), c ^ (k & 1)


def _exchange_copy(k, a, src_ref, land_ref, send_sems, recv_sems, scatter, outgoing):
    px, py, pc = _peer_of(k)
    peer = 4 * px + 2 * py + pc
    me = 4 * lax.axis_index("x") + 2 * lax.axis_index("y") + lax.axis_index("c")
    idx = a * (N_DEV - 1) + k - 1
    return pltpu.make_async_remote_copy(
        src_ref=src_ref.at[peer] if scatter else src_ref, dst_ref=land_ref.at[me if outgoing else peer],
        send_sem=send_sems.at[idx], recv_sem=recv_sems.at[idx],
        device_id=(px, py, pc), device_id_type=pl.DeviceIdType.MESH)


def _exchange_start(arrs, scatter, name):
    n = len(arrs)
    land_shapes = [a.shape if scatter else (N_DEV,) + a.shape for a in arrs]

    def body(*refs):
        srcs, lands = refs[:n], refs[n:2 * n]
        send_sems, recv_sems = refs[2 * n], refs[2 * n + 1]
        token = refs[-1]
        for k in range(1, N_DEV):
            for a in range(n):
                _exchange_copy(k, a, srcs[a], lands[a], send_sems, recv_sems, scatter, True).start()
        token[...] = jnp.zeros_like(token)

    sems = pltpu.SemaphoreType.DMA((n * (N_DEV - 1),))
    out_shape = ([sems, sems] + [pltpu.HBM(a.shape, a.dtype) for a in arrs]
                 + [pltpu.HBM(s, a.dtype) for s, a in zip(land_shapes, arrs)]
                 + [jax.ShapeDtypeStruct((8, 128), F32)])
    ins = ([pltpu.with_memory_space_constraint(a, pltpu.HBM) for a in arrs]
           + [pltpu.with_memory_space_constraint(lax.empty(s, a.dtype), pltpu.HBM) for s, a in zip(land_shapes, arrs)])
    res = pl.pallas_call(
        body, name=name, out_shape=out_shape, in_specs=[HBM_SPEC] * (2 * n),
        out_specs=[SEM_SPEC, SEM_SPEC] + [HBM_SPEC] * (2 * n) + [pl.BlockSpec(memory_space=pltpu.VMEM)],
        input_output_aliases={i: 2 + i for i in range(2 * n)},
        compiler_params=pltpu.CompilerParams(has_side_effects=DATAFLOW))(*ins)
    return res[0], res[1], list(res[2:2 + n]), list(res[2 + n:2 + 2 * n]), res[-1]


def _exchange_wait(started, after, scatter, name):
    send_sems, recv_sems, srcs, lands, _ = started
    n = len(srcs)

    def body(*refs):
        src_refs, land_refs = refs[:n], refs[n:2 * n]
        s_sems, r_sems = refs[2 * n], refs[2 * n + 1]
        for k in range(1, N_DEV):
            for a in range(n):
                back = _exchange_copy(k, a, src_refs[a], land_refs[a], s_sems, r_sems, scatter, False)
                back.wait_send()
                back.wait_recv()

    out_shape = [pltpu.HBM(a.shape, a.dtype) for a in srcs] + [pltpu.HBM(a.shape, a.dtype) for a in lands]
    res = pl.pallas_call(
        body, name=name, out_shape=out_shape,
        in_specs=[HBM_SPEC] * (2 * n) + [SEM_SPEC, SEM_SPEC, pl.BlockSpec(memory_space=pl.ANY)],
        out_specs=[HBM_SPEC] * (2 * n), input_output_aliases={i: i for i in range(2 * n)},
        compiler_params=pltpu.CompilerParams(has_side_effects=DATAFLOW))(*srcs, *lands, send_sems, recv_sems, after)
    return list(res[:n]), list(res[n:])


def _own_slot(land, own, me):
    return lax.dynamic_update_index_in_dim(land, own, me, 0)


class _Cols:
    def __init__(self, D):
        Wb = D // 2
        sizes = (Wb, Wb, Wb, Wb, Wb, Wb, Wb, ATT_KV * ATT_HD, ATT_KV * ATT_HD, Wb, 3 * D)
        offs = np.concatenate([[0], np.cumsum(sizes)])
        assert all(int(o) % 256 == 0 for o in offs)
        (self.rq, self.rk, self.rv, self.rg, self.pv, self.pg,
         self.aq, self.ak, self.av, self.ag, self.mg) = (int(o) // 256 for o in offs[:-1])
        self.width = int(offs[-1])
        self.sizes = sizes


def _rope_tables(S):
    pos = jnp.arange(S, dtype=F32)[:, None]
    inv_r = 1.0 / (RET_ROPE_BASE ** jnp.linspace(0.0, 1.0, RET_HD // 2, dtype=F32))
    ang_r = pos * inv_r[None, :]
    inv_a = ROPE_THETA ** (-jnp.arange(ROPE_HALF, dtype=F32) / ROPE_HALF)
    ang_a = pos * inv_a[None, :]
    ca, sa = jnp.cos(ang_a), jnp.sin(ang_a)
    z16 = jnp.zeros((S, ROPE_HALF), F32)
    rest = ATT_HD - 2 * ROPE_HALF
    cf = jnp.concatenate([ca, ca, jnp.ones((S, rest), F32)], axis=1)
    s_up = jnp.concatenate([z16, sa, jnp.zeros((S, rest), F32)], axis=1)
    s_dn = jnp.concatenate([-sa, z16, jnp.zeros((S, rest), F32)], axis=1)
    return jnp.cos(ang_r), jnp.sin(ang_r), cf, s_up, s_dn


def _layer_fwd(x, p, tabs, cols, deps=()):
    cos_r, sin_r, cf, s_up, s_dn = tabs
    S, D = x.shape
    h = _rmsnorm_fwd(x, p["norm_g"], deps)
    z = _matmul(h, p["w_in"], "nn", F32, 1024, 512, D, "in_proj")
    rcols = (cols.rq, cols.rk, cols.rv, cols.rg)
    sf, sb = _ret_state(z, cols.rk, RET_HD ** -0.5, z, cols.rv, cos_r, sin_r, p["af"], p["ab"], "fwd",
                        "ret_state_fwd")
    o_ret, ua = _ret_fwd(z, cos_r, sin_r, sf, sb, p["af"], p["ab"], rcols)
    y_pool, ub = _pool_fwd(z, p["pool_w"], p["pool_scale"], cols.pv, cols.pg)
    qn, kn, vb = _attn_prep(z, p["q_gain"], p["k_gain"], cf, s_up, s_dn,
                            cols.aq // 4, cols.ak, cols.av)
    o_att, uc = _attn_fwd(qn, kn, vb, z, p["sink"], cols.ag * 2)
    ya, yb, yc, merged = _branch_merge(ua, ub, uc, p["w_ret"], p["w_pool"], p["w_att"], z, cols.mg)
    out = _matmul(merged, p["w_out"], "nn", F32, 1024, 512, D, "out_proj", res=x)
    saved = dict(x=x, h=h, z=z, sf=sf, sb=sb, o_ret=o_ret, ua=ua, y_pool=y_pool, ub=ub,
                 qn=qn, kn=kn, vb=vb, o_att=o_att, uc=uc, ya=ya, yb=yb, yc=yc, merged=merged)
    return out, saved


def _layer_bwd(dx, dxb, p, sv, tabs, cols, deps=()):
    cos_r, sin_r, cf, s_up, s_dn = tabs
    z = sv["z"]
    S, D = dx.shape
    dya, dyb, dyc, dmg0, dmg1, dmg2 = _merge_bwd(dxb, p["w_out"], sv["ya"], sv["yb"], sv["yc"], z, cols.mg)
    grads = {"w_out": _matmul(sv["merged"], dxb, "tn", ACT, 512, 512, 1024, "dw_out")}
    dus = {}
    for nm, u, dy in (("ret", sv["ua"], dya), ("pool", sv["ub"], dyb), ("att", sv["uc"], dyc)):
        dus[nm] = _matmul(dy, p["w_" + nm], "nt", F32, 1024, 512, D, "du_" + nm)
        grads["w_" + nm] = _matmul(u, dy, "tn", ACT, 512, 512, 1024, "dw_" + nm)
    rcols = (cols.rq, cols.rk, cols.rv, cols.rg)
    do_ret, d_rg = _ret_gate_bwd(dus["ret"], sv["o_ret"], z, cols.rg, deps)
    eb, ef = _ret_state(z, cols.rq, 1.0, do_ret, 0, cos_r, sin_r, p["af"], p["ab"], "bwd", "ret_state_bwd")
    d_rq, d_rk, d_rv, d_decay = _ret_bwd(z, do_ret, cos_r, sin_r, sv["sf"], sv["sb"], ef, eb,
                                         p["af"], p["ab"], rcols)
    dpc, d_pg, d_pscale, g_pool_w = _pool_bwd_a(z, p["pool_w"], p["pool_scale"], sv["y_pool"], dus["pool"],
                                                cols.pv, cols.pg)
    d_pv = _pool_bwd_b(dpc)
    grads["pool_w"] = g_pool_w.astype(ACT)
    dqn, dkn, dvv, d_ag, d_sink = _attn_bwd(sv["qn"], sv["kn"], sv["vb"], sv["o_att"], dus["att"], z,
                                            p["sink"], cols.ag * 2)
    d_aq, d_ak, d_av, d_qg, d_kg = _attn_prep_bwd(z, dqn, dkn, dvv, p["q_gain"], p["k_gain"], cf, s_up, s_dn,
                                                  cols.aq // 4, cols.ak)
    dz = jnp.concatenate([d_rq, d_rk, d_rv, d_rg, d_pv, d_pg, d_aq, d_ak, d_av, d_ag, dmg0, dmg1, dmg2],
                         axis=1)
    grads["w_in"] = _matmul(sv["h"], dz, "tn", ACT, 512, 512, 1024, "dw_in")
    dh = _matmul(dz, p["w_in"], "nt", F32, 1024, 512, 512, "dh")
    dx_in, dxb_in, d_norm_g = _rmsnorm_bwd(sv["x"], p["norm_g"], dh, dx)
    misc = jnp.concatenate([d_decay[:, 0, 0], d_decay[:, 0, 1], d_sink[:, 0, :ATT_G].reshape(-1)])
    misc = jnp.pad(misc, (0, 128 - misc.shape[0]))[None, :]
    small = jnp.concatenate([d_norm_g, d_pscale, d_qg, d_kg, misc], axis=1)
    return dx_in, dxb_in, grads, small


def _pack_small(norm_g, pool_scale, q_gain, k_gain, af, ab, sink):
    L = norm_g.shape[0]
    misc = jnp.concatenate([af, ab, sink], axis=1)
    misc = jnp.pad(misc, ((0, 0), (0, 128 - misc.shape[1])))
    return jnp.concatenate([norm_g, pool_scale, q_gain, k_gain, misc], axis=1)


def _unpack_small(a, D):
    Wb = D // 2
    o = np.cumsum([0, D, Wb, ATT_HD, ATT_HD])
    misc = a[:, o[4]:]
    return (a[:, o[0]:o[1]], a[:, o[1]:o[2]], a[:, o[2]:o[3]], a[:, o[3]:o[4]],
            misc[:, :RET_HEADS], misc[:, RET_HEADS:2 * RET_HEADS],
            misc[:, 2 * RET_HEADS:2 * RET_HEADS + ATT_Q])


def _local_step(x, t, n_layers, get_layer, on_grads, tabs, cols, first_dep=None):
    saved, layers = [], []
    after = first_dep
    for l in range(n_layers):
        p = get_layer(l, after)
        x, sv = _layer_fwd(x, p, tabs, cols, (first_dep,) if (l == 0 and first_dep is not None) else ())
        after = x
        layers.append(p)
        saved.append(sv)
    dx, dxb, lsum = _loss_head(x, t)
    smalls, deps = [], ()
    for l in reversed(range(n_layers)):
        dx, dxb, g, sm = _layer_bwd(dx, dxb, layers[l], saved[l], tabs, cols, deps)
        tok = on_grads(l, g)
        deps = () if tok is None else (tok,)
        smalls.append(sm)
    return 0.5 * lsum[0, 0], dx, jnp.concatenate(smalls[::-1], axis=0)


WEIGHT_KEYS = ("w_in", "w_ret", "w_pool", "w_att", "w_out", "pool_w")


def kernel(x, norm_g, w_in, ret_decay_fwd, ret_decay_bwd, pool_w, pool_scale, attn_q_gain, attn_k_gain, attn_sink, w_ret, w_pool, w_att, w_out, loss_target, m_norm_g, m_w_in, m_ret_decay_fwd, m_ret_decay_bwd, m_pool_w, m_pool_scale, m_attn_q_gain, m_attn_k_gain, m_attn_sink, m_w_ret, m_w_pool, m_w_att, m_w_out, v_norm_g, v_w_in, v_ret_decay_fwd, v_ret_decay_bwd, v_pool_w, v_pool_scale, v_attn_q_gain, v_attn_k_gain, v_attn_sink, v_w_ret, v_w_pool, v_w_att, v_w_out):
    L = norm_g.shape[0]
    _, S, D = x.shape
    Wb = D // 2
    G = len(POOL_WINDOWS)
    cols = _Cols(D)
    tabs = _rope_tables(S)
    me = 4 * lax.axis_index("x") + 2 * lax.axis_index("y") + lax.axis_index("c")
    weights = dict(w_in=w_in, w_ret=w_ret, w_pool=w_pool, w_att=w_att, w_out=w_out, pool_w=pool_w)

    gathers, tok = [], None
    for l in range(L):
        shards = []
        for k in WEIGHT_KEYS:
            w = weights[k][l]
            if tok is not None:
                w = w + tok[0, 0]
            shards.append(w.astype(MXU))
        st = _exchange_start(shards, False, f"gather_start_{l}")
        gathers.append(st)
        tok = st[-1]

    def cols_full(g, rows):
        return jnp.transpose(g, (1, 0, 2)).reshape(rows, -1)

    def get_layer(l, after):
        srcs, lands = _exchange_wait(gathers[l], after, False, f"gather_wait_{l}")
        g_in, g_ret, g_pool, g_att, g_out, g_pw = [_own_slot(ld, sr, me) for ld, sr in zip(lands, srcs)]
        return dict(
            norm_g=norm_g[l][None, :], w_in=cols_full(g_in, D),
            w_ret=cols_full(g_ret, Wb), w_pool=cols_full(g_pool, Wb), w_att=cols_full(g_att, Wb),
            w_out=g_out.reshape(D, D),
            pool_w=jnp.transpose(g_pw, (1, 0, 2, 3)).reshape(G, POOL_GD, POOL_GD),
            pool_scale=pool_scale[l][None, :], q_gain=attn_q_gain[l][None, :], k_gain=attn_k_gain[l][None, :],
            af=ret_decay_fwd[l], ab=ret_decay_bwd[l], sink=attn_sink[l])

    def col_slots(g, rows):
        return jnp.transpose(g.reshape(rows, N_DEV, -1), (1, 0, 2))

    scatters = {}

    def on_grads(l, g):
        slots = [col_slots(g["w_in"], D), col_slots(g["w_ret"], Wb), col_slots(g["w_pool"], Wb),
                 col_slots(g["w_att"], Wb), g["w_out"].reshape(N_DEV, D // N_DEV, D),
                 jnp.transpose(g["pool_w"].reshape(G, N_DEV, POOL_GD // N_DEV, POOL_GD), (1, 0, 2, 3))]
        scatters[l] = _exchange_start(slots, True, f"scatter_start_{l}")
        return scatters[l][-1]

    loss_local, grad_x, small = _local_step(x[0], loss_target[0], L, get_layer, on_grads, tabs, cols, tok)
    loss = lax.psum(loss_local, ("x", "y", "c"))
    small_all, = _exchange([small], False, "gather_small_grads")

    moments = dict(w_in=(m_w_in, v_w_in), w_ret=(m_w_ret, v_w_ret), w_pool=(m_w_pool, v_w_pool),
                   w_att=(m_w_att, v_w_att), w_out=(m_w_out, v_w_out), pool_w=(m_pool_w, v_pool_w))
    per_layer = {}
    after = small_all
    for l in reversed(range(L)):
        srcs, lands = _exchange_wait(scatters[l], after, True, f"scatter_wait_{l}")
        per_layer[l] = {}
        for k, ld, sr in zip(WEIGHT_KEYS, lands, srcs):
            parts = _own_slot(ld, lax.dynamic_index_in_dim(sr, me, 0, keepdims=False), me)
            m, v = moments[k]
            per_layer[l][k] = _adamw(weights[k][l], parts, m[l], v[l], "adamw_" + k)
        after = per_layer[l]["w_out"][1]
    res = {k: [jnp.stack([per_layer[l][k][i] for l in range(L)]) for i in range(4)] for k in WEIGHT_KEYS}

    sw = _pack_small(norm_g, pool_scale, attn_q_gain, attn_k_gain, ret_decay_fwd, ret_decay_bwd, attn_sink)
    sm = _pack_small(m_norm_g, m_pool_scale, m_attn_q_gain, m_attn_k_gain, m_ret_decay_fwd, m_ret_decay_bwd,
                     m_attn_sink)
    sv_ = _pack_small(v_norm_g, v_pool_scale, v_attn_q_gain, v_attn_k_gain, v_ret_decay_fwd, v_ret_decay_bwd,
                      v_attn_sink)
    small_res = [_unpack_small(a, D) for a in _adamw(sw, small_all, sm, sv_, "adamw_small")]

    def ordered(i):
        ng, ps, qg, kg, af, ab, sk = small_res[i]
        return (ng, res["w_in"][i], af, ab, res["pool_w"][i], ps, qg, kg, sk,
                res["w_ret"][i], res["w_pool"][i], res["w_att"][i], res["w_out"][i])

    return (loss, grad_x[None], *ordered(0), *ordered(1), *ordered(2), *ordered(3))
```

```python
import functools

import numpy as np
import jax
import jax.numpy as jnp
from jax import lax
from jax.experimental import pallas as pl
from jax.experimental.pallas import tpu as pltpu

F32 = jnp.float32
MXU = jnp.bfloat16
ACT = jnp.bfloat16

N_DEV = 8
RMS_EPS = 1e-6
NEG_BIG = -1e30
RET_HEADS = 4
RET_HD = 256
CH = 128
RET_ROPE_BASE = 10000.0
POOL_WINDOWS = (2, 4, 8, 16)
POOL_GD = 256
POOL_PAD = 8
ATT_HD = 128
ATT_Q = 8
ATT_KV = 2
ATT_G = ATT_Q // ATT_KV
ATT_WIN = 128
ATT_BLK = 128
ATT_SPAN = 3 * ATT_BLK
ROPE_THETA = 500000.0
ROPE_HALF = 16

ADAM_LR = 0.001
ADAM_B1 = 0.9
ADAM_B2 = 0.999
ADAM_EPS = 1e-08
ADAM_WD = 0.01
ADAM_STEP = 10

VMEM_LIMIT = 48 * 1024 * 1024

NN = ((1,), (0,))
NT = ((1,), (1,))
TN = ((0,), (0,))


def _dot(a, b, dims):
    return lax.dot_general(a.astype(MXU), b.astype(MXU), (dims, ((), ())),
                           preferred_element_type=F32)


def _sigmoid(x):
    return 1.0 / (1.0 + jnp.exp(-x))


def _params(*sem):
    return pltpu.CompilerParams(dimension_semantics=sem, vmem_limit_bytes=VMEM_LIMIT)


def _sum_all(x):
    return jnp.sum(jnp.sum(x, axis=1, keepdims=True), axis=0, keepdims=True)


def _fiota(shape, dim):
    return lax.broadcasted_iota(jnp.int32, shape, dim).astype(F32)


SMEM_SPEC = pl.BlockSpec(memory_space=pltpu.SMEM)


def _matmul(a, b, mode, out_dtype, tm, tn, tk, name, res=None, deps=()):
    if mode == "tn":
        K, M = a.shape
    else:
        M, K = a.shape
    N = b.shape[0] if mode == "nt" else b.shape[1]
    tm, tn, tk = min(tm, M), min(tn, N), min(tk, K)
    assert M % tm == 0 and N % tn == 0 and K % tk == 0, (name, M, N, K, tm, tn, tk)
    nk = K // tk
    dims = {"nn": NN, "nt": NT, "tn": TN}[mode]
    a_spec = (pl.BlockSpec((tk, tm), lambda i, j, k: (k, i)) if mode == "tn"
              else pl.BlockSpec((tm, tk), lambda i, j, k: (i, k)))
    b_spec = (pl.BlockSpec((tn, tk), lambda i, j, k: (j, k)) if mode == "nt"
              else pl.BlockSpec((tk, tn), lambda i, j, k: (k, j)))
    o_spec = pl.BlockSpec((tm, tn), lambda i, j, k: (i, j))
    has_res = res is not None
    n_in = 2 + has_res + len(deps)

    def body(*refs):
        a_ref, b_ref = refs[:2]
        r_ref = refs[2] if has_res else None
        o_ref = refs[n_in]

        def finish(out):
            if has_res:
                out = out + r_ref[...]
            o_ref[...] = out.astype(out_dtype)

        if nk == 1:
            finish(_dot(a_ref[...], b_ref[...], dims))
            return
        acc = refs[n_in + 1]
        k = pl.program_id(2)

        @pl.when(k == 0)
        def _():
            acc[...] = jnp.zeros_like(acc)

        acc[...] += _dot(a_ref[...], b_ref[...], dims)

        @pl.when(k == nk - 1)
        def _():
            finish(acc[...])

    ins = [a, b] + ([res] if has_res else []) + list(deps)
    in_specs = ([a_spec, b_spec] + ([o_spec] if has_res else [])
                + [pl.BlockSpec((8, 128), lambda i, j, k: (0, 0))] * len(deps))
    return pl.pallas_call(
        body, grid=(M // tm, N // tn, nk), in_specs=in_specs, out_specs=o_spec,
        out_shape=jax.ShapeDtypeStruct((M, N), out_dtype),
        scratch_shapes=[pltpu.VMEM((tm, tn), F32)] if nk > 1 else [], name=name,
        compiler_params=_params("parallel", "parallel", "arbitrary"))(*ins)


DEP_SPEC1 = pl.BlockSpec((8, 128), lambda i: (0, 0))
DEP_SPEC2 = pl.BlockSpec((8, 128), lambda i, j: (0, 0))


def _rmsnorm_fwd(x, g, deps=()):
    S, D = x.shape
    tm = min(512, S)
    assert S % tm == 0

    def body(x_ref, g_ref, *rest):
        h_ref, ht_ref = rest[-2:]
        xv = x_ref[...]
        r = lax.rsqrt(jnp.mean(xv * xv, axis=-1, keepdims=True) + RMS_EPS)
        hv = xv * r * g_ref[...]
        h_ref[...] = hv.astype(ACT)
        ht_ref[...] = hv.T.astype(ACT)

    row = pl.BlockSpec((tm, D), lambda i: (i, 0))
    return pl.pallas_call(
        body, grid=(S // tm,), in_specs=[row, pl.BlockSpec((1, D), lambda i: (0, 0))] + [DEP_SPEC1] * len(deps),
        out_specs=[row, pl.BlockSpec((D, tm), lambda i: (0, i))],
        out_shape=[jax.ShapeDtypeStruct((S, D), ACT), jax.ShapeDtypeStruct((D, S), ACT)], name="rmsnorm_fwd",
        compiler_params=_params("parallel"))(x, g, *deps)


def _rmsnorm_bwd(x, g, dh, dres):
    S, D = x.shape
    tm = min(256, S)

    def body(x_ref, g_ref, dh_ref, dr_ref, dx_ref, dxb_ref, dg_ref):
        xv = x_ref[...]
        r = lax.rsqrt(jnp.mean(xv * xv, axis=-1, keepdims=True) + RMS_EPS)
        xh = xv * r
        dhv = dh_ref[...]
        dxh = dhv * g_ref[...]
        dx = r * (dxh - xh * jnp.mean(dxh * xh, axis=-1, keepdims=True)) + dr_ref[...]
        dx_ref[...] = dx
        dxb_ref[...] = dx.astype(ACT)

        @pl.when(pl.program_id(0) == 0)
        def _():
            dg_ref[...] = jnp.zeros_like(dg_ref)

        dg_ref[...] += jnp.sum(dhv * xh, axis=0, keepdims=True)

    row = pl.BlockSpec((tm, D), lambda i: (i, 0))
    vec = pl.BlockSpec((1, D), lambda i: (0, 0))
    return pl.pallas_call(
        body, grid=(S // tm,), in_specs=[row, vec, row, row], out_specs=[row, row, vec],
        out_shape=[jax.ShapeDtypeStruct((S, D), F32), jax.ShapeDtypeStruct((S, D), ACT),
                   jax.ShapeDtypeStruct((1, D), F32)],
        name="rmsnorm_bwd", compiler_params=_params("arbitrary"))(x, g, dh, dres)


def _rot256(x, c, s):
    x1, x2 = x[:, :128], x[:, 128:]
    return jnp.concatenate([x1 * c - x2 * s, x2 * c + x1 * s], axis=1)


def _rot256_t(g, c, s):
    g1, g2 = g[:, :128], g[:, 128:]
    return jnp.concatenate([g1 * c + g2 * s, g2 * c - g1 * s], axis=1)


def _log_decay(a_ref, h, shape):
    return -jnp.exp(jnp.full(shape, a_ref[h], F32))


def _ret_state(xsrc, xbase, xscale, ysrc, ybase, cos, sin, af, ab, mode, name):
    S = xsrc.shape[0]
    nC = S // CH
    H = RET_HEADS

    def body(x1_ref, y1_ref, c1_ref, s1_ref, x2_ref, y2_ref, c2_ref, s2_ref, af_ref, ab_ref,
             st1_ref, st2_ref, acc1, acc2):
        h = pl.program_id(0)

        @pl.when(pl.program_id(1) == 0)
        def _():
            acc1[...] = jnp.zeros_like(acc1)
            acc2[...] = jnp.zeros_like(acc2)

        lgf = _log_decay(af_ref, h, (CH, 1))
        lgb = _log_decay(ab_ref, h, (CH, 1))
        j = _fiota((CH, 1), 0)
        if mode == "fwd":
            w1, d1 = jnp.exp(lgf * (CH - 1.0 - j)), jnp.exp(lgf[:1] * CH)
            w2, d2 = jnp.exp(lgb * j), jnp.exp(lgb[:1] * CH)
        else:
            w1, d1 = jnp.exp(lgb * (CH - j)), jnp.exp(lgb[:1] * CH)
            w2, d2 = jnp.exp(lgf * (j + 1.0)), jnp.exp(lgf[:1] * CH)
        xa = _rot256(x1_ref[...], c1_ref[...], s1_ref[...]) * xscale
        st1_ref[...] = acc1[...]
        acc1[...] = d1 * acc1[...] + _dot(xa * w1, y1_ref[...], TN)
        xb = _rot256(x2_ref[...], c2_ref[...], s2_ref[...]) * xscale
        st2_ref[...] = acc2[...]
        acc2[...] = d2 * acc2[...] + _dot(xb * w2, y2_ref[...], TN)

    def fw(col):
        return lambda h, c: (c, col + h)

    def rv(col):
        return lambda h, c: (nC - 1 - c, col + h)

    in_specs = [
        pl.BlockSpec((CH, RET_HD), fw(xbase)), pl.BlockSpec((CH, RET_HD), fw(ybase)),
        pl.BlockSpec((CH, 128), lambda h, c: (c, 0)), pl.BlockSpec((CH, 128), lambda h, c: (c, 0)),
        pl.BlockSpec((CH, RET_HD), rv(xbase)), pl.BlockSpec((CH, RET_HD), rv(ybase)),
        pl.BlockSpec((CH, 128), lambda h, c: (nC - 1 - c, 0)),
        pl.BlockSpec((CH, 128), lambda h, c: (nC - 1 - c, 0)),
        SMEM_SPEC, SMEM_SPEC]
    out_specs = [pl.BlockSpec((None, None, RET_HD, RET_HD), lambda h, c: (h, c, 0, 0)),
                 pl.BlockSpec((None, None, RET_HD, RET_HD), lambda h, c: (h, nC - 1 - c, 0, 0))]
    st = jax.ShapeDtypeStruct((H, nC, RET_HD, RET_HD), F32)
    return pl.pallas_call(
        body, grid=(H, nC), in_specs=in_specs, out_specs=out_specs, out_shape=[st, st],
        scratch_shapes=[pltpu.VMEM((RET_HD, RET_HD), F32), pltpu.VMEM((RET_HD, RET_HD), F32)],
        name=name, compiler_params=_params("parallel", "arbitrary"))(
            xsrc, ysrc, cos, sin, xsrc, ysrc, cos, sin, af, ab)


def _decay_mask(lgf1, lgb1):
    lag = _fiota((CH, CH), 0) - _fiota((CH, CH), 1)
    alag = jnp.abs(lag)
    return lag, jnp.where(lag >= 0, jnp.exp(lgf1 * alag), jnp.exp(lgb1 * alag))


def _ret_fwd(z, cos, sin, sf, sb, af, ab, cols):
    S = z.shape[0]
    nC = S // CH
    H = RET_HEADS
    rq, rk, rv, rg = cols
    W = H * RET_HD

    def body(q_ref, k_ref, v_ref, g_ref, c_ref, s_ref, sf_ref, sb_ref, af_ref, ab_ref,
             o_ref, u_ref, ut_ref):
        h = pl.program_id(0)
        lgf = _log_decay(af_ref, h, (CH, 1))
        lgb = _log_decay(ab_ref, h, (CH, 1))
        j = _fiota((CH, 1), 0)
        c, s = c_ref[...], s_ref[...]
        q = _rot256(q_ref[...], c, s)
        k = _rot256(k_ref[...], c, s) * (RET_HD ** -0.5)
        _, dm = _decay_mask(lgf[:1], lgb[:1])
        p = _dot(q, k, NT) * dm
        o = (_dot(p, v_ref[...], NN)
             + _dot(q * jnp.exp(lgf * (j + 1.0)), sf_ref[...], NN)
             + _dot(q * jnp.exp(lgb * (CH - j)), sb_ref[...], NN))
        o_ref[...] = o
        on = o * lax.rsqrt(jnp.mean(o * o, axis=-1, keepdims=True) + RMS_EPS)
        g = g_ref[...]
        u = on * (g * _sigmoid(g))
        u_ref[...] = u.astype(ACT)
        ut_ref[...] = u.T.astype(ACT)

    def zc(col):
        return pl.BlockSpec((CH, RET_HD), lambda h, c: (c, col + h))

    tab = pl.BlockSpec((CH, 128), lambda h, c: (c, 0))
    stt = pl.BlockSpec((None, None, RET_HD, RET_HD), lambda h, c: (h, c, 0, 0))
    out = pl.BlockSpec((CH, RET_HD), lambda h, c: (c, h))
    return pl.pallas_call(
        body, grid=(H, nC),
        in_specs=[zc(rq), zc(rk), zc(rv), zc(rg), tab, tab, stt, stt, SMEM_SPEC, SMEM_SPEC],
        out_specs=[out, out, pl.BlockSpec((RET_HD, CH), lambda h, c: (h, c))],
        out_shape=[jax.ShapeDtypeStruct((S, W), F32), jax.ShapeDtypeStruct((S, W), ACT),
                   jax.ShapeDtypeStruct((W, S), ACT)],
        name="ret_fwd", compiler_params=_params("parallel", "parallel"))(
            z, z, z, z, cos, sin, sf, sb, af, ab)


def _ret_gate_bwd(du, o_pre, z, rg):
    S, W = du.shape
    H = RET_HEADS
    tm = min(512, S)
    assert S % tm == 0

    def body(du_ref, o_ref, g_ref, do_ref, dg_ref):
        o = o_ref[...]
        r = lax.rsqrt(jnp.mean(o * o, axis=-1, keepdims=True) + RMS_EPS)
        on = o * r
        g = g_ref[...]
        sg = _sigmoid(g)
        duv = du_ref[...]
        don = duv * (g * sg)
        dg_ref[...] = (duv * on * (sg * (1.0 + g * (1.0 - sg)))).astype(ACT)
        do_ref[...] = r * (don - on * jnp.mean(don * on, axis=-1, keepdims=True))

    blk = pl.BlockSpec((tm, RET_HD), lambda i, h: (i, h))
    return pl.pallas_call(
        body, grid=(S // tm, H),
        in_specs=[blk, blk, pl.BlockSpec((tm, RET_HD), lambda i, h: (i, rg + h))],
        out_specs=[blk, blk],
        out_shape=[jax.ShapeDtypeStruct((S, W), F32), jax.ShapeDtypeStruct((S, W), ACT)],
        name="ret_gate_bwd", compiler_params=_params("parallel", "parallel"))(du, o_pre, z)


def _ret_bwd(z, do, cos, sin, sf, sb, ef, eb, af, ab, cols):
    S = z.shape[0]
    nC = S // CH
    H = RET_HEADS
    rq, rk, rv, _ = cols
    W = H * RET_HD

    def body(q_ref, k_ref, v_ref, do_ref, c_ref, s_ref, sf_ref, sb_ref, ef_ref, eb_ref,
             af_ref, ab_ref, dq_ref, dk_ref, dv_ref, da_ref):
        h = pl.program_id(0)
        lgf = _log_decay(af_ref, h, (CH, 1))
        lgb = _log_decay(ab_ref, h, (CH, 1))
        j = _fiota((CH, 1), 0)
        c, s = c_ref[...], s_ref[...]
        scale = RET_HD ** -0.5
        q = _rot256(q_ref[...], c, s)
        k = _rot256(k_ref[...], c, s) * scale
        v = v_ref[...]
        do = do_ref[...]
        sf_, sb_, ef_, eb_ = sf_ref[...], sb_ref[...], ef_ref[...], eb_ref[...]
        a_w = jnp.exp(lgf * (j + 1.0))
        b_w = jnp.exp(lgb * (CH - j))
        wf = jnp.exp(lgf * (CH - 1.0 - j))
        wb = jnp.exp(lgb * j)
        lag, dm = _decay_mask(lgf[:1], lgb[:1])
        sc = _dot(q, k, NT)
        gg = _dot(do, v, NT)
        dg = gg * dm
        x1 = _dot(do, sf_, NT) * a_w
        x2 = _dot(do, sb_, NT) * b_w
        y1 = _dot(v, ef_, NT) * wf
        y2 = _dot(v, eb_, NT) * wb
        dq = _dot(dg, k, NN) + x1 + x2
        dk = _dot(dg, q, TN) + y1 + y2
        dv = _dot(sc * dm, do, TN) + _dot(k * wf, ef_, NN) + _dot(k * wb, eb_, NN)
        dq_ref[...] = _rot256_t(dq, c, s).astype(ACT)
        dk_ref[...] = (_rot256_t(dk, c, s) * scale).astype(ACT)
        dv_ref[...] = dv.astype(ACT)
        t = dm * gg * sc
        qx1 = jnp.sum(q * x1, axis=-1, keepdims=True)
        qx2 = jnp.sum(q * x2, axis=-1, keepdims=True)
        ky1 = jnp.sum(k * y1, axis=-1, keepdims=True)
        ky2 = jnp.sum(k * y2, axis=-1, keepdims=True)
        dlf = (_sum_all(jnp.where(lag > 0, lag * t, 0.0))
               + _sum_all((j + 1.0) * qx1 + (CH - 1.0 - j) * ky1)
               + CH * jnp.exp(lgf[:1] * CH) * _sum_all(ef_ * sf_))
        dlb = (_sum_all(jnp.where(lag < 0, -lag * t, 0.0))
               + _sum_all((CH - j) * qx2 + j * ky2)
               + CH * jnp.exp(lgb[:1] * CH) * _sum_all(eb_ * sb_))
        row = lax.broadcasted_iota(jnp.int32, (8, 128), 0)
        lane = lax.broadcasted_iota(jnp.int32, (8, 128), 1)
        tile = jnp.where((row == 0) & (lane == 0), dlf * lgf[:1],
                         jnp.where((row == 0) & (lane == 1), dlb * lgb[:1], 0.0))

        @pl.when(pl.program_id(1) == 0)
        def _():
            da_ref[...] = jnp.zeros_like(da_ref)

        da_ref[...] += tile

    def zc(col):
        return pl.BlockSpec((CH, RET_HD), lambda h, c: (c, col + h))

    tab = pl.BlockSpec((CH, 128), lambda h, c: (c, 0))
    stt = pl.BlockSpec((None, None, RET_HD, RET_HD), lambda h, c: (h, c, 0, 0))
    out = pl.BlockSpec((CH, RET_HD), lambda h, c: (c, h))
    dz = jax.ShapeDtypeStruct((S, W), ACT)
    return pl.pallas_call(
        body, grid=(H, nC),
        in_specs=[zc(rq), zc(rk), zc(rv), out, tab, tab, stt, stt, stt, stt, SMEM_SPEC, SMEM_SPEC],
        out_specs=[out, out, out, pl.BlockSpec((None, 8, 128), lambda h, c: (h, 0, 0))],
        out_shape=[dz, dz, dz, jax.ShapeDtypeStruct((H, 8, 128), F32)],
        name="ret_bwd", compiler_params=_params("parallel", "arbitrary"))(
            z, z, z, do, cos, sin, sf, sb, ef, eb, af, ab)


def _fill_padded(pad_ref, src_ref, S):
    zeros = jnp.zeros((POOL_PAD, POOL_GD), F32)
    pad_ref[pl.ds(0, POOL_PAD), :] = zeros
    pad_ref[pl.ds(POOL_PAD, S), :] = src_ref[...]
    pad_ref[pl.ds(S + POOL_PAD, POOL_PAD), :] = zeros


def _window_sum(ext, T, lo, hi):
    n = T + 2 * POOL_PAD
    acc = None
    for k in range(lo, hi):
        sh = ext if k == 0 else pltpu.roll(ext, (-k) % n, 0)
        piece = sh[POOL_PAD:POOL_PAD + T]
        acc = piece if acc is None else acc + piece
    return acc


def _window_count(pos, w, S):
    lo = jnp.maximum(pos - w // 2, 0)
    hi = jnp.minimum(pos + w // 2, S)
    return (hi - lo).astype(F32)


def _pool_fwd(z, pw, scale, pv, pg):
    S = z.shape[0]
    G = len(POOL_WINDOWS)
    T = min(512, S)
    W = G * POOL_GD

    def body(x_ref, g_ref, pw_ref, sc_ref, y_ref, u_ref, ut_ref, pad, p_scr):
        grp = pl.program_id(0)
        i = pl.program_id(1)

        @pl.when(i == 0)
        def _():
            _fill_padded(pad, x_ref, S)

        r0 = pl.multiple_of(i * T, T)
        ext = pad[pl.ds(r0, T + 2 * POOL_PAD), :]
        pos = r0 + lax.broadcasted_iota(jnp.int32, (T, 1), 0)
        for gi, w in enumerate(POOL_WINDOWS):
            @pl.when(grp == gi)
            def _(w=w):
                acc = _window_sum(ext, T, -(w // 2), w // 2)
                p_scr[...] = acc / _window_count(pos, w, S) - ext[POOL_PAD:POOL_PAD + T]

        y = _dot(p_scr[...], pw_ref[...], NN)
        y_ref[...] = y
        g = g_ref[...]
        u = y * sc_ref[...] * (g * _sigmoid(g))
        u_ref[...] = u.astype(ACT)
        ut_ref[...] = u.T.astype(ACT)

    blk = pl.BlockSpec((T, POOL_GD), lambda g, i: (i, g))
    return pl.pallas_call(
        body, grid=(G, S // T),
        in_specs=[pl.BlockSpec((S, POOL_GD), lambda g, i: (0, pv + g)),
                  pl.BlockSpec((T, POOL_GD), lambda g, i: (i, pg + g)),
                  pl.BlockSpec((None, POOL_GD, POOL_GD), lambda g, i: (g, 0, 0)),
                  pl.BlockSpec((1, POOL_GD), lambda g, i: (0, g))],
        out_specs=[blk, blk, pl.BlockSpec((POOL_GD, T), lambda g, i: (g, i))],
        out_shape=[jax.ShapeDtypeStruct((S, W), F32), jax.ShapeDtypeStruct((S, W), ACT),
                   jax.ShapeDtypeStruct((W, S), ACT)],
        scratch_shapes=[pltpu.VMEM((S + 2 * POOL_PAD, POOL_GD), F32), pltpu.VMEM((T, POOL_GD), F32)],
        name="pool_fwd", compiler_params=_params("parallel", "arbitrary"))(z, z, pw, scale)


def _pool_bwd_a(z, pw, scale, y_raw, du, pv, pg):
    S = z.shape[0]
    G = len(POOL_WINDOWS)
    T = min(512, S)
    W = G * POOL_GD

    def body(x_ref, g_ref, pw_ref, sc_ref, y_ref, du_ref, dpc_ref, dg_ref, dsc_ref, dpw_ref,
             pad, p_scr, c_scr):
        grp = pl.program_id(0)
        i = pl.program_id(1)

        @pl.when(i == 0)
        def _():
            _fill_padded(pad, x_ref, S)
            dsc_ref[...] = jnp.zeros_like(dsc_ref)
            dpw_ref[...] = jnp.zeros_like(dpw_ref)

        r0 = pl.multiple_of(i * T, T)
        ext = pad[pl.ds(r0, T + 2 * POOL_PAD), :]
        pos = r0 + lax.broadcasted_iota(jnp.int32, (T, 1), 0)
        for gi, w in enumerate(POOL_WINDOWS):
            @pl.when(grp == gi)
            def _(w=w):
                cnt = _window_count(pos, w, S)
                acc = _window_sum(ext, T, -(w // 2), w // 2)
                p_scr[...] = acc / cnt - ext[POOL_PAD:POOL_PAD + T]
                c_scr[...] = jnp.broadcast_to(cnt, (T, 128))

        g = g_ref[...]
        sg = _sigmoid(g)
        duv = du_ref[...]
        y = y_ref[...]
        scl = sc_ref[...]
        dy = duv * (scl * (g * sg))
        dg_ref[...] = (duv * y * scl * (sg * (1.0 + g * (1.0 - sg)))).astype(ACT)
        dsc_ref[...] += jnp.sum(duv * y * (g * sg), axis=0, keepdims=True)
        dpw_ref[...] += _dot(p_scr[...], dy, TN)
        dpc_ref[...] = _dot(dy, pw_ref[...], NT) / c_scr[:, :1]

    blk = pl.BlockSpec((T, POOL_GD), lambda g, i: (i, g))
    return pl.pallas_call(
        body, grid=(G, S // T),
        in_specs=[pl.BlockSpec((S, POOL_GD), lambda g, i: (0, pv + g)),
                  pl.BlockSpec((T, POOL_GD), lambda g, i: (i, pg + g)),
                  pl.BlockSpec((None, POOL_GD, POOL_GD), lambda g, i: (g, 0, 0)),
                  pl.BlockSpec((1, POOL_GD), lambda g, i: (0, g)), blk, blk],
        out_specs=[blk, blk, pl.BlockSpec((1, POOL_GD), lambda g, i: (0, g)),
                   pl.BlockSpec((None, POOL_GD, POOL_GD), lambda g, i: (g, 0, 0))],
        out_shape=[jax.ShapeDtypeStruct((S, W), F32), jax.ShapeDtypeStruct((S, W), ACT),
                   jax.ShapeDtypeStruct((1, W), F32), jax.ShapeDtypeStruct((G, POOL_GD, POOL_GD), F32)],
        scratch_shapes=[pltpu.VMEM((S + 2 * POOL_PAD, POOL_GD), F32), pltpu.VMEM((T, POOL_GD), F32),
                        pltpu.VMEM((T, 128), F32)],
        name="pool_bwd_a", compiler_params=_params("parallel", "arbitrary"))(z, z, pw, scale, y_raw, du)


def _pool_bwd_b(dpc):
    S, W = dpc.shape
    G = len(POOL_WINDOWS)
    T = min(512, S)

    def body(x_ref, o_ref, pad, acc_scr):
        grp = pl.program_id(0)
        i = pl.program_id(1)

        @pl.when(i == 0)
        def _():
            _fill_padded(pad, x_ref, S)

        r0 = pl.multiple_of(i * T, T)
        ext = pad[pl.ds(r0, T + 2 * POOL_PAD), :]
        pos = r0 + lax.broadcasted_iota(jnp.int32, (T, 1), 0)
        for gi, w in enumerate(POOL_WINDOWS):
            @pl.when(grp == gi)
            def _(w=w):
                acc = _window_sum(ext, T, -(w // 2) + 1, w // 2 + 1)
                acc_scr[...] = acc - ext[POOL_PAD:POOL_PAD + T] * _window_count(pos, w, S)

        o_ref[...] = acc_scr[...].astype(ACT)

    blk = pl.BlockSpec((T, POOL_GD), lambda g, i: (i, g))
    return pl.pallas_call(
        body, grid=(G, S // T),
        in_specs=[pl.BlockSpec((S, POOL_GD), lambda g, i: (0, g))], out_specs=blk,
        out_shape=jax.ShapeDtypeStruct((S, W), ACT),
        scratch_shapes=[pltpu.VMEM((S + 2 * POOL_PAD, POOL_GD), F32), pltpu.VMEM((T, POOL_GD), F32)],
        name="pool_bwd_b", compiler_params=_params("parallel", "arbitrary"))(dpc)


def _rope128(x, cf, sa, sb):
    return x * cf + pltpu.roll(x, ROPE_HALF, 1) * sa + pltpu.roll(x, ATT_HD - ROPE_HALF, 1) * sb


def _rope128_t(g, cf, sa, sb):
    return g * cf + pltpu.roll(g * sa, ATT_HD - ROPE_HALF, 1) + pltpu.roll(g * sb, ROPE_HALF, 1)


def _attn_prep(z, qgain, kgain, cf, sa, sb, aq, ak, av):
    S = z.shape[0]
    T = min(512, S)
    QW, KW = ATT_Q * ATT_HD, ATT_KV * ATT_HD

    def body(q_ref, k_ref, v_ref, qg_ref, kg_ref, cf_ref, sa_ref, sb_ref, qn_ref, kn_ref, vb_ref):
        cfv, sav, sbv = cf_ref[...], sa_ref[...], sb_ref[...]

        def prep(x, gain):
            r = lax.rsqrt(jnp.mean(x * x, axis=-1, keepdims=True) + RMS_EPS)
            return _rope128(x * r * gain, cfv, sav, sbv)

        for hh in range(ATT_Q):
            sl = slice(hh * ATT_HD, (hh + 1) * ATT_HD)
            qn_ref[:, sl] = prep(q_ref[:, sl], qg_ref[...]).astype(ACT)
        for hh in range(ATT_KV):
            sl = slice(hh * ATT_HD, (hh + 1) * ATT_HD)
            kn_ref[:, sl] = prep(k_ref[:, sl], kg_ref[...]).astype(ACT)
        vb_ref[...] = v_ref[...].astype(ACT)

    tab = pl.BlockSpec((T, ATT_HD), lambda i: (i, 0))
    gain = pl.BlockSpec((1, ATT_HD), lambda i: (0, 0))
    return pl.pallas_call(
        body, grid=(S // T,),
        in_specs=[pl.BlockSpec((T, QW), lambda i: (i, aq)), pl.BlockSpec((T, KW), lambda i: (i, ak)),
                  pl.BlockSpec((T, KW), lambda i: (i, av)), gain, gain, tab, tab, tab],
        out_specs=[pl.BlockSpec((T, QW), lambda i: (i, 0)), pl.BlockSpec((T, KW), lambda i: (i, 0)),
                   pl.BlockSpec((T, KW), lambda i: (i, 0))],
        out_shape=[jax.ShapeDtypeStruct((S, QW), ACT), jax.ShapeDtypeStruct((S, KW), ACT),
                   jax.ShapeDtypeStruct((S, KW), ACT)],
        name="attn_prep", compiler_params=_params("parallel"))(z, z, z, qgain, kgain, cf, sa, sb)


def _attn_window(i, S):
    start = jnp.clip(i * ATT_BLK - ATT_BLK, 0, S - ATT_SPAN)
    start = pl.multiple_of(start, ATT_BLK)
    qpos = i * ATT_BLK + lax.broadcasted_iota(jnp.int32, (ATT_BLK, ATT_SPAN), 0)
    kpos = start + lax.broadcasted_iota(jnp.int32, (ATT_BLK, ATT_SPAN), 1)
    return start, jnp.abs(kpos - qpos) <= ATT_WIN


def _attn_probs(q, kw, valid, sink):
    s = _dot(q, kw, NT) * (ATT_HD ** -0.5)
    s = jnp.where(valid, s, NEG_BIG)
    m = jnp.maximum(jnp.max(s, axis=-1, keepdims=True), sink)
    p = jnp.exp(s - m)
    es = jnp.exp(sink - m)
    den = jnp.sum(p, axis=-1, keepdims=True) + es
    return p / den, es / den


def _attn_fwd(qn, kn, vb, z, sink, ag):
    S = qn.shape[0]
    nB = S // ATT_BLK
    assert S >= ATT_SPAN
    QW = ATT_Q * ATT_HD

    def body(q_ref, k_ref, v_ref, g_ref, sink_ref, o_ref, u_ref, ut_ref):
        h = pl.program_id(0)
        i = pl.program_id(1)
        start, valid = _attn_window(i, S)
        kw = k_ref[pl.ds(start, ATT_SPAN), :]
        vw = v_ref[pl.ds(start, ATT_SPAN), :]
        sk = jnp.full((ATT_BLK, 1), sink_ref[h], F32)
        pn, _ = _attn_probs(q_ref[...], kw, valid, sk)
        o = _dot(pn, vw, NN)
        o_ref[...] = o
        g = g_ref[...]
        u = o * (g * _sigmoid(g))
        u_ref[...] = u.astype(ACT)
        ut_ref[...] = u.T.astype(ACT)

    blk = pl.BlockSpec((ATT_BLK, ATT_HD), lambda h, i: (i, h))
    kv = pl.BlockSpec((S, ATT_HD), lambda h, i: (0, h // ATT_G))
    return pl.pallas_call(
        body, grid=(ATT_Q, nB),
        in_specs=[blk, kv, kv, pl.BlockSpec((ATT_BLK, ATT_HD), lambda h, i: (i, ag + h)), SMEM_SPEC],
        out_specs=[blk, blk, pl.BlockSpec((ATT_HD, ATT_BLK), lambda h, i: (h, i))],
        out_shape=[jax.ShapeDtypeStruct((S, QW), F32), jax.ShapeDtypeStruct((S, QW), ACT),
                   jax.ShapeDtypeStruct((QW, S), ACT)],
        name="attn_fwd", compiler_params=_params("parallel", "parallel"))(qn, kn, vb, z, sink)


def _attn_bwd(qn, kn, vb, o, du, z, sink, ag):
    S = qn.shape[0]
    nB = S // ATT_BLK
    QW, KW = ATT_Q * ATT_HD, ATT_KV * ATT_HD
    GW = ATT_G * ATT_HD

    def body(q_ref, k_ref, v_ref, o_ref, du_ref, g_ref, sink_ref,
             dq_ref, dk_ref, dv_ref, dg_ref, ds_ref):
        kvh = pl.program_id(0)
        i = pl.program_id(1)

        @pl.when(i == 0)
        def _():
            dk_ref[...] = jnp.zeros_like(dk_ref)
            dv_ref[...] = jnp.zeros_like(dv_ref)
            ds_ref[...] = jnp.zeros_like(ds_ref)

        start, valid = _attn_window(i, S)
        kw = k_ref[pl.ds(start, ATT_SPAN), :]
        vw = v_ref[pl.ds(start, ATT_SPAN), :]
        row = lax.broadcasted_iota(jnp.int32, (8, 128), 0)
        lane = lax.broadcasted_iota(jnp.int32, (8, 128), 1)
        dsink = jnp.zeros((8, 128), F32)
        dk_acc = jnp.zeros((ATT_SPAN, ATT_HD), F32)
        dv_acc = jnp.zeros((ATT_SPAN, ATT_HD), F32)
        for gi in range(ATT_G):
            sl = slice(gi * ATT_HD, (gi + 1) * ATT_HD)
            q = q_ref[:, sl]
            ov = o_ref[:, sl]
            g = g_ref[:, sl]
            duv = du_ref[:, sl]
            sg = _sigmoid(g)
            do = duv * (g * sg)
            dg_ref[:, sl] = (duv * ov * (sg * (1.0 + g * (1.0 - sg)))).astype(ACT)
            sk = jnp.full((ATT_BLK, 1), sink_ref[kvh * ATT_G + gi], F32)
            pn, psink = _attn_probs(q, kw, valid, sk)
            delta = jnp.sum(do * ov, axis=-1, keepdims=True)
            dsc = pn * (_dot(do, vw, NT) - delta) * (ATT_HD ** -0.5)
            dq_ref[:, sl] = _dot(dsc, kw, NN)
            dk_acc = dk_acc + _dot(dsc, q, TN)
            dv_acc = dv_acc + _dot(pn, do, TN)
            dsink = dsink + jnp.where((row == 0) & (lane == gi), -_sum_all(psink * delta), 0.0)
        dk_ref[pl.ds(start, ATT_SPAN), :] += dk_acc
        dv_ref[pl.ds(start, ATT_SPAN), :] += dv_acc
        ds_ref[...] += dsink

    grp = pl.BlockSpec((ATT_BLK, GW), lambda k, i: (i, k))
    kv = pl.BlockSpec((S, ATT_HD), lambda k, i: (0, k))
    return pl.pallas_call(
        body, grid=(ATT_KV, nB),
        in_specs=[grp, kv, kv, grp, grp,
                  pl.BlockSpec((ATT_BLK, GW), lambda k, i: (i, ag // ATT_G + k)), SMEM_SPEC],
        out_specs=[grp, kv, kv, grp, pl.BlockSpec((None, 8, 128), lambda k, i: (k, 0, 0))],
        out_shape=[jax.ShapeDtypeStruct((S, QW), F32), jax.ShapeDtypeStruct((S, KW), F32),
                   jax.ShapeDtypeStruct((S, KW), F32), jax.ShapeDtypeStruct((S, QW), ACT),
                   jax.ShapeDtypeStruct((ATT_KV, 8, 128), F32)],
        name="attn_bwd", compiler_params=_params("parallel", "arbitrary"))(qn, kn, vb, o, du, z, sink)


def _attn_prep_bwd(z, dqn, dkn, dv, qgain, kgain, cf, sa, sb, aq, ak):
    S = z.shape[0]
    T = min(512, S)
    QW, KW = ATT_Q * ATT_HD, ATT_KV * ATT_HD

    def body(q_ref, k_ref, dqn_ref, dkn_ref, dv_ref, qg_ref, kg_ref, cf_ref, sa_ref, sb_ref,
             dq_ref, dk_ref, dvb_ref, dqg_ref, dkg_ref):
        cfv, sav, sbv = cf_ref[...], sa_ref[...], sb_ref[...]

        @pl.when(pl.program_id(0) == 0)
        def _():
            dqg_ref[...] = jnp.zeros_like(dqg_ref)
            dkg_ref[...] = jnp.zeros_like(dkg_ref)

        def back(x, gn, gain):
            r = lax.rsqrt(jnp.mean(x * x, axis=-1, keepdims=True) + RMS_EPS)
            xh = x * r
            dy = _rope128_t(gn, cfv, sav, sbv)
            dxh = dy * gain
            dx = r * (dxh - xh * jnp.mean(dxh * xh, axis=-1, keepdims=True))
            return dx, jnp.sum(dy * xh, axis=0, keepdims=True)

        dqg = jnp.zeros((1, ATT_HD), F32)
        for hh in range(ATT_Q):
            sl = slice(hh * ATT_HD, (hh + 1) * ATT_HD)
            dx, dgn = back(q_ref[:, sl], dqn_ref[:, sl], qg_ref[...])
            dq_ref[:, sl] = dx.astype(ACT)
            dqg = dqg + dgn
        dkg = jnp.zeros((1, ATT_HD), F32)
        for hh in range(ATT_KV):
            sl = slice(hh * ATT_HD, (hh + 1) * ATT_HD)
            dx, dgn = back(k_ref[:, sl], dkn_ref[:, sl], kg_ref[...])
            dk_ref[:, sl] = dx.astype(ACT)
            dkg = dkg + dgn
        dvb_ref[...] = dv_ref[...].astype(ACT)
        dqg_ref[...] += dqg
        dkg_ref[...] += dkg

    tab = pl.BlockSpec((T, ATT_HD), lambda i: (i, 0))
    gain = pl.BlockSpec((1, ATT_HD), lambda i: (0, 0))
    qb = pl.BlockSpec((T, QW), lambda i: (i, 0))
    kb = pl.BlockSpec((T, KW), lambda i: (i, 0))
    return pl.pallas_call(
        body, grid=(S // T,),
        in_specs=[pl.BlockSpec((T, QW), lambda i: (i, aq)), pl.BlockSpec((T, KW), lambda i: (i, ak)),
                  qb, kb, kb, gain, gain, tab, tab, tab],
        out_specs=[qb, kb, kb, gain, gain],
        out_shape=[jax.ShapeDtypeStruct((S, QW), ACT), jax.ShapeDtypeStruct((S, KW), ACT),
                   jax.ShapeDtypeStruct((S, KW), ACT), jax.ShapeDtypeStruct((1, ATT_HD), F32),
                   jax.ShapeDtypeStruct((1, ATT_HD), F32)],
        name="attn_prep_bwd", compiler_params=_params("arbitrary"))(
            z, z, dqn, dkn, dv, qgain, kgain, cf, sa, sb)


def _branch_merge(ua, ub, uc, wr, wp, wa, z, mg):
    S, W = ua.shape
    D = wr.shape[1]
    tm, tn = min(512, S), min(512, D)
    nb = D // tn

    def body(ua_ref, ub_ref, uc_ref, wr_ref, wp_ref, wa_ref, g0_ref, g1_ref, g2_ref,
             ya_ref, yb_ref, yc_ref, m_ref, mt_ref):
        ya = _dot(ua_ref[...], wr_ref[...], NN)
        yb = _dot(ub_ref[...], wp_ref[...], NN)
        yc = _dot(uc_ref[...], wa_ref[...], NN)
        ya_ref[...] = ya.astype(ACT)
        yb_ref[...] = yb.astype(ACT)
        yc_ref[...] = yc.astype(ACT)
        m = _sigmoid(g0_ref[...]) * ya + _sigmoid(g1_ref[...]) * yb + _sigmoid(g2_ref[...]) * yc
        m_ref[...] = m.astype(ACT)
        mt_ref[...] = m.T.astype(ACT)

    u = pl.BlockSpec((tm, W), lambda i, j: (i, 0))
    w = pl.BlockSpec((W, tn), lambda i, j: (0, j))
    o = pl.BlockSpec((tm, tn), lambda i, j: (i, j))

    assert (mg * POOL_GD) % tn == 0
    base = (mg * POOL_GD) // tn

    def gate(k):
        return pl.BlockSpec((tm, tn), lambda i, j: (i, base + k * nb + j))

    sd = jax.ShapeDtypeStruct((S, D), ACT)
    return pl.pallas_call(
        body, grid=(S // tm, nb), in_specs=[u, u, u, w, w, w, gate(0), gate(1), gate(2)],
        out_specs=[o, o, o, o, pl.BlockSpec((tn, tm), lambda i, j: (j, i))],
        out_shape=[sd, sd, sd, sd, jax.ShapeDtypeStruct((D, S), ACT)], name="branch_merge",
        compiler_params=_params("parallel", "parallel"))(ua, ub, uc, wr, wp, wa, z, z, z)


def _merge_bwd(dxb, wo, ya, yb, yc, z, mg):
    S, D = dxb.shape
    tm, tn = min(512, S), min(512, D)
    nb = D // tn
    base = (mg * POOL_GD) // tn

    def body(dx_ref, wo_ref, ya_ref, yb_ref, yc_ref, g0_ref, g1_ref, g2_ref,
             da_ref, db_ref, dc_ref, dg0_ref, dg1_ref, dg2_ref):
        dm = _dot(dx_ref[...], wo_ref[...], NT)
        for y_ref, g_ref, dy_ref, dg_ref in ((ya_ref, g0_ref, da_ref, dg0_ref),
                                             (yb_ref, g1_ref, db_ref, dg1_ref),
                                             (yc_ref, g2_ref, dc_ref, dg2_ref)):
            sg = _sigmoid(g_ref[...])
            dy_ref[...] = (sg * dm).astype(ACT)
            dg_ref[...] = (dm * y_ref[...].astype(F32) * (sg * (1.0 - sg))).astype(ACT)

    o = pl.BlockSpec((tm, tn), lambda i, j: (i, j))

    def gate(k):
        return pl.BlockSpec((tm, tn), lambda i, j: (i, base + k * nb + j))

    sd = jax.ShapeDtypeStruct((S, D), ACT)
    return pl.pallas_call(
        body, grid=(S // tm, nb),
        in_specs=[pl.BlockSpec((tm, D), lambda i, j: (i, 0)), pl.BlockSpec((tn, D), lambda i, j: (j, 0)),
                  o, o, o, gate(0), gate(1), gate(2)],
        out_specs=[o] * 6, out_shape=[sd] * 6, name="merge_bwd",
        compiler_params=_params("parallel", "parallel"))(dxb, wo, ya, yb, yc, z, z, z)


def _loss_head(y, t):
    S, D = y.shape
    tm = min(256, S)

    def body(y_ref, t_ref, dy_ref, dyb_ref, l_ref):
        e = y_ref[...] - t_ref[...]
        dy = e * (1.0 / D)
        dy_ref[...] = dy
        dyb_ref[...] = dy.astype(ACT)

        @pl.when(pl.program_id(0) == 0)
        def _():
            l_ref[...] = jnp.zeros_like(l_ref)

        l_ref[...] += jnp.sum(jnp.mean(e * e, axis=-1, keepdims=True), axis=0, keepdims=True)

    row = pl.BlockSpec((tm, D), lambda i: (i, 0))
    return pl.pallas_call(
        body, grid=(S // tm,), in_specs=[row, row],
        out_specs=[row, row, pl.BlockSpec((1, 1), lambda i: (0, 0))],
        out_shape=[jax.ShapeDtypeStruct((S, D), F32), jax.ShapeDtypeStruct((S, D), ACT),
                   jax.ShapeDtypeStruct((1, 1), F32)],
        name="loss_head", compiler_params=_params("arbitrary"))(y, t)


def _adamw(w, layer, parts, m, v, name):
    L = w.shape[0]
    shape = w.shape[1:]
    C = shape[-1]
    R = int(np.prod(shape[:-1]))
    w3, m3, v3 = (a.reshape(L, R, C) for a in (w, m, v))
    p3 = parts.reshape(N_DEV, R, C)
    tr = min(64, R)
    assert R % tr == 0
    c1 = 1.0 / (1.0 - ADAM_B1 ** ADAM_STEP)
    c2 = 1.0 / (1.0 - ADAM_B2 ** ADAM_STEP)

    def body(w_ref, p_ref, m_ref, v_ref, g_ref, d_ref, nm_ref, nv_ref):
        g = p_ref[0].astype(F32)
        for k in range(1, N_DEV):
            g = g + p_ref[k].astype(F32)
        nm = ADAM_B1 * m_ref[...] + (1.0 - ADAM_B1) * g
        nv = ADAM_B2 * v_ref[...] + (1.0 - ADAM_B2) * (g * g)
        g_ref[...] = g
        nm_ref[...] = nm
        nv_ref[...] = nv
        d_ref[...] = -ADAM_LR * ((nm * c1) / (jnp.sqrt(nv * c2) + ADAM_EPS) + ADAM_WD * w_ref[...])

    lay = pl.BlockSpec((None, tr, C), lambda i: (layer, i, 0))
    out = pl.BlockSpec((tr, C), lambda i: (i, 0))
    sd = jax.ShapeDtypeStruct((R, C), F32)
    outs = pl.pallas_call(
        body, grid=(R // tr,),
        in_specs=[lay, pl.BlockSpec((N_DEV, tr, C), lambda i: (0, i, 0)), lay, lay],
        out_specs=[out] * 4, out_shape=[sd] * 4, name=name,
        compiler_params=_params("parallel"))(w3, p3, m3, v3)
    return [a.reshape(shape) for a in outs]


def _exchange(arrs, scatter, name):
    n = len(arrs)
    out_shape = [jax.ShapeDtypeStruct(a.shape if scatter else (N_DEV,) + a.shape, a.dtype) for a in arrs]

    def body(*refs):
        ins, outs = refs[:n], refs[n:2 * n]
        send_sems, recv_sems, local_sems = refs[2 * n:]
        x, y, c = lax.axis_index("x"), lax.axis_index("y"), lax.axis_index("c")
        me = 4 * x + 2 * y + c
        copies = []
        for a in range(n):
            src = ins[a].at[me] if scatter else ins[a]
            own = pltpu.make_async_copy(src, outs[a].at[me], local_sems.at[a])
            own.start()
            copies.append(own)
        sends, recvs = [], []
        for k in range(1, N_DEV):
            px, py, pc = x ^ (k >> 2), y ^ ((k >> 1) & 1), c ^ (k & 1)
            peer = 4 * px + 2 * py + pc
            for a in range(n):
                src = ins[a].at[peer] if scatter else ins[a]
                cp = pltpu.make_async_remote_copy(
                    src_ref=src, dst_ref=outs[a].at[me],
                    send_sem=send_sems.at[a, k - 1], recv_sem=recv_sems.at[a, k - 1],
                    device_id=(px, py, pc), device_id_type=pl.DeviceIdType.MESH)
                cp.start()
                sends.append(cp)
                recvs.append(pltpu.make_async_remote_copy(
                    src_ref=src, dst_ref=outs[a].at[peer],
                    send_sem=send_sems.at[a, k - 1], recv_sem=recv_sems.at[a, k - 1],
                    device_id=(px, py, pc), device_id_type=pl.DeviceIdType.MESH))
        for cp in recvs:
            cp.wait_recv()
        for cp in sends:
            cp.wait_send()
        for cp in copies:
            cp.wait()

    any_spec = pl.BlockSpec(memory_space=pl.ANY)
    return pl.pallas_call(
        body, in_specs=[any_spec] * n, out_specs=[any_spec] * n, out_shape=out_shape,
        scratch_shapes=[pltpu.SemaphoreType.DMA((n, N_DEV - 1)), pltpu.SemaphoreType.DMA((n, N_DEV - 1)),
                        pltpu.SemaphoreType.DMA((n,))],
        name=name)(*arrs)


HBM_SPEC = pl.BlockSpec(memory_space=pltpu.HBM)
SEM_SPEC = pl.BlockSpec(memory_space=pltpu.SEMAPHORE)
DATAFLOW = pltpu.SideEffectType.DATAFLOW_SIDE_EFFECTING


def _peer_of(k):
    x, y, c = lax.axis_index("x"), lax.axis_index("y"), lax.axis_index("c")
    return x ^ (k >> 2), y ^ ((k >> 1) & 1), c ^ (k & 1)


def _exchange_copy(k, a, src_ref, land_ref, send_sems, recv_sems, scatter, outgoing):
    px, py, pc = _peer_of(k)
    peer = 4 * px + 2 * py + pc
    me = 4 * lax.axis_index("x") + 2 * lax.axis_index("y") + lax.axis_index("c")
    idx = a * (N_DEV - 1) + k - 1
    return pltpu.make_async_remote_copy(
        src_ref=src_ref.at[peer] if scatter else src_ref, dst_ref=land_ref.at[me if outgoing else peer],
        send_sem=send_sems.at[idx], recv_sem=recv_sems.at[idx],
        device_id=(px, py, pc), device_id_type=pl.DeviceIdType.MESH)


def _exchange_start(arrs, scatter, name):
    n = len(arrs)
    land_shapes = [a.shape if scatter else (N_DEV,) + a.shape for a in arrs]

    def body(*refs):
        srcs, lands = refs[:n], refs[n:2 * n]
        send_sems, recv_sems = refs[2 * n], refs[2 * n + 1]
        token = refs[-1]
        for k in range(1, N_DEV):
            for a in range(n):
                _exchange_copy(k, a, srcs[a], lands[a], send_sems, recv_sems, scatter, True).start()
        token[...] = jnp.zeros_like(token)

    sems = pltpu.SemaphoreType.DMA((n * (N_DEV - 1),))
    out_shape = ([sems, sems] + [pltpu.HBM(a.shape, a.dtype) for a in arrs]
                 + [pltpu.HBM(s, a.dtype) for s, a in zip(land_shapes, arrs)]
                 + [jax.ShapeDtypeStruct((8, 128), F32)])
    ins = ([pltpu.with_memory_space_constraint(a, pltpu.HBM) for a in arrs]
           + [pltpu.with_memory_space_constraint(lax.empty(s, a.dtype), pltpu.HBM) for s, a in zip(land_shapes, arrs)])
    res = pl.pallas_call(
        body, name=name, out_shape=out_shape, in_specs=[HBM_SPEC] * (2 * n),
        out_specs=[SEM_SPEC, SEM_SPEC] + [HBM_SPEC] * (2 * n) + [pl.BlockSpec(memory_space=pltpu.VMEM)],
        input_output_aliases={i: 2 + i for i in range(2 * n)},
        compiler_params=pltpu.CompilerParams(has_side_effects=DATAFLOW))(*ins)
    return res[0], res[1], list(res[2:2 + n]), list(res[2 + n:2 + 2 * n]), res[-1]


def _exchange_wait(started, after, scatter, name):
    send_sems, recv_sems, srcs, lands, _ = started
    n = len(srcs)

    def body(*refs):
        src_refs, land_refs = refs[:n], refs[n:2 * n]
        s_sems, r_sems = refs[2 * n], refs[2 * n + 1]
        for k in range(1, N_DEV):
            for a in range(n):
                back = _exchange_copy(k, a, src_refs[a], land_refs[a], s_sems, r_sems, scatter, False)
                back.wait_send()
                back.wait_recv()

    out_shape = [pltpu.HBM(a.shape, a.dtype) for a in srcs] + [pltpu.HBM(a.shape, a.dtype) for a in lands]
    res = pl.pallas_call(
        body, name=name, out_shape=out_shape,
        in_specs=[HBM_SPEC] * (2 * n) + [SEM_SPEC, SEM_SPEC, pl.BlockSpec(memory_space=pl.ANY)],
        out_specs=[HBM_SPEC] * (2 * n), input_output_aliases={i: i for i in range(2 * n)},
        compiler_params=pltpu.CompilerParams(has_side_effects=DATAFLOW))(*srcs, *lands, send_sems, recv_sems, after)
    return list(res[:n]), list(res[n:])


def _own_slot(land, own, me):
    return lax.dynamic_update_index_in_dim(land, own, me, 0)


class _Cols:
    def __init__(self, D):
        Wb = D // 2
        sizes = (Wb, Wb, Wb, Wb, Wb, Wb, Wb, ATT_KV * ATT_HD, ATT_KV * ATT_HD, Wb, 3 * D)
        offs = np.concatenate([[0], np.cumsum(sizes)])
        assert all(int(o) % 256 == 0 for o in offs)
        (self.rq, self.rk, self.rv, self.rg, self.pv, self.pg,
         self.aq, self.ak, self.av, self.ag, self.mg) = (int(o) // 256 for o in offs[:-1])
        self.width = int(offs[-1])
        self.sizes = sizes


def _rope_tables(S):
    pos = jnp.arange(S, dtype=F32)[:, None]
    inv_r = 1.0 / (RET_ROPE_BASE ** jnp.linspace(0.0, 1.0, RET_HD // 2, dtype=F32))
    ang_r = pos * inv_r[None, :]
    inv_a = ROPE_THETA ** (-jnp.arange(ROPE_HALF, dtype=F32) / ROPE_HALF)
    ang_a = pos * inv_a[None, :]
    ca, sa = jnp.cos(ang_a), jnp.sin(ang_a)
    z16 = jnp.zeros((S, ROPE_HALF), F32)
    rest = ATT_HD - 2 * ROPE_HALF
    cf = jnp.concatenate([ca, ca, jnp.ones((S, rest), F32)], axis=1)
    s_up = jnp.concatenate([z16, sa, jnp.zeros((S, rest), F32)], axis=1)
    s_dn = jnp.concatenate([-sa, z16, jnp.zeros((S, rest), F32)], axis=1)
    return jnp.cos(ang_r), jnp.sin(ang_r), cf, s_up, s_dn


def _layer_fwd(x, p, tabs, cols, deps=()):
    cos_r, sin_r, cf, s_up, s_dn = tabs
    S, D = x.shape
    h, ht = _rmsnorm_fwd(x, p["norm_g"], deps)
    z = _matmul(h, p["w_in"], "nn", F32, 1024, 512, D, "in_proj")
    rcols = (cols.rq, cols.rk, cols.rv, cols.rg)
    sf, sb = _ret_state(z, cols.rk, RET_HD ** -0.5, z, cols.rv, cos_r, sin_r, p["af"], p["ab"], "fwd",
                        "ret_state_fwd")
    o_ret, ua, uat = _ret_fwd(z, cos_r, sin_r, sf, sb, p["af"], p["ab"], rcols)
    y_pool, ub, ubt = _pool_fwd(z, p["pool_w"], p["pool_scale"], cols.pv, cols.pg)
    qn, kn, vb = _attn_prep(z, p["q_gain"], p["k_gain"], cf, s_up, s_dn,
                            cols.aq // 4, cols.ak, cols.av)
    o_att, uc, uct = _attn_fwd(qn, kn, vb, z, p["sink"], cols.ag * 2)
    ya, yb, yc, merged, mergedt = _branch_merge(ua, ub, uc, p["w_ret"], p["w_pool"], p["w_att"], z, cols.mg)
    out = _matmul(merged, p["w_out"], "nn", F32, 1024, 512, D, "out_proj", res=x)
    saved = dict(x=x, ht=ht, z=z, sf=sf, sb=sb, o_ret=o_ret, uat=uat, y_pool=y_pool, ubt=ubt,
                 qn=qn, kn=kn, vb=vb, o_att=o_att, uct=uct, ya=ya, yb=yb, yc=yc, mergedt=mergedt)
    return out, saved


def _layer_bwd(dx, dxb, p, sv, tabs, cols, on_grads):
    cos_r, sin_r, cf, s_up, s_dn = tabs
    z = sv["z"]
    S, D = dx.shape
    dya, dyb, dyc, dmg0, dmg1, dmg2 = _merge_bwd(dxb, p["w_out"], sv["ya"], sv["yb"], sv["yc"], z, cols.mg)
    grads = {"w_out": _matmul(sv["mergedt"], dxb, "nn", ACT, 1024, 512, S, "dw_out")}
    dus = {}
    for nm, ut, dy in (("ret", sv["uat"], dya), ("pool", sv["ubt"], dyb), ("att", sv["uct"], dyc)):
        dus[nm] = _matmul(dy, p["w_" + nm], "nt", F32, 1024, 512, D, "du_" + nm)
        grads["w_" + nm] = _matmul(ut, dy, "nn", ACT, 1024, 512, S, "dw_" + nm)
    rcols = (cols.rq, cols.rk, cols.rv, cols.rg)
    do_ret, d_rg = _ret_gate_bwd(dus["ret"], sv["o_ret"], z, cols.rg)
    eb, ef = _ret_state(z, cols.rq, 1.0, do_ret, 0, cos_r, sin_r, p["af"], p["ab"], "bwd", "ret_state_bwd")
    d_rq, d_rk, d_rv, d_decay = _ret_bwd(z, do_ret, cos_r, sin_r, sv["sf"], sv["sb"], ef, eb,
                                         p["af"], p["ab"], rcols)
    dpc, d_pg, d_pscale, g_pool_w = _pool_bwd_a(z, p["pool_w"], p["pool_scale"], sv["y_pool"], dus["pool"],
                                                cols.pv, cols.pg)
    d_pv = _pool_bwd_b(dpc)
    grads["pool_w"] = g_pool_w.astype(ACT)
    dqn, dkn, dvv, d_ag, d_sink = _attn_bwd(sv["qn"], sv["kn"], sv["vb"], sv["o_att"], dus["att"], z,
                                            p["sink"], cols.ag * 2)
    d_aq, d_ak, d_av, d_qg, d_kg = _attn_prep_bwd(z, dqn, dkn, dvv, p["q_gain"], p["k_gain"], cf, s_up, s_dn,
                                                  cols.aq // 4, cols.ak)
    dz = jnp.concatenate([d_rq, d_rk, d_rv, d_rg, d_pv, d_pg, d_aq, d_ak, d_av, d_ag, dmg0, dmg1, dmg2],
                         axis=1)
    grads["w_in"] = _matmul(sv["ht"], dz, "nn", ACT, 1024, 512, S, "dw_in")
    tok = on_grads(grads)
    dh = _matmul(dz, p["w_in"], "nt", F32, 1024, 512, 29 * 128, "dh", deps=() if tok is None else (tok,))
    dx_in, dxb_in, d_norm_g = _rmsnorm_bwd(sv["x"], p["norm_g"], dh, dx)
    misc = jnp.concatenate([d_decay[:, 0, 0], d_decay[:, 0, 1], d_sink[:, 0, :ATT_G].reshape(-1)])
    misc = jnp.pad(misc, (0, 128 - misc.shape[0]))[None, :]
    small = jnp.concatenate([d_norm_g, d_pscale, d_qg, d_kg, misc], axis=1)
    return dx_in, dxb_in, small


def _pack_small(norm_g, pool_scale, q_gain, k_gain, af, ab, sink):
    L = norm_g.shape[0]
    misc = jnp.concatenate([af, ab, sink], axis=1)
    misc = jnp.pad(misc, ((0, 0), (0, 128 - misc.shape[1])))
    return jnp.concatenate([norm_g, pool_scale, q_gain, k_gain, misc], axis=1)


def _unpack_small(a, D):
    Wb = D // 2
    o = np.cumsum([0, D, Wb, ATT_HD, ATT_HD])
    misc = a[:, o[4]:]
    return (a[:, o[0]:o[1]], a[:, o[1]:o[2]], a[:, o[2]:o[3]], a[:, o[3]:o[4]],
            misc[:, :RET_HEADS], misc[:, RET_HEADS:2 * RET_HEADS],
            misc[:, 2 * RET_HEADS:2 * RET_HEADS + ATT_Q])


def _local_step(x, t, n_layers, get_layer, on_grads, tabs, cols, first_dep=None):
    saved, layers = [], []
    after = first_dep
    for l in range(n_layers):
        p = get_layer(l, after)
        x, sv = _layer_fwd(x, p, tabs, cols, (first_dep,) if (l == 0 and first_dep is not None) else ())
        after = x
        layers.append(p)
        saved.append(sv)
    dx, dxb, lsum = _loss_head(x, t)
    smalls = []
    for l in reversed(range(n_layers)):
        dx, dxb, sm = _layer_bwd(dx, dxb, layers[l], saved[l], tabs, cols, functools.partial(on_grads, l))
        smalls.append(sm)
    return 0.5 * lsum[0, 0], dx, jnp.concatenate(smalls[::-1], axis=0)


WEIGHT_KEYS = ("w_in", "w_ret", "w_pool", "w_att", "w_out", "pool_w")


def kernel(x, norm_g, w_in, ret_decay_fwd, ret_decay_bwd, pool_w, pool_scale, attn_q_gain, attn_k_gain, attn_sink, w_ret, w_pool, w_att, w_out, loss_target, m_norm_g, m_w_in, m_ret_decay_fwd, m_ret_decay_bwd, m_pool_w, m_pool_scale, m_attn_q_gain, m_attn_k_gain, m_attn_sink, m_w_ret, m_w_pool, m_w_att, m_w_out, v_norm_g, v_w_in, v_ret_decay_fwd, v_ret_decay_bwd, v_pool_w, v_pool_scale, v_attn_q_gain, v_attn_k_gain, v_attn_sink, v_w_ret, v_w_pool, v_w_att, v_w_out):
    L = norm_g.shape[0]
    _, S, D = x.shape
    Wb = D // 2
    G = len(POOL_WINDOWS)
    cols = _Cols(D)
    tabs = _rope_tables(S)
    me = 4 * lax.axis_index("x") + 2 * lax.axis_index("y") + lax.axis_index("c")
    weights = dict(w_in=w_in, w_ret=w_ret, w_pool=w_pool, w_att=w_att, w_out=w_out, pool_w=pool_w)

    gathers, tok = [], None
    for l in range(L):
        shards = []
        for k in WEIGHT_KEYS:
            w = weights[k][l]
            if tok is not None:
                w = w + tok[0, 0]
            shards.append(w.astype(MXU))
        st = _exchange_start(shards, False, f"gather_start_{l}")
        gathers.append(st)
        tok = st[-1]

    def cols_full(g, rows):
        return jnp.transpose(g, (1, 0, 2)).reshape(rows, -1)

    def get_layer(l, after):
        srcs, lands = _exchange_wait(gathers[l], after, False, f"gather_wait_{l}")
        g_in, g_ret, g_pool, g_att, g_out, g_pw = [_own_slot(ld, sr, me) for ld, sr in zip(lands, srcs)]
        return dict(
            norm_g=norm_g[l][None, :], w_in=cols_full(g_in, D),
            w_ret=cols_full(g_ret, Wb), w_pool=cols_full(g_pool, Wb), w_att=cols_full(g_att, Wb),
            w_out=g_out.reshape(D, D),
            pool_w=jnp.transpose(g_pw, (1, 0, 2, 3)).reshape(G, POOL_GD, POOL_GD),
            pool_scale=pool_scale[l][None, :], q_gain=attn_q_gain[l][None, :], k_gain=attn_k_gain[l][None, :],
            af=ret_decay_fwd[l], ab=ret_decay_bwd[l], sink=attn_sink[l])

    def col_slots(g, rows):
        return jnp.transpose(g.reshape(rows, N_DEV, -1), (1, 0, 2))

    scatters = {}

    def on_grads(l, g):
        slots = [col_slots(g["w_in"], D), col_slots(g["w_ret"], Wb), col_slots(g["w_pool"], Wb),
                 col_slots(g["w_att"], Wb), g["w_out"].reshape(N_DEV, D // N_DEV, D),
                 jnp.transpose(g["pool_w"].reshape(G, N_DEV, POOL_GD // N_DEV, POOL_GD), (1, 0, 2, 3))]
        scatters[l] = _exchange_start(slots, True, f"scatter_start_{l}")
        return scatters[l][-1]

    loss_local, grad_x, small = _local_step(x[0], loss_target[0], L, get_layer, on_grads, tabs, cols, tok)
    loss = lax.psum(loss_local, ("x", "y", "c"))
    small_all, = _exchange([small], False, "gather_small_grads")

    moments = dict(w_in=(m_w_in, v_w_in), w_ret=(m_w_ret, v_w_ret), w_pool=(m_w_pool, v_w_pool),
                   w_att=(m_w_att, v_w_att), w_out=(m_w_out, v_w_out), pool_w=(m_pool_w, v_pool_w))
    per_layer = {}
    after = small_all
    for l in reversed(range(L)):
        srcs, lands = _exchange_wait(scatters[l], after, True, f"scatter_wait_{l}")
        per_layer[l] = {}
        for k, ld, sr in zip(WEIGHT_KEYS, lands, srcs):
            parts = _own_slot(ld, lax.dynamic_index_in_dim(sr, me, 0, keepdims=False), me)
            m, v = moments[k]
            per_layer[l][k] = _adamw(weights[k], l, parts, m, v, "adamw_" + k)
        after = per_layer[l]["w_out"][1]
    res = {k: [jnp.stack([per_layer[l][k][i] for l in range(L)]) for i in range(4)] for k in WEIGHT_KEYS}

    sw = _pack_small(norm_g, pool_scale, attn_q_gain, attn_k_gain, ret_decay_fwd, ret_decay_bwd, attn_sink)
    sm = _pack_small(m_norm_g, m_pool_scale, m_attn_q_gain, m_attn_k_gain, m_ret_decay_fwd, m_ret_decay_bwd,
                     m_attn_sink)
    sv_ = _pack_small(v_norm_g, v_pool_scale, v_attn_q_gain, v_attn_k_gain, v_ret_decay_fwd, v_ret_decay_bwd,
                      v_attn_sink)
    small_res = [_unpack_small(a, D) for a in _adamw(sw[None], 0, small_all, sm[None], sv_[None], "adamw_small")]

    def ordered(i):
        ng, ps, qg, kg, af, ab, sk = small_res[i]
        return (ng, res["w_in"][i], af, ab, res["pool_w"][i], ps, qg, kg, sk,
                res["w_ret"][i], res["w_pool"][i], res["w_att"][i], res["w_out"][i])

    return (loss, grad_x[None], *ordered(0), *ordered(1), *ordered(2), *ordered(3))
```

```python
import functools

import numpy as np
import jax
import jax.numpy as jnp
from jax import lax
from jax.experimental import pallas as pl
from jax.experimental.pallas import tpu as pltpu

F32 = jnp.float32
MXU = jnp.bfloat16
ACT = jnp.bfloat16

N_DEV = 8
RMS_EPS = 1e-6
NEG_BIG = -1e30
RET_HEADS = 4
RET_HD = 256
CH = 128
RET_ROPE_BASE = 10000.0
POOL_WINDOWS = (2, 4, 8, 16)
POOL_GD = 256
POOL_PAD = 8
ATT_HD = 128
ATT_Q = 8
ATT_KV = 2
ATT_G = ATT_Q // ATT_KV
ATT_WIN = 128
ATT_BLK = 128
ATT_SPAN = 3 * ATT_BLK
ROPE_THETA = 500000.0
ROPE_HALF = 16

ADAM_LR = 0.001
ADAM_B1 = 0.9
ADAM_B2 = 0.999
ADAM_EPS = 1e-08
ADAM_WD = 0.01
ADAM_STEP = 10

VMEM_LIMIT = 48 * 1024 * 1024

NN = ((1,), (0,))
NT = ((1,), (1,))
TN = ((0,), (0,))


def _dot(a, b, dims):
    return lax.dot_general(a.astype(MXU), b.astype(MXU), (dims, ((), ())),
                           preferred_element_type=F32)


def _sigmoid(x):
    return 1.0 / (1.0 + jnp.exp(-x))


def _params(*sem):
    return pltpu.CompilerParams(dimension_semantics=sem, vmem_limit_bytes=VMEM_LIMIT)


def _sum_all(x):
    return jnp.sum(jnp.sum(x, axis=1, keepdims=True), axis=0, keepdims=True)


def _fiota(shape, dim):
    return lax.broadcasted_iota(jnp.int32, shape, dim).astype(F32)


SMEM_SPEC = pl.BlockSpec(memory_space=pltpu.SMEM)


def _matmul(a, b, mode, out_dtype, tm, tn, tk, name, res=None, deps=()):
    if mode == "tn":
        K, M = a.shape
    else:
        M, K = a.shape
    N = b.shape[0] if mode == "nt" else b.shape[1]
    tm, tn, tk = min(tm, M), min(tn, N), min(tk, K)
    assert M % tm == 0 and N % tn == 0 and K % tk == 0, (name, M, N, K, tm, tn, tk)
    nk = K // tk
    dims = {"nn": NN, "nt": NT, "tn": TN}[mode]
    a_spec = (pl.BlockSpec((tk, tm), lambda i, j, k: (k, i)) if mode == "tn"
              else pl.BlockSpec((tm, tk), lambda i, j, k: (i, k)))
    b_spec = (pl.BlockSpec((tn, tk), lambda i, j, k: (j, k)) if mode == "nt"
              else pl.BlockSpec((tk, tn), lambda i, j, k: (k, j)))
    o_spec = pl.BlockSpec((tm, tn), lambda i, j, k: (i, j))
    has_res = res is not None
    n_in = 2 + has_res + len(deps)

    def body(*refs):
        a_ref, b_ref = refs[:2]
        r_ref = refs[2] if has_res else None
        o_ref = refs[n_in]

        def finish(out):
            if has_res:
                out = out + r_ref[...]
            o_ref[...] = out.astype(out_dtype)

        if nk == 1:
            finish(_dot(a_ref[...], b_ref[...], dims))
            return
        acc = refs[n_in + 1]
        k = pl.program_id(2)

        @pl.when(k == 0)
        def _():
            acc[...] = jnp.zeros_like(acc)

        acc[...] += _dot(a_ref[...], b_ref[...], dims)

        @pl.when(k == nk - 1)
        def _():
            finish(acc[...])

    ins = [a, b] + ([res] if has_res else []) + list(deps)
    in_specs = ([a_spec, b_spec] + ([o_spec] if has_res else [])
                + [pl.BlockSpec((8, 128), lambda i, j, k: (0, 0))] * len(deps))
    return pl.pallas_call(
        body, grid=(M // tm, N // tn, nk), in_specs=in_specs, out_specs=o_spec,
        out_shape=jax.ShapeDtypeStruct((M, N), out_dtype),
        scratch_shapes=[pltpu.VMEM((tm, tn), F32)] if nk > 1 else [], name=name,
        compiler_params=_params("parallel", "parallel", "arbitrary"))(*ins)


DEP_SPEC1 = pl.BlockSpec((8, 128), lambda i: (0, 0))
DEP_SPEC2 = pl.BlockSpec((8, 128), lambda i, j: (0, 0))


def _rmsnorm_fwd(x, g, deps=()):
    S, D = x.shape
    tm = min(512, S)
    assert S % tm == 0

    def body(x_ref, g_ref, *rest):
        h_ref, ht_ref = rest[-2:]
        xv = x_ref[...]
        r = lax.rsqrt(jnp.mean(xv * xv, axis=-1, keepdims=True) + RMS_EPS)
        hv = xv * r * g_ref[...]
        h_ref[...] = hv.astype(ACT)
        ht_ref[...] = hv.T.astype(ACT)

    row = pl.BlockSpec((tm, D), lambda i: (i, 0))
    return pl.pallas_call(
        body, grid=(S // tm,), in_specs=[row, pl.BlockSpec((1, D), lambda i: (0, 0))] + [DEP_SPEC1] * len(deps),
        out_specs=[row, pl.BlockSpec((D, tm), lambda i: (0, i))],
        out_shape=[jax.ShapeDtypeStruct((S, D), ACT), jax.ShapeDtypeStruct((D, S), ACT)], name="rmsnorm_fwd",
        compiler_params=_params("parallel"))(x, g, *deps)


def _rmsnorm_bwd(x, g, dh, dres):
    S, D = x.shape
    tm = min(256, S)

    def body(x_ref, g_ref, dh_ref, dr_ref, dx_ref, dxb_ref, dg_ref):
        xv = x_ref[...]
        r = lax.rsqrt(jnp.mean(xv * xv, axis=-1, keepdims=True) + RMS_EPS)
        xh = xv * r
        dhv = dh_ref[...]
        dxh = dhv * g_ref[...]
        dx = r * (dxh - xh * jnp.mean(dxh * xh, axis=-1, keepdims=True)) + dr_ref[...]
        dx_ref[...] = dx
        dxb_ref[...] = dx.astype(ACT)

        @pl.when(pl.program_id(0) == 0)
        def _():
            dg_ref[...] = jnp.zeros_like(dg_ref)

        dg_ref[...] += jnp.sum(dhv * xh, axis=0, keepdims=True)

    row = pl.BlockSpec((tm, D), lambda i: (i, 0))
    vec = pl.BlockSpec((1, D), lambda i: (0, 0))
    return pl.pallas_call(
        body, grid=(S // tm,), in_specs=[row, vec, row, row], out_specs=[row, row, vec],
        out_shape=[jax.ShapeDtypeStruct((S, D), F32), jax.ShapeDtypeStruct((S, D), ACT),
                   jax.ShapeDtypeStruct((1, D), F32)],
        name="rmsnorm_bwd", compiler_params=_params("arbitrary"))(x, g, dh, dres)


def _rot256(x, c, s):
    x1, x2 = x[:, :128], x[:, 128:]
    return jnp.concatenate([x1 * c - x2 * s, x2 * c + x1 * s], axis=1)


def _rot256_t(g, c, s):
    g1, g2 = g[:, :128], g[:, 128:]
    return jnp.concatenate([g1 * c + g2 * s, g2 * c - g1 * s], axis=1)


def _log_decay(a_ref, h, shape):
    return -jnp.exp(jnp.full(shape, a_ref[h], F32))


def _ret_state(xsrc, xbase, xscale, ysrc, ybase, cos, sin, af, ab, mode, name):
    S = xsrc.shape[0]
    nC = S // CH
    H = RET_HEADS
    W = H * RET_HD
    assert (xbase * RET_HD) % W == 0 and (ybase * RET_HD) % W == 0

    def body(x1_ref, y1_ref, c1_ref, s1_ref, x2_ref, y2_ref, c2_ref, s2_ref, af_ref, ab_ref,
             st1_ref, st2_ref, acc1, acc2):
        @pl.when(pl.program_id(0) == 0)
        def _():
            acc1[...] = jnp.zeros_like(acc1)
            acc2[...] = jnp.zeros_like(acc2)

        j = _fiota((CH, 1), 0)
        ca, sa, cb, sb_ = c1_ref[...], s1_ref[...], c2_ref[...], s2_ref[...]
        for h in range(H):
            sl = slice(h * RET_HD, (h + 1) * RET_HD)
            lgf = _log_decay(af_ref, h, (CH, 1))
            lgb = _log_decay(ab_ref, h, (CH, 1))
            if mode == "fwd":
                w1, d1 = jnp.exp(lgf * (CH - 1.0 - j)), jnp.exp(lgf[:1] * CH)
                w2, d2 = jnp.exp(lgb * j), jnp.exp(lgb[:1] * CH)
            else:
                w1, d1 = jnp.exp(lgb * (CH - j)), jnp.exp(lgb[:1] * CH)
                w2, d2 = jnp.exp(lgf * (j + 1.0)), jnp.exp(lgf[:1] * CH)
            xa = _rot256(x1_ref[:, sl], ca, sa) * xscale
            st1_ref[h] = acc1[h]
            acc1[h] = d1 * acc1[h] + _dot(xa * w1, y1_ref[:, sl], TN)
            xb = _rot256(x2_ref[:, sl], cb, sb_) * xscale
            st2_ref[h] = acc2[h]
            acc2[h] = d2 * acc2[h] + _dot(xb * w2, y2_ref[:, sl], TN)

    xcol, ycol = (xbase * RET_HD) // W, (ybase * RET_HD) // W
    in_specs = [
        pl.BlockSpec((CH, W), lambda c: (c, xcol)), pl.BlockSpec((CH, W), lambda c: (c, ycol)),
        pl.BlockSpec((CH, 128), lambda c: (c, 0)), pl.BlockSpec((CH, 128), lambda c: (c, 0)),
        pl.BlockSpec((CH, W), lambda c: (nC - 1 - c, xcol)), pl.BlockSpec((CH, W), lambda c: (nC - 1 - c, ycol)),
        pl.BlockSpec((CH, 128), lambda c: (nC - 1 - c, 0)),
        pl.BlockSpec((CH, 128), lambda c: (nC - 1 - c, 0)),
        SMEM_SPEC, SMEM_SPEC]
    out_specs = [pl.BlockSpec((H, None, RET_HD, RET_HD), lambda c: (0, c, 0, 0)),
                 pl.BlockSpec((H, None, RET_HD, RET_HD), lambda c: (0, nC - 1 - c, 0, 0))]
    st = jax.ShapeDtypeStruct((H, nC, RET_HD, RET_HD), F32)
    return pl.pallas_call(
        body, grid=(nC,), in_specs=in_specs, out_specs=out_specs, out_shape=[st, st],
        scratch_shapes=[pltpu.VMEM((H, RET_HD, RET_HD), F32), pltpu.VMEM((H, RET_HD, RET_HD), F32)],
        name=name, compiler_params=_params("arbitrary"))(
            xsrc, ysrc, cos, sin, xsrc, ysrc, cos, sin, af, ab)


def _decay_mask(lgf1, lgb1):
    lag = _fiota((CH, CH), 0) - _fiota((CH, CH), 1)
    alag = jnp.abs(lag)
    return lag, jnp.where(lag >= 0, jnp.exp(lgf1 * alag), jnp.exp(lgb1 * alag))


def _ret_fwd(z, cos, sin, sf, sb, af, ab, cols):
    S = z.shape[0]
    nC = S // CH
    H = RET_HEADS
    rq, rk, rv, rg = cols
    W = H * RET_HD
    assert all((c * RET_HD) % W == 0 for c in cols)

    def body(q_ref, k_ref, v_ref, g_ref, c_ref, s_ref, sf_ref, sb_ref, af_ref, ab_ref,
             o_ref, u_ref, ut_ref):
        j = _fiota((CH, 1), 0)
        c, s = c_ref[...], s_ref[...]
        for h in range(H):
            sl = slice(h * RET_HD, (h + 1) * RET_HD)
            lgf = _log_decay(af_ref, h, (CH, 1))
            lgb = _log_decay(ab_ref, h, (CH, 1))
            q = _rot256(q_ref[:, sl], c, s)
            k = _rot256(k_ref[:, sl], c, s) * (RET_HD ** -0.5)
            _, dm = _decay_mask(lgf[:1], lgb[:1])
            p = _dot(q, k, NT) * dm
            o = (_dot(p, v_ref[:, sl], NN)
                 + _dot(q * jnp.exp(lgf * (j + 1.0)), sf_ref[h], NN)
                 + _dot(q * jnp.exp(lgb * (CH - j)), sb_ref[h], NN))
            o_ref[:, sl] = o
            on = o * lax.rsqrt(jnp.mean(o * o, axis=-1, keepdims=True) + RMS_EPS)
            g = g_ref[:, sl]
            u = on * (g * _sigmoid(g))
            u_ref[:, sl] = u.astype(ACT)
            ut_ref[sl, :] = u.T.astype(ACT)

    def zc(col):
        return pl.BlockSpec((CH, W), lambda c: (c, (col * RET_HD) // W))

    tab = pl.BlockSpec((CH, 128), lambda c: (c, 0))
    stt = pl.BlockSpec((H, None, RET_HD, RET_HD), lambda c: (0, c, 0, 0))
    out = pl.BlockSpec((CH, W), lambda c: (c, 0))
    return pl.pallas_call(
        body, grid=(nC,),
        in_specs=[zc(rq), zc(rk), zc(rv), zc(rg), tab, tab, stt, stt, SMEM_SPEC, SMEM_SPEC],
        out_specs=[out, out, pl.BlockSpec((W, CH), lambda c: (0, c))],
        out_shape=[jax.ShapeDtypeStruct((S, W), F32), jax.ShapeDtypeStruct((S, W), ACT),
                   jax.ShapeDtypeStruct((W, S), ACT)],
        name="ret_fwd", compiler_params=_params("parallel"))(
            z, z, z, z, cos, sin, sf, sb, af, ab)


def _ret_gate_bwd(du, o_pre, z, rg):
    S, W = du.shape
    H = RET_HEADS
    tm = min(512, S)
    assert S % tm == 0

    def body(du_ref, o_ref, g_ref, do_ref, dg_ref):
        o = o_ref[...]
        r = lax.rsqrt(jnp.mean(o * o, axis=-1, keepdims=True) + RMS_EPS)
        on = o * r
        g = g_ref[...]
        sg = _sigmoid(g)
        duv = du_ref[...]
        don = duv * (g * sg)
        dg_ref[...] = (duv * on * (sg * (1.0 + g * (1.0 - sg)))).astype(ACT)
        do_ref[...] = r * (don - on * jnp.mean(don * on, axis=-1, keepdims=True))

    blk = pl.BlockSpec((tm, RET_HD), lambda i, h: (i, h))
    return pl.pallas_call(
        body, grid=(S // tm, H),
        in_specs=[blk, blk, pl.BlockSpec((tm, RET_HD), lambda i, h: (i, rg + h))],
        out_specs=[blk, blk],
        out_shape=[jax.ShapeDtypeStruct((S, W), F32), jax.ShapeDtypeStruct((S, W), ACT)],
        name="ret_gate_bwd", compiler_params=_params("parallel", "parallel"))(du, o_pre, z)


def _ret_bwd(z, do, cos, sin, sf, sb, ef, eb, af, ab, cols):
    S = z.shape[0]
    nC = S // CH
    H = RET_HEADS
    rq, rk, rv, _ = cols
    W = H * RET_HD

    def body(q_ref, k_ref, v_ref, do_ref, c_ref, s_ref, sf_ref, sb_ref, ef_ref, eb_ref,
             af_ref, ab_ref, dq_ref, dk_ref, dv_ref, da_ref):
        @pl.when(pl.program_id(0) == 0)
        def _():
            da_ref[...] = jnp.zeros_like(da_ref)

        j = _fiota((CH, 1), 0)
        c, s = c_ref[...], s_ref[...]
        scale = RET_HD ** -0.5
        row = lax.broadcasted_iota(jnp.int32, (8, 128), 0)
        lane = lax.broadcasted_iota(jnp.int32, (8, 128), 1)
        for h in range(H):
            sl = slice(h * RET_HD, (h + 1) * RET_HD)
            lgf = _log_decay(af_ref, h, (CH, 1))
            lgb = _log_decay(ab_ref, h, (CH, 1))
            q = _rot256(q_ref[:, sl], c, s)
            k = _rot256(k_ref[:, sl], c, s) * scale
            v = v_ref[:, sl]
            do = do_ref[:, sl]
            sf_, sb_, ef_, eb_ = sf_ref[h], sb_ref[h], ef_ref[h], eb_ref[h]
            a_w = jnp.exp(lgf * (j + 1.0))
            b_w = jnp.exp(lgb * (CH - j))
            wf = jnp.exp(lgf * (CH - 1.0 - j))
            wb = jnp.exp(lgb * j)
            lag, dm = _decay_mask(lgf[:1], lgb[:1])
            sc = _dot(q, k, NT)
            gg = _dot(do, v, NT)
            dg = gg * dm
            x1 = _dot(do, sf_, NT) * a_w
            x2 = _dot(do, sb_, NT) * b_w
            y1 = _dot(v, ef_, NT) * wf
            y2 = _dot(v, eb_, NT) * wb
            dq = _dot(dg, k, NN) + x1 + x2
            dk = _dot(dg, q, TN) + y1 + y2
            dv = _dot(sc * dm, do, TN) + _dot(k * wf, ef_, NN) + _dot(k * wb, eb_, NN)
            dq_ref[:, sl] = _rot256_t(dq, c, s).astype(ACT)
            dk_ref[:, sl] = (_rot256_t(dk, c, s) * scale).astype(ACT)
            dv_ref[:, sl] = dv.astype(ACT)
            t = dm * gg * sc
            qx1 = jnp.sum(q * x1, axis=-1, keepdims=True)
            qx2 = jnp.sum(q * x2, axis=-1, keepdims=True)
            ky1 = jnp.sum(k * y1, axis=-1, keepdims=True)
            ky2 = jnp.sum(k * y2, axis=-1, keepdims=True)
            dlf = (_sum_all(jnp.where(lag > 0, lag * t, 0.0))
                   + _sum_all((j + 1.0) * qx1 + (CH - 1.0 - j) * ky1)
                   + CH * jnp.exp(lgf[:1] * CH) * _sum_all(ef_ * sf_))
            dlb = (_sum_all(jnp.where(lag < 0, -lag * t, 0.0))
                   + _sum_all((CH - j) * qx2 + j * ky2)
                   + CH * jnp.exp(lgb[:1] * CH) * _sum_all(eb_ * sb_))
            da_ref[h] += jnp.where((row == 0) & (lane == 0), dlf * lgf[:1],
                                   jnp.where((row == 0) & (lane == 1), dlb * lgb[:1], 0.0))

    def zc(col):
        return pl.BlockSpec((CH, W), lambda c: (c, (col * RET_HD) // W))

    tab = pl.BlockSpec((CH, 128), lambda c: (c, 0))
    stt = pl.BlockSpec((H, None, RET_HD, RET_HD), lambda c: (0, c, 0, 0))
    out = pl.BlockSpec((CH, W), lambda c: (c, 0))
    dz = jax.ShapeDtypeStruct((S, W), ACT)
    return pl.pallas_call(
        body, grid=(nC,),
        in_specs=[zc(rq), zc(rk), zc(rv), out, tab, tab, stt, stt, stt, stt, SMEM_SPEC, SMEM_SPEC],
        out_specs=[out, out, out, pl.BlockSpec((H, 8, 128), lambda c: (0, 0, 0))],
        out_shape=[dz, dz, dz, jax.ShapeDtypeStruct((H, 8, 128), F32)],
        name="ret_bwd", compiler_params=_params("arbitrary"))(
            z, z, z, do, cos, sin, sf, sb, ef, eb, af, ab)


def _fill_padded(pad_ref, src_ref, S):
    zeros = jnp.zeros((POOL_PAD, POOL_GD), F32)
    pad_ref[pl.ds(0, POOL_PAD), :] = zeros
    pad_ref[pl.ds(POOL_PAD, S), :] = src_ref[...]
    pad_ref[pl.ds(S + POOL_PAD, POOL_PAD), :] = zeros


def _window_sum(ext, T, lo, hi):
    n = T + 2 * POOL_PAD
    acc = None
    for k in range(lo, hi):
        sh = ext if k == 0 else pltpu.roll(ext, (-k) % n, 0)
        piece = sh[POOL_PAD:POOL_PAD + T]
        acc = piece if acc is None else acc + piece
    return acc


def _window_count(pos, w, S):
    lo = jnp.maximum(pos - w // 2, 0)
    hi = jnp.minimum(pos + w // 2, S)
    return (hi - lo).astype(F32)


def _pool_fwd(z, pw, scale, pv, pg):
    S = z.shape[0]
    G = len(POOL_WINDOWS)
    T = min(512, S)
    W = G * POOL_GD

    def body(x_ref, g_ref, pw_ref, sc_ref, y_ref, u_ref, ut_ref, pad, p_scr):
        grp = pl.program_id(0)
        i = pl.program_id(1)

        @pl.when(i == 0)
        def _():
            _fill_padded(pad, x_ref, S)

        r0 = pl.multiple_of(i * T, T)
        ext = pad[pl.ds(r0, T + 2 * POOL_PAD), :]
        pos = r0 + lax.broadcasted_iota(jnp.int32, (T, 1), 0)
        for gi, w in enumerate(POOL_WINDOWS):
            @pl.when(grp == gi)
            def _(w=w):
                acc = _window_sum(ext, T, -(w // 2), w // 2)
                p_scr[...] = acc / _window_count(pos, w, S) - ext[POOL_PAD:POOL_PAD + T]

        y = _dot(p_scr[...], pw_ref[...], NN)
        y_ref[...] = y
        g = g_ref[...]
        u = y * sc_ref[...] * (g * _sigmoid(g))
        u_ref[...] = u.astype(ACT)
        ut_ref[...] = u.T.astype(ACT)

    blk = pl.BlockSpec((T, POOL_GD), lambda g, i: (i, g))
    return pl.pallas_call(
        body, grid=(G, S // T),
        in_specs=[pl.BlockSpec((S, POOL_GD), lambda g, i: (0, pv + g)),
                  pl.BlockSpec((T, POOL_GD), lambda g, i: (i, pg + g)),
                  pl.BlockSpec((None, POOL_GD, POOL_GD), lambda g, i: (g, 0, 0)),
                  pl.BlockSpec((1, POOL_GD), lambda g, i: (0, g))],
        out_specs=[blk, blk, pl.BlockSpec((POOL_GD, T), lambda g, i: (g, i))],
        out_shape=[jax.ShapeDtypeStruct((S, W), F32), jax.ShapeDtypeStruct((S, W), ACT),
                   jax.ShapeDtypeStruct((W, S), ACT)],
        scratch_shapes=[pltpu.VMEM((S + 2 * POOL_PAD, POOL_GD), F32), pltpu.VMEM((T, POOL_GD), F32)],
        name="pool_fwd", compiler_params=_params("parallel", "arbitrary"))(z, z, pw, scale)


def _pool_bwd_a(z, pw, scale, y_raw, du, pv, pg):
    S = z.shape[0]
    G = len(POOL_WINDOWS)
    T = min(512, S)
    W = G * POOL_GD

    def body(x_ref, g_ref, pw_ref, sc_ref, y_ref, du_ref, dpc_ref, dg_ref, dsc_ref, dpw_ref,
             pad, p_scr, c_scr):
        grp = pl.program_id(0)
        i = pl.program_id(1)

        @pl.when(i == 0)
        def _():
            _fill_padded(pad, x_ref, S)
            dsc_ref[...] = jnp.zeros_like(dsc_ref)
            dpw_ref[...] = jnp.zeros_like(dpw_ref)

        r0 = pl.multiple_of(i * T, T)
        ext = pad[pl.ds(r0, T + 2 * POOL_PAD), :]
        pos = r0 + lax.broadcasted_iota(jnp.int32, (T, 1), 0)
        for gi, w in enumerate(POOL_WINDOWS):
            @pl.when(grp == gi)
            def _(w=w):
                cnt = _window_count(pos, w, S)
                acc = _window_sum(ext, T, -(w // 2), w // 2)
                p_scr[...] = acc / cnt - ext[POOL_PAD:POOL_PAD + T]
                c_scr[...] = jnp.broadcast_to(cnt, (T, 128))

        g = g_ref[...]
        sg = _sigmoid(g)
        duv = du_ref[...]
        y = y_ref[...]
        scl = sc_ref[...]
        dy = duv * (scl * (g * sg))
        dg_ref[...] = (duv * y * scl * (sg * (1.0 + g * (1.0 - sg)))).astype(ACT)
        dsc_ref[...] += jnp.sum(duv * y * (g * sg), axis=0, keepdims=True)
        dpw_ref[...] += _dot(p_scr[...], dy, TN)
        dpc_ref[...] = _dot(dy, pw_ref[...], NT) / c_scr[:, :1]

    blk = pl.BlockSpec((T, POOL_GD), lambda g, i: (i, g))
    return pl.pallas_call(
        body, grid=(G, S // T),
        in_specs=[pl.BlockSpec((S, POOL_GD), lambda g, i: (0, pv + g)),
                  pl.BlockSpec((T, POOL_GD), lambda g, i: (i, pg + g)),
                  pl.BlockSpec((None, POOL_GD, POOL_GD), lambda g, i: (g, 0, 0)),
                  pl.BlockSpec((1, POOL_GD), lambda g, i: (0, g)), blk, blk],
        out_specs=[blk, blk, pl.BlockSpec((1, POOL_GD), lambda g, i: (0, g)),
                   pl.BlockSpec((None, POOL_GD, POOL_GD), lambda g, i: (g, 0, 0))],
        out_shape=[jax.ShapeDtypeStruct((S, W), F32), jax.ShapeDtypeStruct((S, W), ACT),
                   jax.ShapeDtypeStruct((1, W), F32), jax.ShapeDtypeStruct((G, POOL_GD, POOL_GD), F32)],
        scratch_shapes=[pltpu.VMEM((S + 2 * POOL_PAD, POOL_GD), F32), pltpu.VMEM((T, POOL_GD), F32),
                        pltpu.VMEM((T, 128), F32)],
        name="pool_bwd_a", compiler_params=_params("parallel", "arbitrary"))(z, z, pw, scale, y_raw, du)


def _pool_bwd_b(dpc):
    S, W = dpc.shape
    G = len(POOL_WINDOWS)
    T = min(512, S)

    def body(x_ref, o_ref, pad, acc_scr):
        grp = pl.program_id(0)
        i = pl.program_id(1)

        @pl.when(i == 0)
        def _():
            _fill_padded(pad, x_ref, S)

        r0 = pl.multiple_of(i * T, T)
        ext = pad[pl.ds(r0, T + 2 * POOL_PAD), :]
        pos = r0 + lax.broadcasted_iota(jnp.int32, (T, 1), 0)
        for gi, w in enumerate(POOL_WINDOWS):
            @pl.when(grp == gi)
            def _(w=w):
                acc = _window_sum(ext, T, -(w // 2) + 1, w // 2 + 1)
                acc_scr[...] = acc - ext[POOL_PAD:POOL_PAD + T] * _window_count(pos, w, S)

        o_ref[...] = acc_scr[...].astype(ACT)

    blk = pl.BlockSpec((T, POOL_GD), lambda g, i: (i, g))
    return pl.pallas_call(
        body, grid=(G, S // T),
        in_specs=[pl.BlockSpec((S, POOL_GD), lambda g, i: (0, g))], out_specs=blk,
        out_shape=jax.ShapeDtypeStruct((S, W), ACT),
        scratch_shapes=[pltpu.VMEM((S + 2 * POOL_PAD, POOL_GD), F32), pltpu.VMEM((T, POOL_GD), F32)],
        name="pool_bwd_b", compiler_params=_params("parallel", "arbitrary"))(dpc)


def _rope128(x, cf, sa, sb):
    return x * cf + pltpu.roll(x, ROPE_HALF, 1) * sa + pltpu.roll(x, ATT_HD - ROPE_HALF, 1) * sb


def _rope128_t(g, cf, sa, sb):
    return g * cf + pltpu.roll(g * sa, ATT_HD - ROPE_HALF, 1) + pltpu.roll(g * sb, ROPE_HALF, 1)


def _attn_prep(z, qgain, kgain, cf, sa, sb, aq, ak, av):
    S = z.shape[0]
    T = min(512, S)
    QW, KW = ATT_Q * ATT_HD, ATT_KV * ATT_HD

    def body(q_ref, k_ref, v_ref, qg_ref, kg_ref, cf_ref, sa_ref, sb_ref, qn_ref, kn_ref, vb_ref):
        cfv, sav, sbv = cf_ref[...], sa_ref[...], sb_ref[...]

        def prep(x, gain):
            r = lax.rsqrt(jnp.mean(x * x, axis=-1, keepdims=True) + RMS_EPS)
            return _rope128(x * r * gain, cfv, sav, sbv)

        for hh in range(ATT_Q):
            sl = slice(hh * ATT_HD, (hh + 1) * ATT_HD)
            qn_ref[:, sl] = prep(q_ref[:, sl], qg_ref[...]).astype(ACT)
        for hh in range(ATT_KV):
            sl = slice(hh * ATT_HD, (hh + 1) * ATT_HD)
            kn_ref[:, sl] = prep(k_ref[:, sl], kg_ref[...]).astype(ACT)
        vb_ref[...] = v_ref[...].astype(ACT)

    tab = pl.BlockSpec((T, ATT_HD), lambda i: (i, 0))
    gain = pl.BlockSpec((1, ATT_HD), lambda i: (0, 0))
    return pl.pallas_call(
        body, grid=(S // T,),
        in_specs=[pl.BlockSpec((T, QW), lambda i: (i, aq)), pl.BlockSpec((T, KW), lambda i: (i, ak)),
                  pl.BlockSpec((T, KW), lambda i: (i, av)), gain, gain, tab, tab, tab],
        out_specs=[pl.BlockSpec((T, QW), lambda i: (i, 0)), pl.BlockSpec((T, KW), lambda i: (i, 0)),
                   pl.BlockSpec((T, KW), lambda i: (i, 0))],
        out_shape=[jax.ShapeDtypeStruct((S, QW), ACT), jax.ShapeDtypeStruct((S, KW), ACT),
                   jax.ShapeDtypeStruct((S, KW), ACT)],
        name="attn_prep", compiler_params=_params("parallel"))(z, z, z, qgain, kgain, cf, sa, sb)


def _attn_window(i, S):
    start = jnp.clip(i * ATT_BLK - ATT_BLK, 0, S - ATT_SPAN)
    start = pl.multiple_of(start, ATT_BLK)
    qpos = i * ATT_BLK + lax.broadcasted_iota(jnp.int32, (ATT_BLK, ATT_SPAN), 0)
    kpos = start + lax.broadcasted_iota(jnp.int32, (ATT_BLK, ATT_SPAN), 1)
    return start, jnp.abs(kpos - qpos) <= ATT_WIN


def _attn_probs(q, kw, valid, sink):
    s = _dot(q, kw, NT) * (ATT_HD ** -0.5)
    s = jnp.where(valid, s, NEG_BIG)
    m = jnp.maximum(jnp.max(s, axis=-1, keepdims=True), sink)
    p = jnp.exp(s - m)
    es = jnp.exp(sink - m)
    den = jnp.sum(p, axis=-1, keepdims=True) + es
    return p / den, es / den


def _attn_fwd(qn, kn, vb, z, sink, ag):
    S = qn.shape[0]
    nB = S // ATT_BLK
    assert S >= ATT_SPAN
    QW = ATT_Q * ATT_HD

    GW = ATT_G * ATT_HD

    def body(q_ref, k_ref, v_ref, g_ref, sink_ref, o_ref, u_ref, ut_ref):
        kvh = pl.program_id(0)
        i = pl.program_id(1)
        start, valid = _attn_window(i, S)
        kw = k_ref[pl.ds(start, ATT_SPAN), :]
        vw = v_ref[pl.ds(start, ATT_SPAN), :]
        for gi in range(ATT_G):
            sl = slice(gi * ATT_HD, (gi + 1) * ATT_HD)
            sk = jnp.full((ATT_BLK, 1), sink_ref[kvh * ATT_G + gi], F32)
            pn, _ = _attn_probs(q_ref[:, sl], kw, valid, sk)
            o = _dot(pn, vw, NN)
            o_ref[:, sl] = o
            g = g_ref[:, sl]
            u = o * (g * _sigmoid(g))
            u_ref[:, sl] = u.astype(ACT)
            ut_ref[sl, :] = u.T.astype(ACT)

    blk = pl.BlockSpec((ATT_BLK, GW), lambda k, i: (i, k))
    kv = pl.BlockSpec((S, ATT_HD), lambda k, i: (0, k))
    return pl.pallas_call(
        body, grid=(ATT_KV, nB),
        in_specs=[blk, kv, kv, pl.BlockSpec((ATT_BLK, GW), lambda k, i: (i, ag // ATT_G + k)), SMEM_SPEC],
        out_specs=[blk, blk, pl.BlockSpec((GW, ATT_BLK), lambda k, i: (k, i))],
        out_shape=[jax.ShapeDtypeStruct((S, QW), F32), jax.ShapeDtypeStruct((S, QW), ACT),
                   jax.ShapeDtypeStruct((QW, S), ACT)],
        name="attn_fwd", compiler_params=_params("parallel", "parallel"))(qn, kn, vb, z, sink)


def _attn_bwd(qn, kn, vb, o, du, z, sink, ag):
    S = qn.shape[0]
    nB = S // ATT_BLK
    QW, KW = ATT_Q * ATT_HD, ATT_KV * ATT_HD
    GW = ATT_G * ATT_HD

    def body(q_ref, k_ref, v_ref, o_ref, du_ref, g_ref, sink_ref,
             dq_ref, dk_ref, dv_ref, dg_ref, ds_ref):
        kvh = pl.program_id(0)
        i = pl.program_id(1)

        @pl.when(i == 0)
        def _():
            dk_ref[...] = jnp.zeros_like(dk_ref)
            dv_ref[...] = jnp.zeros_like(dv_ref)
            ds_ref[...] = jnp.zeros_like(ds_ref)

        start, valid = _attn_window(i, S)
        kw = k_ref[pl.ds(start, ATT_SPAN), :]
        vw = v_ref[pl.ds(start, ATT_SPAN), :]
        row = lax.broadcasted_iota(jnp.int32, (8, 128), 0)
        lane = lax.broadcasted_iota(jnp.int32, (8, 128), 1)
        dsink = jnp.zeros((8, 128), F32)
        dk_acc = jnp.zeros((ATT_SPAN, ATT_HD), F32)
        dv_acc = jnp.zeros((ATT_SPAN, ATT_HD), F32)
        for gi in range(ATT_G):
            sl = slice(gi * ATT_HD, (gi + 1) * ATT_HD)
            q = q_ref[:, sl]
            ov = o_ref[:, sl]
            g = g_ref[:, sl]
            duv = du_ref[:, sl]
            sg = _sigmoid(g)
            do = duv * (g * sg)
            dg_ref[:, sl] = (duv * ov * (sg * (1.0 + g * (1.0 - sg)))).astype(ACT)
            sk = jnp.full((ATT_BLK, 1), sink_ref[kvh * ATT_G + gi], F32)
            pn, psink = _attn_probs(q, kw, valid, sk)
            delta = jnp.sum(do * ov, axis=-1, keepdims=True)
            dsc = pn * (_dot(do, vw, NT) - delta) * (ATT_HD ** -0.5)
            dq_ref[:, sl] = _dot(dsc, kw, NN)
            dk_acc = dk_acc + _dot(dsc, q, TN)
            dv_acc = dv_acc + _dot(pn, do, TN)
            dsink = dsink + jnp.where((row == 0) & (lane == gi), -_sum_all(psink * delta), 0.0)
        dk_ref[pl.ds(start, ATT_SPAN), :] += dk_acc
        dv_ref[pl.ds(start, ATT_SPAN), :] += dv_acc
        ds_ref[...] += dsink

    grp = pl.BlockSpec((ATT_BLK, GW), lambda k, i: (i, k))
    kv = pl.BlockSpec((S, ATT_HD), lambda k, i: (0, k))
    return pl.pallas_call(
        body, grid=(ATT_KV, nB),
        in_specs=[grp, kv, kv, grp, grp,
                  pl.BlockSpec((ATT_BLK, GW), lambda k, i: (i, ag // ATT_G + k)), SMEM_SPEC],
        out_specs=[grp, kv, kv, grp, pl.BlockSpec((None, 8, 128), lambda k, i: (k, 0, 0))],
        out_shape=[jax.ShapeDtypeStruct((S, QW), F32), jax.ShapeDtypeStruct((S, KW), F32),
                   jax.ShapeDtypeStruct((S, KW), F32), jax.ShapeDtypeStruct((S, QW), ACT),
                   jax.ShapeDtypeStruct((ATT_KV, 8, 128), F32)],
        name="attn_bwd", compiler_params=_params("parallel", "arbitrary"))(qn, kn, vb, o, du, z, sink)


def _attn_prep_bwd(z, dqn, dkn, dv, qgain, kgain, cf, sa, sb, aq, ak):
    S = z.shape[0]
    T = min(512, S)
    QW, KW = ATT_Q * ATT_HD, ATT_KV * ATT_HD

    def body(q_ref, k_ref, dqn_ref, dkn_ref, dv_ref, qg_ref, kg_ref, cf_ref, sa_ref, sb_ref,
             dq_ref, dk_ref, dvb_ref, dqg_ref, dkg_ref):
        cfv, sav, sbv = cf_ref[...], sa_ref[...], sb_ref[...]

        @pl.when(pl.program_id(0) == 0)
        def _():
            dqg_ref[...] = jnp.zeros_like(dqg_ref)
            dkg_ref[...] = jnp.zeros_like(dkg_ref)

        def back(x, gn, gain):
            r = lax.rsqrt(jnp.mean(x * x, axis=-1, keepdims=True) + RMS_EPS)
            xh = x * r
            dy = _rope128_t(gn, cfv, sav, sbv)
            dxh = dy * gain
            dx = r * (dxh - xh * jnp.mean(dxh * xh, axis=-1, keepdims=True))
            return dx, jnp.sum(dy * xh, axis=0, keepdims=True)

        dqg = jnp.zeros((1, ATT_HD), F32)
        for hh in range(ATT_Q):
            sl = slice(hh * ATT_HD, (hh + 1) * ATT_HD)
            dx, dgn = back(q_ref[:, sl], dqn_ref[:, sl], qg_ref[...])
            dq_ref[:, sl] = dx.astype(ACT)
            dqg = dqg + dgn
        dkg = jnp.zeros((1, ATT_HD), F32)
        for hh in range(ATT_KV):
            sl = slice(hh * ATT_HD, (hh + 1) * ATT_HD)
            dx, dgn = back(k_ref[:, sl], dkn_ref[:, sl], kg_ref[...])
            dk_ref[:, sl] = dx.astype(ACT)
            dkg = dkg + dgn
        dvb_ref[...] = dv_ref[...].astype(ACT)
        dqg_ref[...] += dqg
        dkg_ref[...] += dkg

    tab = pl.BlockSpec((T, ATT_HD), lambda i: (i, 0))
    gain = pl.BlockSpec((1, ATT_HD), lambda i: (0, 0))
    qb = pl.BlockSpec((T, QW), lambda i: (i, 0))
    kb = pl.BlockSpec((T, KW), lambda i: (i, 0))
    return pl.pallas_call(
        body, grid=(S // T,),
        in_specs=[pl.BlockSpec((T, QW), lambda i: (i, aq)), pl.BlockSpec((T, KW), lambda i: (i, ak)),
                  qb, kb, kb, gain, gain, tab, tab, tab],
        out_specs=[qb, kb, kb, gain, gain],
        out_shape=[jax.ShapeDtypeStruct((S, QW), ACT), jax.ShapeDtypeStruct((S, KW), ACT),
                   jax.ShapeDtypeStruct((S, KW), ACT), jax.ShapeDtypeStruct((1, ATT_HD), F32),
                   jax.ShapeDtypeStruct((1, ATT_HD), F32)],
        name="attn_prep_bwd", compiler_params=_params("arbitrary"))(
            z, z, dqn, dkn, dv, qgain, kgain, cf, sa, sb)


def _branch_merge(ua, ub, uc, wr, wp, wa, z, mg):
    S, W = ua.shape
    D = wr.shape[1]
    tm, tn = min(512, S), min(512, D)
    nb = D // tn

    def body(ua_ref, ub_ref, uc_ref, wr_ref, wp_ref, wa_ref, g0_ref, g1_ref, g2_ref,
             ya_ref, yb_ref, yc_ref, m_ref, mt_ref):
        ya = _dot(ua_ref[...], wr_ref[...], NN)
        yb = _dot(ub_ref[...], wp_ref[...], NN)
        yc = _dot(uc_ref[...], wa_ref[...], NN)
        ya_ref[...] = ya.astype(ACT)
        yb_ref[...] = yb.astype(ACT)
        yc_ref[...] = yc.astype(ACT)
        m = _sigmoid(g0_ref[...]) * ya + _sigmoid(g1_ref[...]) * yb + _sigmoid(g2_ref[...]) * yc
        m_ref[...] = m.astype(ACT)
        mt_ref[...] = m.T.astype(ACT)

    u = pl.BlockSpec((tm, W), lambda i, j: (i, 0))
    w = pl.BlockSpec((W, tn), lambda i, j: (0, j))
    o = pl.BlockSpec((tm, tn), lambda i, j: (i, j))

    assert (mg * POOL_GD) % tn == 0
    base = (mg * POOL_GD) // tn

    def gate(k):
        return pl.BlockSpec((tm, tn), lambda i, j: (i, base + k * nb + j))

    sd = jax.ShapeDtypeStruct((S, D), ACT)
    return pl.pallas_call(
        body, grid=(S // tm, nb), in_specs=[u, u, u, w, w, w, gate(0), gate(1), gate(2)],
        out_specs=[o, o, o, o, pl.BlockSpec((tn, tm), lambda i, j: (j, i))],
        out_shape=[sd, sd, sd, sd, jax.ShapeDtypeStruct((D, S), ACT)], name="branch_merge",
        compiler_params=_params("parallel", "parallel"))(ua, ub, uc, wr, wp, wa, z, z, z)


def _merge_bwd(dxb, wo, ya, yb, yc, z, mg):
    S, D = dxb.shape
    tm, tn = min(512, S), min(512, D)
    nb = D // tn
    base = (mg * POOL_GD) // tn

    def body(dx_ref, wo_ref, ya_ref, yb_ref, yc_ref, g0_ref, g1_ref, g2_ref,
             da_ref, db_ref, dc_ref, dg0_ref, dg1_ref, dg2_ref):
        dm = _dot(dx_ref[...], wo_ref[...], NT)
        for y_ref, g_ref, dy_ref, dg_ref in ((ya_ref, g0_ref, da_ref, dg0_ref),
                                             (yb_ref, g1_ref, db_ref, dg1_ref),
                                             (yc_ref, g2_ref, dc_ref, dg2_ref)):
            sg = _sigmoid(g_ref[...])
            dy_ref[...] = (sg * dm).astype(ACT)
            dg_ref[...] = (dm * y_ref[...].astype(F32) * (sg * (1.0 - sg))).astype(ACT)

    o = pl.BlockSpec((tm, tn), lambda i, j: (i, j))

    def gate(k):
        return pl.BlockSpec((tm, tn), lambda i, j: (i, base + k * nb + j))

    sd = jax.ShapeDtypeStruct((S, D), ACT)
    return pl.pallas_call(
        body, grid=(S // tm, nb),
        in_specs=[pl.BlockSpec((tm, D), lambda i, j: (i, 0)), pl.BlockSpec((tn, D), lambda i, j: (j, 0)),
                  o, o, o, gate(0), gate(1), gate(2)],
        out_specs=[o] * 6, out_shape=[sd] * 6, name="merge_bwd",
        compiler_params=_params("parallel", "parallel"))(dxb, wo, ya, yb, yc, z, z, z)


def _loss_head(y, t):
    S, D = y.shape
    tm = min(256, S)

    def body(y_ref, t_ref, dy_ref, dyb_ref, l_ref):
        e = y_ref[...] - t_ref[...]
        dy = e * (1.0 / D)
        dy_ref[...] = dy
        dyb_ref[...] = dy.astype(ACT)

        @pl.when(pl.program_id(0) == 0)
        def _():
            l_ref[...] = jnp.zeros_like(l_ref)

        l_ref[...] += jnp.sum(jnp.mean(e * e, axis=-1, keepdims=True), axis=0, keepdims=True)

    row = pl.BlockSpec((tm, D), lambda i: (i, 0))
    return pl.pallas_call(
        body, grid=(S // tm,), in_specs=[row, row],
        out_specs=[row, row, pl.BlockSpec((1, 1), lambda i: (0, 0))],
        out_shape=[jax.ShapeDtypeStruct((S, D), F32), jax.ShapeDtypeStruct((S, D), ACT),
                   jax.ShapeDtypeStruct((1, 1), F32)],
        name="loss_head", compiler_params=_params("arbitrary"))(y, t)


def _adamw(w, layer, parts, m, v, name):
    L = w.shape[0]
    shape = w.shape[1:]
    C = shape[-1]
    R = int(np.prod(shape[:-1]))
    w3, m3, v3 = (a.reshape(L, R, C) for a in (w, m, v))
    p3 = parts.reshape(N_DEV, R, C)
    tr = min(64, R)
    assert R % tr == 0
    c1 = 1.0 / (1.0 - ADAM_B1 ** ADAM_STEP)
    c2 = 1.0 / (1.0 - ADAM_B2 ** ADAM_STEP)

    def body(w_ref, p_ref, m_ref, v_ref, g_ref, d_ref, nm_ref, nv_ref):
        g = p_ref[0].astype(F32)
        for k in range(1, N_DEV):
            g = g + p_ref[k].astype(F32)
        nm = ADAM_B1 * m_ref[...] + (1.0 - ADAM_B1) * g
        nv = ADAM_B2 * v_ref[...] + (1.0 - ADAM_B2) * (g * g)
        g_ref[...] = g
        nm_ref[...] = nm
        nv_ref[...] = nv
        d_ref[...] = -ADAM_LR * ((nm * c1) / (jnp.sqrt(nv * c2) + ADAM_EPS) + ADAM_WD * w_ref[...])

    lay = pl.BlockSpec((None, tr, C), lambda i: (layer, i, 0))
    out = pl.BlockSpec((tr, C), lambda i: (i, 0))
    sd = jax.ShapeDtypeStruct((R, C), F32)
    outs = pl.pallas_call(
        body, grid=(R // tr,),
        in_specs=[lay, pl.BlockSpec((N_DEV, tr, C), lambda i: (0, i, 0)), lay, lay],
        out_specs=[out] * 4, out_shape=[sd] * 4, name=name,
        compiler_params=_params("parallel"))(w3, p3, m3, v3)
    return [a.reshape(shape) for a in outs]


def _exchange(arrs, scatter, name):
    n = len(arrs)
    out_shape = [jax.ShapeDtypeStruct(a.shape if scatter else (N_DEV,) + a.shape, a.dtype) for a in arrs]

    def body(*refs):
        ins, outs = refs[:n], refs[n:2 * n]
        send_sems, recv_sems, local_sems = refs[2 * n:]
        x, y, c = lax.axis_index("x"), lax.axis_index("y"), lax.axis_index("c")
        me = 4 * x + 2 * y + c
        copies = []
        for a in range(n):
            src = ins[a].at[me] if scatter else ins[a]
            own = pltpu.make_async_copy(src, outs[a].at[me], local_sems.at[a])
            own.start()
            copies.append(own)
        sends, recvs = [], []
        for k in range(1, N_DEV):
            px, py, pc = x ^ (k >> 2), y ^ ((k >> 1) & 1), c ^ (k & 1)
            peer = 4 * px + 2 * py + pc
            for a in range(n):
                src = ins[a].at[peer] if scatter else ins[a]
                cp = pltpu.make_async_remote_copy(
                    src_ref=src, dst_ref=outs[a].at[me],
                    send_sem=send_sems.at[a, k - 1], recv_sem=recv_sems.at[a, k - 1],
                    device_id=(px, py, pc), device_id_type=pl.DeviceIdType.MESH)
                cp.start()
                sends.append(cp)
                recvs.append(pltpu.make_async_remote_copy(
                    src_ref=src, dst_ref=outs[a].at[peer],
                    send_sem=send_sems.at[a, k - 1], recv_sem=recv_sems.at[a, k - 1],
                    device_id=(px, py, pc), device_id_type=pl.DeviceIdType.MESH))
        for cp in recvs:
            cp.wait_recv()
        for cp in sends:
            cp.wait_send()
        for cp in copies:
            cp.wait()

    any_spec = pl.BlockSpec(memory_space=pl.ANY)
    return pl.pallas_call(
        body, in_specs=[any_spec] * n, out_specs=[any_spec] * n, out_shape=out_shape,
        scratch_shapes=[pltpu.SemaphoreType.DMA((n, N_DEV - 1)), pltpu.SemaphoreType.DMA((n, N_DEV - 1)),
                        pltpu.SemaphoreType.DMA((n,))],
        name=name)(*arrs)


HBM_SPEC = pl.BlockSpec(memory_space=pltpu.HBM)
SEM_SPEC = pl.BlockSpec(memory_space=pltpu.SEMAPHORE)
DATAFLOW = pltpu.SideEffectType.DATAFLOW_SIDE_EFFECTING


def _peer_of(k):
    x, y, c = lax.axis_index("x"), lax.axis_index("y"), lax.axis_index("c")
    return x ^ (k >> 2), y ^ ((k >> 1) & 1), c ^ (k & 1)


def _exchange_copy(k, a, src_ref, land_ref, send_sems, recv_sems, scatter, outgoing):
    px, py, pc = _peer_of(k)
    peer = 4 * px + 2 * py + pc
    me = 4 * lax.axis_index("x") + 2 * lax.axis_index("y") + lax.axis_index("c")
    idx = a * (N_DEV - 1) + k - 1
    return pltpu.make_async_remote_copy(
        src_ref=src_ref.at[peer] if scatter else src_ref, dst_ref=land_ref.at[me if outgoing else peer],
        send_sem=send_sems.at[idx], recv_sem=recv_sems.at[idx],
        device_id=(px, py, pc), device_id_type=pl.DeviceIdType.MESH)


def _exchange_start(arrs, scatter, name):
    n = len(arrs)
    land_shapes = [a.shape if scatter else (N_DEV,) + a.shape for a in arrs]

    def body(*refs):
        srcs, lands = refs[:n], refs[n:2 * n]
        send_sems, recv_sems = refs[2 * n], refs[2 * n + 1]
        token = refs[-1]
        for k in range(1, N_DEV):
            for a in range(n):
                _exchange_copy(k, a, srcs[a], lands[a], send_sems, recv_sems, scatter, True).start()
        token[...] = jnp.zeros_like(token)

    sems = pltpu.SemaphoreType.DMA((n * (N_DEV - 1),))
    out_shape = ([sems, sems] + [pltpu.HBM(a.shape, a.dtype) for a in arrs]
                 + [pltpu.HBM(s, a.dtype) for s, a in zip(land_shapes, arrs)]
                 + [jax.ShapeDtypeStruct((8, 128), F32)])
    ins = ([pltpu.with_memory_space_constraint(a, pltpu.HBM) for a in arrs]
           + [pltpu.with_memory_space_constraint(lax.empty(s, a.dtype), pltpu.HBM) for s, a in zip(land_shapes, arrs)])
    res = pl.pallas_call(
        body, name=name, out_shape=out_shape, in_specs=[HBM_SPEC] * (2 * n),
        out_specs=[SEM_SPEC, SEM_SPEC] + [HBM_SPEC] * (2 * n) + [pl.BlockSpec(memory_space=pltpu.VMEM)],
        input_output_aliases={i: 2 + i for i in range(2 * n)},
        compiler_params=pltpu.CompilerParams(has_side_effects=DATAFLOW))(*ins)
    return res[0], res[1], list(res[2:2 + n]), list(res[2 + n:2 + 2 * n]), res[-1]


def _exchange_wait(started, after, scatter, name):
    send_sems, recv_sems, srcs, lands, _ = started
    n = len(srcs)

    def body(*refs):
        src_refs, land_refs = refs[:n], refs[n:2 * n]
        s_sems, r_sems = refs[2 * n], refs[2 * n + 1]
        for k in range(1, N_DEV):
            for a in range(n):
                back = _exchange_copy(k, a, src_refs[a], land_refs[a], s_sems, r_sems, scatter, False)
                back.wait_send()
                back.wait_recv()

    out_shape = [pltpu.HBM(a.shape, a.dtype) for a in srcs] + [pltpu.HBM(a.shape, a.dtype) for a in lands]
    res = pl.pallas_call(
        body, name=name, out_shape=out_shape,
        in_specs=[HBM_SPEC] * (2 * n) + [SEM_SPEC, SEM_SPEC, pl.BlockSpec(memory_space=pl.ANY)],
        out_specs=[HBM_SPEC] * (2 * n), input_output_aliases={i: i for i in range(2 * n)},
        compiler_params=pltpu.CompilerParams(has_side_effects=DATAFLOW))(*srcs, *lands, send_sems, recv_sems, after)
    return list(res[:n]), list(res[n:])


def _own_slot(land, own, me):
    return lax.dynamic_update_index_in_dim(land, own, me, 0)


class _Cols:
    def __init__(self, D):
        Wb = D // 2
        sizes = (Wb, Wb, Wb, Wb, Wb, Wb, Wb, ATT_KV * ATT_HD, ATT_KV * ATT_HD, Wb, 3 * D)
        offs = np.concatenate([[0], np.cumsum(sizes)])
        assert all(int(o) % 256 == 0 for o in offs)
        (self.rq, self.rk, self.rv, self.rg, self.pv, self.pg,
         self.aq, self.ak, self.av, self.ag, self.mg) = (int(o) // 256 for o in offs[:-1])
        self.width = int(offs[-1])
        self.sizes = sizes


def _rope_tables(S):
    pos = jnp.arange(S, dtype=F32)[:, None]
    inv_r = 1.0 / (RET_ROPE_BASE ** jnp.linspace(0.0, 1.0, RET_HD // 2, dtype=F32))
    ang_r = pos * inv_r[None, :]
    inv_a = ROPE_THETA ** (-jnp.arange(ROPE_HALF, dtype=F32) / ROPE_HALF)
    ang_a = pos * inv_a[None, :]
    ca, sa = jnp.cos(ang_a), jnp.sin(ang_a)
    z16 = jnp.zeros((S, ROPE_HALF), F32)
    rest = ATT_HD - 2 * ROPE_HALF
    cf = jnp.concatenate([ca, ca, jnp.ones((S, rest), F32)], axis=1)
    s_up = jnp.concatenate([z16, sa, jnp.zeros((S, rest), F32)], axis=1)
    s_dn = jnp.concatenate([-sa, z16, jnp.zeros((S, rest), F32)], axis=1)
    return jnp.cos(ang_r), jnp.sin(ang_r), cf, s_up, s_dn


def _layer_fwd(x, p, tabs, cols, deps=()):
    cos_r, sin_r, cf, s_up, s_dn = tabs
    S, D = x.shape
    h, ht = _rmsnorm_fwd(x, p["norm_g"], deps)
    z = _matmul(h, p["w_in"], "nn", F32, 1024, 512, D, "in_proj")
    if "rest" in p:
        p = {**{k: v for k, v in p.items() if k != "rest"}, **p["rest"](z)}
    rcols = (cols.rq, cols.rk, cols.rv, cols.rg)
    sf, sb = _ret_state(z, cols.rk, RET_HD ** -0.5, z, cols.rv, cos_r, sin_r, p["af"], p["ab"], "fwd",
                        "ret_state_fwd")
    o_ret, ua, uat = _ret_fwd(z, cos_r, sin_r, sf, sb, p["af"], p["ab"], rcols)
    y_pool, ub, ubt = _pool_fwd(z, p["pool_w"], p["pool_scale"], cols.pv, cols.pg)
    qn, kn, vb = _attn_prep(z, p["q_gain"], p["k_gain"], cf, s_up, s_dn,
                            cols.aq // 4, cols.ak, cols.av)
    o_att, uc, uct = _attn_fwd(qn, kn, vb, z, p["sink"], cols.ag * 2)
    ya, yb, yc, merged, mergedt = _branch_merge(ua, ub, uc, p["w_ret"], p["w_pool"], p["w_att"], z, cols.mg)
    out = _matmul(merged, p["w_out"], "nn", F32, 1024, 512, D, "out_proj", res=x)
    saved = dict(x=x, ht=ht, z=z, sf=sf, sb=sb, o_ret=o_ret, uat=uat, y_pool=y_pool, ubt=ubt,
                 qn=qn, kn=kn, vb=vb, o_att=o_att, uct=uct, ya=ya, yb=yb, yc=yc, mergedt=mergedt)
    return out, saved, p


def _layer_bwd(dx, dxb, p, sv, tabs, cols, on_grads):
    cos_r, sin_r, cf, s_up, s_dn = tabs
    z = sv["z"]
    S, D = dx.shape
    dya, dyb, dyc, dmg0, dmg1, dmg2 = _merge_bwd(dxb, p["w_out"], sv["ya"], sv["yb"], sv["yc"], z, cols.mg)
    grads = {"w_out": _matmul(sv["mergedt"], dxb, "nn", ACT, 1024, 512, S, "dw_out")}
    dus = {}
    for nm, ut, dy in (("ret", sv["uat"], dya), ("pool", sv["ubt"], dyb), ("att", sv["uct"], dyc)):
        dus[nm] = _matmul(dy, p["w_" + nm], "nt", F32, 1024, 512, D, "du_" + nm)
        grads["w_" + nm] = _matmul(ut, dy, "nn", ACT, 1024, 512, S, "dw_" + nm)
    rcols = (cols.rq, cols.rk, cols.rv, cols.rg)
    do_ret, d_rg = _ret_gate_bwd(dus["ret"], sv["o_ret"], z, cols.rg)
    eb, ef = _ret_state(z, cols.rq, 1.0, do_ret, 0, cos_r, sin_r, p["af"], p["ab"], "bwd", "ret_state_bwd")
    d_rq, d_rk, d_rv, d_decay = _ret_bwd(z, do_ret, cos_r, sin_r, sv["sf"], sv["sb"], ef, eb,
                                         p["af"], p["ab"], rcols)
    dpc, d_pg, d_pscale, g_pool_w = _pool_bwd_a(z, p["pool_w"], p["pool_scale"], sv["y_pool"], dus["pool"],
                                                cols.pv, cols.pg)
    d_pv = _pool_bwd_b(dpc)
    grads["pool_w"] = g_pool_w.astype(ACT)
    dqn, dkn, dvv, d_ag, d_sink = _attn_bwd(sv["qn"], sv["kn"], sv["vb"], sv["o_att"], dus["att"], z,
                                            p["sink"], cols.ag * 2)
    d_aq, d_ak, d_av, d_qg, d_kg = _attn_prep_bwd(z, dqn, dkn, dvv, p["q_gain"], p["k_gain"], cf, s_up, s_dn,
                                                  cols.aq // 4, cols.ak)
    dz = jnp.concatenate([d_rq, d_rk, d_rv, d_rg, d_pv, d_pg, d_aq, d_ak, d_av, d_ag, dmg0, dmg1, dmg2],
                         axis=1)
    grads["w_in"] = _matmul(sv["ht"], dz, "nn", ACT, 1024, 512, S, "dw_in")
    tok = on_grads(grads)
    dh = _matmul(dz, p["w_in"], "nt", F32, 1024, 512, 29 * 128, "dh", deps=() if tok is None else (tok,))
    dx_in, dxb_in, d_norm_g = _rmsnorm_bwd(sv["x"], p["norm_g"], dh, dx)
    misc = jnp.concatenate([d_decay[:, 0, 0], d_decay[:, 0, 1], d_sink[:, 0, :ATT_G].reshape(-1)])
    misc = jnp.pad(misc, (0, 128 - misc.shape[0]))[None, :]
    small = jnp.concatenate([d_norm_g, d_pscale, d_qg, d_kg, misc], axis=1)
    return dx_in, dxb_in, small


def _pack_small(norm_g, pool_scale, q_gain, k_gain, af, ab, sink):
    L = norm_g.shape[0]
    misc = jnp.concatenate([af, ab, sink], axis=1)
    misc = jnp.pad(misc, ((0, 0), (0, 128 - misc.shape[1])))
    return jnp.concatenate([norm_g, pool_scale, q_gain, k_gain, misc], axis=1)


def _unpack_small(a, D):
    Wb = D // 2
    o = np.cumsum([0, D, Wb, ATT_HD, ATT_HD])
    misc = a[:, o[4]:]
    return (a[:, o[0]:o[1]], a[:, o[1]:o[2]], a[:, o[2]:o[3]], a[:, o[3]:o[4]],
            misc[:, :RET_HEADS], misc[:, RET_HEADS:2 * RET_HEADS],
            misc[:, 2 * RET_HEADS:2 * RET_HEADS + ATT_Q])


def _local_step(x, t, n_layers, get_layer, on_grads, tabs, cols, first_dep=None):
    saved, layers = [], []
    after = first_dep
    for l in range(n_layers):
        p = get_layer(l, after)
        x, sv, p = _layer_fwd(x, p, tabs, cols, (first_dep,) if (l == 0 and first_dep is not None) else ())
        after = x
        layers.append(p)
        saved.append(sv)
    dx, dxb, lsum = _loss_head(x, t)
    smalls = []
    for l in reversed(range(n_layers)):
        dx, dxb, sm = _layer_bwd(dx, dxb, layers[l], saved[l], tabs, cols, functools.partial(on_grads, l))
        smalls.append(sm)
    return 0.5 * lsum[0, 0], dx, jnp.concatenate(smalls[::-1], axis=0)


WEIGHT_KEYS = ("w_in", "w_ret", "w_pool", "w_att", "w_out", "pool_w")


def kernel(x, norm_g, w_in, ret_decay_fwd, ret_decay_bwd, pool_w, pool_scale, attn_q_gain, attn_k_gain, attn_sink, w_ret, w_pool, w_att, w_out, loss_target, m_norm_g, m_w_in, m_ret_decay_fwd, m_ret_decay_bwd, m_pool_w, m_pool_scale, m_attn_q_gain, m_attn_k_gain, m_attn_sink, m_w_ret, m_w_pool, m_w_att, m_w_out, v_norm_g, v_w_in, v_ret_decay_fwd, v_ret_decay_bwd, v_pool_w, v_pool_scale, v_attn_q_gain, v_attn_k_gain, v_attn_sink, v_w_ret, v_w_pool, v_w_att, v_w_out):
    L = norm_g.shape[0]
    _, S, D = x.shape
    Wb = D // 2
    G = len(POOL_WINDOWS)
    cols = _Cols(D)
    tabs = _rope_tables(S)
    me = 4 * lax.axis_index("x") + 2 * lax.axis_index("y") + lax.axis_index("c")
    weights = dict(w_in=w_in, w_ret=w_ret, w_pool=w_pool, w_att=w_att, w_out=w_out, pool_w=pool_w)

    gathers, tok = [], None
    for l in range(L):
        started = []
        for part, keys in (("a", WEIGHT_KEYS[:1]), ("b", WEIGHT_KEYS[1:])):
            shards = []
            for k in keys:
                w = weights[k][l]
                if tok is not None:
                    w = w + tok[0, 0]
                shards.append(w.astype(MXU))
            st = _exchange_start(shards, False, f"gather_start_{l}{part}")
            started.append(st)
            tok = st[-1]
        gathers.append(started)

    def cols_full(g, rows):
        return jnp.transpose(g, (1, 0, 2)).reshape(rows, -1)

    def get_layer(l, after):
        srcs, lands = _exchange_wait(gathers[l][0], after, False, f"gather_wait_{l}a")
        g_in = _own_slot(lands[0], srcs[0], me)

        def rest(z):
            srcs, lands = _exchange_wait(gathers[l][1], z, False, f"gather_wait_{l}b")
            g_ret, g_pool, g_att, g_out, g_pw = [_own_slot(ld, sr, me) for ld, sr in zip(lands, srcs)]
            return dict(
                w_ret=cols_full(g_ret, Wb), w_pool=cols_full(g_pool, Wb), w_att=cols_full(g_att, Wb),
                w_out=g_out.reshape(D, D),
                pool_w=jnp.transpose(g_pw, (1, 0, 2, 3)).reshape(G, POOL_GD, POOL_GD))

        return dict(
            norm_g=norm_g[l][None, :], w_in=cols_full(g_in, D), rest=rest,
            pool_scale=pool_scale[l][None, :], q_gain=attn_q_gain[l][None, :], k_gain=attn_k_gain[l][None, :],
            af=ret_decay_fwd[l], ab=ret_decay_bwd[l], sink=attn_sink[l])

    def col_slots(g, rows):
        return jnp.transpose(g.reshape(rows, N_DEV, -1), (1, 0, 2))

    scatters = {}

    def on_grads(l, g):
        slots = [col_slots(g["w_in"], D), col_slots(g["w_ret"], Wb), col_slots(g["w_pool"], Wb),
                 col_slots(g["w_att"], Wb), g["w_out"].reshape(N_DEV, D // N_DEV, D),
                 jnp.transpose(g["pool_w"].reshape(G, N_DEV, POOL_GD // N_DEV, POOL_GD), (1, 0, 2, 3))]
        scatters[l] = _exchange_start(slots, True, f"scatter_start_{l}")
        return scatters[l][-1]

    loss_local, grad_x, small = _local_step(x[0], loss_target[0], L, get_layer, on_grads, tabs, cols, tok)
    loss = lax.psum(loss_local, ("x", "y", "c"))
    small_all, = _exchange([small], False, "gather_small_grads")

    moments = dict(w_in=(m_w_in, v_w_in), w_ret=(m_w_ret, v_w_ret), w_pool=(m_w_pool, v_w_pool),
                   w_att=(m_w_att, v_w_att), w_out=(m_w_out, v_w_out), pool_w=(m_pool_w, v_pool_w))
    per_layer = {}
    after = small_all
    for l in reversed(range(L)):
        srcs, lands = _exchange_wait(scatters[l], after, True, f"scatter_wait_{l}")
        per_layer[l] = {}
        for k, ld, sr in zip(WEIGHT_KEYS, lands, srcs):
            parts = _own_slot(ld, lax.dynamic_index_in_dim(sr, me, 0, keepdims=False), me)
            m, v = moments[k]
            per_layer[l][k] = _adamw(weights[k], l, parts, m, v, "adamw_" + k)
        after = per_layer[l]["w_out"][1]
    res = {k: [jnp.stack([per_layer[l][k][i] for l in range(L)]) for i in range(4)] for k in WEIGHT_KEYS}

    sw = _pack_small(norm_g, pool_scale, attn_q_gain, attn_k_gain, ret_decay_fwd, ret_decay_bwd, attn_sink)
    sm = _pack_small(m_norm_g, m_pool_scale, m_attn_q_gain, m_attn_k_gain, m_ret_decay_fwd, m_ret_decay_bwd,
                     m_attn_sink)
    sv_ = _pack_small(v_norm_g, v_pool_scale, v_attn_q_gain, v_attn_k_gain, v_ret_decay_fwd, v_ret_decay_bwd,
                      v_attn_sink)
    small_res = [_unpack_small(a, D) for a in _adamw(sw[None], 0, small_all, sm[None], sv_[None], "adamw_small")]

    def ordered(i):
        ng, ps, qg, kg, af, ab, sk = small_res[i]
        return (ng, res["w_in"][i], af, ab, res["pool_w"][i], ps, qg, kg, sk,
                res["w_ret"][i], res["w_pool"][i], res["w_att"][i], res["w_out"][i])

    return (loss, grad_x[None], *ordered(0), *ordered(1), *ordered(2), *ordered(3))
```

```python
import functools

import numpy as np
import jax
import jax.numpy as jnp
from jax import lax
from jax.experimental import pallas as pl
from jax.experimental.pallas import tpu as pltpu

F32 = jnp.float32
MXU = jnp.bfloat16
ACT = jnp.bfloat16

N_DEV = 8
RMS_EPS = 1e-6
NEG_BIG = -1e30
RET_HEADS = 4
RET_HD = 256
CH = 128
RET_ROPE_BASE = 10000.0
POOL_WINDOWS = (2, 4, 8, 16)
POOL_GD = 256
POOL_PAD = 8
ATT_HD = 128
ATT_Q = 8
ATT_KV = 2
ATT_G = ATT_Q // ATT_KV
ATT_WIN = 128
ATT_BLK = 128
ATT_SPAN = 3 * ATT_BLK
ROPE_THETA = 500000.0
ROPE_HALF = 16

ADAM_LR = 0.001
ADAM_B1 = 0.9
ADAM_B2 = 0.999
ADAM_EPS = 1e-08
ADAM_WD = 0.01
ADAM_STEP = 10

VMEM_LIMIT = 48 * 1024 * 1024

NN = ((1,), (0,))
NT = ((1,), (1,))
TN = ((0,), (0,))


def _dot(a, b, dims):
    return lax.dot_general(a.astype(MXU), b.astype(MXU), (dims, ((), ())),
                           preferred_element_type=F32)


def _sigmoid(x):
    return 1.0 / (1.0 + jnp.exp(-x))


def _params(*sem):
    return pltpu.CompilerParams(dimension_semantics=sem, vmem_limit_bytes=VMEM_LIMIT)


def _sum_all(x):
    return jnp.sum(jnp.sum(x, axis=1, keepdims=True), axis=0, keepdims=True)


def _fiota(shape, dim):
    return lax.broadcasted_iota(jnp.int32, shape, dim).astype(F32)


SMEM_SPEC = pl.BlockSpec(memory_space=pltpu.SMEM)


def _matmul(a, b, mode, out_dtype, tm, tn, tk, name, res=None, deps=()):
    if mode == "tn":
        K, M = a.shape
    else:
        M, K = a.shape
    N = b.shape[0] if mode == "nt" else b.shape[1]
    tm, tn, tk = min(tm, M), min(tn, N), min(tk, K)
    assert M % tm == 0 and N % tn == 0 and K % tk == 0, (name, M, N, K, tm, tn, tk)
    nk = K // tk
    dims = {"nn": NN, "nt": NT, "tn": TN}[mode]
    a_spec = (pl.BlockSpec((tk, tm), lambda i, j, k: (k, i)) if mode == "tn"
              else pl.BlockSpec((tm, tk), lambda i, j, k: (i, k)))
    b_spec = (pl.BlockSpec((tn, tk), lambda i, j, k: (j, k)) if mode == "nt"
              else pl.BlockSpec((tk, tn), lambda i, j, k: (k, j)))
    o_spec = pl.BlockSpec((tm, tn), lambda i, j, k: (i, j))
    has_res = res is not None
    n_in = 2 + has_res + len(deps)

    def body(*refs):
        a_ref, b_ref = refs[:2]
        r_ref = refs[2] if has_res else None
        o_ref = refs[n_in]

        def finish(out):
            if has_res:
                out = out + r_ref[...]
            o_ref[...] = out.astype(out_dtype)

        if nk == 1:
            finish(_dot(a_ref[...], b_ref[...], dims))
            return
        acc = refs[n_in + 1]
        k = pl.program_id(2)

        @pl.when(k == 0)
        def _():
            acc[...] = jnp.zeros_like(acc)

        acc[...] += _dot(a_ref[...], b_ref[...], dims)

        @pl.when(k == nk - 1)
        def _():
            finish(acc[...])

    ins = [a, b] + ([res] if has_res else []) + list(deps)
    in_specs = ([a_spec, b_spec] + ([o_spec] if has_res else [])
                + [pl.BlockSpec((8, 128), lambda i, j, k: (0, 0))] * len(deps))
    return pl.pallas_call(
        body, grid=(M // tm, N // tn, nk), in_specs=in_specs, out_specs=o_spec,
        out_shape=jax.ShapeDtypeStruct((M, N), out_dtype),
        scratch_shapes=[pltpu.VMEM((tm, tn), F32)] if nk > 1 else [], name=name,
        compiler_params=_params("parallel", "parallel", "arbitrary"))(*ins)


DEP_SPEC1 = pl.BlockSpec((8, 128), lambda i: (0, 0))
DEP_SPEC2 = pl.BlockSpec((8, 128), lambda i, j: (0, 0))


def _rmsnorm_fwd(x, g, deps=()):
    S, D = x.shape
    tm = min(512, S)
    assert S % tm == 0

    def body(x_ref, g_ref, *rest):
        h_ref, ht_ref = rest[-2:]
        xv = x_ref[...]
        r = lax.rsqrt(jnp.mean(xv * xv, axis=-1, keepdims=True) + RMS_EPS)
        hv = xv * r * g_ref[...]
        h_ref[...] = hv.astype(ACT)
        ht_ref[...] = hv.T.astype(ACT)

    row = pl.BlockSpec((tm, D), lambda i: (i, 0))
    return pl.pallas_call(
        body, grid=(S // tm,), in_specs=[row, pl.BlockSpec((1, D), lambda i: (0, 0))] + [DEP_SPEC1] * len(deps),
        out_specs=[row, pl.BlockSpec((D, tm), lambda i: (0, i))],
        out_shape=[jax.ShapeDtypeStruct((S, D), ACT), jax.ShapeDtypeStruct((D, S), ACT)], name="rmsnorm_fwd",
        compiler_params=_params("parallel"))(x, g, *deps)


def _rmsnorm_bwd(x, g, dh, dres):
    S, D = x.shape
    tm = min(256, S)

    def body(x_ref, g_ref, dh_ref, dr_ref, dx_ref, dxb_ref, dg_ref):
        xv = x_ref[...]
        r = lax.rsqrt(jnp.mean(xv * xv, axis=-1, keepdims=True) + RMS_EPS)
        xh = xv * r
        dhv = dh_ref[...]
        dxh = dhv * g_ref[...]
        dx = r * (dxh - xh * jnp.mean(dxh * xh, axis=-1, keepdims=True)) + dr_ref[...]
        dx_ref[...] = dx
        dxb_ref[...] = dx.astype(ACT)

        @pl.when(pl.program_id(0) == 0)
        def _():
            dg_ref[...] = jnp.zeros_like(dg_ref)

        dg_ref[...] += jnp.sum(dhv * xh, axis=0, keepdims=True)

    row = pl.BlockSpec((tm, D), lambda i: (i, 0))
    vec = pl.BlockSpec((1, D), lambda i: (0, 0))
    return pl.pallas_call(
        body, grid=(S // tm,), in_specs=[row, vec, row, row], out_specs=[row, row, vec],
        out_shape=[jax.ShapeDtypeStruct((S, D), F32), jax.ShapeDtypeStruct((S, D), ACT),
                   jax.ShapeDtypeStruct((1, D), F32)],
        name="rmsnorm_bwd", compiler_params=_params("arbitrary"))(x, g, dh, dres)


def _rot256(x, c, s):
    x1, x2 = x[:, :128], x[:, 128:]
    return jnp.concatenate([x1 * c - x2 * s, x2 * c + x1 * s], axis=1)


def _rot256_t(g, c, s):
    g1, g2 = g[:, :128], g[:, 128:]
    return jnp.concatenate([g1 * c + g2 * s, g2 * c - g1 * s], axis=1)


def _log_decay(a_ref, h, shape):
    return -jnp.exp(jnp.full(shape, a_ref[h], F32))


def _ret_state(xsrc, xbase, xscale, ysrc, ybase, cos, sin, af, ab, mode, name):
    S = xsrc.shape[0]
    nC = S // CH
    H = RET_HEADS
    W = H * RET_HD
    assert (xbase * RET_HD) % W == 0 and (ybase * RET_HD) % W == 0

    def body(x1_ref, y1_ref, c1_ref, s1_ref, x2_ref, y2_ref, c2_ref, s2_ref, af_ref, ab_ref,
             st1_ref, st2_ref, acc1, acc2):
        @pl.when(pl.program_id(0) == 0)
        def _():
            acc1[...] = jnp.zeros_like(acc1)
            acc2[...] = jnp.zeros_like(acc2)

        j = _fiota((CH, 1), 0)
        ca, sa, cb, sb_ = c1_ref[...], s1_ref[...], c2_ref[...], s2_ref[...]
        for h in range(H):
            sl = slice(h * RET_HD, (h + 1) * RET_HD)
            lgf = _log_decay(af_ref, h, (CH, 1))
            lgb = _log_decay(ab_ref, h, (CH, 1))
            if mode == "fwd":
                w1, d1 = jnp.exp(lgf * (CH - 1.0 - j)), jnp.exp(lgf[:1] * CH)
                w2, d2 = jnp.exp(lgb * j), jnp.exp(lgb[:1] * CH)
            else:
                w1, d1 = jnp.exp(lgb * (CH - j)), jnp.exp(lgb[:1] * CH)
                w2, d2 = jnp.exp(lgf * (j + 1.0)), jnp.exp(lgf[:1] * CH)
            xa = _rot256(x1_ref[:, sl], ca, sa) * xscale
            st1_ref[h] = acc1[h]
            acc1[h] = d1 * acc1[h] + _dot(xa * w1, y1_ref[:, sl], TN)
            xb = _rot256(x2_ref[:, sl], cb, sb_) * xscale
            st2_ref[h] = acc2[h]
            acc2[h] = d2 * acc2[h] + _dot(xb * w2, y2_ref[:, sl], TN)

    xcol, ycol = (xbase * RET_HD) // W, (ybase * RET_HD) // W
    in_specs = [
        pl.BlockSpec((CH, W), lambda c: (c, xcol)), pl.BlockSpec((CH, W), lambda c: (c, ycol)),
        pl.BlockSpec((CH, 128), lambda c: (c, 0)), pl.BlockSpec((CH, 128), lambda c: (c, 0)),
        pl.BlockSpec((CH, W), lambda c: (nC - 1 - c, xcol)), pl.BlockSpec((CH, W), lambda c: (nC - 1 - c, ycol)),
        pl.BlockSpec((CH, 128), lambda c: (nC - 1 - c, 0)),
        pl.BlockSpec((CH, 128), lambda c: (nC - 1 - c, 0)),
        SMEM_SPEC, SMEM_SPEC]
    out_specs = [pl.BlockSpec((H, None, RET_HD, RET_HD), lambda c: (0, c, 0, 0)),
                 pl.BlockSpec((H, None, RET_HD, RET_HD), lambda c: (0, nC - 1 - c, 0, 0))]
    st = jax.ShapeDtypeStruct((H, nC, RET_HD, RET_HD), F32)
    return pl.pallas_call(
        body, grid=(nC,), in_specs=in_specs, out_specs=out_specs, out_shape=[st, st],
        scratch_shapes=[pltpu.VMEM((H, RET_HD, RET_HD), F32), pltpu.VMEM((H, RET_HD, RET_HD), F32)],
        name=name, compiler_params=_params("arbitrary"))(
            xsrc, ysrc, cos, sin, xsrc, ysrc, cos, sin, af, ab)


def _decay_mask(lgf1, lgb1):
    lag = _fiota((CH, CH), 0) - _fiota((CH, CH), 1)
    alag = jnp.abs(lag)
    return lag, jnp.where(lag >= 0, jnp.exp(lgf1 * alag), jnp.exp(lgb1 * alag))


def _ret_fwd(z, cos, sin, sf, sb, af, ab, cols):
    S = z.shape[0]
    nC = S // CH
    H = RET_HEADS
    rq, rk, rv, rg = cols
    W = H * RET_HD
    assert all((c * RET_HD) % W == 0 for c in cols)

    def body(q_ref, k_ref, v_ref, g_ref, c_ref, s_ref, sf_ref, sb_ref, af_ref, ab_ref,
             o_ref, u_ref, ut_ref):
        j = _fiota((CH, 1), 0)
        c, s = c_ref[...], s_ref[...]
        for h in range(H):
            sl = slice(h * RET_HD, (h + 1) * RET_HD)
            lgf = _log_decay(af_ref, h, (CH, 1))
            lgb = _log_decay(ab_ref, h, (CH, 1))
            q = _rot256(q_ref[:, sl], c, s)
            k = _rot256(k_ref[:, sl], c, s) * (RET_HD ** -0.5)
            _, dm = _decay_mask(lgf[:1], lgb[:1])
            p = _dot(q, k, NT) * dm
            o = (_dot(p, v_ref[:, sl], NN)
                 + _dot(q * jnp.exp(lgf * (j + 1.0)), sf_ref[h], NN)
                 + _dot(q * jnp.exp(lgb * (CH - j)), sb_ref[h], NN))
            o_ref[:, sl] = o
            on = o * lax.rsqrt(jnp.mean(o * o, axis=-1, keepdims=True) + RMS_EPS)
            g = g_ref[:, sl]
            u = on * (g * _sigmoid(g))
            u_ref[:, sl] = u.astype(ACT)
            ut_ref[sl, :] = u.T.astype(ACT)

    def zc(col):
        return pl.BlockSpec((CH, W), lambda c: (c, (col * RET_HD) // W))

    tab = pl.BlockSpec((CH, 128), lambda c: (c, 0))
    stt = pl.BlockSpec((H, None, RET_HD, RET_HD), lambda c: (0, c, 0, 0))
    out = pl.BlockSpec((CH, W), lambda c: (c, 0))
    return pl.pallas_call(
        body, grid=(nC,),
        in_specs=[zc(rq), zc(rk), zc(rv), zc(rg), tab, tab, stt, stt, SMEM_SPEC, SMEM_SPEC],
        out_specs=[out, out, pl.BlockSpec((W, CH), lambda c: (0, c))],
        out_shape=[jax.ShapeDtypeStruct((S, W), F32), jax.ShapeDtypeStruct((S, W), ACT),
                   jax.ShapeDtypeStruct((W, S), ACT)],
        name="ret_fwd", compiler_params=_params("parallel"))(
            z, z, z, z, cos, sin, sf, sb, af, ab)


def _ret_gate_bwd(du, o_pre, z, rg):
    S, W = du.shape
    H = RET_HEADS
    tm = min(512, S)
    assert S % tm == 0

    def body(du_ref, o_ref, g_ref, do_ref, dg_ref):
        o = o_ref[...]
        r = lax.rsqrt(jnp.mean(o * o, axis=-1, keepdims=True) + RMS_EPS)
        on = o * r
        g = g_ref[...]
        sg = _sigmoid(g)
        duv = du_ref[...]
        don = duv * (g * sg)
        dg_ref[...] = (duv * on * (sg * (1.0 + g * (1.0 - sg)))).astype(ACT)
        do_ref[...] = r * (don - on * jnp.mean(don * on, axis=-1, keepdims=True))

    blk = pl.BlockSpec((tm, RET_HD), lambda i, h: (i, h))
    return pl.pallas_call(
        body, grid=(S // tm, H),
        in_specs=[blk, blk, pl.BlockSpec((tm, RET_HD), lambda i, h: (i, rg + h))],
        out_specs=[blk, blk],
        out_shape=[jax.ShapeDtypeStruct((S, W), F32), jax.ShapeDtypeStruct((S, W), ACT)],
        name="ret_gate_bwd", compiler_params=_params("parallel", "parallel"))(du, o_pre, z)


def _ret_bwd(z, do, cos, sin, sf, sb, ef, eb, af, ab, cols):
    S = z.shape[0]
    nC = S // CH
    H = RET_HEADS
    rq, rk, rv, _ = cols
    W = H * RET_HD

    def body(q_ref, k_ref, v_ref, do_ref, c_ref, s_ref, sf_ref, sb_ref, ef_ref, eb_ref,
             af_ref, ab_ref, dq_ref, dk_ref, dv_ref, da_ref):
        @pl.when(pl.program_id(0) == 0)
        def _():
            da_ref[...] = jnp.zeros_like(da_ref)

        j = _fiota((CH, 1), 0)
        c, s = c_ref[...], s_ref[...]
        scale = RET_HD ** -0.5
        row = lax.broadcasted_iota(jnp.int32, (8, 128), 0)
        lane = lax.broadcasted_iota(jnp.int32, (8, 128), 1)
        for h in range(H):
            sl = slice(h * RET_HD, (h + 1) * RET_HD)
            lgf = _log_decay(af_ref, h, (CH, 1))
            lgb = _log_decay(ab_ref, h, (CH, 1))
            q = _rot256(q_ref[:, sl], c, s)
            k = _rot256(k_ref[:, sl], c, s) * scale
            v = v_ref[:, sl]
            do = do_ref[:, sl]
            sf_, sb_, ef_, eb_ = sf_ref[h], sb_ref[h], ef_ref[h], eb_ref[h]
            a_w = jnp.exp(lgf * (j + 1.0))
            b_w = jnp.exp(lgb * (CH - j))
            wf = jnp.exp(lgf * (CH - 1.0 - j))
            wb = jnp.exp(lgb * j)
            lag, dm = _decay_mask(lgf[:1], lgb[:1])
            sc = _dot(q, k, NT)
            gg = _dot(do, v, NT)
            dg = gg * dm
            x1 = _dot(do, sf_, NT) * a_w
            x2 = _dot(do, sb_, NT) * b_w
            y1 = _dot(v, ef_, NT) * wf
            y2 = _dot(v, eb_, NT) * wb
            dq = _dot(dg, k, NN) + x1 + x2
            dk = _dot(dg, q, TN) + y1 + y2
            dv = _dot(sc * dm, do, TN) + _dot(k * wf, ef_, NN) + _dot(k * wb, eb_, NN)
            dq_ref[:, sl] = _rot256_t(dq, c, s).astype(ACT)
            dk_ref[:, sl] = (_rot256_t(dk, c, s) * scale).astype(ACT)
            dv_ref[:, sl] = dv.astype(ACT)
            t = dm * gg * sc
            qx1 = jnp.sum(q * x1, axis=-1, keepdims=True)
            qx2 = jnp.sum(q * x2, axis=-1, keepdims=True)
            ky1 = jnp.sum(k * y1, axis=-1, keepdims=True)
            ky2 = jnp.sum(k * y2, axis=-1, keepdims=True)
            dlf = (_sum_all(jnp.where(lag > 0, lag * t, 0.0))
                   + _sum_all((j + 1.0) * qx1 + (CH - 1.0 - j) * ky1)
                   + CH * jnp.exp(lgf[:1] * CH) * _sum_all(ef_ * sf_))
            dlb = (_sum_all(jnp.where(lag < 0, -lag * t, 0.0))
                   + _sum_all((CH - j) * qx2 + j * ky2)
                   + CH * jnp.exp(lgb[:1] * CH) * _sum_all(eb_ * sb_))
            da_ref[h] += jnp.where((row == 0) & (lane == 0), dlf * lgf[:1],
                                   jnp.where((row == 0) & (lane == 1), dlb * lgb[:1], 0.0))

    def zc(col):
        return pl.BlockSpec((CH, W), lambda c: (c, (col * RET_HD) // W))

    tab = pl.BlockSpec((CH, 128), lambda c: (c, 0))
    stt = pl.BlockSpec((H, None, RET_HD, RET_HD), lambda c: (0, c, 0, 0))
    out = pl.BlockSpec((CH, W), lambda c: (c, 0))
    dz = jax.ShapeDtypeStruct((S, W), ACT)
    return pl.pallas_call(
        body, grid=(nC,),
        in_specs=[zc(rq), zc(rk), zc(rv), out, tab, tab, stt, stt, stt, stt, SMEM_SPEC, SMEM_SPEC],
        out_specs=[out, out, out, pl.BlockSpec((H, 8, 128), lambda c: (0, 0, 0))],
        out_shape=[dz, dz, dz, jax.ShapeDtypeStruct((H, 8, 128), F32)],
        name="ret_bwd", compiler_params=_params("arbitrary"))(
            z, z, z, do, cos, sin, sf, sb, ef, eb, af, ab)


def _fill_padded(pad_ref, src_ref, S):
    zeros = jnp.zeros((POOL_PAD, POOL_GD), F32)
    pad_ref[pl.ds(0, POOL_PAD), :] = zeros
    pad_ref[pl.ds(POOL_PAD, S), :] = src_ref[...]
    pad_ref[pl.ds(S + POOL_PAD, POOL_PAD), :] = zeros


def _window_sum(ext, T, lo, hi):
    n = T + 2 * POOL_PAD
    acc = None
    for k in range(lo, hi):
        sh = ext if k == 0 else pltpu.roll(ext, (-k) % n, 0)
        piece = sh[POOL_PAD:POOL_PAD + T]
        acc = piece if acc is None else acc + piece
    return acc


def _window_count(pos, w, S):
    lo = jnp.maximum(pos - w // 2, 0)
    hi = jnp.minimum(pos + w // 2, S)
    return (hi - lo).astype(F32)


def _pool_fwd(z, pw, scale, pv, pg):
    S = z.shape[0]
    G = len(POOL_WINDOWS)
    T = min(512, S)
    W = G * POOL_GD

    def body(x_ref, g_ref, pw_ref, sc_ref, y_ref, u_ref, ut_ref, pad, p_scr):
        grp = pl.program_id(0)
        i = pl.program_id(1)

        @pl.when(i == 0)
        def _():
            _fill_padded(pad, x_ref, S)

        r0 = pl.multiple_of(i * T, T)
        ext = pad[pl.ds(r0, T + 2 * POOL_PAD), :]
        pos = r0 + lax.broadcasted_iota(jnp.int32, (T, 1), 0)
        for gi, w in enumerate(POOL_WINDOWS):
            @pl.when(grp == gi)
            def _(w=w):
                acc = _window_sum(ext, T, -(w // 2), w // 2)
                p_scr[...] = acc / _window_count(pos, w, S) - ext[POOL_PAD:POOL_PAD + T]

        y = _dot(p_scr[...], pw_ref[...], NN)
        y_ref[...] = y
        g = g_ref[...]
        u = y * sc_ref[...] * (g * _sigmoid(g))
        u_ref[...] = u.astype(ACT)
        ut_ref[...] = u.T.astype(ACT)

    blk = pl.BlockSpec((T, POOL_GD), lambda g, i: (i, g))
    return pl.pallas_call(
        body, grid=(G, S // T),
        in_specs=[pl.BlockSpec((S, POOL_GD), lambda g, i: (0, pv + g)),
                  pl.BlockSpec((T, POOL_GD), lambda g, i: (i, pg + g)),
                  pl.BlockSpec((None, POOL_GD, POOL_GD), lambda g, i: (g, 0, 0)),
                  pl.BlockSpec((1, POOL_GD), lambda g, i: (0, g))],
        out_specs=[blk, blk, pl.BlockSpec((POOL_GD, T), lambda g, i: (g, i))],
        out_shape=[jax.ShapeDtypeStruct((S, W), F32), jax.ShapeDtypeStruct((S, W), ACT),
                   jax.ShapeDtypeStruct((W, S), ACT)],
        scratch_shapes=[pltpu.VMEM((S + 2 * POOL_PAD, POOL_GD), F32), pltpu.VMEM((T, POOL_GD), F32)],
        name="pool_fwd", compiler_params=_params("parallel", "arbitrary"))(z, z, pw, scale)


def _pool_bwd_a(z, pw, scale, y_raw, du, pv, pg):
    S = z.shape[0]
    G = len(POOL_WINDOWS)
    T = min(512, S)
    W = G * POOL_GD

    def body(x_ref, g_ref, pw_ref, sc_ref, y_ref, du_ref, dpc_ref, dg_ref, dsc_ref, dpw_ref,
             pad, p_scr, c_scr):
        grp = pl.program_id(0)
        i = pl.program_id(1)

        @pl.when(i == 0)
        def _():
            _fill_padded(pad, x_ref, S)
            dsc_ref[...] = jnp.zeros_like(dsc_ref)
            dpw_ref[...] = jnp.zeros_like(dpw_ref)

        r0 = pl.multiple_of(i * T, T)
        ext = pad[pl.ds(r0, T + 2 * POOL_PAD), :]
        pos = r0 + lax.broadcasted_iota(jnp.int32, (T, 1), 0)
        for gi, w in enumerate(POOL_WINDOWS):
            @pl.when(grp == gi)
            def _(w=w):
                cnt = _window_count(pos, w, S)
                acc = _window_sum(ext, T, -(w // 2), w // 2)
                p_scr[...] = acc / cnt - ext[POOL_PAD:POOL_PAD + T]
                c_scr[...] = jnp.broadcast_to(cnt, (T, 128))

        g = g_ref[...]
        sg = _sigmoid(g)
        duv = du_ref[...]
        y = y_ref[...]
        scl = sc_ref[...]
        dy = duv * (scl * (g * sg))
        dg_ref[...] = (duv * y * scl * (sg * (1.0 + g * (1.0 - sg)))).astype(ACT)
        dsc_ref[...] += jnp.sum(duv * y * (g * sg), axis=0, keepdims=True)
        dpw_ref[...] += _dot(p_scr[...], dy, TN)
        dpc_ref[...] = _dot(dy, pw_ref[...], NT) / c_scr[:, :1]

    blk = pl.BlockSpec((T, POOL_GD), lambda g, i: (i, g))
    return pl.pallas_call(
        body, grid=(G, S // T),
        in_specs=[pl.BlockSpec((S, POOL_GD), lambda g, i: (0, pv + g)),
                  pl.BlockSpec((T, POOL_GD), lambda g, i: (i, pg + g)),
                  pl.BlockSpec((None, POOL_GD, POOL_GD), lambda g, i: (g, 0, 0)),
                  pl.BlockSpec((1, POOL_GD), lambda g, i: (0, g)), blk, blk],
        out_specs=[blk, blk, pl.BlockSpec((1, POOL_GD), lambda g, i: (0, g)),
                   pl.BlockSpec((None, POOL_GD, POOL_GD), lambda g, i: (g, 0, 0))],
        out_shape=[jax.ShapeDtypeStruct((S, W), F32), jax.ShapeDtypeStruct((S, W), ACT),
                   jax.ShapeDtypeStruct((1, W), F32), jax.ShapeDtypeStruct((G, POOL_GD, POOL_GD), F32)],
        scratch_shapes=[pltpu.VMEM((S + 2 * POOL_PAD, POOL_GD), F32), pltpu.VMEM((T, POOL_GD), F32),
                        pltpu.VMEM((T, 128), F32)],
        name="pool_bwd_a", compiler_params=_params("parallel", "arbitrary"))(z, z, pw, scale, y_raw, du)


def _pool_bwd_b(dpc):
    S, W = dpc.shape
    G = len(POOL_WINDOWS)
    T = min(512, S)

    def body(x_ref, o_ref, pad, acc_scr):
        grp = pl.program_id(0)
        i = pl.program_id(1)

        @pl.when(i == 0)
        def _():
            _fill_padded(pad, x_ref, S)

        r0 = pl.multiple_of(i * T, T)
        ext = pad[pl.ds(r0, T + 2 * POOL_PAD), :]
        pos = r0 + lax.broadcasted_iota(jnp.int32, (T, 1), 0)
        for gi, w in enumerate(POOL_WINDOWS):
            @pl.when(grp == gi)
            def _(w=w):
                acc = _window_sum(ext, T, -(w // 2) + 1, w // 2 + 1)
                acc_scr[...] = acc - ext[POOL_PAD:POOL_PAD + T] * _window_count(pos, w, S)

        o_ref[...] = acc_scr[...].astype(ACT)

    blk = pl.BlockSpec((T, POOL_GD), lambda g, i: (i, g))
    return pl.pallas_call(
        body, grid=(G, S // T),
        in_specs=[pl.BlockSpec((S, POOL_GD), lambda g, i: (0, g))], out_specs=blk,
        out_shape=jax.ShapeDtypeStruct((S, W), ACT),
        scratch_shapes=[pltpu.VMEM((S + 2 * POOL_PAD, POOL_GD), F32), pltpu.VMEM((T, POOL_GD), F32)],
        name="pool_bwd_b", compiler_params=_params("parallel", "arbitrary"))(dpc)


def _rope128(x, cf, sa, sb):
    return x * cf + pltpu.roll(x, ROPE_HALF, 1) * sa + pltpu.roll(x, ATT_HD - ROPE_HALF, 1) * sb


def _rope128_t(g, cf, sa, sb):
    return g * cf + pltpu.roll(g * sa, ATT_HD - ROPE_HALF, 1) + pltpu.roll(g * sb, ROPE_HALF, 1)


def _attn_prep(z, qgain, kgain, cf, sa, sb, aq, ak, av):
    S = z.shape[0]
    T = min(512, S)
    QW, KW = ATT_Q * ATT_HD, ATT_KV * ATT_HD

    def body(q_ref, k_ref, v_ref, qg_ref, kg_ref, cf_ref, sa_ref, sb_ref, qn_ref, kn_ref, vb_ref):
        cfv, sav, sbv = cf_ref[...], sa_ref[...], sb_ref[...]

        def prep(x, gain):
            r = lax.rsqrt(jnp.mean(x * x, axis=-1, keepdims=True) + RMS_EPS)
            return _rope128(x * r * gain, cfv, sav, sbv)

        for hh in range(ATT_Q):
            sl = slice(hh * ATT_HD, (hh + 1) * ATT_HD)
            qn_ref[:, sl] = prep(q_ref[:, sl], qg_ref[...]).astype(ACT)
        for hh in range(ATT_KV):
            sl = slice(hh * ATT_HD, (hh + 1) * ATT_HD)
            kn_ref[:, sl] = prep(k_ref[:, sl], kg_ref[...]).astype(ACT)
        vb_ref[...] = v_ref[...].astype(ACT)

    tab = pl.BlockSpec((T, ATT_HD), lambda i: (i, 0))
    gain = pl.BlockSpec((1, ATT_HD), lambda i: (0, 0))
    return pl.pallas_call(
        body, grid=(S // T,),
        in_specs=[pl.BlockSpec((T, QW), lambda i: (i, aq)), pl.BlockSpec((T, KW), lambda i: (i, ak)),
                  pl.BlockSpec((T, KW), lambda i: (i, av)), gain, gain, tab, tab, tab],
        out_specs=[pl.BlockSpec((T, QW), lambda i: (i, 0)), pl.BlockSpec((T, KW), lambda i: (i, 0)),
                   pl.BlockSpec((T, KW), lambda i: (i, 0))],
        out_shape=[jax.ShapeDtypeStruct((S, QW), ACT), jax.ShapeDtypeStruct((S, KW), ACT),
                   jax.ShapeDtypeStruct((S, KW), ACT)],
        name="attn_prep", compiler_params=_params("parallel"))(z, z, z, qgain, kgain, cf, sa, sb)


def _attn_window(i, S):
    start = jnp.clip(i * ATT_BLK - ATT_BLK, 0, S - ATT_SPAN)
    start = pl.multiple_of(start, ATT_BLK)
    qpos = i * ATT_BLK + lax.broadcasted_iota(jnp.int32, (ATT_BLK, ATT_SPAN), 0)
    kpos = start + lax.broadcasted_iota(jnp.int32, (ATT_BLK, ATT_SPAN), 1)
    return start, jnp.abs(kpos - qpos) <= ATT_WIN


def _attn_probs(q, kw, valid, sink):
    s = _dot(q, kw, NT) * (ATT_HD ** -0.5)
    s = jnp.where(valid, s, NEG_BIG)
    m = jnp.maximum(jnp.max(s, axis=-1, keepdims=True), sink)
    p = jnp.exp(s - m)
    es = jnp.exp(sink - m)
    den = jnp.sum(p, axis=-1, keepdims=True) + es
    return p / den, es / den


def _attn_fwd(qn, kn, vb, z, sink, ag):
    S = qn.shape[0]
    nB = S // ATT_BLK
    assert S >= ATT_SPAN
    QW = ATT_Q * ATT_HD

    GW = ATT_G * ATT_HD

    def body(q_ref, k_ref, v_ref, g_ref, sink_ref, o_ref, u_ref, ut_ref):
        kvh = pl.program_id(0)
        i = pl.program_id(1)
        start, valid = _attn_window(i, S)
        kw = k_ref[pl.ds(start, ATT_SPAN), :]
        vw = v_ref[pl.ds(start, ATT_SPAN), :]
        for gi in range(ATT_G):
            sl = slice(gi * ATT_HD, (gi + 1) * ATT_HD)
            sk = jnp.full((ATT_BLK, 1), sink_ref[kvh * ATT_G + gi], F32)
            pn, _ = _attn_probs(q_ref[:, sl], kw, valid, sk)
            o = _dot(pn, vw, NN)
            o_ref[:, sl] = o
            g = g_ref[:, sl]
            u = o * (g * _sigmoid(g))
            u_ref[:, sl] = u.astype(ACT)
            ut_ref[sl, :] = u.T.astype(ACT)

    blk = pl.BlockSpec((ATT_BLK, GW), lambda k, i: (i, k))
    kv = pl.BlockSpec((S, ATT_HD), lambda k, i: (0, k))
    return pl.pallas_call(
        body, grid=(ATT_KV, nB),
        in_specs=[blk, kv, kv, pl.BlockSpec((ATT_BLK, GW), lambda k, i: (i, ag // ATT_G + k)), SMEM_SPEC],
        out_specs=[blk, blk, pl.BlockSpec((GW, ATT_BLK), lambda k, i: (k, i))],
        out_shape=[jax.ShapeDtypeStruct((S, QW), F32), jax.ShapeDtypeStruct((S, QW), ACT),
                   jax.ShapeDtypeStruct((QW, S), ACT)],
        name="attn_fwd", compiler_params=_params("parallel", "parallel"))(qn, kn, vb, z, sink)


def _attn_bwd(qn, kn, vb, o, du, z, sink, ag):
    S = qn.shape[0]
    nB = S // ATT_BLK
    QW, KW = ATT_Q * ATT_HD, ATT_KV * ATT_HD
    GW = ATT_G * ATT_HD

    def body(q_ref, k_ref, v_ref, o_ref, du_ref, g_ref, sink_ref,
             dq_ref, dk_ref, dv_ref, dg_ref, ds_ref):
        kvh = pl.program_id(0)
        i = pl.program_id(1)

        @pl.when(i == 0)
        def _():
            dk_ref[...] = jnp.zeros_like(dk_ref)
            dv_ref[...] = jnp.zeros_like(dv_ref)
            ds_ref[...] = jnp.zeros_like(ds_ref)

        start, valid = _attn_window(i, S)
        kw = k_ref[pl.ds(start, ATT_SPAN), :]
        vw = v_ref[pl.ds(start, ATT_SPAN), :]
        row = lax.broadcasted_iota(jnp.int32, (8, 128), 0)
        lane = lax.broadcasted_iota(jnp.int32, (8, 128), 1)
        dsink = jnp.zeros((8, 128), F32)
        dk_acc = jnp.zeros((ATT_SPAN, ATT_HD), F32)
        dv_acc = jnp.zeros((ATT_SPAN, ATT_HD), F32)
        for gi in range(ATT_G):
            sl = slice(gi * ATT_HD, (gi + 1) * ATT_HD)
            q = q_ref[:, sl]
            ov = o_ref[:, sl]
            g = g_ref[:, sl]
            duv = du_ref[:, sl]
            sg = _sigmoid(g)
            do = duv * (g * sg)
            dg_ref[:, sl] = (duv * ov * (sg * (1.0 + g * (1.0 - sg)))).astype(ACT)
            sk = jnp.full((ATT_BLK, 1), sink_ref[kvh * ATT_G + gi], F32)
            pn, psink = _attn_probs(q, kw, valid, sk)
            delta = jnp.sum(do * ov, axis=-1, keepdims=True)
            dsc = pn * (_dot(do, vw, NT) - delta) * (ATT_HD ** -0.5)
            dq_ref[:, sl] = _dot(dsc, kw, NN)
            dk_acc = dk_acc + _dot(dsc, q, TN)
            dv_acc = dv_acc + _dot(pn, do, TN)
            dsink = dsink + jnp.where((row == 0) & (lane == gi), -_sum_all(psink * delta), 0.0)
        dk_ref[pl.ds(start, ATT_SPAN), :] += dk_acc
        dv_ref[pl.ds(start, ATT_SPAN), :] += dv_acc
        ds_ref[...] += dsink

    grp = pl.BlockSpec((ATT_BLK, GW), lambda k, i: (i, k))
    kv = pl.BlockSpec((S, ATT_HD), lambda k, i: (0, k))
    return pl.pallas_call(
        body, grid=(ATT_KV, nB),
        in_specs=[grp, kv, kv, grp, grp,
                  pl.BlockSpec((ATT_BLK, GW), lambda k, i: (i, ag // ATT_G + k)), SMEM_SPEC],
        out_specs=[grp, kv, kv, grp, pl.BlockSpec((None, 8, 128), lambda k, i: (k, 0, 0))],
        out_shape=[jax.ShapeDtypeStruct((S, QW), F32), jax.ShapeDtypeStruct((S, KW), F32),
                   jax.ShapeDtypeStruct((S, KW), F32), jax.ShapeDtypeStruct((S, QW), ACT),
                   jax.ShapeDtypeStruct((ATT_KV, 8, 128), F32)],
        name="attn_bwd", compiler_params=_params("parallel", "arbitrary"))(qn, kn, vb, o, du, z, sink)


def _attn_prep_bwd(z, dqn, dkn, dv, qgain, kgain, cf, sa, sb, aq, ak):
    S = z.shape[0]
    T = min(512, S)
    QW, KW = ATT_Q * ATT_HD, ATT_KV * ATT_HD

    def body(q_ref, k_ref, dqn_ref, dkn_ref, dv_ref, qg_ref, kg_ref, cf_ref, sa_ref, sb_ref,
             dq_ref, dk_ref, dvb_ref, dqg_ref, dkg_ref):
        cfv, sav, sbv = cf_ref[...], sa_ref[...], sb_ref[...]

        @pl.when(pl.program_id(0) == 0)
        def _():
            dqg_ref[...] = jnp.zeros_like(dqg_ref)
            dkg_ref[...] = jnp.zeros_like(dkg_ref)

        def back(x, gn, gain):
            r = lax.rsqrt(jnp.mean(x * x, axis=-1, keepdims=True) + RMS_EPS)
            xh = x * r
            dy = _rope128_t(gn, cfv, sav, sbv)
            dxh = dy * gain
            dx = r * (dxh - xh * jnp.mean(dxh * xh, axis=-1, keepdims=True))
            return dx, jnp.sum(dy * xh, axis=0, keepdims=True)

        dqg = jnp.zeros((1, ATT_HD), F32)
        for hh in range(ATT_Q):
            sl = slice(hh * ATT_HD, (hh + 1) * ATT_HD)
            dx, dgn = back(q_ref[:, sl], dqn_ref[:, sl], qg_ref[...])
            dq_ref[:, sl] = dx.astype(ACT)
            dqg = dqg + dgn
        dkg = jnp.zeros((1, ATT_HD), F32)
        for hh in range(ATT_KV):
            sl = slice(hh * ATT_HD, (hh + 1) * ATT_HD)
            dx, dgn = back(k_ref[:, sl], dkn_ref[:, sl], kg_ref[...])
            dk_ref[:, sl] = dx.astype(ACT)
            dkg = dkg + dgn
        dvb_ref[...] = dv_ref[...].astype(ACT)
        dqg_ref[...] += dqg
        dkg_ref[...] += dkg

    tab = pl.BlockSpec((T, ATT_HD), lambda i: (i, 0))
    gain = pl.BlockSpec((1, ATT_HD), lambda i: (0, 0))
    qb = pl.BlockSpec((T, QW), lambda i: (i, 0))
    kb = pl.BlockSpec((T, KW), lambda i: (i, 0))
    return pl.pallas_call(
        body, grid=(S // T,),
        in_specs=[pl.BlockSpec((T, QW), lambda i: (i, aq)), pl.BlockSpec((T, KW), lambda i: (i, ak)),
                  qb, kb, kb, gain, gain, tab, tab, tab],
        out_specs=[qb, kb, kb, gain, gain],
        out_shape=[jax.ShapeDtypeStruct((S, QW), ACT), jax.ShapeDtypeStruct((S, KW), ACT),
                   jax.ShapeDtypeStruct((S, KW), ACT), jax.ShapeDtypeStruct((1, ATT_HD), F32),
                   jax.ShapeDtypeStruct((1, ATT_HD), F32)],
        name="attn_prep_bwd", compiler_params=_params("arbitrary"))(
            z, z, dqn, dkn, dv, qgain, kgain, cf, sa, sb)


def _branch_merge(ua, ub, uc, wr, wp, wa, z, mg):
    S, W = ua.shape
    D = wr.shape[1]
    tm, tn = min(512, S), min(512, D)
    nb = D // tn

    def body(ua_ref, ub_ref, uc_ref, wr_ref, wp_ref, wa_ref, g0_ref, g1_ref, g2_ref,
             ya_ref, yb_ref, yc_ref, m_ref, mt_ref):
        ya = _dot(ua_ref[...], wr_ref[...], NN)
        yb = _dot(ub_ref[...], wp_ref[...], NN)
        yc = _dot(uc_ref[...], wa_ref[...], NN)
        ya_ref[...] = ya.astype(ACT)
        yb_ref[...] = yb.astype(ACT)
        yc_ref[...] = yc.astype(ACT)
        m = _sigmoid(g0_ref[...]) * ya + _sigmoid(g1_ref[...]) * yb + _sigmoid(g2_ref[...]) * yc
        m_ref[...] = m.astype(ACT)
        mt_ref[...] = m.T.astype(ACT)

    u = pl.BlockSpec((tm, W), lambda i, j: (i, 0))
    w = pl.BlockSpec((W, tn), lambda i, j: (0, j))
    o = pl.BlockSpec((tm, tn), lambda i, j: (i, j))

    assert (mg * POOL_GD) % tn == 0
    base = (mg * POOL_GD) // tn

    def gate(k):
        return pl.BlockSpec((tm, tn), lambda i, j: (i, base + k * nb + j))

    sd = jax.ShapeDtypeStruct((S, D), ACT)
    return pl.pallas_call(
        body, grid=(S // tm, nb), in_specs=[u, u, u, w, w, w, gate(0), gate(1), gate(2)],
        out_specs=[o, o, o, o, pl.BlockSpec((tn, tm), lambda i, j: (j, i))],
        out_shape=[sd, sd, sd, sd, jax.ShapeDtypeStruct((D, S), ACT)], name="branch_merge",
        compiler_params=_params("parallel", "parallel"))(ua, ub, uc, wr, wp, wa, z, z, z)


def _merge_bwd(dxb, wo, ya, yb, yc, z, mg):
    S, D = dxb.shape
    tm, tn = min(512, S), min(512, D)
    nb = D // tn
    base = (mg * POOL_GD) // tn

    def body(dx_ref, wo_ref, ya_ref, yb_ref, yc_ref, g0_ref, g1_ref, g2_ref,
             da_ref, db_ref, dc_ref, dg0_ref, dg1_ref, dg2_ref):
        dm = _dot(dx_ref[...], wo_ref[...], NT)
        for y_ref, g_ref, dy_ref, dg_ref in ((ya_ref, g0_ref, da_ref, dg0_ref),
                                             (yb_ref, g1_ref, db_ref, dg1_ref),
                                             (yc_ref, g2_ref, dc_ref, dg2_ref)):
            sg = _sigmoid(g_ref[...])
            dy_ref[...] = (sg * dm).astype(ACT)
            dg_ref[...] = (dm * y_ref[...].astype(F32) * (sg * (1.0 - sg))).astype(ACT)

    o = pl.BlockSpec((tm, tn), lambda i, j: (i, j))

    def gate(k):
        return pl.BlockSpec((tm, tn), lambda i, j: (i, base + k * nb + j))

    sd = jax.ShapeDtypeStruct((S, D), ACT)
    return pl.pallas_call(
        body, grid=(S // tm, nb),
        in_specs=[pl.BlockSpec((tm, D), lambda i, j: (i, 0)), pl.BlockSpec((tn, D), lambda i, j: (j, 0)),
                  o, o, o, gate(0), gate(1), gate(2)],
        out_specs=[o] * 6, out_shape=[sd] * 6, name="merge_bwd",
        compiler_params=_params("parallel", "parallel"))(dxb, wo, ya, yb, yc, z, z, z)


def _loss_head(y, t):
    S, D = y.shape
    tm = min(256, S)

    def body(y_ref, t_ref, dy_ref, dyb_ref, l_ref):
        e = y_ref[...] - t_ref[...]
        dy = e * (1.0 / D)
        dy_ref[...] = dy
        dyb_ref[...] = dy.astype(ACT)

        @pl.when(pl.program_id(0) == 0)
        def _():
            l_ref[...] = jnp.zeros_like(l_ref)

        l_ref[...] += jnp.sum(jnp.mean(e * e, axis=-1, keepdims=True), axis=0, keepdims=True)

    row = pl.BlockSpec((tm, D), lambda i: (i, 0))
    return pl.pallas_call(
        body, grid=(S // tm,), in_specs=[row, row],
        out_specs=[row, row, pl.BlockSpec((1, 1), lambda i: (0, 0))],
        out_shape=[jax.ShapeDtypeStruct((S, D), F32), jax.ShapeDtypeStruct((S, D), ACT),
                   jax.ShapeDtypeStruct((1, 1), F32)],
        name="loss_head", compiler_params=_params("arbitrary"))(y, t)


def _adamw(w, layer, parts, m, v, name):
    L = w.shape[0]
    shape = w.shape[1:]
    C = shape[-1]
    R = int(np.prod(shape[:-1]))
    w3, m3, v3 = (a.reshape(L, R, C) for a in (w, m, v))
    p3 = parts.reshape(N_DEV, R, C)
    tr = min(64, R)
    assert R % tr == 0
    c1 = 1.0 / (1.0 - ADAM_B1 ** ADAM_STEP)
    c2 = 1.0 / (1.0 - ADAM_B2 ** ADAM_STEP)

    def body(w_ref, p_ref, m_ref, v_ref, g_ref, d_ref, nm_ref, nv_ref):
        g = p_ref[0].astype(F32)
        for k in range(1, N_DEV):
            g = g + p_ref[k].astype(F32)
        nm = ADAM_B1 * m_ref[...] + (1.0 - ADAM_B1) * g
        nv = ADAM_B2 * v_ref[...] + (1.0 - ADAM_B2) * (g * g)
        g_ref[...] = g
        nm_ref[...] = nm
        nv_ref[...] = nv
        d_ref[...] = -ADAM_LR * ((nm * c1) / (jnp.sqrt(nv * c2) + ADAM_EPS) + ADAM_WD * w_ref[...])

    lay = pl.BlockSpec((None, tr, C), lambda i: (layer, i, 0))
    out = pl.BlockSpec((tr, C), lambda i: (i, 0))
    sd = jax.ShapeDtypeStruct((R, C), F32)
    outs = pl.pallas_call(
        body, grid=(R // tr,),
        in_specs=[lay, pl.BlockSpec((N_DEV, tr, C), lambda i: (0, i, 0)), lay, lay],
        out_specs=[out] * 4, out_shape=[sd] * 4, name=name,
        compiler_params=_params("parallel"))(w3, p3, m3, v3)
    return [a.reshape(shape) for a in outs]


def _exchange(arrs, scatter, name):
    n = len(arrs)
    out_shape = [jax.ShapeDtypeStruct(a.shape if scatter else (N_DEV,) + a.shape, a.dtype) for a in arrs]

    def body(*refs):
        ins, outs = refs[:n], refs[n:2 * n]
        send_sems, recv_sems, local_sems = refs[2 * n:]
        x, y, c = lax.axis_index("x"), lax.axis_index("y"), lax.axis_index("c")
        me = 4 * x + 2 * y + c
        copies = []
        for a in range(n):
            src = ins[a].at[me] if scatter else ins[a]
            own = pltpu.make_async_copy(src, outs[a].at[me], local_sems.at[a])
            own.start()
            copies.append(own)
        sends, recvs = [], []
        for k in range(1, N_DEV):
            px, py, pc = x ^ (k >> 2), y ^ ((k >> 1) & 1), c ^ (k & 1)
            peer = 4 * px + 2 * py + pc
            for a in range(n):
                src = ins[a].at[peer] if scatter else ins[a]
                cp = pltpu.make_async_remote_copy(
                    src_ref=src, dst_ref=outs[a].at[me],
                    send_sem=send_sems.at[a, k - 1], recv_sem=recv_sems.at[a, k - 1],
                    device_id=(px, py, pc), device_id_type=pl.DeviceIdType.MESH)
                cp.start()
                sends.append(cp)
                recvs.append(pltpu.make_async_remote_copy(
                    src_ref=src, dst_ref=outs[a].at[peer],
                    send_sem=send_sems.at[a, k - 1], recv_sem=recv_sems.at[a, k - 1],
                    device_id=(px, py, pc), device_id_type=pl.DeviceIdType.MESH))
        for cp in recvs:
            cp.wait_recv()
        for cp in sends:
            cp.wait_send()
        for cp in copies:
            cp.wait()

    any_spec = pl.BlockSpec(memory_space=pl.ANY)
    return pl.pallas_call(
        body, in_specs=[any_spec] * n, out_specs=[any_spec] * n, out_shape=out_shape,
        scratch_shapes=[pltpu.SemaphoreType.DMA((n, N_DEV - 1)), pltpu.SemaphoreType.DMA((n, N_DEV - 1)),
                        pltpu.SemaphoreType.DMA((n,))],
        name=name)(*arrs)


HBM_SPEC = pl.BlockSpec(memory_space=pltpu.HBM)
SEM_SPEC = pl.BlockSpec(memory_space=pltpu.SEMAPHORE)
DATAFLOW = pltpu.SideEffectType.DATAFLOW_SIDE_EFFECTING


def _peer_of(k):
    x, y, c = lax.axis_index("x"), lax.axis_index("y"), lax.axis_index("c")
    return x ^ (k >> 2), y ^ ((k >> 1) & 1), c ^ (k & 1)


def _exchange_copy(k, a, src_ref, land_ref, send_sems, recv_sems, scatter, outgoing):
    px, py, pc = _peer_of(k)
    peer = 4 * px + 2 * py + pc
    me = 4 * lax.axis_index("x") + 2 * lax.axis_index("y") + lax.axis_index("c")
    idx = a * (N_DEV - 1) + k - 1
    return pltpu.make_async_remote_copy(
        src_ref=src_ref.at[peer] if scatter else src_ref, dst_ref=land_ref.at[me if outgoing else peer],
        send_sem=send_sems.at[idx], recv_sem=recv_sems.at[idx],
        device_id=(px, py, pc), device_id_type=pl.DeviceIdType.MESH)


def _exchange_start(arrs, scatter, name):
    n = len(arrs)
    land_shapes = [a.shape if scatter else (N_DEV,) + a.shape for a in arrs]

    def body(*refs):
        srcs, lands = refs[:n], refs[n:2 * n]
        send_sems, recv_sems = refs[2 * n], refs[2 * n + 1]
        token = refs[-1]
        for k in range(1, N_DEV):
            for a in range(n):
                _exchange_copy(k, a, srcs[a], lands[a], send_sems, recv_sems, scatter, True).start()
        token[...] = jnp.zeros_like(token)

    sems = pltpu.SemaphoreType.DMA((n * (N_DEV - 1),))
    out_shape = ([sems, sems] + [pltpu.HBM(a.shape, a.dtype) for a in arrs]
                 + [pltpu.HBM(s, a.dtype) for s, a in zip(land_shapes, arrs)]
                 + [jax.ShapeDtypeStruct((8, 128), F32)])
    ins = ([pltpu.with_memory_space_constraint(a, pltpu.HBM) for a in arrs]
           + [pltpu.with_memory_space_constraint(lax.empty(s, a.dtype), pltpu.HBM) for s, a in zip(land_shapes, arrs)])
    res = pl.pallas_call(
        body, name=name, out_shape=out_shape, in_specs=[HBM_SPEC] * (2 * n),
        out_specs=[SEM_SPEC, SEM_SPEC] + [HBM_SPEC] * (2 * n) + [pl.BlockSpec(memory_space=pltpu.VMEM)],
        input_output_aliases={i: 2 + i for i in range(2 * n)},
        compiler_params=pltpu.CompilerParams(has_side_effects=DATAFLOW))(*ins)
    return res[0], res[1], list(res[2:2 + n]), list(res[2 + n:2 + 2 * n]), res[-1]


def _exchange_wait(started, after, scatter, name):
    send_sems, recv_sems, srcs, lands, _ = started
    n = len(srcs)

    def body(*refs):
        src_refs, land_refs = refs[:n], refs[n:2 * n]
        s_sems, r_sems = refs[2 * n], refs[2 * n + 1]
        for k in range(1, N_DEV):
            for a in range(n):
                back = _exchange_copy(k, a, src_refs[a], land_refs[a], s_sems, r_sems, scatter, False)
                back.wait_send()
                back.wait_recv()

    out_shape = [pltpu.HBM(a.shape, a.dtype) for a in srcs] + [pltpu.HBM(a.shape, a.dtype) for a in lands]
    res = pl.pallas_call(
        body, name=name, out_shape=out_shape,
        in_specs=[HBM_SPEC] * (2 * n) + [SEM_SPEC, SEM_SPEC, pl.BlockSpec(memory_space=pl.ANY)],
        out_specs=[HBM_SPEC] * (2 * n), input_output_aliases={i: i for i in range(2 * n)},
        compiler_params=pltpu.CompilerParams(has_side_effects=DATAFLOW))(*srcs, *lands, send_sems, recv_sems, after)
    return list(res[:n]), list(res[n:])


def _own_slot(land, own, me):
    return lax.dynamic_update_index_in_dim(land, own, me, 0)


class _Cols:
    def __init__(self, D):
        Wb = D // 2
        sizes = (Wb, Wb, Wb, Wb, Wb, Wb, Wb, ATT_KV * ATT_HD, ATT_KV * ATT_HD, Wb, 3 * D)
        offs = np.concatenate([[0], np.cumsum(sizes)])
        assert all(int(o) % 256 == 0 for o in offs)
        (self.rq, self.rk, self.rv, self.rg, self.pv, self.pg,
         self.aq, self.ak, self.av, self.ag, self.mg) = (int(o) // 256 for o in offs[:-1])
        self.width = int(offs[-1])
        self.sizes = sizes


def _rope_tables(S):
    pos = jnp.arange(S, dtype=F32)[:, None]
    inv_r = 1.0 / (RET_ROPE_BASE ** jnp.linspace(0.0, 1.0, RET_HD // 2, dtype=F32))
    ang_r = pos * inv_r[None, :]
    inv_a = ROPE_THETA ** (-jnp.arange(ROPE_HALF, dtype=F32) / ROPE_HALF)
    ang_a = pos * inv_a[None, :]
    ca, sa = jnp.cos(ang_a), jnp.sin(ang_a)
    z16 = jnp.zeros((S, ROPE_HALF), F32)
    rest = ATT_HD - 2 * ROPE_HALF
    cf = jnp.concatenate([ca, ca, jnp.ones((S, rest), F32)], axis=1)
    s_up = jnp.concatenate([z16, sa, jnp.zeros((S, rest), F32)], axis=1)
    s_dn = jnp.concatenate([-sa, z16, jnp.zeros((S, rest), F32)], axis=1)
    return jnp.cos(ang_r), jnp.sin(ang_r), cf, s_up, s_dn


def _layer_fwd(x, p, tabs, cols, deps=()):
    cos_r, sin_r, cf, s_up, s_dn = tabs
    S, D = x.shape
    h, ht = _rmsnorm_fwd(x, p["norm_g"], deps)
    z = _matmul(h, p["w_in_t"], "nt", F32, 1024, 512, D, "in_proj")
    if "rest" in p:
        p = {**{k: v for k, v in p.items() if k != "rest"}, **p["rest"](z)}
    rcols = (cols.rq, cols.rk, cols.rv, cols.rg)
    sf, sb = _ret_state(z, cols.rk, RET_HD ** -0.5, z, cols.rv, cos_r, sin_r, p["af"], p["ab"], "fwd",
                        "ret_state_fwd")
    o_ret, ua, uat = _ret_fwd(z, cos_r, sin_r, sf, sb, p["af"], p["ab"], rcols)
    y_pool, ub, ubt = _pool_fwd(z, p["pool_w"], p["pool_scale"], cols.pv, cols.pg)
    qn, kn, vb = _attn_prep(z, p["q_gain"], p["k_gain"], cf, s_up, s_dn,
                            cols.aq // 4, cols.ak, cols.av)
    o_att, uc, uct = _attn_fwd(qn, kn, vb, z, p["sink"], cols.ag * 2)
    ya, yb, yc, merged, mergedt = _branch_merge(ua, ub, uc, p["w_ret"], p["w_pool"], p["w_att"], z, cols.mg)
    out = _matmul(merged, p["w_out"], "nn", F32, 1024, 512, D, "out_proj", res=x)
    saved = dict(x=x, ht=ht, z=z, sf=sf, sb=sb, o_ret=o_ret, uat=uat, y_pool=y_pool, ubt=ubt,
                 qn=qn, kn=kn, vb=vb, o_att=o_att, uct=uct, ya=ya, yb=yb, yc=yc, mergedt=mergedt)
    return out, saved, p


def _layer_bwd(dx, dxb, p, sv, tabs, cols, on_grads):
    cos_r, sin_r, cf, s_up, s_dn = tabs
    z = sv["z"]
    S, D = dx.shape
    dya, dyb, dyc, dmg0, dmg1, dmg2 = _merge_bwd(dxb, p["w_out"], sv["ya"], sv["yb"], sv["yc"], z, cols.mg)
    grads = {"w_out": _matmul(sv["mergedt"], dxb, "nn", ACT, 1024, 512, S, "dw_out")}
    dus = {}
    for nm, ut, dy in (("ret", sv["uat"], dya), ("pool", sv["ubt"], dyb), ("att", sv["uct"], dyc)):
        dus[nm] = _matmul(dy, p["w_" + nm], "nt", F32, 1024, 512, D, "du_" + nm)
        grads["w_" + nm] = _matmul(ut, dy, "nn", ACT, 1024, 512, S, "dw_" + nm)
    rcols = (cols.rq, cols.rk, cols.rv, cols.rg)
    do_ret, d_rg = _ret_gate_bwd(dus["ret"], sv["o_ret"], z, cols.rg)
    eb, ef = _ret_state(z, cols.rq, 1.0, do_ret, 0, cos_r, sin_r, p["af"], p["ab"], "bwd", "ret_state_bwd")
    d_rq, d_rk, d_rv, d_decay = _ret_bwd(z, do_ret, cos_r, sin_r, sv["sf"], sv["sb"], ef, eb,
                                         p["af"], p["ab"], rcols)
    dpc, d_pg, d_pscale, g_pool_w = _pool_bwd_a(z, p["pool_w"], p["pool_scale"], sv["y_pool"], dus["pool"],
                                                cols.pv, cols.pg)
    d_pv = _pool_bwd_b(dpc)
    grads["pool_w"] = g_pool_w.astype(ACT)
    dqn, dkn, dvv, d_ag, d_sink = _attn_bwd(sv["qn"], sv["kn"], sv["vb"], sv["o_att"], dus["att"], z,
                                            p["sink"], cols.ag * 2)
    d_aq, d_ak, d_av, d_qg, d_kg = _attn_prep_bwd(z, dqn, dkn, dvv, p["q_gain"], p["k_gain"], cf, s_up, s_dn,
                                                  cols.aq // 4, cols.ak)
    dz = jnp.concatenate([d_rq, d_rk, d_rv, d_rg, d_pv, d_pg, d_aq, d_ak, d_av, d_ag, dmg0, dmg1, dmg2],
                         axis=1)
    grads["w_in"] = _matmul(sv["ht"], dz, "nn", ACT, 1024, 512, S, "dw_in")
    tok = on_grads(grads)
    dh = _matmul(dz, p["w_in_t"], "nn", F32, 1024, 512, 29 * 128, "dh", deps=() if tok is None else (tok,))
    dx_in, dxb_in, d_norm_g = _rmsnorm_bwd(sv["x"], p["norm_g"], dh, dx)
    misc = jnp.concatenate([d_decay[:, 0, 0], d_decay[:, 0, 1], d_sink[:, 0, :ATT_G].reshape(-1)])
    misc = jnp.pad(misc, (0, 128 - misc.shape[0]))[None, :]
    small = jnp.concatenate([d_norm_g, d_pscale, d_qg, d_kg, misc], axis=1)
    return dx_in, dxb_in, small


def _pack_small(norm_g, pool_scale, q_gain, k_gain, af, ab, sink):
    L = norm_g.shape[0]
    misc = jnp.concatenate([af, ab, sink], axis=1)
    misc = jnp.pad(misc, ((0, 0), (0, 128 - misc.shape[1])))
    return jnp.concatenate([norm_g, pool_scale, q_gain, k_gain, misc], axis=1)


def _unpack_small(a, D):
    Wb = D // 2
    o = np.cumsum([0, D, Wb, ATT_HD, ATT_HD])
    misc = a[:, o[4]:]
    return (a[:, o[0]:o[1]], a[:, o[1]:o[2]], a[:, o[2]:o[3]], a[:, o[3]:o[4]],
            misc[:, :RET_HEADS], misc[:, RET_HEADS:2 * RET_HEADS],
            misc[:, 2 * RET_HEADS:2 * RET_HEADS + ATT_Q])


def _local_step(x, t, n_layers, get_layer, on_grads, tabs, cols, first_dep=None):
    saved, layers = [], []
    after = first_dep
    for l in range(n_layers):
        p = get_layer(l, after)
        x, sv, p = _layer_fwd(x, p, tabs, cols, (first_dep,) if (l == 0 and first_dep is not None) else ())
        after = x
        layers.append(p)
        saved.append(sv)
    dx, dxb, lsum = _loss_head(x, t)
    smalls = []
    for l in reversed(range(n_layers)):
        dx, dxb, sm = _layer_bwd(dx, dxb, layers[l], saved[l], tabs, cols, functools.partial(on_grads, l))
        smalls.append(sm)
    return 0.5 * lsum[0, 0], dx, jnp.concatenate(smalls[::-1], axis=0)


WEIGHT_KEYS = ("w_in", "w_ret", "w_pool", "w_att", "w_out", "pool_w")


def kernel(x, norm_g, w_in, ret_decay_fwd, ret_decay_bwd, pool_w, pool_scale, attn_q_gain, attn_k_gain, attn_sink, w_ret, w_pool, w_att, w_out, loss_target, m_norm_g, m_w_in, m_ret_decay_fwd, m_ret_decay_bwd, m_pool_w, m_pool_scale, m_attn_q_gain, m_attn_k_gain, m_attn_sink, m_w_ret, m_w_pool, m_w_att, m_w_out, v_norm_g, v_w_in, v_ret_decay_fwd, v_ret_decay_bwd, v_pool_w, v_pool_scale, v_attn_q_gain, v_attn_k_gain, v_attn_sink, v_w_ret, v_w_pool, v_w_att, v_w_out):
    L = norm_g.shape[0]
    _, S, D = x.shape
    Wb = D // 2
    G = len(POOL_WINDOWS)
    cols = _Cols(D)
    tabs = _rope_tables(S)
    me = 4 * lax.axis_index("x") + 2 * lax.axis_index("y") + lax.axis_index("c")
    def tr(a):
        return jnp.transpose(a, (0, 2, 1))

    weights = dict(w_in=tr(w_in), w_ret=w_ret, w_pool=w_pool, w_att=w_att, w_out=w_out, pool_w=pool_w)

    gathers, tok = [], None
    for l in range(L):
        started = []
        for part, keys in (("a", WEIGHT_KEYS[:1]), ("b", WEIGHT_KEYS[1:])):
            shards = []
            for k in keys:
                w = weights[k][l]
                if tok is not None:
                    w = w + tok[0, 0]
                shards.append(w.astype(MXU))
            st = _exchange_start(shards, False, f"gather_start_{l}{part}")
            started.append(st)
            tok = st[-1]
        gathers.append(started)

    def cols_full(g, rows):
        return jnp.transpose(g, (1, 0, 2)).reshape(rows, -1)

    def get_layer(l, after):
        srcs, lands = _exchange_wait(gathers[l][0], after, False, f"gather_wait_{l}a")
        g_in = _own_slot(lands[0], srcs[0], me)

        def rest(z):
            srcs, lands = _exchange_wait(gathers[l][1], z, False, f"gather_wait_{l}b")
            g_ret, g_pool, g_att, g_out, g_pw = [_own_slot(ld, sr, me) for ld, sr in zip(lands, srcs)]
            return dict(
                w_ret=cols_full(g_ret, Wb), w_pool=cols_full(g_pool, Wb), w_att=cols_full(g_att, Wb),
                w_out=g_out.reshape(D, D),
                pool_w=jnp.transpose(g_pw, (1, 0, 2, 3)).reshape(G, POOL_GD, POOL_GD))

        return dict(
            norm_g=norm_g[l][None, :], w_in_t=g_in.reshape(-1, D), rest=rest,
            pool_scale=pool_scale[l][None, :], q_gain=attn_q_gain[l][None, :], k_gain=attn_k_gain[l][None, :],
            af=ret_decay_fwd[l], ab=ret_decay_bwd[l], sink=attn_sink[l])

    def col_slots(g, rows):
        return jnp.transpose(g.reshape(rows, N_DEV, -1), (1, 0, 2))

    scatters = {}

    def on_grads(l, g):
        slots = [jnp.transpose(g["w_in"].reshape(D, N_DEV, -1), (1, 2, 0)),
                 col_slots(g["w_ret"], Wb), col_slots(g["w_pool"], Wb),
                 col_slots(g["w_att"], Wb), g["w_out"].reshape(N_DEV, D // N_DEV, D),
                 jnp.transpose(g["pool_w"].reshape(G, N_DEV, POOL_GD // N_DEV, POOL_GD), (1, 0, 2, 3))]
        scatters[l] = _exchange_start(slots, True, f"scatter_start_{l}")
        return scatters[l][-1]

    loss_local, grad_x, small = _local_step(x[0], loss_target[0], L, get_layer, on_grads, tabs, cols, tok)
    loss = lax.psum(loss_local, ("x", "y", "c"))

    moments = dict(w_in=(tr(m_w_in), tr(v_w_in)), w_ret=(m_w_ret, v_w_ret), w_pool=(m_w_pool, v_w_pool),
                   w_att=(m_w_att, v_w_att), w_out=(m_w_out, v_w_out), pool_w=(m_pool_w, v_pool_w))
    res = {k: [jnp.zeros(weights[k].shape, F32) for _ in range(4)] for k in WEIGHT_KEYS}
    after = grad_x
    small_all = None
    for l in reversed(range(L)):
        if l == 0:
            zero = (lax.bitcast_convert_type(after.reshape(-1)[0], jnp.int32) & 0).astype(F32)
            small_all, = _exchange([small + zero], False, "gather_small_grads")
            after = small_all
        srcs, lands = _exchange_wait(scatters[l], after, True, f"scatter_wait_{l}")
        for k, ld, sr in zip(WEIGHT_KEYS, lands, srcs):
            parts = _own_slot(ld, lax.dynamic_index_in_dim(sr, me, 0, keepdims=False), me)
            m, v = moments[k]
            outs = _adamw(weights[k], l, parts, m, v, "adamw_" + k)
            res[k] = [lax.dynamic_update_index_in_dim(r, o, l, 0) for r, o in zip(res[k], outs)]
            after = outs[1]

    sw = _pack_small(norm_g, pool_scale, attn_q_gain, attn_k_gain, ret_decay_fwd, ret_decay_bwd, attn_sink)
    sm = _pack_small(m_norm_g, m_pool_scale, m_attn_q_gain, m_attn_k_gain, m_ret_decay_fwd, m_ret_decay_bwd,
                     m_attn_sink)
    sv_ = _pack_small(v_norm_g, v_pool_scale, v_attn_q_gain, v_attn_k_gain, v_ret_decay_fwd, v_ret_decay_bwd,
                      v_attn_sink)
    small_res = [_unpack_small(a, D) for a in _adamw(sw[None], 0, small_all, sm[None], sv_[None], "adamw_small")]

    def ordered(i):
        ng, ps, qg, kg, af, ab, sk = small_res[i]
        return (ng, tr(res["w_in"][i]), af, ab, res["pool_w"][i], ps, qg, kg, sk,
                res["w_ret"][i], res["w_pool"][i], res["w_att"][i], res["w_out"][i])

    return (loss, grad_x[None], *ordered(0), *ordered(1), *ordered(2), *ordered(3))
```

```python
import functools

import numpy as np
import jax
import jax.numpy as jnp
from jax import lax
from jax.experimental import pallas as pl
from jax.experimental.pallas import tpu as pltpu

F32 = jnp.float32
MXU = jnp.bfloat16
ACT = jnp.bfloat16

N_DEV = 8
RMS_EPS = 1e-6
NEG_BIG = -1e30
RET_HEADS = 4
RET_HD = 256
CH = 128
RET_ROPE_BASE = 10000.0
POOL_WINDOWS = (2, 4, 8, 16)
POOL_GD = 256
POOL_PAD = 8
ATT_HD = 128
ATT_Q = 8
ATT_KV = 2
ATT_G = ATT_Q // ATT_KV
ATT_WIN = 128
ATT_BLK = 128
ATT_SPAN = 3 * ATT_BLK
ROPE_THETA = 500000.0
ROPE_HALF = 16

ADAM_LR = 0.001
ADAM_B1 = 0.9
ADAM_B2 = 0.999
ADAM_EPS = 1e-08
ADAM_WD = 0.01
ADAM_STEP = 10

VMEM_LIMIT = 48 * 1024 * 1024

NN = ((1,), (0,))
NT = ((1,), (1,))
TN = ((0,), (0,))


def _dot(a, b, dims):
    return lax.dot_general(a.astype(MXU), b.astype(MXU), (dims, ((), ())),
                           preferred_element_type=F32)


def _sigmoid(x):
    return 1.0 / (1.0 + jnp.exp(-x))


def _params(*sem):
    return pltpu.CompilerParams(dimension_semantics=sem, vmem_limit_bytes=VMEM_LIMIT)


def _sum_all(x):
    return jnp.sum(jnp.sum(x, axis=1, keepdims=True), axis=0, keepdims=True)


def _fiota(shape, dim):
    return lax.broadcasted_iota(jnp.int32, shape, dim).astype(F32)


SMEM_SPEC = pl.BlockSpec(memory_space=pltpu.SMEM)


def _matmul(a, b, mode, out_dtype, tm, tn, tk, name, res=None, deps=(), transpose_out=False):
    if mode == "tn":
        K, M = a.shape
    else:
        M, K = a.shape
    N = b.shape[0] if mode == "nt" else b.shape[1]
    tm, tn, tk = min(tm, M), min(tn, N), min(tk, K)
    assert M % tm == 0 and N % tn == 0 and K % tk == 0, (name, M, N, K, tm, tn, tk)
    nk = K // tk
    dims = {"nn": NN, "nt": NT, "tn": TN}[mode]
    a_spec = (pl.BlockSpec((tk, tm), lambda i, j, k: (k, i)) if mode == "tn"
              else pl.BlockSpec((tm, tk), lambda i, j, k: (i, k)))
    b_spec = (pl.BlockSpec((tn, tk), lambda i, j, k: (j, k)) if mode == "nt"
              else pl.BlockSpec((tk, tn), lambda i, j, k: (k, j)))
    o_spec = pl.BlockSpec((tm, tn), lambda i, j, k: (i, j))
    has_res = res is not None
    assert not (has_res and transpose_out)
    n_in = 2 + has_res + len(deps)

    def body(*refs):
        a_ref, b_ref = refs[:2]
        r_ref = refs[2] if has_res else None
        o_ref = refs[n_in]

        def finish(out):
            if has_res:
                out = out + r_ref[...]
            o_ref[...] = (out.T if transpose_out else out).astype(out_dtype)

        if nk == 1:
            finish(_dot(a_ref[...], b_ref[...], dims))
            return
        acc = refs[n_in + 1]
        k = pl.program_id(2)

        @pl.when(k == 0)
        def _():
            acc[...] = jnp.zeros_like(acc)

        acc[...] += _dot(a_ref[...], b_ref[...], dims)

        @pl.when(k == nk - 1)
        def _():
            finish(acc[...])

    ins = [a, b] + ([res] if has_res else []) + list(deps)
    in_specs = ([a_spec, b_spec] + ([o_spec] if has_res else [])
                + [pl.BlockSpec((8, 128), lambda i, j, k: (0, 0))] * len(deps))
    return pl.pallas_call(
        body, grid=(M // tm, N // tn, nk), in_specs=in_specs,
        out_specs=pl.BlockSpec((tn, tm), lambda i, j, k: (j, i)) if transpose_out else o_spec,
        out_shape=jax.ShapeDtypeStruct((N, M) if transpose_out else (M, N), out_dtype),
        scratch_shapes=[pltpu.VMEM((tm, tn), F32)] if nk > 1 else [], name=name,
        compiler_params=_params("parallel", "parallel", "arbitrary"))(*ins)


DEP_SPEC1 = pl.BlockSpec((8, 128), lambda i: (0, 0))
DEP_SPEC2 = pl.BlockSpec((8, 128), lambda i, j: (0, 0))


def _rmsnorm_fwd(x, g, deps=()):
    S, D = x.shape
    tm = min(512, S)
    assert S % tm == 0

    def body(x_ref, g_ref, *rest):
        h_ref, ht_ref = rest[-2:]
        xv = x_ref[...]
        r = lax.rsqrt(jnp.mean(xv * xv, axis=-1, keepdims=True) + RMS_EPS)
        hv = xv * r * g_ref[...]
        h_ref[...] = hv.astype(ACT)
        ht_ref[...] = hv.T.astype(ACT)

    row = pl.BlockSpec((tm, D), lambda i: (i, 0))
    return pl.pallas_call(
        body, grid=(S // tm,), in_specs=[row, pl.BlockSpec((1, D), lambda i: (0, 0))] + [DEP_SPEC1] * len(deps),
        out_specs=[row, pl.BlockSpec((D, tm), lambda i: (0, i))],
        out_shape=[jax.ShapeDtypeStruct((S, D), ACT), jax.ShapeDtypeStruct((D, S), ACT)], name="rmsnorm_fwd",
        compiler_params=_params("parallel"))(x, g, *deps)


def _rmsnorm_bwd(x, g, dh, dres):
    S, D = x.shape
    tm = min(256, S)

    def body(x_ref, g_ref, dh_ref, dr_ref, dx_ref, dxb_ref, dg_ref):
        xv = x_ref[...]
        r = lax.rsqrt(jnp.mean(xv * xv, axis=-1, keepdims=True) + RMS_EPS)
        xh = xv * r
        dhv = dh_ref[...]
        dxh = dhv * g_ref[...]
        dx = r * (dxh - xh * jnp.mean(dxh * xh, axis=-1, keepdims=True)) + dr_ref[...]
        dx_ref[...] = dx
        dxb_ref[...] = dx.astype(ACT)

        @pl.when(pl.program_id(0) == 0)
        def _():
            dg_ref[...] = jnp.zeros_like(dg_ref)

        dg_ref[...] += jnp.sum(dhv * xh, axis=0, keepdims=True)

    row = pl.BlockSpec((tm, D), lambda i: (i, 0))
    vec = pl.BlockSpec((1, D), lambda i: (0, 0))
    return pl.pallas_call(
        body, grid=(S // tm,), in_specs=[row, vec, row, row], out_specs=[row, row, vec],
        out_shape=[jax.ShapeDtypeStruct((S, D), F32), jax.ShapeDtypeStruct((S, D), ACT),
                   jax.ShapeDtypeStruct((1, D), F32)],
        name="rmsnorm_bwd", compiler_params=_params("arbitrary"))(x, g, dh, dres)


def _rot256(x, c, s):
    x1, x2 = x[:, :128], x[:, 128:]
    return jnp.concatenate([x1 * c - x2 * s, x2 * c + x1 * s], axis=1)


def _rot256_t(g, c, s):
    g1, g2 = g[:, :128], g[:, 128:]
    return jnp.concatenate([g1 * c + g2 * s, g2 * c - g1 * s], axis=1)


def _log_decay(a_ref, h, shape):
    return -jnp.exp(jnp.full(shape, a_ref[h], F32))


def _ret_state(xsrc, xbase, xscale, ysrc, ybase, cos, sin, af, ab, mode, name):
    S = xsrc.shape[0]
    nC = S // CH
    H = RET_HEADS
    W = H * RET_HD
    assert (xbase * RET_HD) % W == 0 and (ybase * RET_HD) % W == 0

    def body(x1_ref, y1_ref, c1_ref, s1_ref, x2_ref, y2_ref, c2_ref, s2_ref, af_ref, ab_ref,
             st1_ref, st2_ref, acc1, acc2):
        @pl.when(pl.program_id(0) == 0)
        def _():
            acc1[...] = jnp.zeros_like(acc1)
            acc2[...] = jnp.zeros_like(acc2)

        j = _fiota((CH, 1), 0)
        ca, sa, cb, sb_ = c1_ref[...], s1_ref[...], c2_ref[...], s2_ref[...]
        for h in range(H):
            sl = slice(h * RET_HD, (h + 1) * RET_HD)
            lgf = _log_decay(af_ref, h, (CH, 1))
            lgb = _log_decay(ab_ref, h, (CH, 1))
            if mode == "fwd":
                w1, d1 = jnp.exp(lgf * (CH - 1.0 - j)), jnp.exp(lgf[:1] * CH)
                w2, d2 = jnp.exp(lgb * j), jnp.exp(lgb[:1] * CH)
            else:
                w1, d1 = jnp.exp(lgb * (CH - j)), jnp.exp(lgb[:1] * CH)
                w2, d2 = jnp.exp(lgf * (j + 1.0)), jnp.exp(lgf[:1] * CH)
            xa = _rot256(x1_ref[:, sl], ca, sa) * xscale
            st1_ref[h] = acc1[h]
            acc1[h] = d1 * acc1[h] + _dot(xa * w1, y1_ref[:, sl], TN)
            xb = _rot256(x2_ref[:, sl], cb, sb_) * xscale
            st2_ref[h] = acc2[h]
            acc2[h] = d2 * acc2[h] + _dot(xb * w2, y2_ref[:, sl], TN)

    xcol, ycol = (xbase * RET_HD) // W, (ybase * RET_HD) // W
    in_specs = [
        pl.BlockSpec((CH, W), lambda c: (c, xcol)), pl.BlockSpec((CH, W), lambda c: (c, ycol)),
        pl.BlockSpec((CH, 128), lambda c: (c, 0)), pl.BlockSpec((CH, 128), lambda c: (c, 0)),
        pl.BlockSpec((CH, W), lambda c: (nC - 1 - c, xcol)), pl.BlockSpec((CH, W), lambda c: (nC - 1 - c, ycol)),
        pl.BlockSpec((CH, 128), lambda c: (nC - 1 - c, 0)),
        pl.BlockSpec((CH, 128), lambda c: (nC - 1 - c, 0)),
        SMEM_SPEC, SMEM_SPEC]
    out_specs = [pl.BlockSpec((H, None, RET_HD, RET_HD), lambda c: (0, c, 0, 0)),
                 pl.BlockSpec((H, None, RET_HD, RET_HD), lambda c: (0, nC - 1 - c, 0, 0))]
    st = jax.ShapeDtypeStruct((H, nC, RET_HD, RET_HD), F32)
    return pl.pallas_call(
        body, grid=(nC,), in_specs=in_specs, out_specs=out_specs, out_shape=[st, st],
        scratch_shapes=[pltpu.VMEM((H, RET_HD, RET_HD), F32), pltpu.VMEM((H, RET_HD, RET_HD), F32)],
        name=name, compiler_params=_params("arbitrary"))(
            xsrc, ysrc, cos, sin, xsrc, ysrc, cos, sin, af, ab)


def _decay_mask(lgf1, lgb1):
    lag = _fiota((CH, CH), 0) - _fiota((CH, CH), 1)
    alag = jnp.abs(lag)
    return lag, jnp.where(lag >= 0, jnp.exp(lgf1 * alag), jnp.exp(lgb1 * alag))


def _ret_fwd(z, cos, sin, sf, sb, af, ab, cols):
    S = z.shape[0]
    nC = S // CH
    H = RET_HEADS
    rq, rk, rv, rg = cols
    W = H * RET_HD
    assert all((c * RET_HD) % W == 0 for c in cols)

    def body(q_ref, k_ref, v_ref, g_ref, c_ref, s_ref, sf_ref, sb_ref, af_ref, ab_ref,
             o_ref, u_ref, ut_ref):
        j = _fiota((CH, 1), 0)
        c, s = c_ref[...], s_ref[...]
        for h in range(H):
            sl = slice(h * RET_HD, (h + 1) * RET_HD)
            lgf = _log_decay(af_ref, h, (CH, 1))
            lgb = _log_decay(ab_ref, h, (CH, 1))
            q = _rot256(q_ref[:, sl], c, s)
            k = _rot256(k_ref[:, sl], c, s) * (RET_HD ** -0.5)
            _, dm = _decay_mask(lgf[:1], lgb[:1])
            p = _dot(q, k, NT) * dm
            o = (_dot(p, v_ref[:, sl], NN)
                 + _dot(q * jnp.exp(lgf * (j + 1.0)), sf_ref[h], NN)
                 + _dot(q * jnp.exp(lgb * (CH - j)), sb_ref[h], NN))
            o_ref[:, sl] = o
            on = o * lax.rsqrt(jnp.mean(o * o, axis=-1, keepdims=True) + RMS_EPS)
            g = g_ref[:, sl]
            u = on * (g * _sigmoid(g))
            u_ref[:, sl] = u.astype(ACT)
            ut_ref[sl, :] = u.T.astype(ACT)

    def zc(col):
        return pl.BlockSpec((CH, W), lambda c: (c, (col * RET_HD) // W))

    tab = pl.BlockSpec((CH, 128), lambda c: (c, 0))
    stt = pl.BlockSpec((H, None, RET_HD, RET_HD), lambda c: (0, c, 0, 0))
    out = pl.BlockSpec((CH, W), lambda c: (c, 0))
    return pl.pallas_call(
        body, grid=(nC,),
        in_specs=[zc(rq), zc(rk), zc(rv), zc(rg), tab, tab, stt, stt, SMEM_SPEC, SMEM_SPEC],
        out_specs=[out, out, pl.BlockSpec((W, CH), lambda c: (0, c))],
        out_shape=[jax.ShapeDtypeStruct((S, W), F32), jax.ShapeDtypeStruct((S, W), ACT),
                   jax.ShapeDtypeStruct((W, S), ACT)],
        name="ret_fwd", compiler_params=_params("parallel"))(
            z, z, z, z, cos, sin, sf, sb, af, ab)


def _ret_gate_bwd(du, o_pre, z, rg):
    S, W = du.shape
    H = RET_HEADS
    tm = min(512, S)
    assert S % tm == 0

    def body(du_ref, o_ref, g_ref, do_ref, dg_ref):
        o = o_ref[...]
        r = lax.rsqrt(jnp.mean(o * o, axis=-1, keepdims=True) + RMS_EPS)
        on = o * r
        g = g_ref[...]
        sg = _sigmoid(g)
        duv = du_ref[...]
        don = duv * (g * sg)
        dg_ref[...] = (duv * on * (sg * (1.0 + g * (1.0 - sg)))).astype(ACT)
        do_ref[...] = r * (don - on * jnp.mean(don * on, axis=-1, keepdims=True))

    blk = pl.BlockSpec((tm, RET_HD), lambda i, h: (i, h))
    return pl.pallas_call(
        body, grid=(S // tm, H),
        in_specs=[blk, blk, pl.BlockSpec((tm, RET_HD), lambda i, h: (i, rg + h))],
        out_specs=[blk, blk],
        out_shape=[jax.ShapeDtypeStruct((S, W), F32), jax.ShapeDtypeStruct((S, W), ACT)],
        name="ret_gate_bwd", compiler_params=_params("parallel", "parallel"))(du, o_pre, z)


def _ret_bwd(z, do, cos, sin, sf, sb, ef, eb, af, ab, cols):
    S = z.shape[0]
    nC = S // CH
    H = RET_HEADS
    rq, rk, rv, _ = cols
    W = H * RET_HD

    def body(q_ref, k_ref, v_ref, do_ref, c_ref, s_ref, sf_ref, sb_ref, ef_ref, eb_ref,
             af_ref, ab_ref, dq_ref, dk_ref, dv_ref, da_ref):
        @pl.when(pl.program_id(0) == 0)
        def _():
            da_ref[...] = jnp.zeros_like(da_ref)

        j = _fiota((CH, 1), 0)
        c, s = c_ref[...], s_ref[...]
        scale = RET_HD ** -0.5
        row = lax.broadcasted_iota(jnp.int32, (8, 128), 0)
        lane = lax.broadcasted_iota(jnp.int32, (8, 128), 1)
        for h in range(H):
            sl = slice(h * RET_HD, (h + 1) * RET_HD)
            lgf = _log_decay(af_ref, h, (CH, 1))
            lgb = _log_decay(ab_ref, h, (CH, 1))
            q = _rot256(q_ref[:, sl], c, s)
            k = _rot256(k_ref[:, sl], c, s) * scale
            v = v_ref[:, sl]
            do = do_ref[:, sl]
            sf_, sb_, ef_, eb_ = sf_ref[h], sb_ref[h], ef_ref[h], eb_ref[h]
            a_w = jnp.exp(lgf * (j + 1.0))
            b_w = jnp.exp(lgb * (CH - j))
            wf = jnp.exp(lgf * (CH - 1.0 - j))
            wb = jnp.exp(lgb * j)
            lag, dm = _decay_mask(lgf[:1], lgb[:1])
            sc = _dot(q, k, NT)
            gg = _dot(do, v, NT)
            dg = gg * dm
            x1 = _dot(do, sf_, NT) * a_w
            x2 = _dot(do, sb_, NT) * b_w
            y1 = _dot(v, ef_, NT) * wf
            y2 = _dot(v, eb_, NT) * wb
            dq = _dot(dg, k, NN) + x1 + x2
            dk = _dot(dg, q, TN) + y1 + y2
            dv = _dot(sc * dm, do, TN) + _dot(k * wf, ef_, NN) + _dot(k * wb, eb_, NN)
            dq_ref[:, sl] = _rot256_t(dq, c, s).astype(ACT)
            dk_ref[:, sl] = (_rot256_t(dk, c, s) * scale).astype(ACT)
            dv_ref[:, sl] = dv.astype(ACT)
            t = dm * gg * sc
            qx1 = jnp.sum(q * x1, axis=-1, keepdims=True)
            qx2 = jnp.sum(q * x2, axis=-1, keepdims=True)
            ky1 = jnp.sum(k * y1, axis=-1, keepdims=True)
            ky2 = jnp.sum(k * y2, axis=-1, keepdims=True)
            dlf = (_sum_all(jnp.where(lag > 0, lag * t, 0.0))
                   + _sum_all((j + 1.0) * qx1 + (CH - 1.0 - j) * ky1)
                   + CH * jnp.exp(lgf[:1] * CH) * _sum_all(ef_ * sf_))
            dlb = (_sum_all(jnp.where(lag < 0, -lag * t, 0.0))
                   + _sum_all((CH - j) * qx2 + j * ky2)
                   + CH * jnp.exp(lgb[:1] * CH) * _sum_all(eb_ * sb_))
            da_ref[h] += jnp.where((row == 0) & (lane == 0), dlf * lgf[:1],
                                   jnp.where((row == 0) & (lane == 1), dlb * lgb[:1], 0.0))

    def zc(col):
        return pl.BlockSpec((CH, W), lambda c: (c, (col * RET_HD) // W))

    tab = pl.BlockSpec((CH, 128), lambda c: (c, 0))
    stt = pl.BlockSpec((H, None, RET_HD, RET_HD), lambda c: (0, c, 0, 0))
    out = pl.BlockSpec((CH, W), lambda c: (c, 0))
    dz = jax.ShapeDtypeStruct((S, W), ACT)
    return pl.pallas_call(
        body, grid=(nC,),
        in_specs=[zc(rq), zc(rk), zc(rv), out, tab, tab, stt, stt, stt, stt, SMEM_SPEC, SMEM_SPEC],
        out_specs=[out, out, out, pl.BlockSpec((H, 8, 128), lambda c: (0, 0, 0))],
        out_shape=[dz, dz, dz, jax.ShapeDtypeStruct((H, 8, 128), F32)],
        name="ret_bwd", compiler_params=_params("arbitrary"))(
            z, z, z, do, cos, sin, sf, sb, ef, eb, af, ab)


def _fill_padded(pad_ref, src_ref, S):
    zeros = jnp.zeros((POOL_PAD, POOL_GD), F32)
    pad_ref[pl.ds(0, POOL_PAD), :] = zeros
    pad_ref[pl.ds(POOL_PAD, S), :] = src_ref[...]
    pad_ref[pl.ds(S + POOL_PAD, POOL_PAD), :] = zeros


def _window_sum(ext, T, lo, hi):
    n = T + 2 * POOL_PAD
    acc = None
    for k in range(lo, hi):
        sh = ext if k == 0 else pltpu.roll(ext, (-k) % n, 0)
        piece = sh[POOL_PAD:POOL_PAD + T]
        acc = piece if acc is None else acc + piece
    return acc


def _window_count(pos, w, S):
    lo = jnp.maximum(pos - w // 2, 0)
    hi = jnp.minimum(pos + w // 2, S)
    return (hi - lo).astype(F32)


def _pool_fwd(z, pw, scale, pv, pg):
    S = z.shape[0]
    G = len(POOL_WINDOWS)
    T = min(512, S)
    W = G * POOL_GD

    def body(x_ref, g_ref, pw_ref, sc_ref, y_ref, u_ref, ut_ref, pad, p_scr):
        grp = pl.program_id(0)
        i = pl.program_id(1)

        @pl.when(i == 0)
        def _():
            _fill_padded(pad, x_ref, S)

        r0 = pl.multiple_of(i * T, T)
        ext = pad[pl.ds(r0, T + 2 * POOL_PAD), :]
        pos = r0 + lax.broadcasted_iota(jnp.int32, (T, 1), 0)
        for gi, w in enumerate(POOL_WINDOWS):
            @pl.when(grp == gi)
            def _(w=w):
                acc = _window_sum(ext, T, -(w // 2), w // 2)
                p_scr[...] = acc / _window_count(pos, w, S) - ext[POOL_PAD:POOL_PAD + T]

        y = _dot(p_scr[...], pw_ref[...], NN)
        y_ref[...] = y
        g = g_ref[...]
        u = y * sc_ref[...] * (g * _sigmoid(g))
        u_ref[...] = u.astype(ACT)
        ut_ref[...] = u.T.astype(ACT)

    blk = pl.BlockSpec((T, POOL_GD), lambda g, i: (i, g))
    return pl.pallas_call(
        body, grid=(G, S // T),
        in_specs=[pl.BlockSpec((S, POOL_GD), lambda g, i: (0, pv + g)),
                  pl.BlockSpec((T, POOL_GD), lambda g, i: (i, pg + g)),
                  pl.BlockSpec((None, POOL_GD, POOL_GD), lambda g, i: (g, 0, 0)),
                  pl.BlockSpec((1, POOL_GD), lambda g, i: (0, g))],
        out_specs=[blk, blk, pl.BlockSpec((POOL_GD, T), lambda g, i: (g, i))],
        out_shape=[jax.ShapeDtypeStruct((S, W), F32), jax.ShapeDtypeStruct((S, W), ACT),
                   jax.ShapeDtypeStruct((W, S), ACT)],
        scratch_shapes=[pltpu.VMEM((S + 2 * POOL_PAD, POOL_GD), F32), pltpu.VMEM((T, POOL_GD), F32)],
        name="pool_fwd", compiler_params=_params("parallel", "arbitrary"))(z, z, pw, scale)


def _pool_bwd_a(z, pw, scale, y_raw, du, pv, pg):
    S = z.shape[0]
    G = len(POOL_WINDOWS)
    T = min(512, S)
    W = G * POOL_GD

    def body(x_ref, g_ref, pw_ref, sc_ref, y_ref, du_ref, dpc_ref, dg_ref, dsc_ref, dpw_ref,
             pad, p_scr, c_scr):
        grp = pl.program_id(0)
        i = pl.program_id(1)

        @pl.when(i == 0)
        def _():
            _fill_padded(pad, x_ref, S)
            dsc_ref[...] = jnp.zeros_like(dsc_ref)
            dpw_ref[...] = jnp.zeros_like(dpw_ref)

        r0 = pl.multiple_of(i * T, T)
        ext = pad[pl.ds(r0, T + 2 * POOL_PAD), :]
        pos = r0 + lax.broadcasted_iota(jnp.int32, (T, 1), 0)
        for gi, w in enumerate(POOL_WINDOWS):
            @pl.when(grp == gi)
            def _(w=w):
                cnt = _window_count(pos, w, S)
                acc = _window_sum(ext, T, -(w // 2), w // 2)
                p_scr[...] = acc / cnt - ext[POOL_PAD:POOL_PAD + T]
                c_scr[...] = jnp.broadcast_to(cnt, (T, 128))

        g = g_ref[...]
        sg = _sigmoid(g)
        duv = du_ref[...]
        y = y_ref[...]
        scl = sc_ref[...]
        dy = duv * (scl * (g * sg))
        dg_ref[...] = (duv * y * scl * (sg * (1.0 + g * (1.0 - sg)))).astype(ACT)
        dsc_ref[...] += jnp.sum(duv * y * (g * sg), axis=0, keepdims=True)
        dpw_ref[...] += _dot(p_scr[...], dy, TN)
        dpc_ref[...] = _dot(dy, pw_ref[...], NT) / c_scr[:, :1]

    blk = pl.BlockSpec((T, POOL_GD), lambda g, i: (i, g))
    return pl.pallas_call(
        body, grid=(G, S // T),
        in_specs=[pl.BlockSpec((S, POOL_GD), lambda g, i: (0, pv + g)),
                  pl.BlockSpec((T, POOL_GD), lambda g, i: (i, pg + g)),
                  pl.BlockSpec((None, POOL_GD, POOL_GD), lambda g, i: (g, 0, 0)),
                  pl.BlockSpec((1, POOL_GD), lambda g, i: (0, g)), blk, blk],
        out_specs=[blk, blk, pl.BlockSpec((1, POOL_GD), lambda g, i: (0, g)),
                   pl.BlockSpec((None, POOL_GD, POOL_GD), lambda g, i: (g, 0, 0))],
        out_shape=[jax.ShapeDtypeStruct((S, W), F32), jax.ShapeDtypeStruct((S, W), ACT),
                   jax.ShapeDtypeStruct((1, W), F32), jax.ShapeDtypeStruct((G, POOL_GD, POOL_GD), F32)],
        scratch_shapes=[pltpu.VMEM((S + 2 * POOL_PAD, POOL_GD), F32), pltpu.VMEM((T, POOL_GD), F32),
                        pltpu.VMEM((T, 128), F32)],
        name="pool_bwd_a", compiler_params=_params("parallel", "arbitrary"))(z, z, pw, scale, y_raw, du)


def _pool_bwd_b(dpc):
    S, W = dpc.shape
    G = len(POOL_WINDOWS)
    T = min(512, S)

    def body(x_ref, o_ref, pad, acc_scr):
        grp = pl.program_id(0)
        i = pl.program_id(1)

        @pl.when(i == 0)
        def _():
            _fill_padded(pad, x_ref, S)

        r0 = pl.multiple_of(i * T, T)
        ext = pad[pl.ds(r0, T + 2 * POOL_PAD), :]
        pos = r0 + lax.broadcasted_iota(jnp.int32, (T, 1), 0)
        for gi, w in enumerate(POOL_WINDOWS):
            @pl.when(grp == gi)
            def _(w=w):
                acc = _window_sum(ext, T, -(w // 2) + 1, w // 2 + 1)
                acc_scr[...] = acc - ext[POOL_PAD:POOL_PAD + T] * _window_count(pos, w, S)

        o_ref[...] = acc_scr[...].astype(ACT)

    blk = pl.BlockSpec((T, POOL_GD), lambda g, i: (i, g))
    return pl.pallas_call(
        body, grid=(G, S // T),
        in_specs=[pl.BlockSpec((S, POOL_GD), lambda g, i: (0, g))], out_specs=blk,
        out_shape=jax.ShapeDtypeStruct((S, W), ACT),
        scratch_shapes=[pltpu.VMEM((S + 2 * POOL_PAD, POOL_GD), F32), pltpu.VMEM((T, POOL_GD), F32)],
        name="pool_bwd_b", compiler_params=_params("parallel", "arbitrary"))(dpc)


def _rope128(x, cf, sa, sb):
    return x * cf + pltpu.roll(x, ROPE_HALF, 1) * sa + pltpu.roll(x, ATT_HD - ROPE_HALF, 1) * sb


def _rope128_t(g, cf, sa, sb):
    return g * cf + pltpu.roll(g * sa, ATT_HD - ROPE_HALF, 1) + pltpu.roll(g * sb, ROPE_HALF, 1)


def _attn_prep(z, qgain, kgain, cf, sa, sb, aq, ak, av):
    S = z.shape[0]
    T = min(512, S)
    QW, KW = ATT_Q * ATT_HD, ATT_KV * ATT_HD

    def body(q_ref, k_ref, v_ref, qg_ref, kg_ref, cf_ref, sa_ref, sb_ref, qn_ref, kn_ref, vb_ref):
        cfv, sav, sbv = cf_ref[...], sa_ref[...], sb_ref[...]

        def prep(x, gain):
            r = lax.rsqrt(jnp.mean(x * x, axis=-1, keepdims=True) + RMS_EPS)
            return _rope128(x * r * gain, cfv, sav, sbv)

        for hh in range(ATT_Q):
            sl = slice(hh * ATT_HD, (hh + 1) * ATT_HD)
            qn_ref[:, sl] = prep(q_ref[:, sl], qg_ref[...]).astype(ACT)
        for hh in range(ATT_KV):
            sl = slice(hh * ATT_HD, (hh + 1) * ATT_HD)
            kn_ref[:, sl] = prep(k_ref[:, sl], kg_ref[...]).astype(ACT)
        vb_ref[...] = v_ref[...].astype(ACT)

    tab = pl.BlockSpec((T, ATT_HD), lambda i: (i, 0))
    gain = pl.BlockSpec((1, ATT_HD), lambda i: (0, 0))
    return pl.pallas_call(
        body, grid=(S // T,),
        in_specs=[pl.BlockSpec((T, QW), lambda i: (i, aq)), pl.BlockSpec((T, KW), lambda i: (i, ak)),
                  pl.BlockSpec((T, KW), lambda i: (i, av)), gain, gain, tab, tab, tab],
        out_specs=[pl.BlockSpec((T, QW), lambda i: (i, 0)), pl.BlockSpec((T, KW), lambda i: (i, 0)),
                   pl.BlockSpec((T, KW), lambda i: (i, 0))],
        out_shape=[jax.ShapeDtypeStruct((S, QW), ACT), jax.ShapeDtypeStruct((S, KW), ACT),
                   jax.ShapeDtypeStruct((S, KW), ACT)],
        name="attn_prep", compiler_params=_params("parallel"))(z, z, z, qgain, kgain, cf, sa, sb)


def _attn_window(i, S):
    start = jnp.clip(i * ATT_BLK - ATT_BLK, 0, S - ATT_SPAN)
    start = pl.multiple_of(start, ATT_BLK)
    qpos = i * ATT_BLK + lax.broadcasted_iota(jnp.int32, (ATT_BLK, ATT_SPAN), 0)
    kpos = start + lax.broadcasted_iota(jnp.int32, (ATT_BLK, ATT_SPAN), 1)
    return start, jnp.abs(kpos - qpos) <= ATT_WIN


def _attn_probs(q, kw, valid, sink):
    s = _dot(q, kw, NT) * (ATT_HD ** -0.5)
    s = jnp.where(valid, s, NEG_BIG)
    m = jnp.maximum(jnp.max(s, axis=-1, keepdims=True), sink)
    p = jnp.exp(s - m)
    es = jnp.exp(sink - m)
    den = jnp.sum(p, axis=-1, keepdims=True) + es
    return p / den, es / den


def _attn_fwd(qn, kn, vb, z, sink, ag):
    S = qn.shape[0]
    nB = S // ATT_BLK
    assert S >= ATT_SPAN
    QW = ATT_Q * ATT_HD

    GW = ATT_G * ATT_HD

    def body(q_ref, k_ref, v_ref, g_ref, sink_ref, o_ref, u_ref, ut_ref):
        kvh = pl.program_id(0)
        i = pl.program_id(1)
        start, valid = _attn_window(i, S)
        kw = k_ref[pl.ds(start, ATT_SPAN), :]
        vw = v_ref[pl.ds(start, ATT_SPAN), :]
        for gi in range(ATT_G):
            sl = slice(gi * ATT_HD, (gi + 1) * ATT_HD)
            sk = jnp.full((ATT_BLK, 1), sink_ref[kvh * ATT_G + gi], F32)
            pn, _ = _attn_probs(q_ref[:, sl], kw, valid, sk)
            o = _dot(pn, vw, NN)
            o_ref[:, sl] = o
            g = g_ref[:, sl]
            u = o * (g * _sigmoid(g))
            u_ref[:, sl] = u.astype(ACT)
            ut_ref[sl, :] = u.T.astype(ACT)

    blk = pl.BlockSpec((ATT_BLK, GW), lambda k, i: (i, k))
    kv = pl.BlockSpec((S, ATT_HD), lambda k, i: (0, k))
    return pl.pallas_call(
        body, grid=(ATT_KV, nB),
        in_specs=[blk, kv, kv, pl.BlockSpec((ATT_BLK, GW), lambda k, i: (i, ag // ATT_G + k)), SMEM_SPEC],
        out_specs=[blk, blk, pl.BlockSpec((GW, ATT_BLK), lambda k, i: (k, i))],
        out_shape=[jax.ShapeDtypeStruct((S, QW), F32), jax.ShapeDtypeStruct((S, QW), ACT),
                   jax.ShapeDtypeStruct((QW, S), ACT)],
        name="attn_fwd", compiler_params=_params("parallel", "parallel"))(qn, kn, vb, z, sink)


def _attn_bwd(qn, kn, vb, o, du, z, sink, ag):
    S = qn.shape[0]
    nB = S // ATT_BLK
    QW, KW = ATT_Q * ATT_HD, ATT_KV * ATT_HD
    GW = ATT_G * ATT_HD

    def body(q_ref, k_ref, v_ref, o_ref, du_ref, g_ref, sink_ref,
             dq_ref, dk_ref, dv_ref, dg_ref, ds_ref):
        kvh = pl.program_id(0)
        i = pl.program_id(1)

        @pl.when(i == 0)
        def _():
            dk_ref[...] = jnp.zeros_like(dk_ref)
            dv_ref[...] = jnp.zeros_like(dv_ref)
            ds_ref[...] = jnp.zeros_like(ds_ref)

        start, valid = _attn_window(i, S)
        kw = k_ref[pl.ds(start, ATT_SPAN), :]
        vw = v_ref[pl.ds(start, ATT_SPAN), :]
        row = lax.broadcasted_iota(jnp.int32, (8, 128), 0)
        lane = lax.broadcasted_iota(jnp.int32, (8, 128), 1)
        dsink = jnp.zeros((8, 128), F32)
        dk_acc = jnp.zeros((ATT_SPAN, ATT_HD), F32)
        dv_acc = jnp.zeros((ATT_SPAN, ATT_HD), F32)
        for gi in range(ATT_G):
            sl = slice(gi * ATT_HD, (gi + 1) * ATT_HD)
            q = q_ref[:, sl]
            ov = o_ref[:, sl]
            g = g_ref[:, sl]
            duv = du_ref[:, sl]
            sg = _sigmoid(g)
            do = duv * (g * sg)
            dg_ref[:, sl] = (duv * ov * (sg * (1.0 + g * (1.0 - sg)))).astype(ACT)
            sk = jnp.full((ATT_BLK, 1), sink_ref[kvh * ATT_G + gi], F32)
            pn, psink = _attn_probs(q, kw, valid, sk)
            delta = jnp.sum(do * ov, axis=-1, keepdims=True)
            dsc = pn * (_dot(do, vw, NT) - delta) * (ATT_HD ** -0.5)
            dq_ref[:, sl] = _dot(dsc, kw, NN)
            dk_acc = dk_acc + _dot(dsc, q, TN)
            dv_acc = dv_acc + _dot(pn, do, TN)
            dsink = dsink + jnp.where((row == 0) & (lane == gi), -_sum_all(psink * delta), 0.0)
        dk_ref[pl.ds(start, ATT_SPAN), :] += dk_acc
        dv_ref[pl.ds(start, ATT_SPAN), :] += dv_acc
        ds_ref[...] += dsink

    grp = pl.BlockSpec((ATT_BLK, GW), lambda k, i: (i, k))
    kv = pl.BlockSpec((S, ATT_HD), lambda k, i: (0, k))
    return pl.pallas_call(
        body, grid=(ATT_KV, nB),
        in_specs=[grp, kv, kv, grp, grp,
                  pl.BlockSpec((ATT_BLK, GW), lambda k, i: (i, ag // ATT_G + k)), SMEM_SPEC],
        out_specs=[grp, kv, kv, grp, pl.BlockSpec((None, 8, 128), lambda k, i: (k, 0, 0))],
        out_shape=[jax.ShapeDtypeStruct((S, QW), F32), jax.ShapeDtypeStruct((S, KW), F32),
                   jax.ShapeDtypeStruct((S, KW), F32), jax.ShapeDtypeStruct((S, QW), ACT),
                   jax.ShapeDtypeStruct((ATT_KV, 8, 128), F32)],
        name="attn_bwd", compiler_params=_params("parallel", "arbitrary"))(qn, kn, vb, o, du, z, sink)


def _attn_prep_bwd(z, dqn, dkn, dv, qgain, kgain, cf, sa, sb, aq, ak):
    S = z.shape[0]
    T = min(512, S)
    QW, KW = ATT_Q * ATT_HD, ATT_KV * ATT_HD

    def body(q_ref, k_ref, dqn_ref, dkn_ref, dv_ref, qg_ref, kg_ref, cf_ref, sa_ref, sb_ref,
             dq_ref, dk_ref, dvb_ref, dqg_ref, dkg_ref):
        cfv, sav, sbv = cf_ref[...], sa_ref[...], sb_ref[...]

        @pl.when(pl.program_id(0) == 0)
        def _():
            dqg_ref[...] = jnp.zeros_like(dqg_ref)
            dkg_ref[...] = jnp.zeros_like(dkg_ref)

        def back(x, gn, gain):
            r = lax.rsqrt(jnp.mean(x * x, axis=-1, keepdims=True) + RMS_EPS)
            xh = x * r
            dy = _rope128_t(gn, cfv, sav, sbv)
            dxh = dy * gain
            dx = r * (dxh - xh * jnp.mean(dxh * xh, axis=-1, keepdims=True))
            return dx, jnp.sum(dy * xh, axis=0, keepdims=True)

        dqg = jnp.zeros((1, ATT_HD), F32)
        for hh in range(ATT_Q):
            sl = slice(hh * ATT_HD, (hh + 1) * ATT_HD)
            dx, dgn = back(q_ref[:, sl], dqn_ref[:, sl], qg_ref[...])
            dq_ref[:, sl] = dx.astype(ACT)
            dqg = dqg + dgn
        dkg = jnp.zeros((1, ATT_HD), F32)
        for hh in range(ATT_KV):
            sl = slice(hh * ATT_HD, (hh + 1) * ATT_HD)
            dx, dgn = back(k_ref[:, sl], dkn_ref[:, sl], kg_ref[...])
            dk_ref[:, sl] = dx.astype(ACT)
            dkg = dkg + dgn
        dvb_ref[...] = dv_ref[...].astype(ACT)
        dqg_ref[...] += dqg
        dkg_ref[...] += dkg

    tab = pl.BlockSpec((T, ATT_HD), lambda i: (i, 0))
    gain = pl.BlockSpec((1, ATT_HD), lambda i: (0, 0))
    qb = pl.BlockSpec((T, QW), lambda i: (i, 0))
    kb = pl.BlockSpec((T, KW), lambda i: (i, 0))
    return pl.pallas_call(
        body, grid=(S // T,),
        in_specs=[pl.BlockSpec((T, QW), lambda i: (i, aq)), pl.BlockSpec((T, KW), lambda i: (i, ak)),
                  qb, kb, kb, gain, gain, tab, tab, tab],
        out_specs=[qb, kb, kb, gain, gain],
        out_shape=[jax.ShapeDtypeStruct((S, QW), ACT), jax.ShapeDtypeStruct((S, KW), ACT),
                   jax.ShapeDtypeStruct((S, KW), ACT), jax.ShapeDtypeStruct((1, ATT_HD), F32),
                   jax.ShapeDtypeStruct((1, ATT_HD), F32)],
        name="attn_prep_bwd", compiler_params=_params("arbitrary"))(
            z, z, dqn, dkn, dv, qgain, kgain, cf, sa, sb)


def _branch_merge(ua, ub, uc, wr, wp, wa, z, mg):
    S, W = ua.shape
    D = wr.shape[1]
    tm, tn = min(512, S), min(512, D)
    nb = D // tn

    def body(ua_ref, ub_ref, uc_ref, wr_ref, wp_ref, wa_ref, g0_ref, g1_ref, g2_ref,
             ya_ref, yb_ref, yc_ref, m_ref, mt_ref):
        ya = _dot(ua_ref[...], wr_ref[...], NN)
        yb = _dot(ub_ref[...], wp_ref[...], NN)
        yc = _dot(uc_ref[...], wa_ref[...], NN)
        ya_ref[...] = ya.astype(ACT)
        yb_ref[...] = yb.astype(ACT)
        yc_ref[...] = yc.astype(ACT)
        m = _sigmoid(g0_ref[...]) * ya + _sigmoid(g1_ref[...]) * yb + _sigmoid(g2_ref[...]) * yc
        m_ref[...] = m.astype(ACT)
        mt_ref[...] = m.T.astype(ACT)

    u = pl.BlockSpec((tm, W), lambda i, j: (i, 0))
    w = pl.BlockSpec((W, tn), lambda i, j: (0, j))
    o = pl.BlockSpec((tm, tn), lambda i, j: (i, j))

    assert (mg * POOL_GD) % tn == 0
    base = (mg * POOL_GD) // tn

    def gate(k):
        return pl.BlockSpec((tm, tn), lambda i, j: (i, base + k * nb + j))

    sd = jax.ShapeDtypeStruct((S, D), ACT)
    return pl.pallas_call(
        body, grid=(S // tm, nb), in_specs=[u, u, u, w, w, w, gate(0), gate(1), gate(2)],
        out_specs=[o, o, o, o, pl.BlockSpec((tn, tm), lambda i, j: (j, i))],
        out_shape=[sd, sd, sd, sd, jax.ShapeDtypeStruct((D, S), ACT)], name="branch_merge",
        compiler_params=_params("parallel", "parallel"))(ua, ub, uc, wr, wp, wa, z, z, z)


def _merge_bwd(dxb, wo, ya, yb, yc, z, mg):
    S, D = dxb.shape
    tm, tn = min(512, S), min(512, D)
    nb = D // tn
    base = (mg * POOL_GD) // tn

    def body(dx_ref, wo_ref, ya_ref, yb_ref, yc_ref, g0_ref, g1_ref, g2_ref,
             da_ref, db_ref, dc_ref, dg0_ref, dg1_ref, dg2_ref):
        dm = _dot(dx_ref[...], wo_ref[...], NT)
        for y_ref, g_ref, dy_ref, dg_ref in ((ya_ref, g0_ref, da_ref, dg0_ref),
                                             (yb_ref, g1_ref, db_ref, dg1_ref),
                                             (yc_ref, g2_ref, dc_ref, dg2_ref)):
            sg = _sigmoid(g_ref[...])
            dy_ref[...] = (sg * dm).astype(ACT)
            dg_ref[...] = (dm * y_ref[...].astype(F32) * (sg * (1.0 - sg))).astype(ACT)

    o = pl.BlockSpec((tm, tn), lambda i, j: (i, j))

    def gate(k):
        return pl.BlockSpec((tm, tn), lambda i, j: (i, base + k * nb + j))

    sd = jax.ShapeDtypeStruct((S, D), ACT)
    return pl.pallas_call(
        body, grid=(S // tm, nb),
        in_specs=[pl.BlockSpec((tm, D), lambda i, j: (i, 0)), pl.BlockSpec((tn, D), lambda i, j: (j, 0)),
                  o, o, o, gate(0), gate(1), gate(2)],
        out_specs=[o] * 6, out_shape=[sd] * 6, name="merge_bwd",
        compiler_params=_params("parallel", "parallel"))(dxb, wo, ya, yb, yc, z, z, z)


def _loss_head(y, t):
    S, D = y.shape
    tm = min(256, S)

    def body(y_ref, t_ref, dy_ref, dyb_ref, l_ref):
        e = y_ref[...] - t_ref[...]
        dy = e * (1.0 / D)
        dy_ref[...] = dy
        dyb_ref[...] = dy.astype(ACT)

        @pl.when(pl.program_id(0) == 0)
        def _():
            l_ref[...] = jnp.zeros_like(l_ref)

        l_ref[...] += jnp.sum(jnp.mean(e * e, axis=-1, keepdims=True), axis=0, keepdims=True)

    row = pl.BlockSpec((tm, D), lambda i: (i, 0))
    return pl.pallas_call(
        body, grid=(S // tm,), in_specs=[row, row],
        out_specs=[row, row, pl.BlockSpec((1, 1), lambda i: (0, 0))],
        out_shape=[jax.ShapeDtypeStruct((S, D), F32), jax.ShapeDtypeStruct((S, D), ACT),
                   jax.ShapeDtypeStruct((1, 1), F32)],
        name="loss_head", compiler_params=_params("arbitrary"))(y, t)


def _adamw(w, layer, parts, m, v, name):
    L = w.shape[0]
    shape = w.shape[1:]
    C = shape[-1]
    R = int(np.prod(shape[:-1]))
    w3, m3, v3 = (a.reshape(L, R, C) for a in (w, m, v))
    p3 = parts.reshape(N_DEV, R, C)
    tr = min(64, R)
    assert R % tr == 0
    c1 = 1.0 / (1.0 - ADAM_B1 ** ADAM_STEP)
    c2 = 1.0 / (1.0 - ADAM_B2 ** ADAM_STEP)

    def body(w_ref, p_ref, m_ref, v_ref, g_ref, d_ref, nm_ref, nv_ref):
        g = p_ref[0].astype(F32)
        for k in range(1, N_DEV):
            g = g + p_ref[k].astype(F32)
        nm = ADAM_B1 * m_ref[...] + (1.0 - ADAM_B1) * g
        nv = ADAM_B2 * v_ref[...] + (1.0 - ADAM_B2) * (g * g)
        g_ref[...] = g
        nm_ref[...] = nm
        nv_ref[...] = nv
        d_ref[...] = -ADAM_LR * ((nm * c1) / (jnp.sqrt(nv * c2) + ADAM_EPS) + ADAM_WD * w_ref[...])

    lay = pl.BlockSpec((None, tr, C), lambda i: (layer, i, 0))
    out = pl.BlockSpec((tr, C), lambda i: (i, 0))
    sd = jax.ShapeDtypeStruct((R, C), F32)
    outs = pl.pallas_call(
        body, grid=(R // tr,),
        in_specs=[lay, pl.BlockSpec((N_DEV, tr, C), lambda i: (0, i, 0)), lay, lay],
        out_specs=[out] * 4, out_shape=[sd] * 4, name=name,
        compiler_params=_params("parallel"))(w3, p3, m3, v3)
    return [a.reshape(shape) for a in outs]


def _exchange(arrs, scatter, name):
    n = len(arrs)
    out_shape = [jax.ShapeDtypeStruct(a.shape if scatter else (N_DEV,) + a.shape, a.dtype) for a in arrs]

    def body(*refs):
        ins, outs = refs[:n], refs[n:2 * n]
        send_sems, recv_sems, local_sems = refs[2 * n:]
        x, y, c = lax.axis_index("x"), lax.axis_index("y"), lax.axis_index("c")
        me = 4 * x + 2 * y + c
        copies = []
        for a in range(n):
            src = ins[a].at[me] if scatter else ins[a]
            own = pltpu.make_async_copy(src, outs[a].at[me], local_sems.at[a])
            own.start()
            copies.append(own)
        sends, recvs = [], []
        for k in range(1, N_DEV):
            px, py, pc = x ^ (k >> 2), y ^ ((k >> 1) & 1), c ^ (k & 1)
            peer = 4 * px + 2 * py + pc
            for a in range(n):
                src = ins[a].at[peer] if scatter else ins[a]
                cp = pltpu.make_async_remote_copy(
                    src_ref=src, dst_ref=outs[a].at[me],
                    send_sem=send_sems.at[a, k - 1], recv_sem=recv_sems.at[a, k - 1],
                    device_id=(px, py, pc), device_id_type=pl.DeviceIdType.MESH)
                cp.start()
                sends.append(cp)
                recvs.append(pltpu.make_async_remote_copy(
                    src_ref=src, dst_ref=outs[a].at[peer],
                    send_sem=send_sems.at[a, k - 1], recv_sem=recv_sems.at[a, k - 1],
                    device_id=(px, py, pc), device_id_type=pl.DeviceIdType.MESH))
        for cp in recvs:
            cp.wait_recv()
        for cp in sends:
            cp.wait_send()
        for cp in copies:
            cp.wait()

    any_spec = pl.BlockSpec(memory_space=pl.ANY)
    return pl.pallas_call(
        body, in_specs=[any_spec] * n, out_specs=[any_spec] * n, out_shape=out_shape,
        scratch_shapes=[pltpu.SemaphoreType.DMA((n, N_DEV - 1)), pltpu.SemaphoreType.DMA((n, N_DEV - 1)),
                        pltpu.SemaphoreType.DMA((n,))],
        name=name)(*arrs)


HBM_SPEC = pl.BlockSpec(memory_space=pltpu.HBM)
SEM_SPEC = pl.BlockSpec(memory_space=pltpu.SEMAPHORE)
DATAFLOW = pltpu.SideEffectType.DATAFLOW_SIDE_EFFECTING


def _peer_of(k):
    x, y, c = lax.axis_index("x"), lax.axis_index("y"), lax.axis_index("c")
    return x ^ (k >> 2), y ^ ((k >> 1) & 1), c ^ (k & 1)


def _exchange_copy(k, a, src_ref, land_ref, send_sems, recv_sems, scatter, outgoing):
    px, py, pc = _peer_of(k)
    peer = 4 * px + 2 * py + pc
    me = 4 * lax.axis_index("x") + 2 * lax.axis_index("y") + lax.axis_index("c")
    idx = a * (N_DEV - 1) + k - 1
    return pltpu.make_async_remote_copy(
        src_ref=src_ref.at[peer] if scatter else src_ref, dst_ref=land_ref.at[me if outgoing else peer],
        send_sem=send_sems.at[idx], recv_sem=recv_sems.at[idx],
        device_id=(px, py, pc), device_id_type=pl.DeviceIdType.MESH)


def _exchange_start(arrs, scatter, name):
    n = len(arrs)
    land_shapes = [a.shape if scatter else (N_DEV,) + a.shape for a in arrs]

    def body(*refs):
        srcs, lands = refs[:n], refs[n:2 * n]
        send_sems, recv_sems = refs[2 * n], refs[2 * n + 1]
        token = refs[-1]
        for k in range(1, N_DEV):
            for a in range(n):
                _exchange_copy(k, a, srcs[a], lands[a], send_sems, recv_sems, scatter, True).start()
        token[...] = jnp.zeros_like(token)

    sems = pltpu.SemaphoreType.DMA((n * (N_DEV - 1),))
    out_shape = ([sems, sems] + [pltpu.HBM(a.shape, a.dtype) for a in arrs]
                 + [pltpu.HBM(s, a.dtype) for s, a in zip(land_shapes, arrs)]
                 + [jax.ShapeDtypeStruct((8, 128), F32)])
    ins = ([pltpu.with_memory_space_constraint(a, pltpu.HBM) for a in arrs]
           + [pltpu.with_memory_space_constraint(lax.empty(s, a.dtype), pltpu.HBM) for s, a in zip(land_shapes, arrs)])
    res = pl.pallas_call(
        body, name=name, out_shape=out_shape, in_specs=[HBM_SPEC] * (2 * n),
        out_specs=[SEM_SPEC, SEM_SPEC] + [HBM_SPEC] * (2 * n) + [pl.BlockSpec(memory_space=pltpu.VMEM)],
        input_output_aliases={i: 2 + i for i in range(2 * n)},
        compiler_params=pltpu.CompilerParams(has_side_effects=DATAFLOW))(*ins)
    return res[0], res[1], list(res[2:2 + n]), list(res[2 + n:2 + 2 * n]), res[-1]


def _exchange_wait(started, after, scatter, name):
    send_sems, recv_sems, srcs, lands, _ = started
    n = len(srcs)

    def body(*refs):
        src_refs, land_refs = refs[:n], refs[n:2 * n]
        s_sems, r_sems = refs[2 * n], refs[2 * n + 1]
        for k in range(1, N_DEV):
            for a in range(n):
                back = _exchange_copy(k, a, src_refs[a], land_refs[a], s_sems, r_sems, scatter, False)
                back.wait_send()
                back.wait_recv()

    out_shape = [pltpu.HBM(a.shape, a.dtype) for a in srcs] + [pltpu.HBM(a.shape, a.dtype) for a in lands]
    res = pl.pallas_call(
        body, name=name, out_shape=out_shape,
        in_specs=[HBM_SPEC] * (2 * n) + [SEM_SPEC, SEM_SPEC, pl.BlockSpec(memory_space=pl.ANY)],
        out_specs=[HBM_SPEC] * (2 * n), input_output_aliases={i: i for i in range(2 * n)},
        compiler_params=pltpu.CompilerParams(has_side_effects=DATAFLOW))(*srcs, *lands, send_sems, recv_sems, after)
    return list(res[:n]), list(res[n:])


def _own_slot(land, own, me):
    return lax.dynamic_update_index_in_dim(land, own, me, 0)


class _Cols:
    def __init__(self, D):
        Wb = D // 2
        sizes = (Wb, Wb, Wb, Wb, Wb, Wb, Wb, ATT_KV * ATT_HD, ATT_KV * ATT_HD, Wb, 3 * D)
        offs = np.concatenate([[0], np.cumsum(sizes)])
        assert all(int(o) % 256 == 0 for o in offs)
        (self.rq, self.rk, self.rv, self.rg, self.pv, self.pg,
         self.aq, self.ak, self.av, self.ag, self.mg) = (int(o) // 256 for o in offs[:-1])
        self.width = int(offs[-1])
        self.sizes = sizes


def _rope_tables(S):
    pos = jnp.arange(S, dtype=F32)[:, None]
    inv_r = 1.0 / (RET_ROPE_BASE ** jnp.linspace(0.0, 1.0, RET_HD // 2, dtype=F32))
    ang_r = pos * inv_r[None, :]
    inv_a = ROPE_THETA ** (-jnp.arange(ROPE_HALF, dtype=F32) / ROPE_HALF)
    ang_a = pos * inv_a[None, :]
    ca, sa = jnp.cos(ang_a), jnp.sin(ang_a)
    z16 = jnp.zeros((S, ROPE_HALF), F32)
    rest = ATT_HD - 2 * ROPE_HALF
    cf = jnp.concatenate([ca, ca, jnp.ones((S, rest), F32)], axis=1)
    s_up = jnp.concatenate([z16, sa, jnp.zeros((S, rest), F32)], axis=1)
    s_dn = jnp.concatenate([-sa, z16, jnp.zeros((S, rest), F32)], axis=1)
    return jnp.cos(ang_r), jnp.sin(ang_r), cf, s_up, s_dn


def _layer_fwd(x, p, tabs, cols, deps=()):
    cos_r, sin_r, cf, s_up, s_dn = tabs
    S, D = x.shape
    h, ht = _rmsnorm_fwd(x, p["norm_g"], deps)
    z = _matmul(h, p["w_in_t"], "nt", F32, 1024, 512, D, "in_proj")
    if "rest" in p:
        p = {**{k: v for k, v in p.items() if k != "rest"}, **p["rest"](z)}
    rcols = (cols.rq, cols.rk, cols.rv, cols.rg)
    sf, sb = _ret_state(z, cols.rk, RET_HD ** -0.5, z, cols.rv, cos_r, sin_r, p["af"], p["ab"], "fwd",
                        "ret_state_fwd")
    o_ret, ua, uat = _ret_fwd(z, cos_r, sin_r, sf, sb, p["af"], p["ab"], rcols)
    y_pool, ub, ubt = _pool_fwd(z, p["pool_w"], p["pool_scale"], cols.pv, cols.pg)
    qn, kn, vb = _attn_prep(z, p["q_gain"], p["k_gain"], cf, s_up, s_dn,
                            cols.aq // 4, cols.ak, cols.av)
    o_att, uc, uct = _attn_fwd(qn, kn, vb, z, p["sink"], cols.ag * 2)
    ya, yb, yc, merged, mergedt = _branch_merge(ua, ub, uc, p["w_ret"], p["w_pool"], p["w_att"], z, cols.mg)
    out = _matmul(merged, p["w_out"], "nn", F32, 1024, 512, D, "out_proj", res=x)
    saved = dict(x=x, ht=ht, z=z, sf=sf, sb=sb, o_ret=o_ret, uat=uat, y_pool=y_pool, ubt=ubt,
                 qn=qn, kn=kn, vb=vb, o_att=o_att, uct=uct, ya=ya, yb=yb, yc=yc, mergedt=mergedt)
    return out, saved, p


def _layer_bwd(dx, dxb, p, sv, tabs, cols, on_grads):
    cos_r, sin_r, cf, s_up, s_dn = tabs
    z = sv["z"]
    S, D = dx.shape
    dya, dyb, dyc, dmg0, dmg1, dmg2 = _merge_bwd(dxb, p["w_out"], sv["ya"], sv["yb"], sv["yc"], z, cols.mg)
    grads = {"w_out": _matmul(sv["mergedt"], dxb, "nn", ACT, 1024, 512, S, "dw_out")}
    dus = {}
    for nm, ut, dy in (("ret", sv["uat"], dya), ("pool", sv["ubt"], dyb), ("att", sv["uct"], dyc)):
        dus[nm] = _matmul(dy, p["w_" + nm], "nt", F32, 1024, 512, D, "du_" + nm)
        grads["w_" + nm] = _matmul(ut, dy, "nn", ACT, 1024, 512, S, "dw_" + nm)
    rcols = (cols.rq, cols.rk, cols.rv, cols.rg)
    do_ret, d_rg = _ret_gate_bwd(dus["ret"], sv["o_ret"], z, cols.rg)
    eb, ef = _ret_state(z, cols.rq, 1.0, do_ret, 0, cos_r, sin_r, p["af"], p["ab"], "bwd", "ret_state_bwd")
    d_rq, d_rk, d_rv, d_decay = _ret_bwd(z, do_ret, cos_r, sin_r, sv["sf"], sv["sb"], ef, eb,
                                         p["af"], p["ab"], rcols)
    dpc, d_pg, d_pscale, g_pool_w = _pool_bwd_a(z, p["pool_w"], p["pool_scale"], sv["y_pool"], dus["pool"],
                                                cols.pv, cols.pg)
    d_pv = _pool_bwd_b(dpc)
    grads["pool_w"] = g_pool_w.astype(ACT)
    dqn, dkn, dvv, d_ag, d_sink = _attn_bwd(sv["qn"], sv["kn"], sv["vb"], sv["o_att"], dus["att"], z,
                                            p["sink"], cols.ag * 2)
    d_aq, d_ak, d_av, d_qg, d_kg = _attn_prep_bwd(z, dqn, dkn, dvv, p["q_gain"], p["k_gain"], cf, s_up, s_dn,
                                                  cols.aq // 4, cols.ak)
    dz = jnp.concatenate([d_rq, d_rk, d_rv, d_rg, d_pv, d_pg, d_aq, d_ak, d_av, d_ag, dmg0, dmg1, dmg2],
                         axis=1)
    grads["w_in_t"] = _matmul(sv["ht"], dz, "nn", ACT, 1024, 512, S, "dw_in", transpose_out=True)
    tok = on_grads(grads)
    dh = _matmul(dz, p["w_in_t"], "nn", F32, 1024, 512, 29 * 128, "dh", deps=() if tok is None else (tok,))
    dx_in, dxb_in, d_norm_g = _rmsnorm_bwd(sv["x"], p["norm_g"], dh, dx)
    misc = jnp.concatenate([d_decay[:, 0, 0], d_decay[:, 0, 1], d_sink[:, 0, :ATT_G].reshape(-1)])
    misc = jnp.pad(misc, (0, 128 - misc.shape[0]))[None, :]
    small = jnp.concatenate([d_norm_g, d_pscale, d_qg, d_kg, misc], axis=1)
    return dx_in, dxb_in, small


def _pack_small(norm_g, pool_scale, q_gain, k_gain, af, ab, sink):
    L = norm_g.shape[0]
    misc = jnp.concatenate([af, ab, sink], axis=1)
    misc = jnp.pad(misc, ((0, 0), (0, 128 - misc.shape[1])))
    return jnp.concatenate([norm_g, pool_scale, q_gain, k_gain, misc], axis=1)


def _unpack_small(a, D):
    Wb = D // 2
    o = np.cumsum([0, D, Wb, ATT_HD, ATT_HD])
    misc = a[:, o[4]:]
    return (a[:, o[0]:o[1]], a[:, o[1]:o[2]], a[:, o[2]:o[3]], a[:, o[3]:o[4]],
            misc[:, :RET_HEADS], misc[:, RET_HEADS:2 * RET_HEADS],
            misc[:, 2 * RET_HEADS:2 * RET_HEADS + ATT_Q])


def _local_step(x, t, n_layers, get_layer, on_grads, tabs, cols, first_dep=None):
    saved, layers = [], []
    after = first_dep
    for l in range(n_layers):
        p = get_layer(l, after)
        x, sv, p = _layer_fwd(x, p, tabs, cols, (first_dep,) if (l == 0 and first_dep is not None) else ())
        after = x
        layers.append(p)
        saved.append(sv)
    dx, dxb, lsum = _loss_head(x, t)
    smalls = []
    for l in reversed(range(n_layers)):
        dx, dxb, sm = _layer_bwd(dx, dxb, layers[l], saved[l], tabs, cols, functools.partial(on_grads, l))
        smalls.append(sm)
    return 0.5 * lsum[0, 0], dx, jnp.concatenate(smalls[::-1], axis=0)


WEIGHT_KEYS = ("w_in", "w_ret", "w_pool", "w_att", "w_out", "pool_w")


def kernel(x, norm_g, w_in, ret_decay_fwd, ret_decay_bwd, pool_w, pool_scale, attn_q_gain, attn_k_gain, attn_sink, w_ret, w_pool, w_att, w_out, loss_target, m_norm_g, m_w_in, m_ret_decay_fwd, m_ret_decay_bwd, m_pool_w, m_pool_scale, m_attn_q_gain, m_attn_k_gain, m_attn_sink, m_w_ret, m_w_pool, m_w_att, m_w_out, v_norm_g, v_w_in, v_ret_decay_fwd, v_ret_decay_bwd, v_pool_w, v_pool_scale, v_attn_q_gain, v_attn_k_gain, v_attn_sink, v_w_ret, v_w_pool, v_w_att, v_w_out):
    L = norm_g.shape[0]
    _, S, D = x.shape
    Wb = D // 2
    G = len(POOL_WINDOWS)
    cols = _Cols(D)
    tabs = _rope_tables(S)
    me = 4 * lax.axis_index("x") + 2 * lax.axis_index("y") + lax.axis_index("c")
    def tr(a):
        return jnp.transpose(a, (0, 2, 1))

    weights = dict(w_in=tr(w_in), w_ret=w_ret, w_pool=w_pool, w_att=w_att, w_out=w_out, pool_w=pool_w)

    gathers, tok = [], None
    for l in range(L):
        started = []
        for part, keys in (("a", WEIGHT_KEYS[:1]), ("b", WEIGHT_KEYS[1:])):
            shards = []
            for k in keys:
                w = weights[k][l]
                if tok is not None:
                    w = w + tok[0, 0]
                shards.append(w.astype(MXU))
            st = _exchange_start(shards, False, f"gather_start_{l}{part}")
            started.append(st)
            tok = st[-1]
        gathers.append(started)

    def cols_full(g, rows):
        return jnp.transpose(g, (1, 0, 2)).reshape(rows, -1)

    def get_layer(l, after):
        srcs, lands = _exchange_wait(gathers[l][0], after, False, f"gather_wait_{l}a")
        g_in = _own_slot(lands[0], srcs[0], me)

        def rest(z):
            srcs, lands = _exchange_wait(gathers[l][1], z, False, f"gather_wait_{l}b")
            g_ret, g_pool, g_att, g_out, g_pw = [_own_slot(ld, sr, me) for ld, sr in zip(lands, srcs)]
            return dict(
                w_ret=cols_full(g_ret, Wb), w_pool=cols_full(g_pool, Wb), w_att=cols_full(g_att, Wb),
                w_out=g_out.reshape(D, D),
                pool_w=jnp.transpose(g_pw, (1, 0, 2, 3)).reshape(G, POOL_GD, POOL_GD))

        return dict(
            norm_g=norm_g[l][None, :], w_in_t=g_in.reshape(-1, D), rest=rest,
            pool_scale=pool_scale[l][None, :], q_gain=attn_q_gain[l][None, :], k_gain=attn_k_gain[l][None, :],
            af=ret_decay_fwd[l], ab=ret_decay_bwd[l], sink=attn_sink[l])

    def col_slots(g, rows):
        return jnp.transpose(g.reshape(rows, N_DEV, -1), (1, 0, 2))

    scatters = {}

    def on_grads(l, g):
        slots = [g["w_in_t"].reshape(N_DEV, -1, D),
                 col_slots(g["w_ret"], Wb), col_slots(g["w_pool"], Wb),
                 col_slots(g["w_att"], Wb), g["w_out"].reshape(N_DEV, D // N_DEV, D),
                 jnp.transpose(g["pool_w"].reshape(G, N_DEV, POOL_GD // N_DEV, POOL_GD), (1, 0, 2, 3))]
        scatters[l] = _exchange_start(slots, True, f"scatter_start_{l}")
        return scatters[l][-1]

    loss_local, grad_x, small = _local_step(x[0], loss_target[0], L, get_layer, on_grads, tabs, cols, tok)
    loss = lax.psum(loss_local, ("x", "y", "c"))

    moments = dict(w_in=(tr(m_w_in), tr(v_w_in)), w_ret=(m_w_ret, v_w_ret), w_pool=(m_w_pool, v_w_pool),
                   w_att=(m_w_att, v_w_att), w_out=(m_w_out, v_w_out), pool_w=(m_pool_w, v_pool_w))
    res = {k: [jnp.zeros(weights[k].shape, F32) for _ in range(4)] for k in WEIGHT_KEYS}
    after = grad_x
    small_all = None
    for l in reversed(range(L)):
        if l == 0:
            zero = jnp.sum(lax.bitcast_convert_type(after, jnp.int32) & 0).astype(F32)
            small_all, = _exchange([small + zero], False, "gather_small_grads")
            after = small_all
        srcs, lands = _exchange_wait(scatters[l], after, True, f"scatter_wait_{l}")
        firsts = []
        for k, ld, sr in zip(WEIGHT_KEYS, lands, srcs):
            parts = _own_slot(ld, lax.dynamic_index_in_dim(sr, me, 0, keepdims=False), me)
            m, v = moments[k]
            outs = _adamw(weights[k], l, parts, m, v, "adamw_" + k)
            res[k] = [lax.dynamic_update_index_in_dim(r, o, l, 0) for r, o in zip(res[k], outs)]
            firsts.append(outs[1].reshape(-1)[:1])
        after = jnp.concatenate(firsts)

    sw = _pack_small(norm_g, pool_scale, attn_q_gain, attn_k_gain, ret_decay_fwd, ret_decay_bwd, attn_sink)
    sm = _pack_small(m_norm_g, m_pool_scale, m_attn_q_gain, m_attn_k_gain, m_ret_decay_fwd, m_ret_decay_bwd,
                     m_attn_sink)
    sv_ = _pack_small(v_norm_g, v_pool_scale, v_attn_q_gain, v_attn_k_gain, v_ret_decay_fwd, v_ret_decay_bwd,
                      v_attn_sink)
    small_res = [_unpack_small(a, D) for a in _adamw(sw[None], 0, small_all, sm[None], sv_[None], "adamw_small")]

    def ordered(i):
        ng, ps, qg, kg, af, ab, sk = small_res[i]
        return (ng, tr(res["w_in"][i]), af, ab, res["pool_w"][i], ps, qg, kg, sk,
                res["w_ret"][i], res["w_pool"][i], res["w_att"][i], res["w_out"][i])

    return (loss, grad_x[None], *ordered(0), *ordered(1), *ordered(2), *ordered(3))
```

```python
import functools

import numpy as np
import jax
import jax.numpy as jnp
from jax import lax
from jax.experimental import pallas as pl
from jax.experimental.pallas import tpu as pltpu

F32 = jnp.float32
MXU = jnp.bfloat16
ACT = jnp.bfloat16

N_DEV = 8
RMS_EPS = 1e-6
NEG_BIG = -1e30
RET_HEADS = 4
RET_HD = 256
CH = 128
RET_ROPE_BASE = 10000.0
POOL_WINDOWS = (2, 4, 8, 16)
POOL_GD = 256
POOL_PAD = 8
ATT_HD = 128
ATT_Q = 8
ATT_KV = 2
ATT_G = ATT_Q // ATT_KV
ATT_WIN = 128
ATT_BLK = 128
ATT_SPAN = 3 * ATT_BLK
ROPE_THETA = 500000.0
ROPE_HALF = 16

ADAM_LR = 0.001
ADAM_B1 = 0.9
ADAM_B2 = 0.999
ADAM_EPS = 1e-08
ADAM_WD = 0.01
ADAM_STEP = 10

VMEM_LIMIT = 48 * 1024 * 1024

NN = ((1,), (0,))
NT = ((1,), (1,))
TN = ((0,), (0,))


def _dot(a, b, dims):
    return lax.dot_general(a.astype(MXU), b.astype(MXU), (dims, ((), ())),
                           preferred_element_type=F32)


def _sigmoid(x):
    return 1.0 / (1.0 + jnp.exp(-x))


def _params(*sem):
    return pltpu.CompilerParams(dimension_semantics=sem, vmem_limit_bytes=VMEM_LIMIT)


def _sum_all(x):
    return jnp.sum(jnp.sum(x, axis=1, keepdims=True), axis=0, keepdims=True)


def _fiota(shape, dim):
    return lax.broadcasted_iota(jnp.int32, shape, dim).astype(F32)


SMEM_SPEC = pl.BlockSpec(memory_space=pltpu.SMEM)


def _matmul(a, b, mode, out_dtype, tm, tn, tk, name, res=None, deps=(), transpose_out=False):
    if mode == "tn":
        K, M = a.shape
    else:
        M, K = a.shape
    N = b.shape[0] if mode == "nt" else b.shape[1]
    tm, tn, tk = min(tm, M), min(tn, N), min(tk, K)
    assert M % tm == 0 and N % tn == 0 and K % tk == 0, (name, M, N, K, tm, tn, tk)
    nk = K // tk
    dims = {"nn": NN, "nt": NT, "tn": TN}[mode]
    a_spec = (pl.BlockSpec((tk, tm), lambda i, j, k: (k, i)) if mode == "tn"
              else pl.BlockSpec((tm, tk), lambda i, j, k: (i, k)))
    b_spec = (pl.BlockSpec((tn, tk), lambda i, j, k: (j, k)) if mode == "nt"
              else pl.BlockSpec((tk, tn), lambda i, j, k: (k, j)))
    o_spec = pl.BlockSpec((tm, tn), lambda i, j, k: (i, j))
    has_res = res is not None
    assert not (has_res and transpose_out)
    n_in = 2 + has_res + len(deps)

    def body(*refs):
        a_ref, b_ref = refs[:2]
        r_ref = refs[2] if has_res else None
        o_ref = refs[n_in]

        def finish(out):
            if has_res:
                out = out + r_ref[...]
            o_ref[...] = (out.T if transpose_out else out).astype(out_dtype)

        if nk == 1:
            finish(_dot(a_ref[...], b_ref[...], dims))
            return
        acc = refs[n_in + 1]
        k = pl.program_id(2)

        @pl.when(k == 0)
        def _():
            acc[...] = jnp.zeros_like(acc)

        acc[...] += _dot(a_ref[...], b_ref[...], dims)

        @pl.when(k == nk - 1)
        def _():
            finish(acc[...])

    ins = [a, b] + ([res] if has_res else []) + list(deps)
    in_specs = ([a_spec, b_spec] + ([o_spec] if has_res else [])
                + [pl.BlockSpec((8, 128), lambda i, j, k: (0, 0))] * len(deps))
    return pl.pallas_call(
        body, grid=(M // tm, N // tn, nk), in_specs=in_specs,
        out_specs=pl.BlockSpec((tn, tm), lambda i, j, k: (j, i)) if transpose_out else o_spec,
        out_shape=jax.ShapeDtypeStruct((N, M) if transpose_out else (M, N), out_dtype),
        scratch_shapes=[pltpu.VMEM((tm, tn), F32)] if nk > 1 else [], name=name,
        compiler_params=_params("parallel", "parallel", "arbitrary"))(*ins)


DEP_SPEC1 = pl.BlockSpec((8, 128), lambda i: (0, 0))
DEP_SPEC2 = pl.BlockSpec((8, 128), lambda i, j: (0, 0))


def _rmsnorm_fwd(x, g, deps=()):
    S, D = x.shape
    tm = min(512, S)
    assert S % tm == 0

    def body(x_ref, g_ref, *rest):
        h_ref, ht_ref = rest[-2:]
        xv = x_ref[...]
        r = lax.rsqrt(jnp.mean(xv * xv, axis=-1, keepdims=True) + RMS_EPS)
        hv = xv * r * g_ref[...]
        h_ref[...] = hv.astype(ACT)
        ht_ref[...] = hv.T.astype(ACT)

    row = pl.BlockSpec((tm, D), lambda i: (i, 0))
    return pl.pallas_call(
        body, grid=(S // tm,), in_specs=[row, pl.BlockSpec((1, D), lambda i: (0, 0))] + [DEP_SPEC1] * len(deps),
        out_specs=[row, pl.BlockSpec((D, tm), lambda i: (0, i))],
        out_shape=[jax.ShapeDtypeStruct((S, D), ACT), jax.ShapeDtypeStruct((D, S), ACT)], name="rmsnorm_fwd",
        compiler_params=_params("parallel"))(x, g, *deps)


def _rmsnorm_bwd(x, g, dh, dres):
    S, D = x.shape
    tm = min(256, S)

    def body(x_ref, g_ref, dh_ref, dr_ref, dx_ref, dxb_ref, dg_ref):
        xv = x_ref[...]
        r = lax.rsqrt(jnp.mean(xv * xv, axis=-1, keepdims=True) + RMS_EPS)
        xh = xv * r
        dhv = dh_ref[...]
        dxh = dhv * g_ref[...]
        dx = r * (dxh - xh * jnp.mean(dxh * xh, axis=-1, keepdims=True)) + dr_ref[...]
        dx_ref[...] = dx
        dxb_ref[...] = dx.astype(ACT)

        @pl.when(pl.program_id(0) == 0)
        def _():
            dg_ref[...] = jnp.zeros_like(dg_ref)

        dg_ref[...] += jnp.sum(dhv * xh, axis=0, keepdims=True)

    row = pl.BlockSpec((tm, D), lambda i: (i, 0))
    vec = pl.BlockSpec((1, D), lambda i: (0, 0))
    return pl.pallas_call(
        body, grid=(S // tm,), in_specs=[row, vec, row, row], out_specs=[row, row, vec],
        out_shape=[jax.ShapeDtypeStruct((S, D), F32), jax.ShapeDtypeStruct((S, D), ACT),
                   jax.ShapeDtypeStruct((1, D), F32)],
        name="rmsnorm_bwd", compiler_params=_params("arbitrary"))(x, g, dh, dres)


def _rot256(x, c, s):
    x1, x2 = x[:, :128], x[:, 128:]
    return jnp.concatenate([x1 * c - x2 * s, x2 * c + x1 * s], axis=1)


def _rot256_t(g, c, s):
    g1, g2 = g[:, :128], g[:, 128:]
    return jnp.concatenate([g1 * c + g2 * s, g2 * c - g1 * s], axis=1)


def _log_decay(a_ref, h, shape):
    return -jnp.exp(jnp.full(shape, a_ref[h], F32))


def _ret_state(xsrc, xbase, xscale, ysrc, ybase, cos, sin, af, ab, mode, name):
    S = xsrc.shape[0]
    nC = S // CH
    H = RET_HEADS
    W = H * RET_HD
    assert (xbase * RET_HD) % W == 0 and (ybase * RET_HD) % W == 0

    def body(x1_ref, y1_ref, c1_ref, s1_ref, x2_ref, y2_ref, c2_ref, s2_ref, af_ref, ab_ref,
             st1_ref, st2_ref, acc1, acc2):
        @pl.when(pl.program_id(0) == 0)
        def _():
            acc1[...] = jnp.zeros_like(acc1)
            acc2[...] = jnp.zeros_like(acc2)

        j = _fiota((CH, 1), 0)
        ca, sa, cb, sb_ = c1_ref[...], s1_ref[...], c2_ref[...], s2_ref[...]
        for h in range(H):
            sl = slice(h * RET_HD, (h + 1) * RET_HD)
            lgf = _log_decay(af_ref, h, (CH, 1))
            lgb = _log_decay(ab_ref, h, (CH, 1))
            if mode == "fwd":
                w1, d1 = jnp.exp(lgf * (CH - 1.0 - j)), jnp.exp(lgf[:1] * CH)
                w2, d2 = jnp.exp(lgb * j), jnp.exp(lgb[:1] * CH)
            else:
                w1, d1 = jnp.exp(lgb * (CH - j)), jnp.exp(lgb[:1] * CH)
                w2, d2 = jnp.exp(lgf * (j + 1.0)), jnp.exp(lgf[:1] * CH)
            xa = _rot256(x1_ref[:, sl], ca, sa) * xscale
            st1_ref[h] = acc1[h]
            acc1[h] = d1 * acc1[h] + _dot(xa * w1, y1_ref[:, sl], TN)
            xb = _rot256(x2_ref[:, sl], cb, sb_) * xscale
            st2_ref[h] = acc2[h]
            acc2[h] = d2 * acc2[h] + _dot(xb * w2, y2_ref[:, sl], TN)

    xcol, ycol = (xbase * RET_HD) // W, (ybase * RET_HD) // W
    in_specs = [
        pl.BlockSpec((CH, W), lambda c: (c, xcol)), pl.BlockSpec((CH, W), lambda c: (c, ycol)),
        pl.BlockSpec((CH, 128), lambda c: (c, 0)), pl.BlockSpec((CH, 128), lambda c: (c, 0)),
        pl.BlockSpec((CH, W), lambda c: (nC - 1 - c, xcol)), pl.BlockSpec((CH, W), lambda c: (nC - 1 - c, ycol)),
        pl.BlockSpec((CH, 128), lambda c: (nC - 1 - c, 0)),
        pl.BlockSpec((CH, 128), lambda c: (nC - 1 - c, 0)),
        SMEM_SPEC, SMEM_SPEC]
    out_specs = [pl.BlockSpec((H, None, RET_HD, RET_HD), lambda c: (0, c, 0, 0)),
                 pl.BlockSpec((H, None, RET_HD, RET_HD), lambda c: (0, nC - 1 - c, 0, 0))]
    st = jax.ShapeDtypeStruct((H, nC, RET_HD, RET_HD), F32)
    return pl.pallas_call(
        body, grid=(nC,), in_specs=in_specs, out_specs=out_specs, out_shape=[st, st],
        scratch_shapes=[pltpu.VMEM((H, RET_HD, RET_HD), F32), pltpu.VMEM((H, RET_HD, RET_HD), F32)],
        name=name, compiler_params=_params("arbitrary"))(
            xsrc, ysrc, cos, sin, xsrc, ysrc, cos, sin, af, ab)


def _decay_mask(lgf1, lgb1):
    lag = _fiota((CH, CH), 0) - _fiota((CH, CH), 1)
    alag = jnp.abs(lag)
    return lag, jnp.where(lag >= 0, jnp.exp(lgf1 * alag), jnp.exp(lgb1 * alag))


def _ret_fwd(z, cos, sin, sf, sb, af, ab, cols):
    S = z.shape[0]
    nC = S // CH
    H = RET_HEADS
    rq, rk, rv, rg = cols
    W = H * RET_HD
    assert all((c * RET_HD) % W == 0 for c in cols)

    def body(q_ref, k_ref, v_ref, g_ref, c_ref, s_ref, sf_ref, sb_ref, af_ref, ab_ref,
             o_ref, u_ref, ut_ref):
        j = _fiota((CH, 1), 0)
        c, s = c_ref[...], s_ref[...]
        for h in range(H):
            sl = slice(h * RET_HD, (h + 1) * RET_HD)
            lgf = _log_decay(af_ref, h, (CH, 1))
            lgb = _log_decay(ab_ref, h, (CH, 1))
            q = _rot256(q_ref[:, sl], c, s)
            k = _rot256(k_ref[:, sl], c, s) * (RET_HD ** -0.5)
            _, dm = _decay_mask(lgf[:1], lgb[:1])
            p = _dot(q, k, NT) * dm
            o = (_dot(p, v_ref[:, sl], NN)
                 + _dot(q * jnp.exp(lgf * (j + 1.0)), sf_ref[h], NN)
                 + _dot(q * jnp.exp(lgb * (CH - j)), sb_ref[h], NN))
            o_ref[:, sl] = o
            on = o * lax.rsqrt(jnp.mean(o * o, axis=-1, keepdims=True) + RMS_EPS)
            g = g_ref[:, sl]
            u = on * (g * _sigmoid(g))
            u_ref[:, sl] = u.astype(ACT)
            ut_ref[sl, :] = u.T.astype(ACT)

    def zc(col):
        return pl.BlockSpec((CH, W), lambda c: (c, (col * RET_HD) // W))

    tab = pl.BlockSpec((CH, 128), lambda c: (c, 0))
    stt = pl.BlockSpec((H, None, RET_HD, RET_HD), lambda c: (0, c, 0, 0))
    out = pl.BlockSpec((CH, W), lambda c: (c, 0))
    return pl.pallas_call(
        body, grid=(nC,),
        in_specs=[zc(rq), zc(rk), zc(rv), zc(rg), tab, tab, stt, stt, SMEM_SPEC, SMEM_SPEC],
        out_specs=[out, out, pl.BlockSpec((W, CH), lambda c: (0, c))],
        out_shape=[jax.ShapeDtypeStruct((S, W), F32), jax.ShapeDtypeStruct((S, W), ACT),
                   jax.ShapeDtypeStruct((W, S), ACT)],
        name="ret_fwd", compiler_params=_params("parallel"))(
            z, z, z, z, cos, sin, sf, sb, af, ab)


def _ret_gate_bwd(du, o_pre, z, rg):
    S, W = du.shape
    H = RET_HEADS
    tm = min(512, S)
    assert S % tm == 0

    def body(du_ref, o_ref, g_ref, do_ref, dg_ref):
        o = o_ref[...]
        r = lax.rsqrt(jnp.mean(o * o, axis=-1, keepdims=True) + RMS_EPS)
        on = o * r
        g = g_ref[...]
        sg = _sigmoid(g)
        duv = du_ref[...]
        don = duv * (g * sg)
        dg_ref[...] = (duv * on * (sg * (1.0 + g * (1.0 - sg)))).astype(ACT)
        do_ref[...] = r * (don - on * jnp.mean(don * on, axis=-1, keepdims=True))

    blk = pl.BlockSpec((tm, RET_HD), lambda i, h: (i, h))
    return pl.pallas_call(
        body, grid=(S // tm, H),
        in_specs=[blk, blk, pl.BlockSpec((tm, RET_HD), lambda i, h: (i, rg + h))],
        out_specs=[blk, blk],
        out_shape=[jax.ShapeDtypeStruct((S, W), F32), jax.ShapeDtypeStruct((S, W), ACT)],
        name="ret_gate_bwd", compiler_params=_params("parallel", "parallel"))(du, o_pre, z)


def _ret_bwd(z, do, cos, sin, sf, sb, ef, eb, af, ab, cols):
    S = z.shape[0]
    nC = S // CH
    H = RET_HEADS
    rq, rk, rv, _ = cols
    W = H * RET_HD

    def body(q_ref, k_ref, v_ref, do_ref, c_ref, s_ref, sf_ref, sb_ref, ef_ref, eb_ref,
             af_ref, ab_ref, dq_ref, dk_ref, dv_ref, da_ref):
        @pl.when(pl.program_id(0) == 0)
        def _():
            da_ref[...] = jnp.zeros_like(da_ref)

        j = _fiota((CH, 1), 0)
        c, s = c_ref[...], s_ref[...]
        scale = RET_HD ** -0.5
        row = lax.broadcasted_iota(jnp.int32, (8, 128), 0)
        lane = lax.broadcasted_iota(jnp.int32, (8, 128), 1)
        for h in range(H):
            sl = slice(h * RET_HD, (h + 1) * RET_HD)
            lgf = _log_decay(af_ref, h, (CH, 1))
            lgb = _log_decay(ab_ref, h, (CH, 1))
            q = _rot256(q_ref[:, sl], c, s)
            k = _rot256(k_ref[:, sl], c, s) * scale
            v = v_ref[:, sl]
            do = do_ref[:, sl]
            sf_, sb_, ef_, eb_ = sf_ref[h], sb_ref[h], ef_ref[h], eb_ref[h]
            a_w = jnp.exp(lgf * (j + 1.0))
            b_w = jnp.exp(lgb * (CH - j))
            wf = jnp.exp(lgf * (CH - 1.0 - j))
            wb = jnp.exp(lgb * j)
            lag, dm = _decay_mask(lgf[:1], lgb[:1])
            sc = _dot(q, k, NT)
            gg = _dot(do, v, NT)
            dg = gg * dm
            x1 = _dot(do, sf_, NT) * a_w
            x2 = _dot(do, sb_, NT) * b_w
            y1 = _dot(v, ef_, NT) * wf
            y2 = _dot(v, eb_, NT) * wb
            dq = _dot(dg, k, NN) + x1 + x2
            dk = _dot(dg, q, TN) + y1 + y2
            dv = _dot(sc * dm, do, TN) + _dot(k * wf, ef_, NN) + _dot(k * wb, eb_, NN)
            dq_ref[:, sl] = _rot256_t(dq, c, s).astype(ACT)
            dk_ref[:, sl] = (_rot256_t(dk, c, s) * scale).astype(ACT)
            dv_ref[:, sl] = dv.astype(ACT)
            t = dm * gg * sc
            qx1 = jnp.sum(q * x1, axis=-1, keepdims=True)
            qx2 = jnp.sum(q * x2, axis=-1, keepdims=True)
            ky1 = jnp.sum(k * y1, axis=-1, keepdims=True)
            ky2 = jnp.sum(k * y2, axis=-1, keepdims=True)
            dlf = (_sum_all(jnp.where(lag > 0, lag * t, 0.0))
                   + _sum_all((j + 1.0) * qx1 + (CH - 1.0 - j) * ky1)
                   + CH * jnp.exp(lgf[:1] * CH) * _sum_all(ef_ * sf_))
            dlb = (_sum_all(jnp.where(lag < 0, -lag * t, 0.0))
                   + _sum_all((CH - j) * qx2 + j * ky2)
                   + CH * jnp.exp(lgb[:1] * CH) * _sum_all(eb_ * sb_))
            da_ref[h] += jnp.where((row == 0) & (lane == 0), dlf * lgf[:1],
                                   jnp.where((row == 0) & (lane == 1), dlb * lgb[:1], 0.0))

    def zc(col):
        return pl.BlockSpec((CH, W), lambda c: (c, (col * RET_HD) // W))

    tab = pl.BlockSpec((CH, 128), lambda c: (c, 0))
    stt = pl.BlockSpec((H, None, RET_HD, RET_HD), lambda c: (0, c, 0, 0))
    out = pl.BlockSpec((CH, W), lambda c: (c, 0))
    dz = jax.ShapeDtypeStruct((S, W), ACT)
    return pl.pallas_call(
        body, grid=(nC,),
        in_specs=[zc(rq), zc(rk), zc(rv), out, tab, tab, stt, stt, stt, stt, SMEM_SPEC, SMEM_SPEC],
        out_specs=[out, out, out, pl.BlockSpec((H, 8, 128), lambda c: (0, 0, 0))],
        out_shape=[dz, dz, dz, jax.ShapeDtypeStruct((H, 8, 128), F32)],
        name="ret_bwd", compiler_params=_params("arbitrary"))(
            z, z, z, do, cos, sin, sf, sb, ef, eb, af, ab)


def _fill_padded(pad_ref, src_ref, S):
    zeros = jnp.zeros((POOL_PAD, POOL_GD), F32)
    pad_ref[pl.ds(0, POOL_PAD), :] = zeros
    pad_ref[pl.ds(POOL_PAD, S), :] = src_ref[...]
    pad_ref[pl.ds(S + POOL_PAD, POOL_PAD), :] = zeros


def _window_sum(ext, T, lo, hi):
    n = T + 2 * POOL_PAD
    acc = None
    for k in range(lo, hi):
        sh = ext if k == 0 else pltpu.roll(ext, (-k) % n, 0)
        piece = sh[POOL_PAD:POOL_PAD + T]
        acc = piece if acc is None else acc + piece
    return acc


def _window_count(pos, w, S):
    lo = jnp.maximum(pos - w // 2, 0)
    hi = jnp.minimum(pos + w // 2, S)
    return (hi - lo).astype(F32)


def _pool_fwd(z, pw, scale, pv, pg):
    S = z.shape[0]
    G = len(POOL_WINDOWS)
    T = min(512, S)
    W = G * POOL_GD

    def body(x_ref, g_ref, pw_ref, sc_ref, y_ref, u_ref, ut_ref, pad, p_scr):
        grp = pl.program_id(0)
        i = pl.program_id(1)

        @pl.when(i == 0)
        def _():
            _fill_padded(pad, x_ref, S)

        r0 = pl.multiple_of(i * T, T)
        ext = pad[pl.ds(r0, T + 2 * POOL_PAD), :]
        pos = r0 + lax.broadcasted_iota(jnp.int32, (T, 1), 0)
        for gi, w in enumerate(POOL_WINDOWS):
            @pl.when(grp == gi)
            def _(w=w):
                acc = _window_sum(ext, T, -(w // 2), w // 2)
                p_scr[...] = acc / _window_count(pos, w, S) - ext[POOL_PAD:POOL_PAD + T]

        y = _dot(p_scr[...], pw_ref[...], NN)
        y_ref[...] = y
        g = g_ref[...]
        u = y * sc_ref[...] * (g * _sigmoid(g))
        u_ref[...] = u.astype(ACT)
        ut_ref[...] = u.T.astype(ACT)

    blk = pl.BlockSpec((T, POOL_GD), lambda g, i: (i, g))
    return pl.pallas_call(
        body, grid=(G, S // T),
        in_specs=[pl.BlockSpec((S, POOL_GD), lambda g, i: (0, pv + g)),
                  pl.BlockSpec((T, POOL_GD), lambda g, i: (i, pg + g)),
                  pl.BlockSpec((None, POOL_GD, POOL_GD), lambda g, i: (g, 0, 0)),
                  pl.BlockSpec((1, POOL_GD), lambda g, i: (0, g))],
        out_specs=[blk, blk, pl.BlockSpec((POOL_GD, T), lambda g, i: (g, i))],
        out_shape=[jax.ShapeDtypeStruct((S, W), F32), jax.ShapeDtypeStruct((S, W), ACT),
                   jax.ShapeDtypeStruct((W, S), ACT)],
        scratch_shapes=[pltpu.VMEM((S + 2 * POOL_PAD, POOL_GD), F32), pltpu.VMEM((T, POOL_GD), F32)],
        name="pool_fwd", compiler_params=_params("parallel", "arbitrary"))(z, z, pw, scale)


def _pool_bwd_a(z, pw, scale, y_raw, du, pv, pg):
    S = z.shape[0]
    G = len(POOL_WINDOWS)
    T = min(512, S)
    W = G * POOL_GD

    def body(x_ref, g_ref, pw_ref, sc_ref, y_ref, du_ref, dpc_ref, dg_ref, dsc_ref, dpw_ref,
             pad, p_scr, c_scr):
        grp = pl.program_id(0)
        i = pl.program_id(1)

        @pl.when(i == 0)
        def _():
            _fill_padded(pad, x_ref, S)
            dsc_ref[...] = jnp.zeros_like(dsc_ref)
            dpw_ref[...] = jnp.zeros_like(dpw_ref)

        r0 = pl.multiple_of(i * T, T)
        ext = pad[pl.ds(r0, T + 2 * POOL_PAD), :]
        pos = r0 + lax.broadcasted_iota(jnp.int32, (T, 1), 0)
        for gi, w in enumerate(POOL_WINDOWS):
            @pl.when(grp == gi)
            def _(w=w):
                cnt = _window_count(pos, w, S)
                acc = _window_sum(ext, T, -(w // 2), w // 2)
                p_scr[...] = acc / cnt - ext[POOL_PAD:POOL_PAD + T]
                c_scr[...] = jnp.broadcast_to(cnt, (T, 128))

        g = g_ref[...]
        sg = _sigmoid(g)
        duv = du_ref[...]
        y = y_ref[...]
        scl = sc_ref[...]
        dy = duv * (scl * (g * sg))
        dg_ref[...] = (duv * y * scl * (sg * (1.0 + g * (1.0 - sg)))).astype(ACT)
        dsc_ref[...] += jnp.sum(duv * y * (g * sg), axis=0, keepdims=True)
        dpw_ref[...] += _dot(p_scr[...], dy, TN)
        dpc_ref[...] = _dot(dy, pw_ref[...], NT) / c_scr[:, :1]

    blk = pl.BlockSpec((T, POOL_GD), lambda g, i: (i, g))
    return pl.pallas_call(
        body, grid=(G, S // T),
        in_specs=[pl.BlockSpec((S, POOL_GD), lambda g, i: (0, pv + g)),
                  pl.BlockSpec((T, POOL_GD), lambda g, i: (i, pg + g)),
                  pl.BlockSpec((None, POOL_GD, POOL_GD), lambda g, i: (g, 0, 0)),
                  pl.BlockSpec((1, POOL_GD), lambda g, i: (0, g)), blk, blk],
        out_specs=[blk, blk, pl.BlockSpec((1, POOL_GD), lambda g, i: (0, g)),
                   pl.BlockSpec((None, POOL_GD, POOL_GD), lambda g, i: (g, 0, 0))],
        out_shape=[jax.ShapeDtypeStruct((S, W), F32), jax.ShapeDtypeStruct((S, W), ACT),
                   jax.ShapeDtypeStruct((1, W), F32), jax.ShapeDtypeStruct((G, POOL_GD, POOL_GD), F32)],
        scratch_shapes=[pltpu.VMEM((S + 2 * POOL_PAD, POOL_GD), F32), pltpu.VMEM((T, POOL_GD), F32),
                        pltpu.VMEM((T, 128), F32)],
        name="pool_bwd_a", compiler_params=_params("parallel", "arbitrary"))(z, z, pw, scale, y_raw, du)


def _pool_bwd_b(dpc):
    S, W = dpc.shape
    G = len(POOL_WINDOWS)
    T = min(512, S)

    def body(x_ref, o_ref, pad, acc_scr):
        grp = pl.program_id(0)
        i = pl.program_id(1)

        @pl.when(i == 0)
        def _():
            _fill_padded(pad, x_ref, S)

        r0 = pl.multiple_of(i * T, T)
        ext = pad[pl.ds(r0, T + 2 * POOL_PAD), :]
        pos = r0 + lax.broadcasted_iota(jnp.int32, (T, 1), 0)
        for gi, w in enumerate(POOL_WINDOWS):
            @pl.when(grp == gi)
            def _(w=w):
                acc = _window_sum(ext, T, -(w // 2) + 1, w // 2 + 1)
                acc_scr[...] = acc - ext[POOL_PAD:POOL_PAD + T] * _window_count(pos, w, S)

        o_ref[...] = acc_scr[...].astype(ACT)

    blk = pl.BlockSpec((T, POOL_GD), lambda g, i: (i, g))
    return pl.pallas_call(
        body, grid=(G, S // T),
        in_specs=[pl.BlockSpec((S, POOL_GD), lambda g, i: (0, g))], out_specs=blk,
        out_shape=jax.ShapeDtypeStruct((S, W), ACT),
        scratch_shapes=[pltpu.VMEM((S + 2 * POOL_PAD, POOL_GD), F32), pltpu.VMEM((T, POOL_GD), F32)],
        name="pool_bwd_b", compiler_params=_params("parallel", "arbitrary"))(dpc)


def _rope128(x, cf, sa, sb):
    return x * cf + pltpu.roll(x, ROPE_HALF, 1) * sa + pltpu.roll(x, ATT_HD - ROPE_HALF, 1) * sb


def _rope128_t(g, cf, sa, sb):
    return g * cf + pltpu.roll(g * sa, ATT_HD - ROPE_HALF, 1) + pltpu.roll(g * sb, ROPE_HALF, 1)


def _attn_prep(z, qgain, kgain, cf, sa, sb, aq, ak, av):
    S = z.shape[0]
    T = min(512, S)
    QW, KW = ATT_Q * ATT_HD, ATT_KV * ATT_HD

    def body(q_ref, k_ref, v_ref, qg_ref, kg_ref, cf_ref, sa_ref, sb_ref, qn_ref, kn_ref, vb_ref):
        cfv, sav, sbv = cf_ref[...], sa_ref[...], sb_ref[...]

        def prep(x, gain):
            r = lax.rsqrt(jnp.mean(x * x, axis=-1, keepdims=True) + RMS_EPS)
            return _rope128(x * r * gain, cfv, sav, sbv)

        for hh in range(ATT_Q):
            sl = slice(hh * ATT_HD, (hh + 1) * ATT_HD)
            qn_ref[:, sl] = prep(q_ref[:, sl], qg_ref[...]).astype(ACT)
        for hh in range(ATT_KV):
            sl = slice(hh * ATT_HD, (hh + 1) * ATT_HD)
            kn_ref[:, sl] = prep(k_ref[:, sl], kg_ref[...]).astype(ACT)
        vb_ref[...] = v_ref[...].astype(ACT)

    tab = pl.BlockSpec((T, ATT_HD), lambda i: (i, 0))
    gain = pl.BlockSpec((1, ATT_HD), lambda i: (0, 0))
    return pl.pallas_call(
        body, grid=(S // T,),
        in_specs=[pl.BlockSpec((T, QW), lambda i: (i, aq)), pl.BlockSpec((T, KW), lambda i: (i, ak)),
                  pl.BlockSpec((T, KW), lambda i: (i, av)), gain, gain, tab, tab, tab],
        out_specs=[pl.BlockSpec((T, QW), lambda i: (i, 0)), pl.BlockSpec((T, KW), lambda i: (i, 0)),
                   pl.BlockSpec((T, KW), lambda i: (i, 0))],
        out_shape=[jax.ShapeDtypeStruct((S, QW), ACT), jax.ShapeDtypeStruct((S, KW), ACT),
                   jax.ShapeDtypeStruct((S, KW), ACT)],
        name="attn_prep", compiler_params=_params("parallel"))(z, z, z, qgain, kgain, cf, sa, sb)


def _attn_window(i, S):
    start = jnp.clip(i * ATT_BLK - ATT_BLK, 0, S - ATT_SPAN)
    start = pl.multiple_of(start, ATT_BLK)
    qpos = i * ATT_BLK + lax.broadcasted_iota(jnp.int32, (ATT_BLK, ATT_SPAN), 0)
    kpos = start + lax.broadcasted_iota(jnp.int32, (ATT_BLK, ATT_SPAN), 1)
    return start, jnp.abs(kpos - qpos) <= ATT_WIN


def _attn_probs(q, kw, valid, sink):
    s = _dot(q, kw, NT) * (ATT_HD ** -0.5)
    s = jnp.where(valid, s, NEG_BIG)
    m = jnp.maximum(jnp.max(s, axis=-1, keepdims=True), sink)
    p = jnp.exp(s - m)
    es = jnp.exp(sink - m)
    den = jnp.sum(p, axis=-1, keepdims=True) + es
    return p / den, es / den


def _attn_fwd(qn, kn, vb, z, sink, ag):
    S = qn.shape[0]
    nB = S // ATT_BLK
    assert S >= ATT_SPAN
    QW = ATT_Q * ATT_HD

    GW = ATT_G * ATT_HD

    def body(q_ref, k_ref, v_ref, g_ref, sink_ref, o_ref, u_ref, ut_ref):
        kvh = pl.program_id(0)
        i = pl.program_id(1)
        start, valid = _attn_window(i, S)
        kw = k_ref[pl.ds(start, ATT_SPAN), :]
        vw = v_ref[pl.ds(start, ATT_SPAN), :]
        for gi in range(ATT_G):
            sl = slice(gi * ATT_HD, (gi + 1) * ATT_HD)
            sk = jnp.full((ATT_BLK, 1), sink_ref[kvh * ATT_G + gi], F32)
            pn, _ = _attn_probs(q_ref[:, sl], kw, valid, sk)
            o = _dot(pn, vw, NN)
            o_ref[:, sl] = o
            g = g_ref[:, sl]
            u = o * (g * _sigmoid(g))
            u_ref[:, sl] = u.astype(ACT)
            ut_ref[sl, :] = u.T.astype(ACT)

    blk = pl.BlockSpec((ATT_BLK, GW), lambda k, i: (i, k))
    kv = pl.BlockSpec((S, ATT_HD), lambda k, i: (0, k))
    return pl.pallas_call(
        body, grid=(ATT_KV, nB),
        in_specs=[blk, kv, kv, pl.BlockSpec((ATT_BLK, GW), lambda k, i: (i, ag // ATT_G + k)), SMEM_SPEC],
        out_specs=[blk, blk, pl.BlockSpec((GW, ATT_BLK), lambda k, i: (k, i))],
        out_shape=[jax.ShapeDtypeStruct((S, QW), F32), jax.ShapeDtypeStruct((S, QW), ACT),
                   jax.ShapeDtypeStruct((QW, S), ACT)],
        name="attn_fwd", compiler_params=_params("parallel", "parallel"))(qn, kn, vb, z, sink)


def _attn_bwd(qn, kn, vb, o, du, z, sink, ag):
    S = qn.shape[0]
    nB = S // ATT_BLK
    QW, KW = ATT_Q * ATT_HD, ATT_KV * ATT_HD
    GW = ATT_G * ATT_HD

    def body(q_ref, k_ref, v_ref, o_ref, du_ref, g_ref, sink_ref,
             dq_ref, dk_ref, dv_ref, dg_ref, ds_ref):
        kvh = pl.program_id(0)
        i = pl.program_id(1)

        @pl.when(i == 0)
        def _():
            dk_ref[...] = jnp.zeros_like(dk_ref)
            dv_ref[...] = jnp.zeros_like(dv_ref)
            ds_ref[...] = jnp.zeros_like(ds_ref)

        start, valid = _attn_window(i, S)
        kw = k_ref[pl.ds(start, ATT_SPAN), :]
        vw = v_ref[pl.ds(start, ATT_SPAN), :]
        row = lax.broadcasted_iota(jnp.int32, (8, 128), 0)
        lane = lax.broadcasted_iota(jnp.int32, (8, 128), 1)
        dsink = jnp.zeros((8, 128), F32)
        dk_acc = jnp.zeros((ATT_SPAN, ATT_HD), F32)
        dv_acc = jnp.zeros((ATT_SPAN, ATT_HD), F32)
        for gi in range(ATT_G):
            sl = slice(gi * ATT_HD, (gi + 1) * ATT_HD)
            q = q_ref[:, sl]
            ov = o_ref[:, sl]
            g = g_ref[:, sl]
            duv = du_ref[:, sl]
            sg = _sigmoid(g)
            do = duv * (g * sg)
            dg_ref[:, sl] = (duv * ov * (sg * (1.0 + g * (1.0 - sg)))).astype(ACT)
            sk = jnp.full((ATT_BLK, 1), sink_ref[kvh * ATT_G + gi], F32)
            pn, psink = _attn_probs(q, kw, valid, sk)
            delta = jnp.sum(do * ov, axis=-1, keepdims=True)
            dsc = pn * (_dot(do, vw, NT) - delta) * (ATT_HD ** -0.5)
            dq_ref[:, sl] = _dot(dsc, kw, NN)
            dk_acc = dk_acc + _dot(dsc, q, TN)
            dv_acc = dv_acc + _dot(pn, do, TN)
            dsink = dsink + jnp.where((row == 0) & (lane == gi), -_sum_all(psink * delta), 0.0)
        dk_ref[pl.ds(start, ATT_SPAN), :] += dk_acc
        dv_ref[pl.ds(start, ATT_SPAN), :] += dv_acc
        ds_ref[...] += dsink

    grp = pl.BlockSpec((ATT_BLK, GW), lambda k, i: (i, k))
    kv = pl.BlockSpec((S, ATT_HD), lambda k, i: (0, k))
    return pl.pallas_call(
        body, grid=(ATT_KV, nB),
        in_specs=[grp, kv, kv, grp, grp,
                  pl.BlockSpec((ATT_BLK, GW), lambda k, i: (i, ag // ATT_G + k)), SMEM_SPEC],
        out_specs=[grp, kv, kv, grp, pl.BlockSpec((None, 8, 128), lambda k, i: (k, 0, 0))],
        out_shape=[jax.ShapeDtypeStruct((S, QW), F32), jax.ShapeDtypeStruct((S, KW), F32),
                   jax.ShapeDtypeStruct((S, KW), F32), jax.ShapeDtypeStruct((S, QW), ACT),
                   jax.ShapeDtypeStruct((ATT_KV, 8, 128), F32)],
        name="attn_bwd", compiler_params=_params("parallel", "arbitrary"))(qn, kn, vb, o, du, z, sink)


def _attn_prep_bwd(z, dqn, dkn, dv, qgain, kgain, cf, sa, sb, aq, ak):
    S = z.shape[0]
    T = min(512, S)
    QW, KW = ATT_Q * ATT_HD, ATT_KV * ATT_HD

    def body(q_ref, k_ref, dqn_ref, dkn_ref, dv_ref, qg_ref, kg_ref, cf_ref, sa_ref, sb_ref,
             dq_ref, dk_ref, dvb_ref, dqg_ref, dkg_ref):
        cfv, sav, sbv = cf_ref[...], sa_ref[...], sb_ref[...]

        @pl.when(pl.program_id(0) == 0)
        def _():
            dqg_ref[...] = jnp.zeros_like(dqg_ref)
            dkg_ref[...] = jnp.zeros_like(dkg_ref)

        def back(x, gn, gain):
            r = lax.rsqrt(jnp.mean(x * x, axis=-1, keepdims=True) + RMS_EPS)
            xh = x * r
            dy = _rope128_t(gn, cfv, sav, sbv)
            dxh = dy * gain
            dx = r * (dxh - xh * jnp.mean(dxh * xh, axis=-1, keepdims=True))
            return dx, jnp.sum(dy * xh, axis=0, keepdims=True)

        dqg = jnp.zeros((1, ATT_HD), F32)
        for hh in range(ATT_Q):
            sl = slice(hh * ATT_HD, (hh + 1) * ATT_HD)
            dx, dgn = back(q_ref[:, sl], dqn_ref[:, sl], qg_ref[...])
            dq_ref[:, sl] = dx.astype(ACT)
            dqg = dqg + dgn
        dkg = jnp.zeros((1, ATT_HD), F32)
        for hh in range(ATT_KV):
            sl = slice(hh * ATT_HD, (hh + 1) * ATT_HD)
            dx, dgn = back(k_ref[:, sl], dkn_ref[:, sl], kg_ref[...])
            dk_ref[:, sl] = dx.astype(ACT)
            dkg = dkg + dgn
        dvb_ref[...] = dv_ref[...].astype(ACT)
        dqg_ref[...] += dqg
        dkg_ref[...] += dkg

    tab = pl.BlockSpec((T, ATT_HD), lambda i: (i, 0))
    gain = pl.BlockSpec((1, ATT_HD), lambda i: (0, 0))
    qb = pl.BlockSpec((T, QW), lambda i: (i, 0))
    kb = pl.BlockSpec((T, KW), lambda i: (i, 0))
    return pl.pallas_call(
        body, grid=(S // T,),
        in_specs=[pl.BlockSpec((T, QW), lambda i: (i, aq)), pl.BlockSpec((T, KW), lambda i: (i, ak)),
                  qb, kb, kb, gain, gain, tab, tab, tab],
        out_specs=[qb, kb, kb, gain, gain],
        out_shape=[jax.ShapeDtypeStruct((S, QW), ACT), jax.ShapeDtypeStruct((S, KW), ACT),
                   jax.ShapeDtypeStruct((S, KW), ACT), jax.ShapeDtypeStruct((1, ATT_HD), F32),
                   jax.ShapeDtypeStruct((1, ATT_HD), F32)],
        name="attn_prep_bwd", compiler_params=_params("arbitrary"))(
            z, z, dqn, dkn, dv, qgain, kgain, cf, sa, sb)


def _branch_merge(ua, ub, uc, wr, wp, wa, z, mg):
    S, W = ua.shape
    D = wr.shape[1]
    tm, tn = min(512, S), min(512, D)
    nb = D // tn

    def body(ua_ref, ub_ref, uc_ref, wr_ref, wp_ref, wa_ref, g0_ref, g1_ref, g2_ref,
             ya_ref, yb_ref, yc_ref, m_ref, mt_ref):
        ya = _dot(ua_ref[...], wr_ref[...], NN)
        yb = _dot(ub_ref[...], wp_ref[...], NN)
        yc = _dot(uc_ref[...], wa_ref[...], NN)
        ya_ref[...] = ya.astype(ACT)
        yb_ref[...] = yb.astype(ACT)
        yc_ref[...] = yc.astype(ACT)
        m = _sigmoid(g0_ref[...]) * ya + _sigmoid(g1_ref[...]) * yb + _sigmoid(g2_ref[...]) * yc
        m_ref[...] = m.astype(ACT)
        mt_ref[...] = m.T.astype(ACT)

    u = pl.BlockSpec((tm, W), lambda i, j: (i, 0))
    w = pl.BlockSpec((W, tn), lambda i, j: (0, j))
    o = pl.BlockSpec((tm, tn), lambda i, j: (i, j))

    assert (mg * POOL_GD) % tn == 0
    base = (mg * POOL_GD) // tn

    def gate(k):
        return pl.BlockSpec((tm, tn), lambda i, j: (i, base + k * nb + j))

    sd = jax.ShapeDtypeStruct((S, D), ACT)
    return pl.pallas_call(
        body, grid=(S // tm, nb), in_specs=[u, u, u, w, w, w, gate(0), gate(1), gate(2)],
        out_specs=[o, o, o, o, pl.BlockSpec((tn, tm), lambda i, j: (j, i))],
        out_shape=[sd, sd, sd, sd, jax.ShapeDtypeStruct((D, S), ACT)], name="branch_merge",
        compiler_params=_params("parallel", "parallel"))(ua, ub, uc, wr, wp, wa, z, z, z)


def _merge_bwd(dxb, wo, ya, yb, yc, z, mg):
    S, D = dxb.shape
    tm, tn = min(512, S), min(512, D)
    nb = D // tn
    base = (mg * POOL_GD) // tn

    def body(dx_ref, wo_ref, ya_ref, yb_ref, yc_ref, g0_ref, g1_ref, g2_ref,
             da_ref, db_ref, dc_ref, dg0_ref, dg1_ref, dg2_ref):
        dm = _dot(dx_ref[...], wo_ref[...], NT)
        for y_ref, g_ref, dy_ref, dg_ref in ((ya_ref, g0_ref, da_ref, dg0_ref),
                                             (yb_ref, g1_ref, db_ref, dg1_ref),
                                             (yc_ref, g2_ref, dc_ref, dg2_ref)):
            sg = _sigmoid(g_ref[...])
            dy_ref[...] = (sg * dm).astype(ACT)
            dg_ref[...] = (dm * y_ref[...].astype(F32) * (sg * (1.0 - sg))).astype(ACT)

    o = pl.BlockSpec((tm, tn), lambda i, j: (i, j))

    def gate(k):
        return pl.BlockSpec((tm, tn), lambda i, j: (i, base + k * nb + j))

    sd = jax.ShapeDtypeStruct((S, D), ACT)
    return pl.pallas_call(
        body, grid=(S // tm, nb),
        in_specs=[pl.BlockSpec((tm, D), lambda i, j: (i, 0)), pl.BlockSpec((tn, D), lambda i, j: (j, 0)),
                  o, o, o, gate(0), gate(1), gate(2)],
        out_specs=[o] * 6, out_shape=[sd] * 6, name="merge_bwd",
        compiler_params=_params("parallel", "parallel"))(dxb, wo, ya, yb, yc, z, z, z)


def _loss_head(y, t):
    S, D = y.shape
    tm = min(256, S)

    def body(y_ref, t_ref, dy_ref, dyb_ref, l_ref):
        e = y_ref[...] - t_ref[...]
        dy = e * (1.0 / D)
        dy_ref[...] = dy
        dyb_ref[...] = dy.astype(ACT)

        @pl.when(pl.program_id(0) == 0)
        def _():
            l_ref[...] = jnp.zeros_like(l_ref)

        l_ref[...] += jnp.sum(jnp.mean(e * e, axis=-1, keepdims=True), axis=0, keepdims=True)

    row = pl.BlockSpec((tm, D), lambda i: (i, 0))
    return pl.pallas_call(
        body, grid=(S // tm,), in_specs=[row, row],
        out_specs=[row, row, pl.BlockSpec((1, 1), lambda i: (0, 0))],
        out_shape=[jax.ShapeDtypeStruct((S, D), F32), jax.ShapeDtypeStruct((S, D), ACT),
                   jax.ShapeDtypeStruct((1, 1), F32)],
        name="loss_head", compiler_params=_params("arbitrary"))(y, t)


def _adamw(w, layer, parts, m, v, name):
    L = w.shape[0]
    shape = w.shape[1:]
    C = shape[-1]
    R = int(np.prod(shape[:-1]))
    w3, m3, v3 = (a.reshape(L, R, C) for a in (w, m, v))
    p3 = parts.reshape(N_DEV, R, C)
    tr = min(64, R)
    assert R % tr == 0
    c1 = 1.0 / (1.0 - ADAM_B1 ** ADAM_STEP)
    c2 = 1.0 / (1.0 - ADAM_B2 ** ADAM_STEP)

    def body(w_ref, p_ref, m_ref, v_ref, g_ref, d_ref, nm_ref, nv_ref):
        g = p_ref[0].astype(F32)
        for k in range(1, N_DEV):
            g = g + p_ref[k].astype(F32)
        nm = ADAM_B1 * m_ref[...] + (1.0 - ADAM_B1) * g
        nv = ADAM_B2 * v_ref[...] + (1.0 - ADAM_B2) * (g * g)
        g_ref[...] = g
        nm_ref[...] = nm
        nv_ref[...] = nv
        d_ref[...] = -ADAM_LR * ((nm * c1) / (jnp.sqrt(nv * c2) + ADAM_EPS) + ADAM_WD * w_ref[...])

    lay = pl.BlockSpec((None, tr, C), lambda i: (layer, i, 0))
    out = pl.BlockSpec((tr, C), lambda i: (i, 0))
    sd = jax.ShapeDtypeStruct((R, C), F32)
    outs = pl.pallas_call(
        body, grid=(R // tr,),
        in_specs=[lay, pl.BlockSpec((N_DEV, tr, C), lambda i: (0, i, 0)), lay, lay],
        out_specs=[out] * 4, out_shape=[sd] * 4, name=name,
        compiler_params=_params("parallel"))(w3, p3, m3, v3)
    return [a.reshape(shape) for a in outs]


def _exchange(arrs, scatter, name, deps=()):
    n = len(arrs)
    nd = len(deps)
    out_shape = [jax.ShapeDtypeStruct(a.shape if scatter else (N_DEV,) + a.shape, a.dtype) for a in arrs]

    def body(*refs):
        ins, outs = refs[:n], refs[n + nd:2 * n + nd]
        send_sems, recv_sems, local_sems = refs[2 * n + nd:]
        x, y, c = lax.axis_index("x"), lax.axis_index("y"), lax.axis_index("c")
        me = 4 * x + 2 * y + c
        copies = []
        for a in range(n):
            src = ins[a].at[me] if scatter else ins[a]
            own = pltpu.make_async_copy(src, outs[a].at[me], local_sems.at[a])
            own.start()
            copies.append(own)
        sends, recvs = [], []
        for k in range(1, N_DEV):
            px, py, pc = x ^ (k >> 2), y ^ ((k >> 1) & 1), c ^ (k & 1)
            peer = 4 * px + 2 * py + pc
            for a in range(n):
                src = ins[a].at[peer] if scatter else ins[a]
                cp = pltpu.make_async_remote_copy(
                    src_ref=src, dst_ref=outs[a].at[me],
                    send_sem=send_sems.at[a, k - 1], recv_sem=recv_sems.at[a, k - 1],
                    device_id=(px, py, pc), device_id_type=pl.DeviceIdType.MESH)
                cp.start()
                sends.append(cp)
                recvs.append(pltpu.make_async_remote_copy(
                    src_ref=src, dst_ref=outs[a].at[peer],
                    send_sem=send_sems.at[a, k - 1], recv_sem=recv_sems.at[a, k - 1],
                    device_id=(px, py, pc), device_id_type=pl.DeviceIdType.MESH))
        for cp in recvs:
            cp.wait_recv()
        for cp in sends:
            cp.wait_send()
        for cp in copies:
            cp.wait()

    any_spec = pl.BlockSpec(memory_space=pl.ANY)
    return pl.pallas_call(
        body, in_specs=[any_spec] * (n + nd), out_specs=[any_spec] * n, out_shape=out_shape,
        scratch_shapes=[pltpu.SemaphoreType.DMA((n, N_DEV - 1)), pltpu.SemaphoreType.DMA((n, N_DEV - 1)),
                        pltpu.SemaphoreType.DMA((n,))],
        name=name)(*arrs, *deps)


HBM_SPEC = pl.BlockSpec(memory_space=pltpu.HBM)
SEM_SPEC = pl.BlockSpec(memory_space=pltpu.SEMAPHORE)
DATAFLOW = pltpu.SideEffectType.DATAFLOW_SIDE_EFFECTING


def _peer_of(k):
    x, y, c = lax.axis_index("x"), lax.axis_index("y"), lax.axis_index("c")
    return x ^ (k >> 2), y ^ ((k >> 1) & 1), c ^ (k & 1)


def _exchange_copy(k, a, src_ref, land_ref, send_sems, recv_sems, scatter, outgoing):
    px, py, pc = _peer_of(k)
    peer = 4 * px + 2 * py + pc
    me = 4 * lax.axis_index("x") + 2 * lax.axis_index("y") + lax.axis_index("c")
    idx = a * (N_DEV - 1) + k - 1
    return pltpu.make_async_remote_copy(
        src_ref=src_ref.at[peer] if scatter else src_ref, dst_ref=land_ref.at[me if outgoing else peer],
        send_sem=send_sems.at[idx], recv_sem=recv_sems.at[idx],
        device_id=(px, py, pc), device_id_type=pl.DeviceIdType.MESH)


def _exchange_start(arrs, scatter, name):
    n = len(arrs)
    land_shapes = [a.shape if scatter else (N_DEV,) + a.shape for a in arrs]

    def body(*refs):
        srcs, lands = refs[:n], refs[n:2 * n]
        send_sems, recv_sems = refs[2 * n], refs[2 * n + 1]
        token = refs[-1]
        for k in range(1, N_DEV):
            for a in range(n):
                _exchange_copy(k, a, srcs[a], lands[a], send_sems, recv_sems, scatter, True).start()
        token[...] = jnp.zeros_like(token)

    sems = pltpu.SemaphoreType.DMA((n * (N_DEV - 1),))
    out_shape = ([sems, sems] + [pltpu.HBM(a.shape, a.dtype) for a in arrs]
                 + [pltpu.HBM(s, a.dtype) for s, a in zip(land_shapes, arrs)]
                 + [jax.ShapeDtypeStruct((8, 128), F32)])
    ins = ([pltpu.with_memory_space_constraint(a, pltpu.HBM) for a in arrs]
           + [pltpu.with_memory_space_constraint(lax.empty(s, a.dtype), pltpu.HBM) for s, a in zip(land_shapes, arrs)])
    res = pl.pallas_call(
        body, name=name, out_shape=out_shape, in_specs=[HBM_SPEC] * (2 * n),
        out_specs=[SEM_SPEC, SEM_SPEC] + [HBM_SPEC] * (2 * n) + [pl.BlockSpec(memory_space=pltpu.VMEM)],
        input_output_aliases={i: 2 + i for i in range(2 * n)},
        compiler_params=pltpu.CompilerParams(has_side_effects=DATAFLOW))(*ins)
    return res[0], res[1], list(res[2:2 + n]), list(res[2 + n:2 + 2 * n]), res[-1]


def _exchange_wait(started, after, scatter, name):
    send_sems, recv_sems, srcs, lands, _ = started
    n = len(srcs)

    def body(*refs):
        src_refs, land_refs = refs[:n], refs[n:2 * n]
        s_sems, r_sems = refs[2 * n], refs[2 * n + 1]
        for k in range(1, N_DEV):
            for a in range(n):
                back = _exchange_copy(k, a, src_refs[a], land_refs[a], s_sems, r_sems, scatter, False)
                back.wait_send()
                back.wait_recv()

    out_shape = [pltpu.HBM(a.shape, a.dtype) for a in srcs] + [pltpu.HBM(a.shape, a.dtype) for a in lands]
    res = pl.pallas_call(
        body, name=name, out_shape=out_shape,
        in_specs=[HBM_SPEC] * (2 * n) + [SEM_SPEC, SEM_SPEC, pl.BlockSpec(memory_space=pl.ANY)],
        out_specs=[HBM_SPEC] * (2 * n), input_output_aliases={i: i for i in range(2 * n)},
        compiler_params=pltpu.CompilerParams(has_side_effects=DATAFLOW))(*srcs, *lands, send_sems, recv_sems, after)
    return list(res[:n]), list(res[n:])


def _own_slot(land, own, me):
    return lax.dynamic_update_index_in_dim(land, own, me, 0)


class _Cols:
    def __init__(self, D):
        Wb = D // 2
        sizes = (Wb, Wb, Wb, Wb, Wb, Wb, Wb, ATT_KV * ATT_HD, ATT_KV * ATT_HD, Wb, 3 * D)
        offs = np.concatenate([[0], np.cumsum(sizes)])
        assert all(int(o) % 256 == 0 for o in offs)
        (self.rq, self.rk, self.rv, self.rg, self.pv, self.pg,
         self.aq, self.ak, self.av, self.ag, self.mg) = (int(o) // 256 for o in offs[:-1])
        self.width = int(offs[-1])
        self.sizes = sizes


def _rope_tables(S):
    pos = jnp.arange(S, dtype=F32)[:, None]
    inv_r = 1.0 / (RET_ROPE_BASE ** jnp.linspace(0.0, 1.0, RET_HD // 2, dtype=F32))
    ang_r = pos * inv_r[None, :]
    inv_a = ROPE_THETA ** (-jnp.arange(ROPE_HALF, dtype=F32) / ROPE_HALF)
    ang_a = pos * inv_a[None, :]
    ca, sa = jnp.cos(ang_a), jnp.sin(ang_a)
    z16 = jnp.zeros((S, ROPE_HALF), F32)
    rest = ATT_HD - 2 * ROPE_HALF
    cf = jnp.concatenate([ca, ca, jnp.ones((S, rest), F32)], axis=1)
    s_up = jnp.concatenate([z16, sa, jnp.zeros((S, rest), F32)], axis=1)
    s_dn = jnp.concatenate([-sa, z16, jnp.zeros((S, rest), F32)], axis=1)
    return jnp.cos(ang_r), jnp.sin(ang_r), cf, s_up, s_dn


def _layer_fwd(x, p, tabs, cols, deps=()):
    cos_r, sin_r, cf, s_up, s_dn = tabs
    S, D = x.shape
    h, ht = _rmsnorm_fwd(x, p["norm_g"], deps)
    z = _matmul(h, p["w_in_t"], "nt", F32, 2048, 512, D, "in_proj")
    if "rest" in p:
        p = {**{k: v for k, v in p.items() if k != "rest"}, **p["rest"](z)}
    rcols = (cols.rq, cols.rk, cols.rv, cols.rg)
    sf, sb = _ret_state(z, cols.rk, RET_HD ** -0.5, z, cols.rv, cos_r, sin_r, p["af"], p["ab"], "fwd",
                        "ret_state_fwd")
    o_ret, ua, uat = _ret_fwd(z, cos_r, sin_r, sf, sb, p["af"], p["ab"], rcols)
    y_pool, ub, ubt = _pool_fwd(z, p["pool_w"], p["pool_scale"], cols.pv, cols.pg)
    qn, kn, vb = _attn_prep(z, p["q_gain"], p["k_gain"], cf, s_up, s_dn,
                            cols.aq // 4, cols.ak, cols.av)
    o_att, uc, uct = _attn_fwd(qn, kn, vb, z, p["sink"], cols.ag * 2)
    ya, yb, yc, merged, mergedt = _branch_merge(ua, ub, uc, p["w_ret"], p["w_pool"], p["w_att"], z, cols.mg)
    out = _matmul(merged, p["w_out"], "nn", F32, 1024, 512, D, "out_proj", res=x)
    saved = dict(x=x, ht=ht, z=z, sf=sf, sb=sb, o_ret=o_ret, uat=uat, y_pool=y_pool, ubt=ubt,
                 qn=qn, kn=kn, vb=vb, o_att=o_att, uct=uct, ya=ya, yb=yb, yc=yc, mergedt=mergedt)
    return out, saved, p


def _layer_bwd(dx, dxb, p, sv, tabs, cols, on_grads):
    cos_r, sin_r, cf, s_up, s_dn = tabs
    z = sv["z"]
    S, D = dx.shape
    dya, dyb, dyc, dmg0, dmg1, dmg2 = _merge_bwd(dxb, p["w_out"], sv["ya"], sv["yb"], sv["yc"], z, cols.mg)
    grads = {"w_out": _matmul(sv["mergedt"], dxb, "nn", ACT, 1024, 512, S, "dw_out")}
    dus = {}
    for nm, ut, dy in (("ret", sv["uat"], dya), ("pool", sv["ubt"], dyb), ("att", sv["uct"], dyc)):
        dus[nm] = _matmul(dy, p["w_" + nm], "nt", F32, 1024, 512, D, "du_" + nm)
        grads["w_" + nm] = _matmul(ut, dy, "nn", ACT, 1024, 512, S, "dw_" + nm)
    rcols = (cols.rq, cols.rk, cols.rv, cols.rg)
    do_ret, d_rg = _ret_gate_bwd(dus["ret"], sv["o_ret"], z, cols.rg)
    eb, ef = _ret_state(z, cols.rq, 1.0, do_ret, 0, cos_r, sin_r, p["af"], p["ab"], "bwd", "ret_state_bwd")
    d_rq, d_rk, d_rv, d_decay = _ret_bwd(z, do_ret, cos_r, sin_r, sv["sf"], sv["sb"], ef, eb,
                                         p["af"], p["ab"], rcols)
    dpc, d_pg, d_pscale, g_pool_w = _pool_bwd_a(z, p["pool_w"], p["pool_scale"], sv["y_pool"], dus["pool"],
                                                cols.pv, cols.pg)
    d_pv = _pool_bwd_b(dpc)
    grads["pool_w"] = g_pool_w.astype(ACT)
    dqn, dkn, dvv, d_ag, d_sink = _attn_bwd(sv["qn"], sv["kn"], sv["vb"], sv["o_att"], dus["att"], z,
                                            p["sink"], cols.ag * 2)
    d_aq, d_ak, d_av, d_qg, d_kg = _attn_prep_bwd(z, dqn, dkn, dvv, p["q_gain"], p["k_gain"], cf, s_up, s_dn,
                                                  cols.aq // 4, cols.ak)
    dz = jnp.concatenate([d_rq, d_rk, d_rv, d_rg, d_pv, d_pg, d_aq, d_ak, d_av, d_ag, dmg0, dmg1, dmg2],
                         axis=1)
    grads["w_in_t"] = _matmul(sv["ht"], dz, "nn", ACT, 1024, 512, S, "dw_in", transpose_out=True)
    tok = on_grads(grads)
    dh = _matmul(dz, p["w_in_t"], "nn", F32, 1024, 512, 29 * 128, "dh", deps=() if tok is None else (tok,))
    dx_in, dxb_in, d_norm_g = _rmsnorm_bwd(sv["x"], p["norm_g"], dh, dx)
    misc = jnp.concatenate([d_decay[:, 0, 0], d_decay[:, 0, 1], d_sink[:, 0, :ATT_G].reshape(-1)])
    misc = jnp.pad(misc, (0, 128 - misc.shape[0]))[None, :]
    small = jnp.concatenate([d_norm_g, d_pscale, d_qg, d_kg, misc], axis=1)
    return dx_in, dxb_in, small


def _pack_small(norm_g, pool_scale, q_gain, k_gain, af, ab, sink):
    L = norm_g.shape[0]
    misc = jnp.concatenate([af, ab, sink], axis=1)
    misc = jnp.pad(misc, ((0, 0), (0, 128 - misc.shape[1])))
    return jnp.concatenate([norm_g, pool_scale, q_gain, k_gain, misc], axis=1)


def _unpack_small(a, D):
    Wb = D // 2
    o = np.cumsum([0, D, Wb, ATT_HD, ATT_HD])
    misc = a[:, o[4]:]
    return (a[:, o[0]:o[1]], a[:, o[1]:o[2]], a[:, o[2]:o[3]], a[:, o[3]:o[4]],
            misc[:, :RET_HEADS], misc[:, RET_HEADS:2 * RET_HEADS],
            misc[:, 2 * RET_HEADS:2 * RET_HEADS + ATT_Q])


def _local_step(x, t, n_layers, get_layer, on_grads, tabs, cols, first_dep=None):
    saved, layers = [], []
    after = first_dep
    for l in range(n_layers):
        p = get_layer(l, after)
        x, sv, p = _layer_fwd(x, p, tabs, cols, (first_dep,) if (l == 0 and first_dep is not None) else ())
        after = x
        layers.append(p)
        saved.append(sv)
    dx, dxb, lsum = _loss_head(x, t)
    smalls = []
    for l in reversed(range(n_layers)):
        dx, dxb, sm = _layer_bwd(dx, dxb, layers[l], saved[l], tabs, cols, functools.partial(on_grads, l))
        smalls.append(sm)
    return 0.5 * lsum[0, 0], dx, jnp.concatenate(smalls[::-1], axis=0)


WEIGHT_KEYS = ("w_in", "w_ret", "w_pool", "w_att", "w_out", "pool_w")


def kernel(x, norm_g, w_in, ret_decay_fwd, ret_decay_bwd, pool_w, pool_scale, attn_q_gain, attn_k_gain, attn_sink, w_ret, w_pool, w_att, w_out, loss_target, m_norm_g, m_w_in, m_ret_decay_fwd, m_ret_decay_bwd, m_pool_w, m_pool_scale, m_attn_q_gain, m_attn_k_gain, m_attn_sink, m_w_ret, m_w_pool, m_w_att, m_w_out, v_norm_g, v_w_in, v_ret_decay_fwd, v_ret_decay_bwd, v_pool_w, v_pool_scale, v_attn_q_gain, v_attn_k_gain, v_attn_sink, v_w_ret, v_w_pool, v_w_att, v_w_out):
    L = norm_g.shape[0]
    _, S, D = x.shape
    Wb = D // 2
    G = len(POOL_WINDOWS)
    cols = _Cols(D)
    tabs = _rope_tables(S)
    me = 4 * lax.axis_index("x") + 2 * lax.axis_index("y") + lax.axis_index("c")
    def tr(a):
        return jnp.transpose(a, (0, 2, 1))

    weights = dict(w_in=tr(w_in), w_ret=w_ret, w_pool=w_pool, w_att=w_att, w_out=w_out, pool_w=pool_w)

    gathers, tok = [], None
    for l in range(L):
        started = []
        for part, keys in (("a", WEIGHT_KEYS[:1]), ("b", WEIGHT_KEYS[1:])):
            shards = []
            for k in keys:
                w = weights[k][l]
                if tok is not None:
                    w = w + tok[0, 0]
                shards.append(w.astype(MXU))
            st = _exchange_start(shards, False, f"gather_start_{l}{part}")
            started.append(st)
            tok = st[-1]
        gathers.append(started)

    def cols_full(g, rows):
        return jnp.transpose(g, (1, 0, 2)).reshape(rows, -1)

    def get_layer(l, after):
        srcs, lands = _exchange_wait(gathers[l][0], after, False, f"gather_wait_{l}a")
        g_in = _own_slot(lands[0], srcs[0], me)

        def rest(z):
            srcs, lands = _exchange_wait(gathers[l][1], z, False, f"gather_wait_{l}b")
            g_ret, g_pool, g_att, g_out, g_pw = [_own_slot(ld, sr, me) for ld, sr in zip(lands, srcs)]
            return dict(
                w_ret=cols_full(g_ret, Wb), w_pool=cols_full(g_pool, Wb), w_att=cols_full(g_att, Wb),
                w_out=g_out.reshape(D, D),
                pool_w=jnp.transpose(g_pw, (1, 0, 2, 3)).reshape(G, POOL_GD, POOL_GD))

        return dict(
            norm_g=norm_g[l][None, :], w_in_t=g_in.reshape(-1, D), rest=rest,
            pool_scale=pool_scale[l][None, :], q_gain=attn_q_gain[l][None, :], k_gain=attn_k_gain[l][None, :],
            af=ret_decay_fwd[l], ab=ret_decay_bwd[l], sink=attn_sink[l])

    def col_slots(g, rows):
        return jnp.transpose(g.reshape(rows, N_DEV, -1), (1, 0, 2))

    scatters = {}

    def on_grads(l, g):
        slots = [g["w_in_t"].reshape(N_DEV, -1, D),
                 col_slots(g["w_ret"], Wb), col_slots(g["w_pool"], Wb),
                 col_slots(g["w_att"], Wb), g["w_out"].reshape(N_DEV, D // N_DEV, D),
                 jnp.transpose(g["pool_w"].reshape(G, N_DEV, POOL_GD // N_DEV, POOL_GD), (1, 0, 2, 3))]
        scatters[l] = _exchange_start(slots, True, f"scatter_start_{l}")
        return scatters[l][-1]

    loss_local, grad_x, small = _local_step(x[0], loss_target[0], L, get_layer, on_grads, tabs, cols, tok)
    loss = lax.psum(loss_local, ("x", "y", "c"))

    moments = dict(w_in=(tr(m_w_in), tr(v_w_in)), w_ret=(m_w_ret, v_w_ret), w_pool=(m_w_pool, v_w_pool),
                   w_att=(m_w_att, v_w_att), w_out=(m_w_out, v_w_out), pool_w=(m_pool_w, v_pool_w))
    res = {k: [jnp.zeros(weights[k].shape, F32) for _ in range(4)] for k in WEIGHT_KEYS}
    after = grad_x
    small_all = None
    for l in reversed(range(L)):
        if l == 0:
            small_all, = _exchange([small], False, "gather_small_grads", deps=(after,))
            after = small_all
        srcs, lands = _exchange_wait(scatters[l], after, True, f"scatter_wait_{l}")
        firsts = []
        for k, ld, sr in zip(WEIGHT_KEYS, lands, srcs):
            parts = _own_slot(ld, lax.dynamic_index_in_dim(sr, me, 0, keepdims=False), me)
            m, v = moments[k]
            outs = _adamw(weights[k], l, parts, m, v, "adamw_" + k)
            res[k] = [lax.dynamic_update_index_in_dim(r, o, l, 0) for r, o in zip(res[k], outs)]
            firsts.append(outs[1].reshape(-1)[:1])
        after = jnp.concatenate(firsts)

    sw = _pack_small(norm_g, pool_scale, attn_q_gain, attn_k_gain, ret_decay_fwd, ret_decay_bwd, attn_sink)
    sm = _pack_small(m_norm_g, m_pool_scale, m_attn_q_gain, m_attn_k_gain, m_ret_decay_fwd, m_ret_decay_bwd,
                     m_attn_sink)
    sv_ = _pack_small(v_norm_g, v_pool_scale, v_attn_q_gain, v_attn_k_gain, v_ret_decay_fwd, v_ret_decay_bwd,
                      v_attn_sink)
    small_res = [_unpack_small(a, D) for a in _adamw(sw[None], 0, small_all, sm[None], sv_[None], "adamw_small")]

    def ordered(i):
        ng, ps, qg, kg, af, ab, sk = small_res[i]
        return (ng, tr(res["w_in"][i]), af, ab, res["pool_w"][i], ps, qg, kg, sk,
                res["w_ret"][i], res["w_pool"][i], res["w_att"][i], res["w_out"][i])

    return (loss, grad_x[None], *ordered(0), *ordered(1), *ordered(2), *ordered(3))
```

```python
import functools

import numpy as np
import jax
import jax.numpy as jnp
from jax import lax
from jax.experimental import pallas as pl
from jax.experimental.pallas import tpu as pltpu

F32 = jnp.float32
MXU = jnp.bfloat16
ACT = jnp.bfloat16

N_DEV = 8
RMS_EPS = 1e-6
NEG_BIG = -1e30
RET_HEADS = 4
RET_HD = 256
CH = 128
RET_ROPE_BASE = 10000.0
POOL_WINDOWS = (2, 4, 8, 16)
POOL_GD = 256
POOL_PAD = 8
ATT_HD = 128
ATT_Q = 8
ATT_KV = 2
ATT_G = ATT_Q // ATT_KV
ATT_WIN = 128
ATT_BLK = 128
ATT_SPAN = 3 * ATT_BLK
ROPE_THETA = 500000.0
ROPE_HALF = 16

ADAM_LR = 0.001
ADAM_B1 = 0.9
ADAM_B2 = 0.999
ADAM_EPS = 1e-08
ADAM_WD = 0.01
ADAM_STEP = 10

VMEM_LIMIT = 48 * 1024 * 1024

NN = ((1,), (0,))
NT = ((1,), (1,))
TN = ((0,), (0,))


def _dot(a, b, dims):
    return lax.dot_general(a.astype(MXU), b.astype(MXU), (dims, ((), ())),
                           preferred_element_type=F32)


def _sigmoid(x):
    return 1.0 / (1.0 + jnp.exp(-x))


def _params(*sem):
    return pltpu.CompilerParams(dimension_semantics=sem, vmem_limit_bytes=VMEM_LIMIT)


def _sum_all(x):
    return jnp.sum(jnp.sum(x, axis=1, keepdims=True), axis=0, keepdims=True)


def _fiota(shape, dim):
    return lax.broadcasted_iota(jnp.int32, shape, dim).astype(F32)


SMEM_SPEC = pl.BlockSpec(memory_space=pltpu.SMEM)


def _matmul(a, b, mode, out_dtype, tm, tn, tk, name, res=None, deps=(), transpose_out=False):
    if mode == "tn":
        K, M = a.shape
    else:
        M, K = a.shape
    N = b.shape[0] if mode == "nt" else b.shape[1]
    tm, tn, tk = min(tm, M), min(tn, N), min(tk, K)
    assert M % tm == 0 and N % tn == 0 and K % tk == 0, (name, M, N, K, tm, tn, tk)
    nk = K // tk
    dims = {"nn": NN, "nt": NT, "tn": TN}[mode]
    a_spec = (pl.BlockSpec((tk, tm), lambda i, j, k: (k, i)) if mode == "tn"
              else pl.BlockSpec((tm, tk), lambda i, j, k: (i, k)))
    b_spec = (pl.BlockSpec((tn, tk), lambda i, j, k: (j, k)) if mode == "nt"
              else pl.BlockSpec((tk, tn), lambda i, j, k: (k, j)))
    o_spec = pl.BlockSpec((tm, tn), lambda i, j, k: (i, j))
    has_res = res is not None
    assert not (has_res and transpose_out)
    n_in = 2 + has_res + len(deps)

    def body(*refs):
        a_ref, b_ref = refs[:2]
        r_ref = refs[2] if has_res else None
        o_ref = refs[n_in]

        def finish(out):
            if has_res:
                out = out + r_ref[...]
            o_ref[...] = (out.T if transpose_out else out).astype(out_dtype)

        if nk == 1:
            finish(_dot(a_ref[...], b_ref[...], dims))
            return
        acc = refs[n_in + 1]
        k = pl.program_id(2)

        @pl.when(k == 0)
        def _():
            acc[...] = jnp.zeros_like(acc)

        acc[...] += _dot(a_ref[...], b_ref[...], dims)

        @pl.when(k == nk - 1)
        def _():
            finish(acc[...])

    ins = [a, b] + ([res] if has_res else []) + list(deps)
    in_specs = ([a_spec, b_spec] + ([o_spec] if has_res else [])
                + [pl.BlockSpec((8, 128), lambda i, j, k: (0, 0))] * len(deps))
    return pl.pallas_call(
        body, grid=(M // tm, N // tn, nk), in_specs=in_specs,
        out_specs=pl.BlockSpec((tn, tm), lambda i, j, k: (j, i)) if transpose_out else o_spec,
        out_shape=jax.ShapeDtypeStruct((N, M) if transpose_out else (M, N), out_dtype),
        scratch_shapes=[pltpu.VMEM((tm, tn), F32)] if nk > 1 else [], name=name,
        compiler_params=_params("parallel", "parallel", "arbitrary"))(*ins)


DEP_SPEC1 = pl.BlockSpec((8, 128), lambda i: (0, 0))
DEP_SPEC2 = pl.BlockSpec((8, 128), lambda i, j: (0, 0))


def _rmsnorm_fwd(x, g, deps=()):
    S, D = x.shape
    tm = min(512, S)
    assert S % tm == 0

    def body(x_ref, g_ref, *rest):
        h_ref, ht_ref = rest[-2:]
        xv = x_ref[...]
        r = lax.rsqrt(jnp.mean(xv * xv, axis=-1, keepdims=True) + RMS_EPS)
        hv = xv * r * g_ref[...]
        h_ref[...] = hv.astype(ACT)
        ht_ref[...] = hv.T.astype(ACT)

    row = pl.BlockSpec((tm, D), lambda i: (i, 0))
    return pl.pallas_call(
        body, grid=(S // tm,), in_specs=[row, pl.BlockSpec((1, D), lambda i: (0, 0))] + [DEP_SPEC1] * len(deps),
        out_specs=[row, pl.BlockSpec((D, tm), lambda i: (0, i))],
        out_shape=[jax.ShapeDtypeStruct((S, D), ACT), jax.ShapeDtypeStruct((D, S), ACT)], name="rmsnorm_fwd",
        compiler_params=_params("parallel"))(x, g, *deps)


def _rmsnorm_bwd(x, g, dh, dres):
    S, D = x.shape
    tm = min(256, S)

    def body(x_ref, g_ref, dh_ref, dr_ref, dx_ref, dxb_ref, dg_ref):
        xv = x_ref[...]
        r = lax.rsqrt(jnp.mean(xv * xv, axis=-1, keepdims=True) + RMS_EPS)
        xh = xv * r
        dhv = dh_ref[...]
        dxh = dhv * g_ref[...]
        dx = r * (dxh - xh * jnp.mean(dxh * xh, axis=-1, keepdims=True)) + dr_ref[...]
        dx_ref[...] = dx
        dxb_ref[...] = dx.astype(ACT)

        @pl.when(pl.program_id(0) == 0)
        def _():
            dg_ref[...] = jnp.zeros_like(dg_ref)

        dg_ref[...] += jnp.sum(dhv * xh, axis=0, keepdims=True)

    row = pl.BlockSpec((tm, D), lambda i: (i, 0))
    vec = pl.BlockSpec((1, D), lambda i: (0, 0))
    return pl.pallas_call(
        body, grid=(S // tm,), in_specs=[row, vec, row, row], out_specs=[row, row, vec],
        out_shape=[jax.ShapeDtypeStruct((S, D), F32), jax.ShapeDtypeStruct((S, D), ACT),
                   jax.ShapeDtypeStruct((1, D), F32)],
        name="rmsnorm_bwd", compiler_params=_params("arbitrary"))(x, g, dh, dres)


def _rot256(x, c, s):
    x1, x2 = x[:, :128], x[:, 128:]
    return jnp.concatenate([x1 * c - x2 * s, x2 * c + x1 * s], axis=1)


def _rot256_t(g, c, s):
    g1, g2 = g[:, :128], g[:, 128:]
    return jnp.concatenate([g1 * c + g2 * s, g2 * c - g1 * s], axis=1)


def _log_decay(a_ref, h, shape):
    return -jnp.exp(jnp.full(shape, a_ref[h], F32))


def _ret_state(xsrc, xbase, xscale, ysrc, ybase, cos, sin, af, ab, mode, name):
    S = xsrc.shape[0]
    nC = S // CH
    H = RET_HEADS
    W = H * RET_HD
    assert (xbase * RET_HD) % W == 0 and (ybase * RET_HD) % W == 0

    def body(x1_ref, y1_ref, c1_ref, s1_ref, x2_ref, y2_ref, c2_ref, s2_ref, af_ref, ab_ref,
             st1_ref, st2_ref, acc1, acc2):
        @pl.when(pl.program_id(0) == 0)
        def _():
            acc1[...] = jnp.zeros_like(acc1)
            acc2[...] = jnp.zeros_like(acc2)

        j = _fiota((CH, 1), 0)
        ca, sa, cb, sb_ = c1_ref[...], s1_ref[...], c2_ref[...], s2_ref[...]
        for h in range(H):
            sl = slice(h * RET_HD, (h + 1) * RET_HD)
            lgf = _log_decay(af_ref, h, (CH, 1))
            lgb = _log_decay(ab_ref, h, (CH, 1))
            if mode == "fwd":
                w1, d1 = jnp.exp(lgf * (CH - 1.0 - j)), jnp.exp(lgf[:1] * CH)
                w2, d2 = jnp.exp(lgb * j), jnp.exp(lgb[:1] * CH)
            else:
                w1, d1 = jnp.exp(lgb * (CH - j)), jnp.exp(lgb[:1] * CH)
                w2, d2 = jnp.exp(lgf * (j + 1.0)), jnp.exp(lgf[:1] * CH)
            xa = _rot256(x1_ref[:, sl], ca, sa) * xscale
            st1_ref[h] = acc1[h]
            acc1[h] = d1 * acc1[h] + _dot(xa * w1, y1_ref[:, sl], TN)
            xb = _rot256(x2_ref[:, sl], cb, sb_) * xscale
            st2_ref[h] = acc2[h]
            acc2[h] = d2 * acc2[h] + _dot(xb * w2, y2_ref[:, sl], TN)

    xcol, ycol = (xbase * RET_HD) // W, (ybase * RET_HD) // W
    in_specs = [
        pl.BlockSpec((CH, W), lambda c: (c, xcol)), pl.BlockSpec((CH, W), lambda c: (c, ycol)),
        pl.BlockSpec((CH, 128), lambda c: (c, 0)), pl.BlockSpec((CH, 128), lambda c: (c, 0)),
        pl.BlockSpec((CH, W), lambda c: (nC - 1 - c, xcol)), pl.BlockSpec((CH, W), lambda c: (nC - 1 - c, ycol)),
        pl.BlockSpec((CH, 128), lambda c: (nC - 1 - c, 0)),
        pl.BlockSpec((CH, 128), lambda c: (nC - 1 - c, 0)),
        SMEM_SPEC, SMEM_SPEC]
    out_specs = [pl.BlockSpec((H, None, RET_HD, RET_HD), lambda c: (0, c, 0, 0)),
                 pl.BlockSpec((H, None, RET_HD, RET_HD), lambda c: (0, nC - 1 - c, 0, 0))]
    st = jax.ShapeDtypeStruct((H, nC, RET_HD, RET_HD), F32)
    return pl.pallas_call(
        body, grid=(nC,), in_specs=in_specs, out_specs=out_specs, out_shape=[st, st],
        scratch_shapes=[pltpu.VMEM((H, RET_HD, RET_HD), F32), pltpu.VMEM((H, RET_HD, RET_HD), F32)],
        name=name, compiler_params=_params("arbitrary"))(
            xsrc, ysrc, cos, sin, xsrc, ysrc, cos, sin, af, ab)


def _decay_mask(lgf1, lgb1):
    lag = _fiota((CH, CH), 0) - _fiota((CH, CH), 1)
    alag = jnp.abs(lag)
    return lag, jnp.where(lag >= 0, jnp.exp(lgf1 * alag), jnp.exp(lgb1 * alag))


def _ret_fwd(z, cos, sin, sf, sb, af, ab, cols):
    S = z.shape[0]
    nC = S // CH
    H = RET_HEADS
    rq, rk, rv, rg = cols
    W = H * RET_HD
    assert all((c * RET_HD) % W == 0 for c in cols)

    def body(q_ref, k_ref, v_ref, g_ref, c_ref, s_ref, sf_ref, sb_ref, af_ref, ab_ref,
             o_ref, u_ref, ut_ref):
        j = _fiota((CH, 1), 0)
        c, s = c_ref[...], s_ref[...]
        for h in range(H):
            sl = slice(h * RET_HD, (h + 1) * RET_HD)
            lgf = _log_decay(af_ref, h, (CH, 1))
            lgb = _log_decay(ab_ref, h, (CH, 1))
            q = _rot256(q_ref[:, sl], c, s)
            k = _rot256(k_ref[:, sl], c, s) * (RET_HD ** -0.5)
            _, dm = _decay_mask(lgf[:1], lgb[:1])
            p = _dot(q, k, NT) * dm
            o = (_dot(p, v_ref[:, sl], NN)
                 + _dot(q * jnp.exp(lgf * (j + 1.0)), sf_ref[h], NN)
                 + _dot(q * jnp.exp(lgb * (CH - j)), sb_ref[h], NN))
            o_ref[:, sl] = o
            on = o * lax.rsqrt(jnp.mean(o * o, axis=-1, keepdims=True) + RMS_EPS)
            g = g_ref[:, sl]
            u = on * (g * _sigmoid(g))
            u_ref[:, sl] = u.astype(ACT)
            ut_ref[sl, :] = u.T.astype(ACT)

    def zc(col):
        return pl.BlockSpec((CH, W), lambda c: (c, (col * RET_HD) // W))

    tab = pl.BlockSpec((CH, 128), lambda c: (c, 0))
    stt = pl.BlockSpec((H, None, RET_HD, RET_HD), lambda c: (0, c, 0, 0))
    out = pl.BlockSpec((CH, W), lambda c: (c, 0))
    return pl.pallas_call(
        body, grid=(nC,),
        in_specs=[zc(rq), zc(rk), zc(rv), zc(rg), tab, tab, stt, stt, SMEM_SPEC, SMEM_SPEC],
        out_specs=[out, out, pl.BlockSpec((W, CH), lambda c: (0, c))],
        out_shape=[jax.ShapeDtypeStruct((S, W), F32), jax.ShapeDtypeStruct((S, W), ACT),
                   jax.ShapeDtypeStruct((W, S), ACT)],
        name="ret_fwd", compiler_params=_params("parallel"))(
            z, z, z, z, cos, sin, sf, sb, af, ab)


def _ret_gate_bwd(du, o_pre, z, rg):
    S, W = du.shape
    H = RET_HEADS
    tm = min(512, S)
    assert S % tm == 0

    def body(du_ref, o_ref, g_ref, do_ref, dg_ref):
        o = o_ref[...]
        r = lax.rsqrt(jnp.mean(o * o, axis=-1, keepdims=True) + RMS_EPS)
        on = o * r
        g = g_ref[...]
        sg = _sigmoid(g)
        duv = du_ref[...]
        don = duv * (g * sg)
        dg_ref[...] = (duv * on * (sg * (1.0 + g * (1.0 - sg)))).astype(ACT)
        do_ref[...] = r * (don - on * jnp.mean(don * on, axis=-1, keepdims=True))

    blk = pl.BlockSpec((tm, RET_HD), lambda i, h: (i, h))
    return pl.pallas_call(
        body, grid=(S // tm, H),
        in_specs=[blk, blk, pl.BlockSpec((tm, RET_HD), lambda i, h: (i, rg + h))],
        out_specs=[blk, blk],
        out_shape=[jax.ShapeDtypeStruct((S, W), F32), jax.ShapeDtypeStruct((S, W), ACT)],
        name="ret_gate_bwd", compiler_params=_params("parallel", "parallel"))(du, o_pre, z)


def _ret_bwd(z, do, cos, sin, sf, sb, ef, eb, af, ab, cols):
    S = z.shape[0]
    nC = S // CH
    H = RET_HEADS
    rq, rk, rv, _ = cols
    W = H * RET_HD

    def body(q_ref, k_ref, v_ref, do_ref, c_ref, s_ref, sf_ref, sb_ref, ef_ref, eb_ref,
             af_ref, ab_ref, dq_ref, dk_ref, dv_ref, da_ref):
        @pl.when(pl.program_id(0) == 0)
        def _():
            da_ref[...] = jnp.zeros_like(da_ref)

        j = _fiota((CH, 1), 0)
        c, s = c_ref[...], s_ref[...]
        scale = RET_HD ** -0.5
        row = lax.broadcasted_iota(jnp.int32, (8, 128), 0)
        lane = lax.broadcasted_iota(jnp.int32, (8, 128), 1)
        for h in range(H):
            sl = slice(h * RET_HD, (h + 1) * RET_HD)
            lgf = _log_decay(af_ref, h, (CH, 1))
            lgb = _log_decay(ab_ref, h, (CH, 1))
            q = _rot256(q_ref[:, sl], c, s)
            k = _rot256(k_ref[:, sl], c, s) * scale
            v = v_ref[:, sl]
            do = do_ref[:, sl]
            sf_, sb_, ef_, eb_ = sf_ref[h], sb_ref[h], ef_ref[h], eb_ref[h]
            a_w = jnp.exp(lgf * (j + 1.0))
            b_w = jnp.exp(lgb * (CH - j))
            wf = jnp.exp(lgf * (CH - 1.0 - j))
            wb = jnp.exp(lgb * j)
            lag, dm = _decay_mask(lgf[:1], lgb[:1])
            sc = _dot(q, k, NT)
            gg = _dot(do, v, NT)
            dg = gg * dm
            x1 = _dot(do, sf_, NT) * a_w
            x2 = _dot(do, sb_, NT) * b_w
            y1 = _dot(v, ef_, NT) * wf
            y2 = _dot(v, eb_, NT) * wb
            dq = _dot(dg, k, NN) + x1 + x2
            dk = _dot(dg, q, TN) + y1 + y2
            dv = _dot(sc * dm, do, TN) + _dot(k * wf, ef_, NN) + _dot(k * wb, eb_, NN)
            dq_ref[:, sl] = _rot256_t(dq, c, s).astype(ACT)
            dk_ref[:, sl] = (_rot256_t(dk, c, s) * scale).astype(ACT)
            dv_ref[:, sl] = dv.astype(ACT)
            t = dm * gg * sc
            qx1 = jnp.sum(q * x1, axis=-1, keepdims=True)
            qx2 = jnp.sum(q * x2, axis=-1, keepdims=True)
            ky1 = jnp.sum(k * y1, axis=-1, keepdims=True)
            ky2 = jnp.sum(k * y2, axis=-1, keepdims=True)
            dlf = (_sum_all(jnp.where(lag > 0, lag * t, 0.0))
                   + _sum_all((j + 1.0) * qx1 + (CH - 1.0 - j) * ky1)
                   + CH * jnp.exp(lgf[:1] * CH) * _sum_all(ef_ * sf_))
            dlb = (_sum_all(jnp.where(lag < 0, -lag * t, 0.0))
                   + _sum_all((CH - j) * qx2 + j * ky2)
                   + CH * jnp.exp(lgb[:1] * CH) * _sum_all(eb_ * sb_))
            da_ref[h] += jnp.where((row == 0) & (lane == 0), dlf * lgf[:1],
                                   jnp.where((row == 0) & (lane == 1), dlb * lgb[:1], 0.0))

    def zc(col):
        return pl.BlockSpec((CH, W), lambda c: (c, (col * RET_HD) // W))

    tab = pl.BlockSpec((CH, 128), lambda c: (c, 0))
    stt = pl.BlockSpec((H, None, RET_HD, RET_HD), lambda c: (0, c, 0, 0))
    out = pl.BlockSpec((CH, W), lambda c: (c, 0))
    dz = jax.ShapeDtypeStruct((S, W), ACT)
    return pl.pallas_call(
        body, grid=(nC,),
        in_specs=[zc(rq), zc(rk), zc(rv), out, tab, tab, stt, stt, stt, stt, SMEM_SPEC, SMEM_SPEC],
        out_specs=[out, out, out, pl.BlockSpec((H, 8, 128), lambda c: (0, 0, 0))],
        out_shape=[dz, dz, dz, jax.ShapeDtypeStruct((H, 8, 128), F32)],
        name="ret_bwd", compiler_params=_params("arbitrary"))(
            z, z, z, do, cos, sin, sf, sb, ef, eb, af, ab)


def _fill_padded(pad_ref, src_ref, S):
    zeros = jnp.zeros((POOL_PAD, POOL_GD), F32)
    pad_ref[pl.ds(0, POOL_PAD), :] = zeros
    pad_ref[pl.ds(POOL_PAD, S), :] = src_ref[...]
    pad_ref[pl.ds(S + POOL_PAD, POOL_PAD), :] = zeros


def _window_sum(ext, T, lo, hi):
    n = T + 2 * POOL_PAD
    acc = None
    for k in range(lo, hi):
        sh = ext if k == 0 else pltpu.roll(ext, (-k) % n, 0)
        piece = sh[POOL_PAD:POOL_PAD + T]
        acc = piece if acc is None else acc + piece
    return acc


def _window_count(pos, w, S):
    lo = jnp.maximum(pos - w // 2, 0)
    hi = jnp.minimum(pos + w // 2, S)
    return (hi - lo).astype(F32)


def _pool_fwd(z, pw, scale, pv, pg):
    S = z.shape[0]
    G = len(POOL_WINDOWS)
    T = min(512, S)
    W = G * POOL_GD

    def body(x_ref, g_ref, pw_ref, sc_ref, y_ref, u_ref, ut_ref, pad, p_scr):
        grp = pl.program_id(0)
        i = pl.program_id(1)

        @pl.when(i == 0)
        def _():
            _fill_padded(pad, x_ref, S)

        r0 = pl.multiple_of(i * T, T)
        ext = pad[pl.ds(r0, T + 2 * POOL_PAD), :]
        pos = r0 + lax.broadcasted_iota(jnp.int32, (T, 1), 0)
        for gi, w in enumerate(POOL_WINDOWS):
            @pl.when(grp == gi)
            def _(w=w):
                acc = _window_sum(ext, T, -(w // 2), w // 2)
                p_scr[...] = acc / _window_count(pos, w, S) - ext[POOL_PAD:POOL_PAD + T]

        y = _dot(p_scr[...], pw_ref[...], NN)
        y_ref[...] = y
        g = g_ref[...]
        u = y * sc_ref[...] * (g * _sigmoid(g))
        u_ref[...] = u.astype(ACT)
        ut_ref[...] = u.T.astype(ACT)

    blk = pl.BlockSpec((T, POOL_GD), lambda g, i: (i, g))
    return pl.pallas_call(
        body, grid=(G, S // T),
        in_specs=[pl.BlockSpec((S, POOL_GD), lambda g, i: (0, pv + g)),
                  pl.BlockSpec((T, POOL_GD), lambda g, i: (i, pg + g)),
                  pl.BlockSpec((None, POOL_GD, POOL_GD), lambda g, i: (g, 0, 0)),
                  pl.BlockSpec((1, POOL_GD), lambda g, i: (0, g))],
        out_specs=[blk, blk, pl.BlockSpec((POOL_GD, T), lambda g, i: (g, i))],
        out_shape=[jax.ShapeDtypeStruct((S, W), F32), jax.ShapeDtypeStruct((S, W), ACT),
                   jax.ShapeDtypeStruct((W, S), ACT)],
        scratch_shapes=[pltpu.VMEM((S + 2 * POOL_PAD, POOL_GD), F32), pltpu.VMEM((T, POOL_GD), F32)],
        name="pool_fwd", compiler_params=_params("parallel", "arbitrary"))(z, z, pw, scale)


def _pool_bwd_a(z, pw, scale, y_raw, du, pv, pg):
    S = z.shape[0]
    G = len(POOL_WINDOWS)
    T = min(512, S)
    W = G * POOL_GD

    def body(x_ref, g_ref, pw_ref, sc_ref, y_ref, du_ref, dpc_ref, dg_ref, dsc_ref, dpw_ref,
             pad, p_scr, c_scr):
        grp = pl.program_id(0)
        i = pl.program_id(1)

        @pl.when(i == 0)
        def _():
            _fill_padded(pad, x_ref, S)
            dsc_ref[...] = jnp.zeros_like(dsc_ref)
            dpw_ref[...] = jnp.zeros_like(dpw_ref)

        r0 = pl.multiple_of(i * T, T)
        ext = pad[pl.ds(r0, T + 2 * POOL_PAD), :]
        pos = r0 + lax.broadcasted_iota(jnp.int32, (T, 1), 0)
        for gi, w in enumerate(POOL_WINDOWS):
            @pl.when(grp == gi)
            def _(w=w):
                cnt = _window_count(pos, w, S)
                acc = _window_sum(ext, T, -(w // 2), w // 2)
                p_scr[...] = acc / cnt - ext[POOL_PAD:POOL_PAD + T]
                c_scr[...] = jnp.broadcast_to(cnt, (T, 128))

        g = g_ref[...]
        sg = _sigmoid(g)
        duv = du_ref[...]
        y = y_ref[...]
        scl = sc_ref[...]
        dy = duv * (scl * (g * sg))
        dg_ref[...] = (duv * y * scl * (sg * (1.0 + g * (1.0 - sg)))).astype(ACT)
        dsc_ref[...] += jnp.sum(duv * y * (g * sg), axis=0, keepdims=True)
        dpw_ref[...] += _dot(p_scr[...], dy, TN)
        dpc_ref[...] = _dot(dy, pw_ref[...], NT) / c_scr[:, :1]

    blk = pl.BlockSpec((T, POOL_GD), lambda g, i: (i, g))
    return pl.pallas_call(
        body, grid=(G, S // T),
        in_specs=[pl.BlockSpec((S, POOL_GD), lambda g, i: (0, pv + g)),
                  pl.BlockSpec((T, POOL_GD), lambda g, i: (i, pg + g)),
                  pl.BlockSpec((None, POOL_GD, POOL_GD), lambda g, i: (g, 0, 0)),
                  pl.BlockSpec((1, POOL_GD), lambda g, i: (0, g)), blk, blk],
        out_specs=[blk, blk, pl.BlockSpec((1, POOL_GD), lambda g, i: (0, g)),
                   pl.BlockSpec((None, POOL_GD, POOL_GD), lambda g, i: (g, 0, 0))],
        out_shape=[jax.ShapeDtypeStruct((S, W), F32), jax.ShapeDtypeStruct((S, W), ACT),
                   jax.ShapeDtypeStruct((1, W), F32), jax.ShapeDtypeStruct((G, POOL_GD, POOL_GD), F32)],
        scratch_shapes=[pltpu.VMEM((S + 2 * POOL_PAD, POOL_GD), F32), pltpu.VMEM((T, POOL_GD), F32),
                        pltpu.VMEM((T, 128), F32)],
        name="pool_bwd_a", compiler_params=_params("parallel", "arbitrary"))(z, z, pw, scale, y_raw, du)


def _pool_bwd_b(dpc):
    S, W = dpc.shape
    G = len(POOL_WINDOWS)
    T = min(512, S)

    def body(x_ref, o_ref, pad, acc_scr):
        grp = pl.program_id(0)
        i = pl.program_id(1)

        @pl.when(i == 0)
        def _():
            _fill_padded(pad, x_ref, S)

        r0 = pl.multiple_of(i * T, T)
        ext = pad[pl.ds(r0, T + 2 * POOL_PAD), :]
        pos = r0 + lax.broadcasted_iota(jnp.int32, (T, 1), 0)
        for gi, w in enumerate(POOL_WINDOWS):
            @pl.when(grp == gi)
            def _(w=w):
                acc = _window_sum(ext, T, -(w // 2) + 1, w // 2 + 1)
                acc_scr[...] = acc - ext[POOL_PAD:POOL_PAD + T] * _window_count(pos, w, S)

        o_ref[...] = acc_scr[...].astype(ACT)

    blk = pl.BlockSpec((T, POOL_GD), lambda g, i: (i, g))
    return pl.pallas_call(
        body, grid=(G, S // T),
        in_specs=[pl.BlockSpec((S, POOL_GD), lambda g, i: (0, g))], out_specs=blk,
        out_shape=jax.ShapeDtypeStruct((S, W), ACT),
        scratch_shapes=[pltpu.VMEM((S + 2 * POOL_PAD, POOL_GD), F32), pltpu.VMEM((T, POOL_GD), F32)],
        name="pool_bwd_b", compiler_params=_params("parallel", "arbitrary"))(dpc)


def _rope128(x, cf, sa, sb):
    return x * cf + pltpu.roll(x, ROPE_HALF, 1) * sa + pltpu.roll(x, ATT_HD - ROPE_HALF, 1) * sb


def _rope128_t(g, cf, sa, sb):
    return g * cf + pltpu.roll(g * sa, ATT_HD - ROPE_HALF, 1) + pltpu.roll(g * sb, ROPE_HALF, 1)


def _attn_prep(z, qgain, kgain, cf, sa, sb, aq, ak, av):
    S = z.shape[0]
    T = min(512, S)
    QW, KW = ATT_Q * ATT_HD, ATT_KV * ATT_HD

    def body(q_ref, k_ref, v_ref, qg_ref, kg_ref, cf_ref, sa_ref, sb_ref, qn_ref, kn_ref, vb_ref):
        cfv, sav, sbv = cf_ref[...], sa_ref[...], sb_ref[...]

        def prep(x, gain):
            r = lax.rsqrt(jnp.mean(x * x, axis=-1, keepdims=True) + RMS_EPS)
            return _rope128(x * r * gain, cfv, sav, sbv)

        for hh in range(ATT_Q):
            sl = slice(hh * ATT_HD, (hh + 1) * ATT_HD)
            qn_ref[:, sl] = prep(q_ref[:, sl], qg_ref[...]).astype(ACT)
        for hh in range(ATT_KV):
            sl = slice(hh * ATT_HD, (hh + 1) * ATT_HD)
            kn_ref[:, sl] = prep(k_ref[:, sl], kg_ref[...]).astype(ACT)
        vb_ref[...] = v_ref[...].astype(ACT)

    tab = pl.BlockSpec((T, ATT_HD), lambda i: (i, 0))
    gain = pl.BlockSpec((1, ATT_HD), lambda i: (0, 0))
    return pl.pallas_call(
        body, grid=(S // T,),
        in_specs=[pl.BlockSpec((T, QW), lambda i: (i, aq)), pl.BlockSpec((T, KW), lambda i: (i, ak)),
                  pl.BlockSpec((T, KW), lambda i: (i, av)), gain, gain, tab, tab, tab],
        out_specs=[pl.BlockSpec((T, QW), lambda i: (i, 0)), pl.BlockSpec((T, KW), lambda i: (i, 0)),
                   pl.BlockSpec((T, KW), lambda i: (i, 0))],
        out_shape=[jax.ShapeDtypeStruct((S, QW), ACT), jax.ShapeDtypeStruct((S, KW), ACT),
                   jax.ShapeDtypeStruct((S, KW), ACT)],
        name="attn_prep", compiler_params=_params("parallel"))(z, z, z, qgain, kgain, cf, sa, sb)


def _attn_window(i, S):
    start = jnp.clip(i * ATT_BLK - ATT_BLK, 0, S - ATT_SPAN)
    start = pl.multiple_of(start, ATT_BLK)
    qpos = i * ATT_BLK + lax.broadcasted_iota(jnp.int32, (ATT_BLK, ATT_SPAN), 0)
    kpos = start + lax.broadcasted_iota(jnp.int32, (ATT_BLK, ATT_SPAN), 1)
    return start, jnp.abs(kpos - qpos) <= ATT_WIN


def _attn_probs(q, kw, valid, sink):
    s = _dot(q, kw, NT) * (ATT_HD ** -0.5)
    s = jnp.where(valid, s, NEG_BIG)
    m = jnp.maximum(jnp.max(s, axis=-1, keepdims=True), sink)
    p = jnp.exp(s - m)
    es = jnp.exp(sink - m)
    den = jnp.sum(p, axis=-1, keepdims=True) + es
    return p / den, es / den


def _attn_fwd(qn, kn, vb, z, sink, ag):
    S = qn.shape[0]
    nB = S // ATT_BLK
    assert S >= ATT_SPAN
    QW = ATT_Q * ATT_HD

    GW = ATT_G * ATT_HD

    def body(q_ref, k_ref, v_ref, g_ref, sink_ref, o_ref, u_ref, ut_ref):
        kvh = pl.program_id(0)
        i = pl.program_id(1)
        start, valid = _attn_window(i, S)
        kw = k_ref[pl.ds(start, ATT_SPAN), :]
        vw = v_ref[pl.ds(start, ATT_SPAN), :]
        for gi in range(ATT_G):
            sl = slice(gi * ATT_HD, (gi + 1) * ATT_HD)
            sk = jnp.full((ATT_BLK, 1), sink_ref[kvh * ATT_G + gi], F32)
            pn, _ = _attn_probs(q_ref[:, sl], kw, valid, sk)
            o = _dot(pn, vw, NN)
            o_ref[:, sl] = o
            g = g_ref[:, sl]
            u = o * (g * _sigmoid(g))
            u_ref[:, sl] = u.astype(ACT)
            ut_ref[sl, :] = u.T.astype(ACT)

    blk = pl.BlockSpec((ATT_BLK, GW), lambda k, i: (i, k))
    kv = pl.BlockSpec((S, ATT_HD), lambda k, i: (0, k))
    return pl.pallas_call(
        body, grid=(ATT_KV, nB),
        in_specs=[blk, kv, kv, pl.BlockSpec((ATT_BLK, GW), lambda k, i: (i, ag // ATT_G + k)), SMEM_SPEC],
        out_specs=[blk, blk, pl.BlockSpec((GW, ATT_BLK), lambda k, i: (k, i))],
        out_shape=[jax.ShapeDtypeStruct((S, QW), F32), jax.ShapeDtypeStruct((S, QW), ACT),
                   jax.ShapeDtypeStruct((QW, S), ACT)],
        name="attn_fwd", compiler_params=_params("parallel", "parallel"))(qn, kn, vb, z, sink)


def _attn_bwd(qn, kn, vb, o, du, z, sink, ag):
    S = qn.shape[0]
    nB = S // ATT_BLK
    QW, KW = ATT_Q * ATT_HD, ATT_KV * ATT_HD
    GW = ATT_G * ATT_HD

    def body(q_ref, k_ref, v_ref, o_ref, du_ref, g_ref, sink_ref,
             dq_ref, dk_ref, dv_ref, dg_ref, ds_ref):
        kvh = pl.program_id(0)
        i = pl.program_id(1)

        @pl.when(i == 0)
        def _():
            dk_ref[...] = jnp.zeros_like(dk_ref)
            dv_ref[...] = jnp.zeros_like(dv_ref)
            ds_ref[...] = jnp.zeros_like(ds_ref)

        start, valid = _attn_window(i, S)
        kw = k_ref[pl.ds(start, ATT_SPAN), :]
        vw = v_ref[pl.ds(start, ATT_SPAN), :]
        row = lax.broadcasted_iota(jnp.int32, (8, 128), 0)
        lane = lax.broadcasted_iota(jnp.int32, (8, 128), 1)
        dsink = jnp.zeros((8, 128), F32)
        dk_acc = jnp.zeros((ATT_SPAN, ATT_HD), F32)
        dv_acc = jnp.zeros((ATT_SPAN, ATT_HD), F32)
        for gi in range(ATT_G):
            sl = slice(gi * ATT_HD, (gi + 1) * ATT_HD)
            q = q_ref[:, sl]
            ov = o_ref[:, sl]
            g = g_ref[:, sl]
            duv = du_ref[:, sl]
            sg = _sigmoid(g)
            do = duv * (g * sg)
            dg_ref[:, sl] = (duv * ov * (sg * (1.0 + g * (1.0 - sg)))).astype(ACT)
            sk = jnp.full((ATT_BLK, 1), sink_ref[kvh * ATT_G + gi], F32)
            pn, psink = _attn_probs(q, kw, valid, sk)
            delta = jnp.sum(do * ov, axis=-1, keepdims=True)
            dsc = pn * (_dot(do, vw, NT) - delta) * (ATT_HD ** -0.5)
            dq_ref[:, sl] = _dot(dsc, kw, NN)
            dk_acc = dk_acc + _dot(dsc, q, TN)
            dv_acc = dv_acc + _dot(pn, do, TN)
            dsink = dsink + jnp.where((row == 0) & (lane == gi), -_sum_all(psink * delta), 0.0)
        dk_ref[pl.ds(start, ATT_SPAN), :] += dk_acc
        dv_ref[pl.ds(start, ATT_SPAN), :] += dv_acc
        ds_ref[...] += dsink

    grp = pl.BlockSpec((ATT_BLK, GW), lambda k, i: (i, k))
    kv = pl.BlockSpec((S, ATT_HD), lambda k, i: (0, k))
    return pl.pallas_call(
        body, grid=(ATT_KV, nB),
        in_specs=[grp, kv, kv, grp, grp,
                  pl.BlockSpec((ATT_BLK, GW), lambda k, i: (i, ag // ATT_G + k)), SMEM_SPEC],
        out_specs=[grp, kv, kv, grp, pl.BlockSpec((None, 8, 128), lambda k, i: (k, 0, 0))],
        out_shape=[jax.ShapeDtypeStruct((S, QW), F32), jax.ShapeDtypeStruct((S, KW), F32),
                   jax.ShapeDtypeStruct((S, KW), F32), jax.ShapeDtypeStruct((S, QW), ACT),
                   jax.ShapeDtypeStruct((ATT_KV, 8, 128), F32)],
        name="attn_bwd", compiler_params=_params("parallel", "arbitrary"))(qn, kn, vb, o, du, z, sink)


def _attn_prep_bwd(z, dqn, dkn, dv, qgain, kgain, cf, sa, sb, aq, ak):
    S = z.shape[0]
    T = min(512, S)
    QW, KW = ATT_Q * ATT_HD, ATT_KV * ATT_HD

    def body(q_ref, k_ref, dqn_ref, dkn_ref, dv_ref, qg_ref, kg_ref, cf_ref, sa_ref, sb_ref,
             dq_ref, dk_ref, dvb_ref, dqg_ref, dkg_ref):
        cfv, sav, sbv = cf_ref[...], sa_ref[...], sb_ref[...]

        @pl.when(pl.program_id(0) == 0)
        def _():
            dqg_ref[...] = jnp.zeros_like(dqg_ref)
            dkg_ref[...] = jnp.zeros_like(dkg_ref)

        def back(x, gn, gain):
            r = lax.rsqrt(jnp.mean(x * x, axis=-1, keepdims=True) + RMS_EPS)
            xh = x * r
            dy = _rope128_t(gn, cfv, sav, sbv)
            dxh = dy * gain
            dx = r * (dxh - xh * jnp.mean(dxh * xh, axis=-1, keepdims=True))
            return dx, jnp.sum(dy * xh, axis=0, keepdims=True)

        dqg = jnp.zeros((1, ATT_HD), F32)
        for hh in range(ATT_Q):
            sl = slice(hh * ATT_HD, (hh + 1) * ATT_HD)
            dx, dgn = back(q_ref[:, sl], dqn_ref[:, sl], qg_ref[...])
            dq_ref[:, sl] = dx.astype(ACT)
            dqg = dqg + dgn
        dkg = jnp.zeros((1, ATT_HD), F32)
        for hh in range(ATT_KV):
            sl = slice(hh * ATT_HD, (hh + 1) * ATT_HD)
            dx, dgn = back(k_ref[:, sl], dkn_ref[:, sl], kg_ref[...])
            dk_ref[:, sl] = dx.astype(ACT)
            dkg = dkg + dgn
        dvb_ref[...] = dv_ref[...].astype(ACT)
        dqg_ref[...] += dqg
        dkg_ref[...] += dkg

    tab = pl.BlockSpec((T, ATT_HD), lambda i: (i, 0))
    gain = pl.BlockSpec((1, ATT_HD), lambda i: (0, 0))
    qb = pl.BlockSpec((T, QW), lambda i: (i, 0))
    kb = pl.BlockSpec((T, KW), lambda i: (i, 0))
    return pl.pallas_call(
        body, grid=(S // T,),
        in_specs=[pl.BlockSpec((T, QW), lambda i: (i, aq)), pl.BlockSpec((T, KW), lambda i: (i, ak)),
                  qb, kb, kb, gain, gain, tab, tab, tab],
        out_specs=[qb, kb, kb, gain, gain],
        out_shape=[jax.ShapeDtypeStruct((S, QW), ACT), jax.ShapeDtypeStruct((S, KW), ACT),
                   jax.ShapeDtypeStruct((S, KW), ACT), jax.ShapeDtypeStruct((1, ATT_HD), F32),
                   jax.ShapeDtypeStruct((1, ATT_HD), F32)],
        name="attn_prep_bwd", compiler_params=_params("arbitrary"))(
            z, z, dqn, dkn, dv, qgain, kgain, cf, sa, sb)


def _branch_merge(ua, ub, uc, wr, wp, wa, z, mg):
    S, W = ua.shape
    D = wr.shape[1]
    tm, tn = min(512, S), min(512, D)
    nb = D // tn

    def body(ua_ref, ub_ref, uc_ref, wr_ref, wp_ref, wa_ref, g0_ref, g1_ref, g2_ref,
             ya_ref, yb_ref, yc_ref, m_ref, mt_ref):
        ya = _dot(ua_ref[...], wr_ref[...], NN)
        yb = _dot(ub_ref[...], wp_ref[...], NN)
        yc = _dot(uc_ref[...], wa_ref[...], NN)
        ya_ref[...] = ya.astype(ACT)
        yb_ref[...] = yb.astype(ACT)
        yc_ref[...] = yc.astype(ACT)
        m = _sigmoid(g0_ref[...]) * ya + _sigmoid(g1_ref[...]) * yb + _sigmoid(g2_ref[...]) * yc
        m_ref[...] = m.astype(ACT)
        mt_ref[...] = m.T.astype(ACT)

    u = pl.BlockSpec((tm, W), lambda i, j: (i, 0))
    w = pl.BlockSpec((W, tn), lambda i, j: (0, j))
    o = pl.BlockSpec((tm, tn), lambda i, j: (i, j))

    assert (mg * POOL_GD) % tn == 0
    base = (mg * POOL_GD) // tn

    def gate(k):
        return pl.BlockSpec((tm, tn), lambda i, j: (i, base + k * nb + j))

    sd = jax.ShapeDtypeStruct((S, D), ACT)
    return pl.pallas_call(
        body, grid=(S // tm, nb), in_specs=[u, u, u, w, w, w, gate(0), gate(1), gate(2)],
        out_specs=[o, o, o, o, pl.BlockSpec((tn, tm), lambda i, j: (j, i))],
        out_shape=[sd, sd, sd, sd, jax.ShapeDtypeStruct((D, S), ACT)], name="branch_merge",
        compiler_params=_params("parallel", "parallel"))(ua, ub, uc, wr, wp, wa, z, z, z)


def _merge_bwd(dxb, wo, ya, yb, yc, z, mg):
    S, D = dxb.shape
    tm, tn = min(512, S), min(512, D)
    nb = D // tn
    base = (mg * POOL_GD) // tn

    def body(dx_ref, wo_ref, ya_ref, yb_ref, yc_ref, g0_ref, g1_ref, g2_ref,
             da_ref, db_ref, dc_ref, dg0_ref, dg1_ref, dg2_ref):
        dm = _dot(dx_ref[...], wo_ref[...], NT)
        for y_ref, g_ref, dy_ref, dg_ref in ((ya_ref, g0_ref, da_ref, dg0_ref),
                                             (yb_ref, g1_ref, db_ref, dg1_ref),
                                             (yc_ref, g2_ref, dc_ref, dg2_ref)):
            sg = _sigmoid(g_ref[...])
            dy_ref[...] = (sg * dm).astype(ACT)
            dg_ref[...] = (dm * y_ref[...].astype(F32) * (sg * (1.0 - sg))).astype(ACT)

    o = pl.BlockSpec((tm, tn), lambda i, j: (i, j))

    def gate(k):
        return pl.BlockSpec((tm, tn), lambda i, j: (i, base + k * nb + j))

    sd = jax.ShapeDtypeStruct((S, D), ACT)
    return pl.pallas_call(
        body, grid=(S // tm, nb),
        in_specs=[pl.BlockSpec((tm, D), lambda i, j: (i, 0)), pl.BlockSpec((tn, D), lambda i, j: (j, 0)),
                  o, o, o, gate(0), gate(1), gate(2)],
        out_specs=[o] * 6, out_shape=[sd] * 6, name="merge_bwd",
        compiler_params=_params("parallel", "parallel"))(dxb, wo, ya, yb, yc, z, z, z)


def _loss_head(y, t):
    S, D = y.shape
    tm = min(256, S)

    def body(y_ref, t_ref, dy_ref, dyb_ref, l_ref):
        e = y_ref[...] - t_ref[...]
        dy = e * (1.0 / D)
        dy_ref[...] = dy
        dyb_ref[...] = dy.astype(ACT)

        @pl.when(pl.program_id(0) == 0)
        def _():
            l_ref[...] = jnp.zeros_like(l_ref)

        l_ref[...] += jnp.sum(jnp.mean(e * e, axis=-1, keepdims=True), axis=0, keepdims=True)

    row = pl.BlockSpec((tm, D), lambda i: (i, 0))
    return pl.pallas_call(
        body, grid=(S // tm,), in_specs=[row, row],
        out_specs=[row, row, pl.BlockSpec((1, 1), lambda i: (0, 0))],
        out_shape=[jax.ShapeDtypeStruct((S, D), F32), jax.ShapeDtypeStruct((S, D), ACT),
                   jax.ShapeDtypeStruct((1, 1), F32)],
        name="loss_head", compiler_params=_params("arbitrary"))(y, t)


def _adamw(w, layer, parts, m, v, name):
    L = w.shape[0]
    shape = w.shape[1:]
    C = shape[-1]
    R = int(np.prod(shape[:-1]))
    w3, m3, v3 = (a.reshape(L, R, C) for a in (w, m, v))
    p3 = parts.reshape(N_DEV, R, C)
    tr = min(64, R)
    assert R % tr == 0
    c1 = 1.0 / (1.0 - ADAM_B1 ** ADAM_STEP)
    c2 = 1.0 / (1.0 - ADAM_B2 ** ADAM_STEP)

    def body(w_ref, p_ref, m_ref, v_ref, g_ref, d_ref, nm_ref, nv_ref):
        g = p_ref[0].astype(F32)
        for k in range(1, N_DEV):
            g = g + p_ref[k].astype(F32)
        nm = ADAM_B1 * m_ref[...] + (1.0 - ADAM_B1) * g
        nv = ADAM_B2 * v_ref[...] + (1.0 - ADAM_B2) * (g * g)
        g_ref[...] = g
        nm_ref[...] = nm
        nv_ref[...] = nv
        d_ref[...] = -ADAM_LR * ((nm * c1) / (jnp.sqrt(nv * c2) + ADAM_EPS) + ADAM_WD * w_ref[...])

    lay = pl.BlockSpec((None, tr, C), lambda i: (layer, i, 0))
    out = pl.BlockSpec((tr, C), lambda i: (i, 0))
    sd = jax.ShapeDtypeStruct((R, C), F32)
    outs = pl.pallas_call(
        body, grid=(R // tr,),
        in_specs=[lay, pl.BlockSpec((N_DEV, tr, C), lambda i: (0, i, 0)), lay, lay],
        out_specs=[out] * 4, out_shape=[sd] * 4, name=name,
        compiler_params=_params("parallel"))(w3, p3, m3, v3)
    return [a.reshape(shape) for a in outs]


def _exchange(arrs, scatter, name, deps=()):
    n = len(arrs)
    nd = len(deps)
    out_shape = [jax.ShapeDtypeStruct(a.shape if scatter else (N_DEV,) + a.shape, a.dtype) for a in arrs]

    def body(*refs):
        ins, outs = refs[:n], refs[n + nd:2 * n + nd]
        send_sems, recv_sems, local_sems = refs[2 * n + nd:]
        x, y, c = lax.axis_index("x"), lax.axis_index("y"), lax.axis_index("c")
        me = 4 * x + 2 * y + c
        copies = []
        for a in range(n):
            src = ins[a].at[me] if scatter else ins[a]
            own = pltpu.make_async_copy(src, outs[a].at[me], local_sems.at[a])
            own.start()
            copies.append(own)
        sends, recvs = [], []
        for k in range(1, N_DEV):
            px, py, pc = x ^ (k >> 2), y ^ ((k >> 1) & 1), c ^ (k & 1)
            peer = 4 * px + 2 * py + pc
            for a in range(n):
                src = ins[a].at[peer] if scatter else ins[a]
                cp = pltpu.make_async_remote_copy(
                    src_ref=src, dst_ref=outs[a].at[me],
                    send_sem=send_sems.at[a, k - 1], recv_sem=recv_sems.at[a, k - 1],
                    device_id=(px, py, pc), device_id_type=pl.DeviceIdType.MESH)
                cp.start()
                sends.append(cp)
                recvs.append(pltpu.make_async_remote_copy(
                    src_ref=src, dst_ref=outs[a].at[peer],
                    send_sem=send_sems.at[a, k - 1], recv_sem=recv_sems.at[a, k - 1],
                    device_id=(px, py, pc), device_id_type=pl.DeviceIdType.MESH))
        for cp in recvs:
            cp.wait_recv()
        for cp in sends:
            cp.wait_send()
        for cp in copies:
            cp.wait()

    any_spec = pl.BlockSpec(memory_space=pl.ANY)
    return pl.pallas_call(
        body, in_specs=[any_spec] * (n + nd), out_specs=[any_spec] * n, out_shape=out_shape,
        scratch_shapes=[pltpu.SemaphoreType.DMA((n, N_DEV - 1)), pltpu.SemaphoreType.DMA((n, N_DEV - 1)),
                        pltpu.SemaphoreType.DMA((n,))],
        name=name)(*arrs, *deps)


HBM_SPEC = pl.BlockSpec(memory_space=pltpu.HBM)
SEM_SPEC = pl.BlockSpec(memory_space=pltpu.SEMAPHORE)
DATAFLOW = pltpu.SideEffectType.DATAFLOW_SIDE_EFFECTING


def _peer_of(k):
    x, y, c = lax.axis_index("x"), lax.axis_index("y"), lax.axis_index("c")
    return x ^ (k >> 2), y ^ ((k >> 1) & 1), c ^ (k & 1)


def _exchange_copy(k, a, src_ref, land_ref, send_sems, recv_sems, scatter, outgoing):
    px, py, pc = _peer_of(k)
    peer = 4 * px + 2 * py + pc
    me = 4 * lax.axis_index("x") + 2 * lax.axis_index("y") + lax.axis_index("c")
    idx = a * (N_DEV - 1) + k - 1
    return pltpu.make_async_remote_copy(
        src_ref=src_ref.at[peer] if scatter else src_ref, dst_ref=land_ref.at[me if outgoing else peer],
        send_sem=send_sems.at[idx], recv_sem=recv_sems.at[idx],
        device_id=(px, py, pc), device_id_type=pl.DeviceIdType.MESH)


def _exchange_start(arrs, scatter, name):
    n = len(arrs)
    land_shapes = [a.shape if scatter else (N_DEV,) + a.shape for a in arrs]

    def body(*refs):
        srcs, lands = refs[:n], refs[n:2 * n]
        send_sems, recv_sems = refs[2 * n], refs[2 * n + 1]
        token = refs[-1]
        for k in range(1, N_DEV):
            for a in range(n):
                _exchange_copy(k, a, srcs[a], lands[a], send_sems, recv_sems, scatter, True).start()
        token[...] = jnp.zeros_like(token)

    sems = pltpu.SemaphoreType.DMA((n * (N_DEV - 1),))
    out_shape = ([sems, sems] + [pltpu.HBM(a.shape, a.dtype) for a in arrs]
                 + [pltpu.HBM(s, a.dtype) for s, a in zip(land_shapes, arrs)]
                 + [jax.ShapeDtypeStruct((8, 128), F32)])
    ins = ([pltpu.with_memory_space_constraint(a, pltpu.HBM) for a in arrs]
           + [pltpu.with_memory_space_constraint(lax.empty(s, a.dtype), pltpu.HBM) for s, a in zip(land_shapes, arrs)])
    res = pl.pallas_call(
        body, name=name, out_shape=out_shape, in_specs=[HBM_SPEC] * (2 * n),
        out_specs=[SEM_SPEC, SEM_SPEC] + [HBM_SPEC] * (2 * n) + [pl.BlockSpec(memory_space=pltpu.VMEM)],
        input_output_aliases={i: 2 + i for i in range(2 * n)},
        compiler_params=pltpu.CompilerParams(has_side_effects=DATAFLOW))(*ins)
    return res[0], res[1], list(res[2:2 + n]), list(res[2 + n:2 + 2 * n]), res[-1]


def _exchange_wait(started, after, scatter, name):
    send_sems, recv_sems, srcs, lands, _ = started
    n = len(srcs)

    def body(*refs):
        src_refs, land_refs = refs[:n], refs[n:2 * n]
        s_sems, r_sems = refs[2 * n], refs[2 * n + 1]
        for k in range(1, N_DEV):
            for a in range(n):
                back = _exchange_copy(k, a, src_refs[a], land_refs[a], s_sems, r_sems, scatter, False)
                back.wait_send()
                back.wait_recv()

    out_shape = [pltpu.HBM(a.shape, a.dtype) for a in srcs] + [pltpu.HBM(a.shape, a.dtype) for a in lands]
    res = pl.pallas_call(
        body, name=name, out_shape=out_shape,
        in_specs=[HBM_SPEC] * (2 * n) + [SEM_SPEC, SEM_SPEC, pl.BlockSpec(memory_space=pl.ANY)],
        out_specs=[HBM_SPEC] * (2 * n), input_output_aliases={i: i for i in range(2 * n)},
        compiler_params=pltpu.CompilerParams(has_side_effects=DATAFLOW))(*srcs, *lands, send_sems, recv_sems, after)
    return list(res[:n]), list(res[n:])


def _own_slot(land, own, me):
    return lax.dynamic_update_index_in_dim(land, own, me, 0)


class _Cols:
    def __init__(self, D):
        Wb = D // 2
        sizes = (Wb, Wb, Wb, Wb, Wb, Wb, Wb, ATT_KV * ATT_HD, ATT_KV * ATT_HD, Wb, 3 * D)
        offs = np.concatenate([[0], np.cumsum(sizes)])
        assert all(int(o) % 256 == 0 for o in offs)
        (self.rq, self.rk, self.rv, self.rg, self.pv, self.pg,
         self.aq, self.ak, self.av, self.ag, self.mg) = (int(o) // 256 for o in offs[:-1])
        self.width = int(offs[-1])
        self.sizes = sizes


def _rope_tables(S):
    pos = jnp.arange(S, dtype=F32)[:, None]
    inv_r = 1.0 / (RET_ROPE_BASE ** jnp.linspace(0.0, 1.0, RET_HD // 2, dtype=F32))
    ang_r = pos * inv_r[None, :]
    inv_a = ROPE_THETA ** (-jnp.arange(ROPE_HALF, dtype=F32) / ROPE_HALF)
    ang_a = pos * inv_a[None, :]
    ca, sa = jnp.cos(ang_a), jnp.sin(ang_a)
    z16 = jnp.zeros((S, ROPE_HALF), F32)
    rest = ATT_HD - 2 * ROPE_HALF
    cf = jnp.concatenate([ca, ca, jnp.ones((S, rest), F32)], axis=1)
    s_up = jnp.concatenate([z16, sa, jnp.zeros((S, rest), F32)], axis=1)
    s_dn = jnp.concatenate([-sa, z16, jnp.zeros((S, rest), F32)], axis=1)
    return jnp.cos(ang_r), jnp.sin(ang_r), cf, s_up, s_dn


def _layer_fwd(x, p, tabs, cols, deps=()):
    cos_r, sin_r, cf, s_up, s_dn = tabs
    S, D = x.shape
    h, ht = _rmsnorm_fwd(x, p["norm_g"], deps)
    z = _matmul(h, p["w_in_t"], "nt", F32, 2048, 512, D, "in_proj")
    if "rest" in p:
        p = {**{k: v for k, v in p.items() if k != "rest"}, **p["rest"](z)}
    rcols = (cols.rq, cols.rk, cols.rv, cols.rg)
    sf, sb = _ret_state(z, cols.rk, RET_HD ** -0.5, z, cols.rv, cos_r, sin_r, p["af"], p["ab"], "fwd",
                        "ret_state_fwd")
    o_ret, ua, uat = _ret_fwd(z, cos_r, sin_r, sf, sb, p["af"], p["ab"], rcols)
    y_pool, ub, ubt = _pool_fwd(z, p["pool_w"], p["pool_scale"], cols.pv, cols.pg)
    qn, kn, vb = _attn_prep(z, p["q_gain"], p["k_gain"], cf, s_up, s_dn,
                            cols.aq // 4, cols.ak, cols.av)
    o_att, uc, uct = _attn_fwd(qn, kn, vb, z, p["sink"], cols.ag * 2)
    ya, yb, yc, merged, mergedt = _branch_merge(ua, ub, uc, p["w_ret"], p["w_pool"], p["w_att"], z, cols.mg)
    out = _matmul(merged, p["w_out"], "nn", F32, 1024, 512, D, "out_proj", res=x)
    saved = dict(x=x, ht=ht, z=z, sf=sf, sb=sb, o_ret=o_ret, uat=uat, y_pool=y_pool, ubt=ubt,
                 qn=qn, kn=kn, vb=vb, o_att=o_att, uct=uct, ya=ya, yb=yb, yc=yc, mergedt=mergedt)
    return out, saved, p


def _layer_bwd(dx, dxb, p, sv, tabs, cols, on_grads):
    cos_r, sin_r, cf, s_up, s_dn = tabs
    z = sv["z"]
    S, D = dx.shape
    dya, dyb, dyc, dmg0, dmg1, dmg2 = _merge_bwd(dxb, p["w_out"], sv["ya"], sv["yb"], sv["yc"], z, cols.mg)
    grads = {"w_out": _matmul(sv["mergedt"], dxb, "nn", ACT, 1024, 512, S, "dw_out")}
    dus = {}
    for nm, ut, dy in (("ret", sv["uat"], dya), ("pool", sv["ubt"], dyb), ("att", sv["uct"], dyc)):
        dus[nm] = _matmul(dy, p["w_" + nm], "nt", F32, 1024, 512, D, "du_" + nm)
        grads["w_" + nm] = _matmul(ut, dy, "nn", ACT, 1024, 512, S, "dw_" + nm)
    rcols = (cols.rq, cols.rk, cols.rv, cols.rg)
    do_ret, d_rg = _ret_gate_bwd(dus["ret"], sv["o_ret"], z, cols.rg)
    eb, ef = _ret_state(z, cols.rq, 1.0, do_ret, 0, cos_r, sin_r, p["af"], p["ab"], "bwd", "ret_state_bwd")
    d_rq, d_rk, d_rv, d_decay = _ret_bwd(z, do_ret, cos_r, sin_r, sv["sf"], sv["sb"], ef, eb,
                                         p["af"], p["ab"], rcols)
    dpc, d_pg, d_pscale, g_pool_w = _pool_bwd_a(z, p["pool_w"], p["pool_scale"], sv["y_pool"], dus["pool"],
                                                cols.pv, cols.pg)
    d_pv = _pool_bwd_b(dpc)
    grads["pool_w"] = g_pool_w.astype(ACT)
    dqn, dkn, dvv, d_ag, d_sink = _attn_bwd(sv["qn"], sv["kn"], sv["vb"], sv["o_att"], dus["att"], z,
                                            p["sink"], cols.ag * 2)
    d_aq, d_ak, d_av, d_qg, d_kg = _attn_prep_bwd(z, dqn, dkn, dvv, p["q_gain"], p["k_gain"], cf, s_up, s_dn,
                                                  cols.aq // 4, cols.ak)
    dz = jnp.concatenate([d_rq, d_rk, d_rv, d_rg, d_pv, d_pg, d_aq, d_ak, d_av, d_ag, dmg0, dmg1, dmg2],
                         axis=1)
    grads["w_in_t"] = _matmul(sv["ht"], dz, "nn", ACT, 1024, 512, S, "dw_in", transpose_out=True)
    tok = on_grads(grads)
    dh = _matmul(dz, p["w_in_t"], "nn", F32, 1024, 512, 29 * 128, "dh", deps=() if tok is None else (tok,))
    dx_in, dxb_in, d_norm_g = _rmsnorm_bwd(sv["x"], p["norm_g"], dh, dx)
    misc = jnp.concatenate([d_decay[:, 0, 0], d_decay[:, 0, 1], d_sink[:, 0, :ATT_G].reshape(-1)])
    misc = jnp.pad(misc, (0, 128 - misc.shape[0]))[None, :]
    small = jnp.concatenate([d_norm_g, d_pscale, d_qg, d_kg, misc], axis=1)
    return dx_in, dxb_in, small


def _pack_small(norm_g, pool_scale, q_gain, k_gain, af, ab, sink):
    L = norm_g.shape[0]
    misc = jnp.concatenate([af, ab, sink], axis=1)
    misc = jnp.pad(misc, ((0, 0), (0, 128 - misc.shape[1])))
    return jnp.concatenate([norm_g, pool_scale, q_gain, k_gain, misc], axis=1)


def _unpack_small(a, D):
    Wb = D // 2
    o = np.cumsum([0, D, Wb, ATT_HD, ATT_HD])
    misc = a[:, o[4]:]
    return (a[:, o[0]:o[1]], a[:, o[1]:o[2]], a[:, o[2]:o[3]], a[:, o[3]:o[4]],
            misc[:, :RET_HEADS], misc[:, RET_HEADS:2 * RET_HEADS],
            misc[:, 2 * RET_HEADS:2 * RET_HEADS + ATT_Q])


def _local_step(x, t, n_layers, get_layer, on_grads, tabs, cols, first_dep=None):
    saved, layers = [], []
    after = first_dep
    for l in range(n_layers):
        p = get_layer(l, after)
        x, sv, p = _layer_fwd(x, p, tabs, cols, (first_dep,) if (l == 0 and first_dep is not None) else ())
        after = x
        layers.append(p)
        saved.append(sv)
    dx, dxb, lsum = _loss_head(x, t)
    smalls = []
    for l in reversed(range(n_layers)):
        dx, dxb, sm = _layer_bwd(dx, dxb, layers[l], saved[l], tabs, cols, functools.partial(on_grads, l))
        smalls.append(sm)
    return 0.5 * lsum[0, 0], dx, jnp.concatenate(smalls[::-1], axis=0)


WEIGHT_KEYS = ("w_in", "w_ret", "w_pool", "w_att", "w_out", "pool_w")


def kernel(x, norm_g, w_in, ret_decay_fwd, ret_decay_bwd, pool_w, pool_scale, attn_q_gain, attn_k_gain, attn_sink, w_ret, w_pool, w_att, w_out, loss_target, m_norm_g, m_w_in, m_ret_decay_fwd, m_ret_decay_bwd, m_pool_w, m_pool_scale, m_attn_q_gain, m_attn_k_gain, m_attn_sink, m_w_ret, m_w_pool, m_w_att, m_w_out, v_norm_g, v_w_in, v_ret_decay_fwd, v_ret_decay_bwd, v_pool_w, v_pool_scale, v_attn_q_gain, v_attn_k_gain, v_attn_sink, v_w_ret, v_w_pool, v_w_att, v_w_out):
    L = norm_g.shape[0]
    _, S, D = x.shape
    Wb = D // 2
    G = len(POOL_WINDOWS)
    cols = _Cols(D)
    tabs = _rope_tables(S)
    me = 4 * lax.axis_index("x") + 2 * lax.axis_index("y") + lax.axis_index("c")
    def tr(a):
        return jnp.transpose(a, (0, 2, 1))

    weights = dict(w_in=tr(w_in), w_ret=w_ret, w_pool=w_pool, w_att=w_att, w_out=w_out, pool_w=pool_w)

    gathers, tok = [], None
    for l in range(L):
        started = []
        for part, keys in (("a", WEIGHT_KEYS[:1]), ("b", WEIGHT_KEYS[1:])):
            shards = []
            for k in keys:
                w = weights[k][l]
                if tok is not None:
                    w = w + tok[0, 0]
                shards.append(w.astype(MXU))
            st = _exchange_start(shards, False, f"gather_start_{l}{part}")
            started.append(st)
            tok = st[-1]
        gathers.append(started)

    def cols_full(g, rows):
        return jnp.transpose(g, (1, 0, 2)).reshape(rows, -1)

    def get_layer(l, after):
        srcs, lands = _exchange_wait(gathers[l][0], after, False, f"gather_wait_{l}a")
        g_in = _own_slot(lands[0], srcs[0], me)

        def rest(z):
            srcs, lands = _exchange_wait(gathers[l][1], z, False, f"gather_wait_{l}b")
            g_ret, g_pool, g_att, g_out, g_pw = [_own_slot(ld, sr, me) for ld, sr in zip(lands, srcs)]
            return dict(
                w_ret=cols_full(g_ret, Wb), w_pool=cols_full(g_pool, Wb), w_att=cols_full(g_att, Wb),
                w_out=g_out.reshape(D, D),
                pool_w=jnp.transpose(g_pw, (1, 0, 2, 3)).reshape(G, POOL_GD, POOL_GD))

        return dict(
            norm_g=norm_g[l][None, :], w_in_t=g_in.reshape(-1, D), rest=rest,
            pool_scale=pool_scale[l][None, :], q_gain=attn_q_gain[l][None, :], k_gain=attn_k_gain[l][None, :],
            af=ret_decay_fwd[l], ab=ret_decay_bwd[l], sink=attn_sink[l])

    def col_slots(g, rows):
        return jnp.transpose(g.reshape(rows, N_DEV, -1), (1, 0, 2))

    scatters = {}

    def on_grads(l, g):
        slots = [g["w_in_t"].reshape(N_DEV, -1, D),
                 col_slots(g["w_ret"], Wb), col_slots(g["w_pool"], Wb),
                 col_slots(g["w_att"], Wb), g["w_out"].reshape(N_DEV, D // N_DEV, D),
                 jnp.transpose(g["pool_w"].reshape(G, N_DEV, POOL_GD // N_DEV, POOL_GD), (1, 0, 2, 3))]
        scatters[l] = _exchange_start(slots, True, f"scatter_start_{l}")
        return scatters[l][-1]

    loss_local, grad_x, small = _local_step(x[0], loss_target[0], L, get_layer, on_grads, tabs, cols, tok)
    small = small.at[0, -1].set(loss_local)

    moments = dict(w_in=(tr(m_w_in), tr(v_w_in)), w_ret=(m_w_ret, v_w_ret), w_pool=(m_w_pool, v_w_pool),
                   w_att=(m_w_att, v_w_att), w_out=(m_w_out, v_w_out), pool_w=(m_pool_w, v_pool_w))
    res = {k: [jnp.zeros(weights[k].shape, F32) for _ in range(4)] for k in WEIGHT_KEYS}
    after = grad_x
    small_all = None
    for l in reversed(range(L)):
        if l == 0:
            small_all, = _exchange([small], False, "gather_small_grads", deps=(after,))
            after = small_all
        srcs, lands = _exchange_wait(scatters[l], after, True, f"scatter_wait_{l}")
        firsts = []
        for k, ld, sr in zip(WEIGHT_KEYS, lands, srcs):
            parts = _own_slot(ld, lax.dynamic_index_in_dim(sr, me, 0, keepdims=False), me)
            m, v = moments[k]
            outs = _adamw(weights[k], l, parts, m, v, "adamw_" + k)
            res[k] = [lax.dynamic_update_index_in_dim(r, o, l, 0) for r, o in zip(res[k], outs)]
            firsts.append(outs[1].reshape(-1)[:1])
        after = jnp.concatenate(firsts)

    sw = _pack_small(norm_g, pool_scale, attn_q_gain, attn_k_gain, ret_decay_fwd, ret_decay_bwd, attn_sink)
    sm = _pack_small(m_norm_g, m_pool_scale, m_attn_q_gain, m_attn_k_gain, m_ret_decay_fwd, m_ret_decay_bwd,
                     m_attn_sink)
    sv_ = _pack_small(v_norm_g, v_pool_scale, v_attn_q_gain, v_attn_k_gain, v_ret_decay_fwd, v_ret_decay_bwd,
                      v_attn_sink)
    small_out = _adamw(sw[None], 0, small_all, sm[None], sv_[None], "adamw_small")
    loss = small_out[0][0, -1]
    small_res = [_unpack_small(a, D) for a in small_out]

    def ordered(i):
        ng, ps, qg, kg, af, ab, sk = small_res[i]
        return (ng, tr(res["w_in"][i]), af, ab, res["pool_w"][i], ps, qg, kg, sk,
                res["w_ret"][i], res["w_pool"][i], res["w_att"][i], res["w_out"][i])

    return (loss, grad_x[None], *ordered(0), *ordered(1), *ordered(2), *ordered(3))
```

```python
import functools

import numpy as np
import jax
import jax.numpy as jnp
from jax import lax
from jax.experimental import pallas as pl
from jax.experimental.pallas import tpu as pltpu

F32 = jnp.float32
MXU = jnp.bfloat16
ACT = jnp.bfloat16

N_DEV = 8
RMS_EPS = 1e-6
NEG_BIG = -1e30
RET_HEADS = 4
RET_HD = 256
CH = 128
RET_ROPE_BASE = 10000.0
POOL_WINDOWS = (2, 4, 8, 16)
POOL_GD = 256
POOL_PAD = 8
ATT_HD = 128
ATT_Q = 8
ATT_KV = 2
ATT_G = ATT_Q // ATT_KV
ATT_WIN = 128
ATT_BLK = 128
ATT_SPAN = 3 * ATT_BLK
ROPE_THETA = 500000.0
ROPE_HALF = 16

ADAM_LR = 0.001
ADAM_B1 = 0.9
ADAM_B2 = 0.999
ADAM_EPS = 1e-08
ADAM_WD = 0.01
ADAM_STEP = 10

VMEM_LIMIT = 48 * 1024 * 1024

NN = ((1,), (0,))
NT = ((1,), (1,))
TN = ((0,), (0,))


def _dot(a, b, dims):
    return lax.dot_general(a.astype(MXU), b.astype(MXU), (dims, ((), ())),
                           preferred_element_type=F32)


def _sigmoid(x):
    return 1.0 / (1.0 + jnp.exp(-x))


def _params(*sem):
    return pltpu.CompilerParams(dimension_semantics=sem, vmem_limit_bytes=VMEM_LIMIT)


def _sum_all(x):
    return jnp.sum(jnp.sum(x, axis=1, keepdims=True), axis=0, keepdims=True)


def _fiota(shape, dim):
    return lax.broadcasted_iota(jnp.int32, shape, dim).astype(F32)


SMEM_SPEC = pl.BlockSpec(memory_space=pltpu.SMEM)


def _matmul(a, b, mode, out_dtype, tm, tn, tk, name, res=None, deps=(), transpose_out=False):
    if mode == "tn":
        K, M = a.shape
    else:
        M, K = a.shape
    N = b.shape[0] if mode == "nt" else b.shape[1]
    tm, tn, tk = min(tm, M), min(tn, N), min(tk, K)
    assert M % tm == 0 and N % tn == 0 and K % tk == 0, (name, M, N, K, tm, tn, tk)
    nk = K // tk
    dims = {"nn": NN, "nt": NT, "tn": TN}[mode]
    a_spec = (pl.BlockSpec((tk, tm), lambda i, j, k: (k, i)) if mode == "tn"
              else pl.BlockSpec((tm, tk), lambda i, j, k: (i, k)))
    b_spec = (pl.BlockSpec((tn, tk), lambda i, j, k: (j, k)) if mode == "nt"
              else pl.BlockSpec((tk, tn), lambda i, j, k: (k, j)))
    o_spec = pl.BlockSpec((tm, tn), lambda i, j, k: (i, j))
    has_res = res is not None
    assert not (has_res and transpose_out)
    n_in = 2 + has_res + len(deps)

    def body(*refs):
        a_ref, b_ref = refs[:2]
        r_ref = refs[2] if has_res else None
        o_ref = refs[n_in]

        def finish(out):
            if has_res:
                out = out + r_ref[...]
            o_ref[...] = (out.T if transpose_out else out).astype(out_dtype)

        if nk == 1:
            finish(_dot(a_ref[...], b_ref[...], dims))
            return
        acc = refs[n_in + 1]
        k = pl.program_id(2)

        @pl.when(k == 0)
        def _():
            acc[...] = jnp.zeros_like(acc)

        acc[...] += _dot(a_ref[...], b_ref[...], dims)

        @pl.when(k == nk - 1)
        def _():
            finish(acc[...])

    ins = [a, b] + ([res] if has_res else []) + list(deps)
    in_specs = ([a_spec, b_spec] + ([o_spec] if has_res else [])
                + [pl.BlockSpec((8, 128), lambda i, j, k: (0, 0))] * len(deps))
    return pl.pallas_call(
        body, grid=(M // tm, N // tn, nk), in_specs=in_specs,
        out_specs=pl.BlockSpec((tn, tm), lambda i, j, k: (j, i)) if transpose_out else o_spec,
        out_shape=jax.ShapeDtypeStruct((N, M) if transpose_out else (M, N), out_dtype),
        scratch_shapes=[pltpu.VMEM((tm, tn), F32)] if nk > 1 else [], name=name,
        compiler_params=_params("parallel", "parallel", "arbitrary"))(*ins)


DEP_SPEC1 = pl.BlockSpec((8, 128), lambda i: (0, 0))


def _rmsnorm_fwd(x, g, deps=()):
    S, D = x.shape
    tm = min(512, S)
    assert S % tm == 0

    def body(x_ref, g_ref, *rest):
        h_ref, ht_ref = rest[-2:]
        xv = x_ref[...]
        r = lax.rsqrt(jnp.mean(xv * xv, axis=-1, keepdims=True) + RMS_EPS)
        hv = xv * r * g_ref[...]
        h_ref[...] = hv.astype(ACT)
        ht_ref[...] = hv.T.astype(ACT)

    row = pl.BlockSpec((tm, D), lambda i: (i, 0))
    return pl.pallas_call(
        body, grid=(S // tm,), in_specs=[row, pl.BlockSpec((1, D), lambda i: (0, 0))] + [DEP_SPEC1] * len(deps),
        out_specs=[row, pl.BlockSpec((D, tm), lambda i: (0, i))],
        out_shape=[jax.ShapeDtypeStruct((S, D), ACT), jax.ShapeDtypeStruct((D, S), ACT)], name="rmsnorm_fwd",
        compiler_params=_params("parallel"))(x, g, *deps)


def _rmsnorm_bwd(x, g, dh, dres):
    S, D = x.shape
    tm = min(256, S)

    def body(x_ref, g_ref, dh_ref, dr_ref, dx_ref, dxb_ref, dg_ref):
        xv = x_ref[...]
        r = lax.rsqrt(jnp.mean(xv * xv, axis=-1, keepdims=True) + RMS_EPS)
        xh = xv * r
        dhv = dh_ref[...]
        dxh = dhv * g_ref[...]
        dx = r * (dxh - xh * jnp.mean(dxh * xh, axis=-1, keepdims=True)) + dr_ref[...]
        dx_ref[...] = dx
        dxb_ref[...] = dx.astype(ACT)

        @pl.when(pl.program_id(0) == 0)
        def _():
            dg_ref[...] = jnp.zeros_like(dg_ref)

        dg_ref[...] += jnp.sum(dhv * xh, axis=0, keepdims=True)

    row = pl.BlockSpec((tm, D), lambda i: (i, 0))
    vec = pl.BlockSpec((1, D), lambda i: (0, 0))
    return pl.pallas_call(
        body, grid=(S // tm,), in_specs=[row, vec, row, row], out_specs=[row, row, vec],
        out_shape=[jax.ShapeDtypeStruct((S, D), F32), jax.ShapeDtypeStruct((S, D), ACT),
                   jax.ShapeDtypeStruct((1, D), F32)],
        name="rmsnorm_bwd", compiler_params=_params("arbitrary"))(x, g, dh, dres)


def _rot256(x, c, s):
    x1, x2 = x[:, :128], x[:, 128:]
    return jnp.concatenate([x1 * c - x2 * s, x2 * c + x1 * s], axis=1)


def _rot256_t(g, c, s):
    g1, g2 = g[:, :128], g[:, 128:]
    return jnp.concatenate([g1 * c + g2 * s, g2 * c - g1 * s], axis=1)


def _log_decay(a_ref, h, shape):
    return -jnp.exp(jnp.full(shape, a_ref[h], F32))


def _ret_state(xsrc, xbase, xscale, ysrc, ybase, cos, sin, af, ab, mode, name):
    S = xsrc.shape[0]
    nC = S // CH
    H = RET_HEADS
    W = H * RET_HD
    assert (xbase * RET_HD) % W == 0 and (ybase * RET_HD) % W == 0

    def body(x1_ref, y1_ref, c1_ref, s1_ref, x2_ref, y2_ref, c2_ref, s2_ref, af_ref, ab_ref,
             st1_ref, st2_ref, acc1, acc2):
        @pl.when(pl.program_id(0) == 0)
        def _():
            acc1[...] = jnp.zeros_like(acc1)
            acc2[...] = jnp.zeros_like(acc2)

        j = _fiota((CH, 1), 0)
        ca, sa, cb, sb_ = c1_ref[...], s1_ref[...], c2_ref[...], s2_ref[...]
        for h in range(H):
            sl = slice(h * RET_HD, (h + 1) * RET_HD)
            lgf = _log_decay(af_ref, h, (CH, 1))
            lgb = _log_decay(ab_ref, h, (CH, 1))
            if mode == "fwd":
                w1, d1 = jnp.exp(lgf * (CH - 1.0 - j)), jnp.exp(lgf[:1] * CH)
                w2, d2 = jnp.exp(lgb * j), jnp.exp(lgb[:1] * CH)
            else:
                w1, d1 = jnp.exp(lgb * (CH - j)), jnp.exp(lgb[:1] * CH)
                w2, d2 = jnp.exp(lgf * (j + 1.0)), jnp.exp(lgf[:1] * CH)
            xa = _rot256(x1_ref[:, sl], ca, sa) * xscale
            st1_ref[h] = acc1[h].astype(ACT)
            acc1[h] = d1 * acc1[h] + _dot(xa * w1, y1_ref[:, sl], TN)
            xb = _rot256(x2_ref[:, sl], cb, sb_) * xscale
            st2_ref[h] = acc2[h].astype(ACT)
            acc2[h] = d2 * acc2[h] + _dot(xb * w2, y2_ref[:, sl], TN)

    xcol, ycol = (xbase * RET_HD) // W, (ybase * RET_HD) // W
    in_specs = [
        pl.BlockSpec((CH, W), lambda c: (c, xcol)), pl.BlockSpec((CH, W), lambda c: (c, ycol)),
        pl.BlockSpec((CH, 128), lambda c: (c, 0)), pl.BlockSpec((CH, 128), lambda c: (c, 0)),
        pl.BlockSpec((CH, W), lambda c: (nC - 1 - c, xcol)), pl.BlockSpec((CH, W), lambda c: (nC - 1 - c, ycol)),
        pl.BlockSpec((CH, 128), lambda c: (nC - 1 - c, 0)),
        pl.BlockSpec((CH, 128), lambda c: (nC - 1 - c, 0)),
        SMEM_SPEC, SMEM_SPEC]
    out_specs = [pl.BlockSpec((H, None, RET_HD, RET_HD), lambda c: (0, c, 0, 0)),
                 pl.BlockSpec((H, None, RET_HD, RET_HD), lambda c: (0, nC - 1 - c, 0, 0))]
    st = jax.ShapeDtypeStruct((H, nC, RET_HD, RET_HD), ACT)
    return pl.pallas_call(
        body, grid=(nC,), in_specs=in_specs, out_specs=out_specs, out_shape=[st, st],
        scratch_shapes=[pltpu.VMEM((H, RET_HD, RET_HD), F32), pltpu.VMEM((H, RET_HD, RET_HD), F32)],
        name=name, compiler_params=_params("arbitrary"))(
            xsrc, ysrc, cos, sin, xsrc, ysrc, cos, sin, af, ab)


def _decay_mask(lgf1, lgb1):
    lag = _fiota((CH, CH), 0) - _fiota((CH, CH), 1)
    alag = jnp.abs(lag)
    return lag, jnp.where(lag >= 0, jnp.exp(lgf1 * alag), jnp.exp(lgb1 * alag))


def _ret_fwd(z, cos, sin, sf, sb, af, ab, cols):
    S = z.shape[0]
    nC = S // CH
    H = RET_HEADS
    rq, rk, rv, rg = cols
    W = H * RET_HD
    assert all((c * RET_HD) % W == 0 for c in cols)

    def body(q_ref, k_ref, v_ref, g_ref, c_ref, s_ref, sf_ref, sb_ref, af_ref, ab_ref,
             o_ref, u_ref, ut_ref):
        j = _fiota((CH, 1), 0)
        c, s = c_ref[...], s_ref[...]
        for h in range(H):
            sl = slice(h * RET_HD, (h + 1) * RET_HD)
            lgf = _log_decay(af_ref, h, (CH, 1))
            lgb = _log_decay(ab_ref, h, (CH, 1))
            q = _rot256(q_ref[:, sl], c, s)
            k = _rot256(k_ref[:, sl], c, s) * (RET_HD ** -0.5)
            _, dm = _decay_mask(lgf[:1], lgb[:1])
            p = _dot(q, k, NT) * dm
            o = (_dot(p, v_ref[:, sl], NN)
                 + _dot(q * jnp.exp(lgf * (j + 1.0)), sf_ref[h], NN)
                 + _dot(q * jnp.exp(lgb * (CH - j)), sb_ref[h], NN))
            o_ref[:, sl] = o
            on = o * lax.rsqrt(jnp.mean(o * o, axis=-1, keepdims=True) + RMS_EPS)
            g = g_ref[:, sl]
            u = on * (g * _sigmoid(g))
            u_ref[:, sl] = u.astype(ACT)
            ut_ref[sl, :] = u.T.astype(ACT)

    def zc(col):
        return pl.BlockSpec((CH, W), lambda c: (c, (col * RET_HD) // W))

    tab = pl.BlockSpec((CH, 128), lambda c: (c, 0))
    stt = pl.BlockSpec((H, None, RET_HD, RET_HD), lambda c: (0, c, 0, 0))
    out = pl.BlockSpec((CH, W), lambda c: (c, 0))
    return pl.pallas_call(
        body, grid=(nC,),
        in_specs=[zc(rq), zc(rk), zc(rv), zc(rg), tab, tab, stt, stt, SMEM_SPEC, SMEM_SPEC],
        out_specs=[out, out, pl.BlockSpec((W, CH), lambda c: (0, c))],
        out_shape=[jax.ShapeDtypeStruct((S, W), F32), jax.ShapeDtypeStruct((S, W), ACT),
                   jax.ShapeDtypeStruct((W, S), ACT)],
        name="ret_fwd", compiler_params=_params("parallel"))(
            z, z, z, z, cos, sin, sf, sb, af, ab)


def _ret_gate_bwd(du, o_pre, z, rg):
    S, W = du.shape
    H = RET_HEADS
    tm = min(512, S)
    assert S % tm == 0

    def body(du_ref, o_ref, g_ref, do_ref, dg_ref):
        o = o_ref[...]
        r = lax.rsqrt(jnp.mean(o * o, axis=-1, keepdims=True) + RMS_EPS)
        on = o * r
        g = g_ref[...]
        sg = _sigmoid(g)
        duv = du_ref[...]
        don = duv * (g * sg)
        dg_ref[...] = (duv * on * (sg * (1.0 + g * (1.0 - sg)))).astype(ACT)
        do_ref[...] = r * (don - on * jnp.mean(don * on, axis=-1, keepdims=True))

    blk = pl.BlockSpec((tm, RET_HD), lambda i, h: (i, h))
    return pl.pallas_call(
        body, grid=(S // tm, H),
        in_specs=[blk, blk, pl.BlockSpec((tm, RET_HD), lambda i, h: (i, rg + h))],
        out_specs=[blk, blk],
        out_shape=[jax.ShapeDtypeStruct((S, W), F32), jax.ShapeDtypeStruct((S, W), ACT)],
        name="ret_gate_bwd", compiler_params=_params("parallel", "parallel"))(du, o_pre, z)


def _ret_bwd(z, do, cos, sin, sf, sb, ef, eb, af, ab, cols):
    S = z.shape[0]
    nC = S // CH
    H = RET_HEADS
    rq, rk, rv, _ = cols
    W = H * RET_HD

    def body(q_ref, k_ref, v_ref, do_ref, c_ref, s_ref, sf_ref, sb_ref, ef_ref, eb_ref,
             af_ref, ab_ref, dq_ref, dk_ref, dv_ref, da_ref):
        @pl.when(pl.program_id(0) == 0)
        def _():
            da_ref[...] = jnp.zeros_like(da_ref)

        j = _fiota((CH, 1), 0)
        c, s = c_ref[...], s_ref[...]
        scale = RET_HD ** -0.5
        row = lax.broadcasted_iota(jnp.int32, (8, 128), 0)
        lane = lax.broadcasted_iota(jnp.int32, (8, 128), 1)
        for h in range(H):
            sl = slice(h * RET_HD, (h + 1) * RET_HD)
            lgf = _log_decay(af_ref, h, (CH, 1))
            lgb = _log_decay(ab_ref, h, (CH, 1))
            q = _rot256(q_ref[:, sl], c, s)
            k = _rot256(k_ref[:, sl], c, s) * scale
            v = v_ref[:, sl]
            do = do_ref[:, sl]
            sf_, sb_, ef_, eb_ = sf_ref[h], sb_ref[h], ef_ref[h], eb_ref[h]
            a_w = jnp.exp(lgf * (j + 1.0))
            b_w = jnp.exp(lgb * (CH - j))
            wf = jnp.exp(lgf * (CH - 1.0 - j))
            wb = jnp.exp(lgb * j)
            lag, dm = _decay_mask(lgf[:1], lgb[:1])
            sc = _dot(q, k, NT)
            gg = _dot(do, v, NT)
            dg = gg * dm
            x1 = _dot(do, sf_, NT) * a_w
            x2 = _dot(do, sb_, NT) * b_w
            y1 = _dot(v, ef_, NT) * wf
            y2 = _dot(v, eb_, NT) * wb
            dq = _dot(dg, k, NN) + x1 + x2
            dk = _dot(dg, q, TN) + y1 + y2
            dv = _dot(sc * dm, do, TN) + _dot(k * wf, ef_, NN) + _dot(k * wb, eb_, NN)
            dq_ref[:, sl] = _rot256_t(dq, c, s).astype(ACT)
            dk_ref[:, sl] = (_rot256_t(dk, c, s) * scale).astype(ACT)
            dv_ref[:, sl] = dv.astype(ACT)
            t = dm * gg * sc
            qx1 = jnp.sum(q * x1, axis=-1, keepdims=True)
            qx2 = jnp.sum(q * x2, axis=-1, keepdims=True)
            ky1 = jnp.sum(k * y1, axis=-1, keepdims=True)
            ky2 = jnp.sum(k * y2, axis=-1, keepdims=True)
            dlf = (_sum_all(jnp.where(lag > 0, lag * t, 0.0))
                   + _sum_all((j + 1.0) * qx1 + (CH - 1.0 - j) * ky1)
                   + CH * jnp.exp(lgf[:1] * CH) * _sum_all(ef_.astype(F32) * sf_.astype(F32)))
            dlb = (_sum_all(jnp.where(lag < 0, -lag * t, 0.0))
                   + _sum_all((CH - j) * qx2 + j * ky2)
                   + CH * jnp.exp(lgb[:1] * CH) * _sum_all(eb_.astype(F32) * sb_.astype(F32)))
            da_ref[h] += jnp.where((row == 0) & (lane == 0), dlf * lgf[:1],
                                   jnp.where((row == 0) & (lane == 1), dlb * lgb[:1], 0.0))

    def zc(col):
        return pl.BlockSpec((CH, W), lambda c: (c, (col * RET_HD) // W))

    tab = pl.BlockSpec((CH, 128), lambda c: (c, 0))
    stt = pl.BlockSpec((H, None, RET_HD, RET_HD), lambda c: (0, c, 0, 0))
    out = pl.BlockSpec((CH, W), lambda c: (c, 0))
    dz = jax.ShapeDtypeStruct((S, W), ACT)
    return pl.pallas_call(
        body, grid=(nC,),
        in_specs=[zc(rq), zc(rk), zc(rv), out, tab, tab, stt, stt, stt, stt, SMEM_SPEC, SMEM_SPEC],
        out_specs=[out, out, out, pl.BlockSpec((H, 8, 128), lambda c: (0, 0, 0))],
        out_shape=[dz, dz, dz, jax.ShapeDtypeStruct((H, 8, 128), F32)],
        name="ret_bwd", compiler_params=_params("arbitrary"))(
            z, z, z, do, cos, sin, sf, sb, ef, eb, af, ab)


def _fill_padded(pad_ref, src_ref, S):
    zeros = jnp.zeros((POOL_PAD, POOL_GD), F32)
    pad_ref[pl.ds(0, POOL_PAD), :] = zeros
    pad_ref[pl.ds(POOL_PAD, S), :] = src_ref[...]
    pad_ref[pl.ds(S + POOL_PAD, POOL_PAD), :] = zeros


def _window_sum(ext, T, lo, hi):
    n = T + 2 * POOL_PAD
    acc = None
    for k in range(lo, hi):
        sh = ext if k == 0 else pltpu.roll(ext, (-k) % n, 0)
        piece = sh[POOL_PAD:POOL_PAD + T]
        acc = piece if acc is None else acc + piece
    return acc


def _window_count(pos, w, S):
    lo = jnp.maximum(pos - w // 2, 0)
    hi = jnp.minimum(pos + w // 2, S)
    return (hi - lo).astype(F32)


def _pool_fwd(z, pw, scale, pv, pg):
    S = z.shape[0]
    G = len(POOL_WINDOWS)
    T = min(512, S)
    W = G * POOL_GD

    def body(x_ref, g_ref, pw_ref, sc_ref, y_ref, u_ref, ut_ref, pad, p_scr):
        grp = pl.program_id(0)
        i = pl.program_id(1)

        @pl.when(i == 0)
        def _():
            _fill_padded(pad, x_ref, S)

        r0 = pl.multiple_of(i * T, T)
        ext = pad[pl.ds(r0, T + 2 * POOL_PAD), :]
        pos = r0 + lax.broadcasted_iota(jnp.int32, (T, 1), 0)
        for gi, w in enumerate(POOL_WINDOWS):
            @pl.when(grp == gi)
            def _(w=w):
                acc = _window_sum(ext, T, -(w // 2), w // 2)
                p_scr[...] = acc / _window_count(pos, w, S) - ext[POOL_PAD:POOL_PAD + T]

        y = _dot(p_scr[...], pw_ref[...], NN)
        y_ref[...] = y
        g = g_ref[...]
        u = y * sc_ref[...] * (g * _sigmoid(g))
        u_ref[...] = u.astype(ACT)
        ut_ref[...] = u.T.astype(ACT)

    blk = pl.BlockSpec((T, POOL_GD), lambda g, i: (i, g))
    return pl.pallas_call(
        body, grid=(G, S // T),
        in_specs=[pl.BlockSpec((S, POOL_GD), lambda g, i: (0, pv + g)),
                  pl.BlockSpec((T, POOL_GD), lambda g, i: (i, pg + g)),
                  pl.BlockSpec((None, POOL_GD, POOL_GD), lambda g, i: (g, 0, 0)),
                  pl.BlockSpec((1, POOL_GD), lambda g, i: (0, g))],
        out_specs=[blk, blk, pl.BlockSpec((POOL_GD, T), lambda g, i: (g, i))],
        out_shape=[jax.ShapeDtypeStruct((S, W), F32), jax.ShapeDtypeStruct((S, W), ACT),
                   jax.ShapeDtypeStruct((W, S), ACT)],
        scratch_shapes=[pltpu.VMEM((S + 2 * POOL_PAD, POOL_GD), F32), pltpu.VMEM((T, POOL_GD), F32)],
        name="pool_fwd", compiler_params=_params("parallel", "arbitrary"))(z, z, pw, scale)


def _pool_bwd_a(z, pw, scale, y_raw, du, pv, pg):
    S = z.shape[0]
    G = len(POOL_WINDOWS)
    T = min(512, S)
    W = G * POOL_GD

    def body(x_ref, g_ref, pw_ref, sc_ref, y_ref, du_ref, dpc_ref, dg_ref, dsc_ref, dpw_ref,
             pad, p_scr, c_scr):
        grp = pl.program_id(0)
        i = pl.program_id(1)

        @pl.when(i == 0)
        def _():
            _fill_padded(pad, x_ref, S)
            dsc_ref[...] = jnp.zeros_like(dsc_ref)
            dpw_ref[...] = jnp.zeros_like(dpw_ref)

        r0 = pl.multiple_of(i * T, T)
        ext = pad[pl.ds(r0, T + 2 * POOL_PAD), :]
        pos = r0 + lax.broadcasted_iota(jnp.int32, (T, 1), 0)
        for gi, w in enumerate(POOL_WINDOWS):
            @pl.when(grp == gi)
            def _(w=w):
                cnt = _window_count(pos, w, S)
                acc = _window_sum(ext, T, -(w // 2), w // 2)
                p_scr[...] = acc / cnt - ext[POOL_PAD:POOL_PAD + T]
                c_scr[...] = jnp.broadcast_to(cnt, (T, 128))

        g = g_ref[...]
        sg = _sigmoid(g)
        duv = du_ref[...]
        y = y_ref[...]
        scl = sc_ref[...]
        dy = duv * (scl * (g * sg))
        dg_ref[...] = (duv * y * scl * (sg * (1.0 + g * (1.0 - sg)))).astype(ACT)
        dsc_ref[...] += jnp.sum(duv * y * (g * sg), axis=0, keepdims=True)
        dpw_ref[...] += _dot(p_scr[...], dy, TN)
        dpc_ref[...] = _dot(dy, pw_ref[...], NT) / c_scr[:, :1]

    blk = pl.BlockSpec((T, POOL_GD), lambda g, i: (i, g))
    return pl.pallas_call(
        body, grid=(G, S // T),
        in_specs=[pl.BlockSpec((S, POOL_GD), lambda g, i: (0, pv + g)),
                  pl.BlockSpec((T, POOL_GD), lambda g, i: (i, pg + g)),
                  pl.BlockSpec((None, POOL_GD, POOL_GD), lambda g, i: (g, 0, 0)),
                  pl.BlockSpec((1, POOL_GD), lambda g, i: (0, g)), blk, blk],
        out_specs=[blk, blk, pl.BlockSpec((1, POOL_GD), lambda g, i: (0, g)),
                   pl.BlockSpec((None, POOL_GD, POOL_GD), lambda g, i: (g, 0, 0))],
        out_shape=[jax.ShapeDtypeStruct((S, W), F32), jax.ShapeDtypeStruct((S, W), ACT),
                   jax.ShapeDtypeStruct((1, W), F32), jax.ShapeDtypeStruct((G, POOL_GD, POOL_GD), F32)],
        scratch_shapes=[pltpu.VMEM((S + 2 * POOL_PAD, POOL_GD), F32), pltpu.VMEM((T, POOL_GD), F32),
                        pltpu.VMEM((T, 128), F32)],
        name="pool_bwd_a", compiler_params=_params("parallel", "arbitrary"))(z, z, pw, scale, y_raw, du)


def _pool_bwd_b(dpc):
    S, W = dpc.shape
    G = len(POOL_WINDOWS)
    T = min(512, S)

    def body(x_ref, o_ref, pad, acc_scr):
        grp = pl.program_id(0)
        i = pl.program_id(1)

        @pl.when(i == 0)
        def _():
            _fill_padded(pad, x_ref, S)

        r0 = pl.multiple_of(i * T, T)
        ext = pad[pl.ds(r0, T + 2 * POOL_PAD), :]
        pos = r0 + lax.broadcasted_iota(jnp.int32, (T, 1), 0)
        for gi, w in enumerate(POOL_WINDOWS):
            @pl.when(grp == gi)
            def _(w=w):
                acc = _window_sum(ext, T, -(w // 2) + 1, w // 2 + 1)
                acc_scr[...] = acc - ext[POOL_PAD:POOL_PAD + T] * _window_count(pos, w, S)

        o_ref[...] = acc_scr[...].astype(ACT)

    blk = pl.BlockSpec((T, POOL_GD), lambda g, i: (i, g))
    return pl.pallas_call(
        body, grid=(G, S // T),
        in_specs=[pl.BlockSpec((S, POOL_GD), lambda g, i: (0, g))], out_specs=blk,
        out_shape=jax.ShapeDtypeStruct((S, W), ACT),
        scratch_shapes=[pltpu.VMEM((S + 2 * POOL_PAD, POOL_GD), F32), pltpu.VMEM((T, POOL_GD), F32)],
        name="pool_bwd_b", compiler_params=_params("parallel", "arbitrary"))(dpc)


def _rope128(x, cf, sa, sb):
    return x * cf + pltpu.roll(x, ROPE_HALF, 1) * sa + pltpu.roll(x, ATT_HD - ROPE_HALF, 1) * sb


def _rope128_t(g, cf, sa, sb):
    return g * cf + pltpu.roll(g * sa, ATT_HD - ROPE_HALF, 1) + pltpu.roll(g * sb, ROPE_HALF, 1)


def _attn_prep(z, qgain, kgain, cf, sa, sb, aq, ak, av):
    S = z.shape[0]
    T = min(512, S)
    QW, KW = ATT_Q * ATT_HD, ATT_KV * ATT_HD

    def body(q_ref, k_ref, v_ref, qg_ref, kg_ref, cf_ref, sa_ref, sb_ref, qn_ref, kn_ref, vb_ref):
        cfv, sav, sbv = cf_ref[...], sa_ref[...], sb_ref[...]

        def prep(x, gain):
            r = lax.rsqrt(jnp.mean(x * x, axis=-1, keepdims=True) + RMS_EPS)
            return _rope128(x * r * gain, cfv, sav, sbv)

        for hh in range(ATT_Q):
            sl = slice(hh * ATT_HD, (hh + 1) * ATT_HD)
            qn_ref[:, sl] = prep(q_ref[:, sl], qg_ref[...]).astype(ACT)
        for hh in range(ATT_KV):
            sl = slice(hh * ATT_HD, (hh + 1) * ATT_HD)
            kn_ref[:, sl] = prep(k_ref[:, sl], kg_ref[...]).astype(ACT)
        vb_ref[...] = v_ref[...].astype(ACT)

    tab = pl.BlockSpec((T, ATT_HD), lambda i: (i, 0))
    gain = pl.BlockSpec((1, ATT_HD), lambda i: (0, 0))
    return pl.pallas_call(
        body, grid=(S // T,),
        in_specs=[pl.BlockSpec((T, QW), lambda i: (i, aq)), pl.BlockSpec((T, KW), lambda i: (i, ak)),
                  pl.BlockSpec((T, KW), lambda i: (i, av)), gain, gain, tab, tab, tab],
        out_specs=[pl.BlockSpec((T, QW), lambda i: (i, 0)), pl.BlockSpec((T, KW), lambda i: (i, 0)),
                   pl.BlockSpec((T, KW), lambda i: (i, 0))],
        out_shape=[jax.ShapeDtypeStruct((S, QW), ACT), jax.ShapeDtypeStruct((S, KW), ACT),
                   jax.ShapeDtypeStruct((S, KW), ACT)],
        name="attn_prep", compiler_params=_params("parallel"))(z, z, z, qgain, kgain, cf, sa, sb)


def _attn_window(i, S):
    start = jnp.clip(i * ATT_BLK - ATT_BLK, 0, S - ATT_SPAN)
    start = pl.multiple_of(start, ATT_BLK)
    qpos = i * ATT_BLK + lax.broadcasted_iota(jnp.int32, (ATT_BLK, ATT_SPAN), 0)
    kpos = start + lax.broadcasted_iota(jnp.int32, (ATT_BLK, ATT_SPAN), 1)
    return start, jnp.abs(kpos - qpos) <= ATT_WIN


def _attn_probs(q, kw, valid, sink):
    s = _dot(q, kw, NT) * (ATT_HD ** -0.5)
    s = jnp.where(valid, s, NEG_BIG)
    m = jnp.maximum(jnp.max(s, axis=-1, keepdims=True), sink)
    p = jnp.exp(s - m)
    es = jnp.exp(sink - m)
    den = jnp.sum(p, axis=-1, keepdims=True) + es
    return p / den, es / den


def _attn_fwd(qn, kn, vb, z, sink, ag):
    S = qn.shape[0]
    nB = S // ATT_BLK
    assert S >= ATT_SPAN
    QW = ATT_Q * ATT_HD

    GW = ATT_G * ATT_HD

    def body(q_ref, k_ref, v_ref, g_ref, sink_ref, o_ref, u_ref, ut_ref):
        kvh = pl.program_id(0)
        i = pl.program_id(1)
        start, valid = _attn_window(i, S)
        kw = k_ref[pl.ds(start, ATT_SPAN), :]
        vw = v_ref[pl.ds(start, ATT_SPAN), :]
        for gi in range(ATT_G):
            sl = slice(gi * ATT_HD, (gi + 1) * ATT_HD)
            sk = jnp.full((ATT_BLK, 1), sink_ref[kvh * ATT_G + gi], F32)
            pn, _ = _attn_probs(q_ref[:, sl], kw, valid, sk)
            o = _dot(pn, vw, NN)
            o_ref[:, sl] = o
            g = g_ref[:, sl]
            u = o * (g * _sigmoid(g))
            u_ref[:, sl] = u.astype(ACT)
            ut_ref[sl, :] = u.T.astype(ACT)

    blk = pl.BlockSpec((ATT_BLK, GW), lambda k, i: (i, k))
    kv = pl.BlockSpec((S, ATT_HD), lambda k, i: (0, k))
    return pl.pallas_call(
        body, grid=(ATT_KV, nB),
        in_specs=[blk, kv, kv, pl.BlockSpec((ATT_BLK, GW), lambda k, i: (i, ag // ATT_G + k)), SMEM_SPEC],
        out_specs=[blk, blk, pl.BlockSpec((GW, ATT_BLK), lambda k, i: (k, i))],
        out_shape=[jax.ShapeDtypeStruct((S, QW), F32), jax.ShapeDtypeStruct((S, QW), ACT),
                   jax.ShapeDtypeStruct((QW, S), ACT)],
        name="attn_fwd", compiler_params=_params("parallel", "parallel"))(qn, kn, vb, z, sink)


def _attn_bwd(qn, kn, vb, o, du, z, sink, ag):
    S = qn.shape[0]
    nB = S // ATT_BLK
    QW, KW = ATT_Q * ATT_HD, ATT_KV * ATT_HD
    GW = ATT_G * ATT_HD

    def body(q_ref, k_ref, v_ref, o_ref, du_ref, g_ref, sink_ref,
             dq_ref, dk_ref, dv_ref, dg_ref, ds_ref):
        kvh = pl.program_id(0)
        i = pl.program_id(1)

        @pl.when(i == 0)
        def _():
            dk_ref[...] = jnp.zeros_like(dk_ref)
            dv_ref[...] = jnp.zeros_like(dv_ref)
            ds_ref[...] = jnp.zeros_like(ds_ref)

        start, valid = _attn_window(i, S)
        kw = k_ref[pl.ds(start, ATT_SPAN), :]
        vw = v_ref[pl.ds(start, ATT_SPAN), :]
        row = lax.broadcasted_iota(jnp.int32, (8, 128), 0)
        lane = lax.broadcasted_iota(jnp.int32, (8, 128), 1)
        dsink = jnp.zeros((8, 128), F32)
        dk_acc = jnp.zeros((ATT_SPAN, ATT_HD), F32)
        dv_acc = jnp.zeros((ATT_SPAN, ATT_HD), F32)
        for gi in range(ATT_G):
            sl = slice(gi * ATT_HD, (gi + 1) * ATT_HD)
            q = q_ref[:, sl]
            ov = o_ref[:, sl]
            g = g_ref[:, sl]
            duv = du_ref[:, sl]
            sg = _sigmoid(g)
            do = duv * (g * sg)
            dg_ref[:, sl] = (duv * ov * (sg * (1.0 + g * (1.0 - sg)))).astype(ACT)
            sk = jnp.full((ATT_BLK, 1), sink_ref[kvh * ATT_G + gi], F32)
            pn, psink = _attn_probs(q, kw, valid, sk)
            delta = jnp.sum(do * ov, axis=-1, keepdims=True)
            dsc = pn * (_dot(do, vw, NT) - delta) * (ATT_HD ** -0.5)
            dq_ref[:, sl] = _dot(dsc, kw, NN)
            dk_acc = dk_acc + _dot(dsc, q, TN)
            dv_acc = dv_acc + _dot(pn, do, TN)
            dsink = dsink + jnp.where((row == 0) & (lane == gi), -_sum_all(psink * delta), 0.0)
        dk_ref[pl.ds(start, ATT_SPAN), :] += dk_acc
        dv_ref[pl.ds(start, ATT_SPAN), :] += dv_acc
        ds_ref[...] += dsink

    grp = pl.BlockSpec((ATT_BLK, GW), lambda k, i: (i, k))
    kv = pl.BlockSpec((S, ATT_HD), lambda k, i: (0, k))
    return pl.pallas_call(
        body, grid=(ATT_KV, nB),
        in_specs=[grp, kv, kv, grp, grp,
                  pl.BlockSpec((ATT_BLK, GW), lambda k, i: (i, ag // ATT_G + k)), SMEM_SPEC],
        out_specs=[grp, kv, kv, grp, pl.BlockSpec((None, 8, 128), lambda k, i: (k, 0, 0))],
        out_shape=[jax.ShapeDtypeStruct((S, QW), F32), jax.ShapeDtypeStruct((S, KW), F32),
                   jax.ShapeDtypeStruct((S, KW), F32), jax.ShapeDtypeStruct((S, QW), ACT),
                   jax.ShapeDtypeStruct((ATT_KV, 8, 128), F32)],
        name="attn_bwd", compiler_params=_params("parallel", "arbitrary"))(qn, kn, vb, o, du, z, sink)


def _attn_prep_bwd(z, dqn, dkn, dv, qgain, kgain, cf, sa, sb, aq, ak):
    S = z.shape[0]
    T = min(512, S)
    QW, KW = ATT_Q * ATT_HD, ATT_KV * ATT_HD

    def body(q_ref, k_ref, dqn_ref, dkn_ref, dv_ref, qg_ref, kg_ref, cf_ref, sa_ref, sb_ref,
             dq_ref, dk_ref, dvb_ref, dqg_ref, dkg_ref):
        cfv, sav, sbv = cf_ref[...], sa_ref[...], sb_ref[...]

        @pl.when(pl.program_id(0) == 0)
        def _():
            dqg_ref[...] = jnp.zeros_like(dqg_ref)
            dkg_ref[...] = jnp.zeros_like(dkg_ref)

        def back(x, gn, gain):
            r = lax.rsqrt(jnp.mean(x * x, axis=-1, keepdims=True) + RMS_EPS)
            xh = x * r
            dy = _rope128_t(gn, cfv, sav, sbv)
            dxh = dy * gain
            dx = r * (dxh - xh * jnp.mean(dxh * xh, axis=-1, keepdims=True))
            return dx, jnp.sum(dy * xh, axis=0, keepdims=True)

        dqg = jnp.zeros((1, ATT_HD), F32)
        for hh in range(ATT_Q):
            sl = slice(hh * ATT_HD, (hh + 1) * ATT_HD)
            dx, dgn = back(q_ref[:, sl], dqn_ref[:, sl], qg_ref[...])
            dq_ref[:, sl] = dx.astype(ACT)
            dqg = dqg + dgn
        dkg = jnp.zeros((1, ATT_HD), F32)
        for hh in range(ATT_KV):
            sl = slice(hh * ATT_HD, (hh + 1) * ATT_HD)
            dx, dgn = back(k_ref[:, sl], dkn_ref[:, sl], kg_ref[...])
            dk_ref[:, sl] = dx.astype(ACT)
            dkg = dkg + dgn
        dvb_ref[...] = dv_ref[...].astype(ACT)
        dqg_ref[...] += dqg
        dkg_ref[...] += dkg

    tab = pl.BlockSpec((T, ATT_HD), lambda i: (i, 0))
    gain = pl.BlockSpec((1, ATT_HD), lambda i: (0, 0))
    qb = pl.BlockSpec((T, QW), lambda i: (i, 0))
    kb = pl.BlockSpec((T, KW), lambda i: (i, 0))
    return pl.pallas_call(
        body, grid=(S // T,),
        in_specs=[pl.BlockSpec((T, QW), lambda i: (i, aq)), pl.BlockSpec((T, KW), lambda i: (i, ak)),
                  qb, kb, kb, gain, gain, tab, tab, tab],
        out_specs=[qb, kb, kb, gain, gain],
        out_shape=[jax.ShapeDtypeStruct((S, QW), ACT), jax.ShapeDtypeStruct((S, KW), ACT),
                   jax.ShapeDtypeStruct((S, KW), ACT), jax.ShapeDtypeStruct((1, ATT_HD), F32),
                   jax.ShapeDtypeStruct((1, ATT_HD), F32)],
        name="attn_prep_bwd", compiler_params=_params("arbitrary"))(
            z, z, dqn, dkn, dv, qgain, kgain, cf, sa, sb)


def _branch_merge(ua, ub, uc, wr, wp, wa, z, mg):
    S, W = ua.shape
    D = wr.shape[1]
    tm, tn = min(512, S), min(512, D)
    nb = D // tn

    def body(ua_ref, ub_ref, uc_ref, wr_ref, wp_ref, wa_ref, g0_ref, g1_ref, g2_ref,
             ya_ref, yb_ref, yc_ref, m_ref, mt_ref):
        ya = _dot(ua_ref[...], wr_ref[...], NN)
        yb = _dot(ub_ref[...], wp_ref[...], NN)
        yc = _dot(uc_ref[...], wa_ref[...], NN)
        ya_ref[...] = ya.astype(ACT)
        yb_ref[...] = yb.astype(ACT)
        yc_ref[...] = yc.astype(ACT)
        m = _sigmoid(g0_ref[...]) * ya + _sigmoid(g1_ref[...]) * yb + _sigmoid(g2_ref[...]) * yc
        m_ref[...] = m.astype(ACT)
        mt_ref[...] = m.T.astype(ACT)

    u = pl.BlockSpec((tm, W), lambda i, j: (i, 0))
    w = pl.BlockSpec((W, tn), lambda i, j: (0, j))
    o = pl.BlockSpec((tm, tn), lambda i, j: (i, j))

    assert (mg * POOL_GD) % tn == 0
    base = (mg * POOL_GD) // tn

    def gate(k):
        return pl.BlockSpec((tm, tn), lambda i, j: (i, base + k * nb + j))

    sd = jax.ShapeDtypeStruct((S, D), ACT)
    return pl.pallas_call(
        body, grid=(S // tm, nb), in_specs=[u, u, u, w, w, w, gate(0), gate(1), gate(2)],
        out_specs=[o, o, o, o, pl.BlockSpec((tn, tm), lambda i, j: (j, i))],
        out_shape=[sd, sd, sd, sd, jax.ShapeDtypeStruct((D, S), ACT)], name="branch_merge",
        compiler_params=_params("parallel", "parallel"))(ua, ub, uc, wr, wp, wa, z, z, z)


def _merge_bwd(dxb, wo, ya, yb, yc, z, mg):
    S, D = dxb.shape
    tm, tn = min(512, S), min(512, D)
    nb = D // tn
    base = (mg * POOL_GD) // tn

    def body(dx_ref, wo_ref, ya_ref, yb_ref, yc_ref, g0_ref, g1_ref, g2_ref,
             da_ref, db_ref, dc_ref, dg0_ref, dg1_ref, dg2_ref):
        dm = _dot(dx_ref[...], wo_ref[...], NT)
        for y_ref, g_ref, dy_ref, dg_ref in ((ya_ref, g0_ref, da_ref, dg0_ref),
                                             (yb_ref, g1_ref, db_ref, dg1_ref),
                                             (yc_ref, g2_ref, dc_ref, dg2_ref)):
            sg = _sigmoid(g_ref[...])
            dy_ref[...] = (sg * dm).astype(ACT)
            dg_ref[...] = (dm * y_ref[...].astype(F32) * (sg * (1.0 - sg))).astype(ACT)

    o = pl.BlockSpec((tm, tn), lambda i, j: (i, j))

    def gate(k):
        return pl.BlockSpec((tm, tn), lambda i, j: (i, base + k * nb + j))

    sd = jax.ShapeDtypeStruct((S, D), ACT)
    return pl.pallas_call(
        body, grid=(S // tm, nb),
        in_specs=[pl.BlockSpec((tm, D), lambda i, j: (i, 0)), pl.BlockSpec((tn, D), lambda i, j: (j, 0)),
                  o, o, o, gate(0), gate(1), gate(2)],
        out_specs=[o] * 6, out_shape=[sd] * 6, name="merge_bwd",
        compiler_params=_params("parallel", "parallel"))(dxb, wo, ya, yb, yc, z, z, z)


def _loss_head(y, t):
    S, D = y.shape
    tm = min(256, S)

    def body(y_ref, t_ref, dy_ref, dyb_ref, l_ref):
        e = y_ref[...] - t_ref[...]
        dy = e * (1.0 / D)
        dy_ref[...] = dy
        dyb_ref[...] = dy.astype(ACT)

        @pl.when(pl.program_id(0) == 0)
        def _():
            l_ref[...] = jnp.zeros_like(l_ref)

        l_ref[...] += jnp.sum(jnp.mean(e * e, axis=-1, keepdims=True), axis=0, keepdims=True)

    row = pl.BlockSpec((tm, D), lambda i: (i, 0))
    return pl.pallas_call(
        body, grid=(S // tm,), in_specs=[row, row],
        out_specs=[row, row, pl.BlockSpec((1, 1), lambda i: (0, 0))],
        out_shape=[jax.ShapeDtypeStruct((S, D), F32), jax.ShapeDtypeStruct((S, D), ACT),
                   jax.ShapeDtypeStruct((1, 1), F32)],
        name="loss_head", compiler_params=_params("arbitrary"))(y, t)


def _adamw(w, layer, parts, m, v, name):
    L = w.shape[0]
    shape = w.shape[1:]
    C = shape[-1]
    R = int(np.prod(shape[:-1]))
    w3, m3, v3 = (a.reshape(L, R, C) for a in (w, m, v))
    p3 = parts.reshape(N_DEV, R, C)
    tr = next((c for c in (1024, 512, 256, 128, 64) if R % c == 0 and c * C * 4 <= (1 << 20)), min(64, R))
    assert R % tr == 0
    c1 = 1.0 / (1.0 - ADAM_B1 ** ADAM_STEP)
    c2 = 1.0 / (1.0 - ADAM_B2 ** ADAM_STEP)

    def body(w_ref, p_ref, m_ref, v_ref, g_ref, d_ref, nm_ref, nv_ref):
        g = p_ref[0].astype(F32)
        for k in range(1, N_DEV):
            g = g + p_ref[k].astype(F32)
        nm = ADAM_B1 * m_ref[...] + (1.0 - ADAM_B1) * g
        nv = ADAM_B2 * v_ref[...] + (1.0 - ADAM_B2) * (g * g)
        g_ref[...] = g
        nm_ref[...] = nm
        nv_ref[...] = nv
        d_ref[...] = -ADAM_LR * ((nm * c1) / (jnp.sqrt(nv * c2) + ADAM_EPS) + ADAM_WD * w_ref[...])

    lay = pl.BlockSpec((None, tr, C), lambda i: (layer, i, 0))
    out = pl.BlockSpec((tr, C), lambda i: (i, 0))
    sd = jax.ShapeDtypeStruct((R, C), F32)
    outs = pl.pallas_call(
        body, grid=(R // tr,),
        in_specs=[lay, pl.BlockSpec((N_DEV, tr, C), lambda i: (0, i, 0)), lay, lay],
        out_specs=[out] * 4, out_shape=[sd] * 4, name=name,
        compiler_params=_params("parallel"))(w3, p3, m3, v3)
    return [a.reshape(shape) for a in outs]


def _exchange(arrs, scatter, name, deps=()):
    n = len(arrs)
    nd = len(deps)
    out_shape = [jax.ShapeDtypeStruct(a.shape if scatter else (N_DEV,) + a.shape, a.dtype) for a in arrs]

    def body(*refs):
        ins, outs = refs[:n], refs[n + nd:2 * n + nd]
        send_sems, recv_sems, local_sems = refs[2 * n + nd:]
        x, y, c = lax.axis_index("x"), lax.axis_index("y"), lax.axis_index("c")
        me = 4 * x + 2 * y + c
        copies = []
        for a in range(n):
            src = ins[a].at[me] if scatter else ins[a]
            own = pltpu.make_async_copy(src, outs[a].at[me], local_sems.at[a])
            own.start()
            copies.append(own)
        sends, recvs = [], []
        for k in range(1, N_DEV):
            px, py, pc = x ^ (k >> 2), y ^ ((k >> 1) & 1), c ^ (k & 1)
            peer = 4 * px + 2 * py + pc
            for a in range(n):
                src = ins[a].at[peer] if scatter else ins[a]
                cp = pltpu.make_async_remote_copy(
                    src_ref=src, dst_ref=outs[a].at[me],
                    send_sem=send_sems.at[a, k - 1], recv_sem=recv_sems.at[a, k - 1],
                    device_id=(px, py, pc), device_id_type=pl.DeviceIdType.MESH)
                cp.start()
                sends.append(cp)
                recvs.append(pltpu.make_async_remote_copy(
                    src_ref=src, dst_ref=outs[a].at[peer],
                    send_sem=send_sems.at[a, k - 1], recv_sem=recv_sems.at[a, k - 1],
                    device_id=(px, py, pc), device_id_type=pl.DeviceIdType.MESH))
        for cp in recvs:
            cp.wait_recv()
        for cp in sends:
            cp.wait_send()
        for cp in copies:
            cp.wait()

    any_spec = pl.BlockSpec(memory_space=pl.ANY)
    return pl.pallas_call(
        body, in_specs=[any_spec] * (n + nd), out_specs=[any_spec] * n, out_shape=out_shape,
        scratch_shapes=[pltpu.SemaphoreType.DMA((n, N_DEV - 1)), pltpu.SemaphoreType.DMA((n, N_DEV - 1)),
                        pltpu.SemaphoreType.DMA((n,))],
        name=name)(*arrs, *deps)


HBM_SPEC = pl.BlockSpec(memory_space=pltpu.HBM)
SEM_SPEC = pl.BlockSpec(memory_space=pltpu.SEMAPHORE)
DATAFLOW = pltpu.SideEffectType.DATAFLOW_SIDE_EFFECTING


def _peer_of(k):
    x, y, c = lax.axis_index("x"), lax.axis_index("y"), lax.axis_index("c")
    return x ^ (k >> 2), y ^ ((k >> 1) & 1), c ^ (k & 1)


def _exchange_copy(k, a, src_ref, land_ref, send_sems, recv_sems, scatter, outgoing):
    px, py, pc = _peer_of(k)
    peer = 4 * px + 2 * py + pc
    me = 4 * lax.axis_index("x") + 2 * lax.axis_index("y") + lax.axis_index("c")
    idx = a * (N_DEV - 1) + k - 1
    return pltpu.make_async_remote_copy(
        src_ref=src_ref.at[peer] if scatter else src_ref, dst_ref=land_ref.at[me if outgoing else peer],
        send_sem=send_sems.at[idx], recv_sem=recv_sems.at[idx],
        device_id=(px, py, pc), device_id_type=pl.DeviceIdType.MESH)


def _exchange_start(arrs, scatter, name):
    n = len(arrs)
    land_shapes = [a.shape if scatter else (N_DEV,) + a.shape for a in arrs]

    def body(*refs):
        srcs, lands = refs[:n], refs[n:2 * n]
        send_sems, recv_sems = refs[2 * n], refs[2 * n + 1]
        token = refs[-1]
        for k in range(1, N_DEV):
            for a in range(n):
                _exchange_copy(k, a, srcs[a], lands[a], send_sems, recv_sems, scatter, True).start()
        token[...] = jnp.zeros_like(token)

    sems = pltpu.SemaphoreType.DMA((n * (N_DEV - 1),))
    out_shape = ([sems, sems] + [pltpu.HBM(a.shape, a.dtype) for a in arrs]
                 + [pltpu.HBM(s, a.dtype) for s, a in zip(land_shapes, arrs)]
                 + [jax.ShapeDtypeStruct((8, 128), F32)])
    ins = ([pltpu.with_memory_space_constraint(a, pltpu.HBM) for a in arrs]
           + [pltpu.with_memory_space_constraint(lax.empty(s, a.dtype), pltpu.HBM) for s, a in zip(land_shapes, arrs)])
    res = pl.pallas_call(
        body, name=name, out_shape=out_shape, in_specs=[HBM_SPEC] * (2 * n),
        out_specs=[SEM_SPEC, SEM_SPEC] + [HBM_SPEC] * (2 * n) + [pl.BlockSpec(memory_space=pltpu.VMEM)],
        input_output_aliases={i: 2 + i for i in range(2 * n)},
        compiler_params=pltpu.CompilerParams(has_side_effects=DATAFLOW))(*ins)
    return res[0], res[1], list(res[2:2 + n]), list(res[2 + n:2 + 2 * n]), res[-1]


def _exchange_wait(started, after, scatter, name):
    send_sems, recv_sems, srcs, lands, _ = started
    n = len(srcs)

    def body(*refs):
        src_refs, land_refs = refs[:n], refs[n:2 * n]
        s_sems, r_sems = refs[2 * n], refs[2 * n + 1]
        for k in range(1, N_DEV):
            for a in range(n):
                back = _exchange_copy(k, a, src_refs[a], land_refs[a], s_sems, r_sems, scatter, False)
                back.wait_send()
                back.wait_recv()

    out_shape = [pltpu.HBM(a.shape, a.dtype) for a in srcs] + [pltpu.HBM(a.shape, a.dtype) for a in lands]
    res = pl.pallas_call(
        body, name=name, out_shape=out_shape,
        in_specs=[HBM_SPEC] * (2 * n) + [SEM_SPEC, SEM_SPEC, pl.BlockSpec(memory_space=pl.ANY)],
        out_specs=[HBM_SPEC] * (2 * n), input_output_aliases={i: i for i in range(2 * n)},
        compiler_params=pltpu.CompilerParams(has_side_effects=DATAFLOW))(*srcs, *lands, send_sems, recv_sems, after)
    return list(res[:n]), list(res[n:])


def _own_slot(land, own, me):
    return lax.dynamic_update_index_in_dim(land, own, me, 0)


class _Cols:
    def __init__(self, D):
        Wb = D // 2
        sizes = (Wb, Wb, Wb, Wb, Wb, Wb, Wb, ATT_KV * ATT_HD, ATT_KV * ATT_HD, Wb, 3 * D)
        offs = np.concatenate([[0], np.cumsum(sizes)])
        assert all(int(o) % 256 == 0 for o in offs)
        (self.rq, self.rk, self.rv, self.rg, self.pv, self.pg,
         self.aq, self.ak, self.av, self.ag, self.mg) = (int(o) // 256 for o in offs[:-1])
        self.width = int(offs[-1])
        self.sizes = sizes


def _rope_tables(S):
    pos = jnp.arange(S, dtype=F32)[:, None]
    inv_r = 1.0 / (RET_ROPE_BASE ** jnp.linspace(0.0, 1.0, RET_HD // 2, dtype=F32))
    ang_r = pos * inv_r[None, :]
    inv_a = ROPE_THETA ** (-jnp.arange(ROPE_HALF, dtype=F32) / ROPE_HALF)
    ang_a = pos * inv_a[None, :]
    ca, sa = jnp.cos(ang_a), jnp.sin(ang_a)
    z16 = jnp.zeros((S, ROPE_HALF), F32)
    rest = ATT_HD - 2 * ROPE_HALF
    cf = jnp.concatenate([ca, ca, jnp.ones((S, rest), F32)], axis=1)
    s_up = jnp.concatenate([z16, sa, jnp.zeros((S, rest), F32)], axis=1)
    s_dn = jnp.concatenate([-sa, z16, jnp.zeros((S, rest), F32)], axis=1)
    return jnp.cos(ang_r), jnp.sin(ang_r), cf, s_up, s_dn


def _layer_fwd(x, p, tabs, cols, deps=()):
    cos_r, sin_r, cf, s_up, s_dn = tabs
    S, D = x.shape
    h, ht = _rmsnorm_fwd(x, p["norm_g"], deps)
    z = _matmul(h, p["w_in_t"], "nt", F32, 2048, 512, D, "in_proj")
    if "rest" in p:
        p = {**{k: v for k, v in p.items() if k != "rest"}, **p["rest"](z)}
    rcols = (cols.rq, cols.rk, cols.rv, cols.rg)
    sf, sb = _ret_state(z, cols.rk, RET_HD ** -0.5, z, cols.rv, cos_r, sin_r, p["af"], p["ab"], "fwd",
                        "ret_state_fwd")
    o_ret, ua, uat = _ret_fwd(z, cos_r, sin_r, sf, sb, p["af"], p["ab"], rcols)
    y_pool, ub, ubt = _pool_fwd(z, p["pool_w"], p["pool_scale"], cols.pv, cols.pg)
    qn, kn, vb = _attn_prep(z, p["q_gain"], p["k_gain"], cf, s_up, s_dn,
                            cols.aq // 4, cols.ak, cols.av)
    o_att, uc, uct = _attn_fwd(qn, kn, vb, z, p["sink"], cols.ag * 2)
    ya, yb, yc, merged, mergedt = _branch_merge(ua, ub, uc, p["w_ret"], p["w_pool"], p["w_att"], z, cols.mg)
    out = _matmul(merged, p["w_out"], "nn", F32, 1024, 512, D, "out_proj", res=x)
    saved = dict(x=x, ht=ht, z=z, sf=sf, sb=sb, o_ret=o_ret, uat=uat, y_pool=y_pool, ubt=ubt,
                 qn=qn, kn=kn, vb=vb, o_att=o_att, uct=uct, ya=ya, yb=yb, yc=yc, mergedt=mergedt)
    return out, saved, p


def _layer_bwd(dx, dxb, p, sv, tabs, cols, on_grads):
    cos_r, sin_r, cf, s_up, s_dn = tabs
    z = sv["z"]
    S, D = dx.shape
    dya, dyb, dyc, dmg0, dmg1, dmg2 = _merge_bwd(dxb, p["w_out"], sv["ya"], sv["yb"], sv["yc"], z, cols.mg)
    grads = {"w_out": _matmul(sv["mergedt"], dxb, "nn", ACT, 1024, 512, S, "dw_out")}
    dus = {}
    for nm, ut, dy in (("ret", sv["uat"], dya), ("pool", sv["ubt"], dyb), ("att", sv["uct"], dyc)):
        dus[nm] = _matmul(dy, p["w_" + nm], "nt", F32, 1024, 512, D, "du_" + nm)
        grads["w_" + nm] = _matmul(ut, dy, "nn", ACT, 1024, 512, S, "dw_" + nm)
    rcols = (cols.rq, cols.rk, cols.rv, cols.rg)
    do_ret, d_rg = _ret_gate_bwd(dus["ret"], sv["o_ret"], z, cols.rg)
    eb, ef = _ret_state(z, cols.rq, 1.0, do_ret, 0, cos_r, sin_r, p["af"], p["ab"], "bwd", "ret_state_bwd")
    d_rq, d_rk, d_rv, d_decay = _ret_bwd(z, do_ret, cos_r, sin_r, sv["sf"], sv["sb"], ef, eb,
                                         p["af"], p["ab"], rcols)
    dpc, d_pg, d_pscale, g_pool_w = _pool_bwd_a(z, p["pool_w"], p["pool_scale"], sv["y_pool"], dus["pool"],
                                                cols.pv, cols.pg)
    d_pv = _pool_bwd_b(dpc)
    grads["pool_w"] = g_pool_w.astype(ACT)
    dqn, dkn, dvv, d_ag, d_sink = _attn_bwd(sv["qn"], sv["kn"], sv["vb"], sv["o_att"], dus["att"], z,
                                            p["sink"], cols.ag * 2)
    d_aq, d_ak, d_av, d_qg, d_kg = _attn_prep_bwd(z, dqn, dkn, dvv, p["q_gain"], p["k_gain"], cf, s_up, s_dn,
                                                  cols.aq // 4, cols.ak)
    dz = jnp.concatenate([d_rq, d_rk, d_rv, d_rg, d_pv, d_pg, d_aq, d_ak, d_av, d_ag, dmg0, dmg1, dmg2],
                         axis=1)
    grads["w_in_t"] = _matmul(sv["ht"], dz, "nn", ACT, 1024, 512, S, "dw_in", transpose_out=True)
    tok = on_grads(grads)
    dh = _matmul(dz, p["w_in_t"], "nn", F32, 1024, 512, 29 * 128, "dh", deps=() if tok is None else (tok,))
    dx_in, dxb_in, d_norm_g = _rmsnorm_bwd(sv["x"], p["norm_g"], dh, dx)
    misc = jnp.concatenate([d_decay[:, 0, 0], d_decay[:, 0, 1], d_sink[:, 0, :ATT_G].reshape(-1)])
    misc = jnp.pad(misc, (0, 128 - misc.shape[0]))[None, :]
    small = jnp.concatenate([d_norm_g, d_pscale, d_qg, d_kg, misc], axis=1)
    return dx_in, dxb_in, small


def _pack_small(norm_g, pool_scale, q_gain, k_gain, af, ab, sink):
    L = norm_g.shape[0]
    misc = jnp.concatenate([af, ab, sink], axis=1)
    misc = jnp.pad(misc, ((0, 0), (0, 128 - misc.shape[1])))
    return jnp.concatenate([norm_g, pool_scale, q_gain, k_gain, misc], axis=1)


def _unpack_small(a, D):
    Wb = D // 2
    o = np.cumsum([0, D, Wb, ATT_HD, ATT_HD])
    misc = a[:, o[4]:]
    return (a[:, o[0]:o[1]], a[:, o[1]:o[2]], a[:, o[2]:o[3]], a[:, o[3]:o[4]],
            misc[:, :RET_HEADS], misc[:, RET_HEADS:2 * RET_HEADS],
            misc[:, 2 * RET_HEADS:2 * RET_HEADS + ATT_Q])


def _local_step(x, t, n_layers, get_layer, on_grads, tabs, cols, first_dep=None):
    saved, layers = [], []
    after = first_dep
    for l in range(n_layers):
        p = get_layer(l, after)
        x, sv, p = _layer_fwd(x, p, tabs, cols, (first_dep,) if (l == 0 and first_dep is not None) else ())
        after = x
        layers.append(p)
        saved.append(sv)
    dx, dxb, lsum = _loss_head(x, t)
    smalls = []
    for l in reversed(range(n_layers)):
        dx, dxb, sm = _layer_bwd(dx, dxb, layers[l], saved[l], tabs, cols, functools.partial(on_grads, l))
        smalls.append(sm)
    return 0.5 * lsum[0, 0], dx, jnp.concatenate(smalls[::-1], axis=0)


WEIGHT_KEYS = ("w_in", "w_ret", "w_pool", "w_att", "w_out", "pool_w")


def kernel(x, norm_g, w_in, ret_decay_fwd, ret_decay_bwd, pool_w, pool_scale, attn_q_gain, attn_k_gain, attn_sink, w_ret, w_pool, w_att, w_out, loss_target, m_norm_g, m_w_in, m_ret_decay_fwd, m_ret_decay_bwd, m_pool_w, m_pool_scale, m_attn_q_gain, m_attn_k_gain, m_attn_sink, m_w_ret, m_w_pool, m_w_att, m_w_out, v_norm_g, v_w_in, v_ret_decay_fwd, v_ret_decay_bwd, v_pool_w, v_pool_scale, v_attn_q_gain, v_attn_k_gain, v_attn_sink, v_w_ret, v_w_pool, v_w_att, v_w_out):
    L = norm_g.shape[0]
    _, S, D = x.shape
    Wb = D // 2
    G = len(POOL_WINDOWS)
    cols = _Cols(D)
    tabs = _rope_tables(S)
    me = 4 * lax.axis_index("x") + 2 * lax.axis_index("y") + lax.axis_index("c")
    def tr(a):
        return jnp.transpose(a, (0, 2, 1))

    weights = dict(w_in=tr(w_in), w_ret=w_ret, w_pool=w_pool, w_att=w_att, w_out=w_out, pool_w=pool_w)

    gathers, tok = [], None
    for l in range(L):
        started = []
        for part, keys in (("a", WEIGHT_KEYS[:1]), ("b", WEIGHT_KEYS[1:])):
            shards = []
            for k in keys:
                w = weights[k][l]
                if tok is not None:
                    w = w + tok[0, 0]
                shards.append(w.astype(MXU))
            st = _exchange_start(shards, False, f"gather_start_{l}{part}")
            started.append(st)
            tok = st[-1]
        gathers.append(started)

    def cols_full(g, rows):
        return jnp.transpose(g, (1, 0, 2)).reshape(rows, -1)

    def get_layer(l, after):
        srcs, lands = _exchange_wait(gathers[l][0], after, False, f"gather_wait_{l}a")
        g_in = _own_slot(lands[0], srcs[0], me)

        def rest(z):
            srcs, lands = _exchange_wait(gathers[l][1], z, False, f"gather_wait_{l}b")
            g_ret, g_pool, g_att, g_out, g_pw = [_own_slot(ld, sr, me) for ld, sr in zip(lands, srcs)]
            return dict(
                w_ret=cols_full(g_ret, Wb), w_pool=cols_full(g_pool, Wb), w_att=cols_full(g_att, Wb),
                w_out=g_out.reshape(D, D),
                pool_w=jnp.transpose(g_pw, (1, 0, 2, 3)).reshape(G, POOL_GD, POOL_GD))

        return dict(
            norm_g=norm_g[l][None, :], w_in_t=g_in.reshape(-1, D), rest=rest,
            pool_scale=pool_scale[l][None, :], q_gain=attn_q_gain[l][None, :], k_gain=attn_k_gain[l][None, :],
            af=ret_decay_fwd[l], ab=ret_decay_bwd[l], sink=attn_sink[l])

    def col_slots(g, rows):
        return jnp.transpose(g.reshape(rows, N_DEV, -1), (1, 0, 2))

    scatters = {}

    def on_grads(l, g):
        slots = [g["w_in_t"].reshape(N_DEV, -1, D),
                 col_slots(g["w_ret"], Wb), col_slots(g["w_pool"], Wb),
                 col_slots(g["w_att"], Wb), g["w_out"].reshape(N_DEV, D // N_DEV, D),
                 jnp.transpose(g["pool_w"].reshape(G, N_DEV, POOL_GD // N_DEV, POOL_GD), (1, 0, 2, 3))]
        scatters[l] = _exchange_start(slots, True, f"scatter_start_{l}")
        return scatters[l][-1]

    loss_local, grad_x, small = _local_step(x[0], loss_target[0], L, get_layer, on_grads, tabs, cols, tok)
    small = small.at[0, -1].set(loss_local)

    moments = dict(w_in=(tr(m_w_in), tr(v_w_in)), w_ret=(m_w_ret, v_w_ret), w_pool=(m_w_pool, v_w_pool),
                   w_att=(m_w_att, v_w_att), w_out=(m_w_out, v_w_out), pool_w=(m_pool_w, v_pool_w))
    res = {k: [jnp.zeros(weights[k].shape, F32) for _ in range(4)] for k in WEIGHT_KEYS}
    after = grad_x
    small_all = None
    for l in reversed(range(L)):
        if l == 0:
            small_all, = _exchange([small], False, "gather_small_grads", deps=(after,))
            after = small_all
        srcs, lands = _exchange_wait(scatters[l], after, True, f"scatter_wait_{l}")
        firsts = []
        for k, ld, sr in zip(WEIGHT_KEYS, lands, srcs):
            parts = _own_slot(ld, lax.dynamic_index_in_dim(sr, me, 0, keepdims=False), me)
            m, v = moments[k]
            outs = _adamw(weights[k], l, parts, m, v, "adamw_" + k)
            res[k] = [lax.dynamic_update_index_in_dim(r, o, l, 0) for r, o in zip(res[k], outs)]
            firsts.append(outs[1].reshape(-1)[:1])
        after = jnp.concatenate(firsts)

    sw = _pack_small(norm_g, pool_scale, attn_q_gain, attn_k_gain, ret_decay_fwd, ret_decay_bwd, attn_sink)
    sm = _pack_small(m_norm_g, m_pool_scale, m_attn_q_gain, m_attn_k_gain, m_ret_decay_fwd, m_ret_decay_bwd,
                     m_attn_sink)
    sv_ = _pack_small(v_norm_g, v_pool_scale, v_attn_q_gain, v_attn_k_gain, v_ret_decay_fwd, v_ret_decay_bwd,
                      v_attn_sink)
    small_out = _adamw(sw[None], 0, small_all, sm[None], sv_[None], "adamw_small")
    loss = small_out[0][0, -1]
    small_res = [_unpack_small(a, D) for a in small_out]

    def ordered(i):
        ng, ps, qg, kg, af, ab, sk = small_res[i]
        return (ng, tr(res["w_in"][i]), af, ab, res["pool_w"][i], ps, qg, kg, sk,
                res["w_ret"][i], res["w_pool"][i], res["w_att"][i], res["w_out"][i])

    return (loss, grad_x[None], *ordered(0), *ordered(1), *ordered(2), *ordered(3))
```

```python
import functools

import numpy as np
import jax
import jax.numpy as jnp
from jax import lax
from jax.experimental import pallas as pl
from jax.experimental.pallas import tpu as pltpu

F32 = jnp.float32
MXU = jnp.bfloat16
ACT = jnp.bfloat16

N_DEV = 8
RMS_EPS = 1e-6
NEG_BIG = -1e30
RET_HEADS = 4
RET_HD = 256
CH = 128
RET_ROPE_BASE = 10000.0
POOL_WINDOWS = (2, 4, 8, 16)
POOL_GD = 256
POOL_PAD = 8
ATT_HD = 128
ATT_Q = 8
ATT_KV = 2
ATT_G = ATT_Q // ATT_KV
ATT_WIN = 128
ATT_BLK = 128
ATT_SPAN = 3 * ATT_BLK
ROPE_THETA = 500000.0
ROPE_HALF = 16

ADAM_LR = 0.001
ADAM_B1 = 0.9
ADAM_B2 = 0.999
ADAM_EPS = 1e-08
ADAM_WD = 0.01
ADAM_STEP = 10

VMEM_LIMIT = 48 * 1024 * 1024

NN = ((1,), (0,))
NT = ((1,), (1,))
TN = ((0,), (0,))


def _dot(a, b, dims):
    return lax.dot_general(a.astype(MXU), b.astype(MXU), (dims, ((), ())),
                           preferred_element_type=F32)


def _sigmoid(x):
    return 1.0 / (1.0 + jnp.exp(-x))


def _params(*sem):
    return pltpu.CompilerParams(dimension_semantics=sem, vmem_limit_bytes=VMEM_LIMIT)


def _sum_all(x):
    return jnp.sum(jnp.sum(x, axis=1, keepdims=True), axis=0, keepdims=True)


def _fiota(shape, dim):
    return lax.broadcasted_iota(jnp.int32, shape, dim).astype(F32)


SMEM_SPEC = pl.BlockSpec(memory_space=pltpu.SMEM)


def _matmul(a, b, mode, out_dtype, tm, tn, tk, name, res=None, deps=(), transpose_out=False):
    if mode == "tn":
        K, M = a.shape
    else:
        M, K = a.shape
    N = b.shape[0] if mode == "nt" else b.shape[1]
    tm, tn, tk = min(tm, M), min(tn, N), min(tk, K)
    assert M % tm == 0 and N % tn == 0 and K % tk == 0, (name, M, N, K, tm, tn, tk)
    nk = K // tk
    dims = {"nn": NN, "nt": NT, "tn": TN}[mode]
    a_spec = (pl.BlockSpec((tk, tm), lambda i, j, k: (k, i)) if mode == "tn"
              else pl.BlockSpec((tm, tk), lambda i, j, k: (i, k)))
    b_spec = (pl.BlockSpec((tn, tk), lambda i, j, k: (j, k)) if mode == "nt"
              else pl.BlockSpec((tk, tn), lambda i, j, k: (k, j)))
    o_spec = pl.BlockSpec((tm, tn), lambda i, j, k: (i, j))
    has_res = res is not None
    assert not (has_res and transpose_out)
    n_in = 2 + has_res + len(deps)

    def body(*refs):
        a_ref, b_ref = refs[:2]
        r_ref = refs[2] if has_res else None
        o_ref = refs[n_in]

        def finish(out):
            if has_res:
                out = out + r_ref[...]
            o_ref[...] = (out.T if transpose_out else out).astype(out_dtype)

        if nk == 1:
            finish(_dot(a_ref[...], b_ref[...], dims))
            return
        acc = refs[n_in + 1]
        k = pl.program_id(2)

        @pl.when(k == 0)
        def _():
            acc[...] = jnp.zeros_like(acc)

        acc[...] += _dot(a_ref[...], b_ref[...], dims)

        @pl.when(k == nk - 1)
        def _():
            finish(acc[...])

    ins = [a, b] + ([res] if has_res else []) + list(deps)
    in_specs = ([a_spec, b_spec] + ([o_spec] if has_res else [])
                + [pl.BlockSpec((8, 128), lambda i, j, k: (0, 0))] * len(deps))
    return pl.pallas_call(
        body, grid=(M // tm, N // tn, nk), in_specs=in_specs,
        out_specs=pl.BlockSpec((tn, tm), lambda i, j, k: (j, i)) if transpose_out else o_spec,
        out_shape=jax.ShapeDtypeStruct((N, M) if transpose_out else (M, N), out_dtype),
        scratch_shapes=[pltpu.VMEM((tm, tn), F32)] if nk > 1 else [], name=name,
        compiler_params=_params("parallel", "parallel", "arbitrary"))(*ins)


DEP_SPEC1 = pl.BlockSpec((8, 128), lambda i: (0, 0))


def _rmsnorm_fwd(x, g, deps=()):
    S, D = x.shape
    tm = min(512, S)
    assert S % tm == 0

    def body(x_ref, g_ref, *rest):
        h_ref, ht_ref = rest[-2:]
        xv = x_ref[...]
        r = lax.rsqrt(jnp.mean(xv * xv, axis=-1, keepdims=True) + RMS_EPS)
        hv = xv * r * g_ref[...]
        h_ref[...] = hv.astype(ACT)
        ht_ref[...] = hv.T.astype(ACT)

    row = pl.BlockSpec((tm, D), lambda i: (i, 0))
    return pl.pallas_call(
        body, grid=(S // tm,), in_specs=[row, pl.BlockSpec((1, D), lambda i: (0, 0))] + [DEP_SPEC1] * len(deps),
        out_specs=[row, pl.BlockSpec((D, tm), lambda i: (0, i))],
        out_shape=[jax.ShapeDtypeStruct((S, D), ACT), jax.ShapeDtypeStruct((D, S), ACT)], name="rmsnorm_fwd",
        compiler_params=_params("parallel"))(x, g, *deps)


def _rmsnorm_bwd(x, g, dh, dres):
    S, D = x.shape
    tm = min(256, S)

    def body(x_ref, g_ref, dh_ref, dr_ref, dx_ref, dxb_ref, dg_ref):
        xv = x_ref[...]
        r = lax.rsqrt(jnp.mean(xv * xv, axis=-1, keepdims=True) + RMS_EPS)
        xh = xv * r
        dhv = dh_ref[...]
        dxh = dhv * g_ref[...]
        dx = r * (dxh - xh * jnp.mean(dxh * xh, axis=-1, keepdims=True)) + dr_ref[...]
        dx_ref[...] = dx
        dxb_ref[...] = dx.astype(ACT)

        @pl.when(pl.program_id(0) == 0)
        def _():
            dg_ref[...] = jnp.zeros_like(dg_ref)

        dg_ref[...] += jnp.sum(dhv * xh, axis=0, keepdims=True)

    row = pl.BlockSpec((tm, D), lambda i: (i, 0))
    vec = pl.BlockSpec((1, D), lambda i: (0, 0))
    return pl.pallas_call(
        body, grid=(S // tm,), in_specs=[row, vec, row, row], out_specs=[row, row, vec],
        out_shape=[jax.ShapeDtypeStruct((S, D), F32), jax.ShapeDtypeStruct((S, D), ACT),
                   jax.ShapeDtypeStruct((1, D), F32)],
        name="rmsnorm_bwd", compiler_params=_params("arbitrary"))(x, g, dh, dres)


def _rot256(x, c, s):
    x1, x2 = x[:, :128], x[:, 128:]
    return jnp.concatenate([x1 * c - x2 * s, x2 * c + x1 * s], axis=1)


def _rot256_t(g, c, s):
    g1, g2 = g[:, :128], g[:, 128:]
    return jnp.concatenate([g1 * c + g2 * s, g2 * c - g1 * s], axis=1)


def _log_decay(a_ref, h, shape):
    return -jnp.exp(jnp.full(shape, a_ref[h], F32))


def _ret_state(xsrc, xbase, xscale, ysrc, ybase, cos, sin, af, ab, mode, name):
    S = xsrc.shape[0]
    nC = S // CH
    H = RET_HEADS
    W = H * RET_HD
    assert (xbase * RET_HD) % W == 0 and (ybase * RET_HD) % W == 0

    def body(x1_ref, y1_ref, c1_ref, s1_ref, x2_ref, y2_ref, c2_ref, s2_ref, af_ref, ab_ref,
             st1_ref, st2_ref, acc1, acc2):
        @pl.when(pl.program_id(0) == 0)
        def _():
            acc1[...] = jnp.zeros_like(acc1)
            acc2[...] = jnp.zeros_like(acc2)

        j = _fiota((CH, 1), 0)
        ca, sa, cb, sb_ = c1_ref[...], s1_ref[...], c2_ref[...], s2_ref[...]
        for h in range(H):
            sl = slice(h * RET_HD, (h + 1) * RET_HD)
            lgf = _log_decay(af_ref, h, (CH, 1))
            lgb = _log_decay(ab_ref, h, (CH, 1))
            if mode == "fwd":
                w1, d1 = jnp.exp(lgf * (CH - 1.0 - j)), jnp.exp(lgf[:1] * CH)
                w2, d2 = jnp.exp(lgb * j), jnp.exp(lgb[:1] * CH)
            else:
                w1, d1 = jnp.exp(lgb * (CH - j)), jnp.exp(lgb[:1] * CH)
                w2, d2 = jnp.exp(lgf * (j + 1.0)), jnp.exp(lgf[:1] * CH)
            xa = _rot256(x1_ref[:, sl], ca, sa) * xscale
            st1_ref[h] = acc1[h].astype(ACT)
            acc1[h] = d1 * acc1[h] + _dot(xa * w1, y1_ref[:, sl], TN)
            xb = _rot256(x2_ref[:, sl], cb, sb_) * xscale
            st2_ref[h] = acc2[h].astype(ACT)
            acc2[h] = d2 * acc2[h] + _dot(xb * w2, y2_ref[:, sl], TN)

    xcol, ycol = (xbase * RET_HD) // W, (ybase * RET_HD) // W
    in_specs = [
        pl.BlockSpec((CH, W), lambda c: (c, xcol)), pl.BlockSpec((CH, W), lambda c: (c, ycol)),
        pl.BlockSpec((CH, 128), lambda c: (c, 0)), pl.BlockSpec((CH, 128), lambda c: (c, 0)),
        pl.BlockSpec((CH, W), lambda c: (nC - 1 - c, xcol)), pl.BlockSpec((CH, W), lambda c: (nC - 1 - c, ycol)),
        pl.BlockSpec((CH, 128), lambda c: (nC - 1 - c, 0)),
        pl.BlockSpec((CH, 128), lambda c: (nC - 1 - c, 0)),
        SMEM_SPEC, SMEM_SPEC]
    out_specs = [pl.BlockSpec((H, None, RET_HD, RET_HD), lambda c: (0, c, 0, 0)),
                 pl.BlockSpec((H, None, RET_HD, RET_HD), lambda c: (0, nC - 1 - c, 0, 0))]
    st = jax.ShapeDtypeStruct((H, nC, RET_HD, RET_HD), ACT)
    return pl.pallas_call(
        body, grid=(nC,), in_specs=in_specs, out_specs=out_specs, out_shape=[st, st],
        scratch_shapes=[pltpu.VMEM((H, RET_HD, RET_HD), F32), pltpu.VMEM((H, RET_HD, RET_HD), F32)],
        name=name, compiler_params=_params("arbitrary"))(
            xsrc, ysrc, cos, sin, xsrc, ysrc, cos, sin, af, ab)


def _decay_mask(lgf1, lgb1):
    lag = _fiota((CH, CH), 0) - _fiota((CH, CH), 1)
    alag = jnp.abs(lag)
    return lag, jnp.where(lag >= 0, jnp.exp(lgf1 * alag), jnp.exp(lgb1 * alag))


def _ret_fwd(z, cos, sin, sf, sb, af, ab, cols):
    S = z.shape[0]
    nC = S // CH
    H = RET_HEADS
    rq, rk, rv, rg = cols
    W = H * RET_HD
    assert all((c * RET_HD) % W == 0 for c in cols)

    def body(q_ref, k_ref, v_ref, g_ref, c_ref, s_ref, sf_ref, sb_ref, af_ref, ab_ref,
             o_ref, u_ref, ut_ref):
        j = _fiota((CH, 1), 0)
        c, s = c_ref[...], s_ref[...]
        for h in range(H):
            sl = slice(h * RET_HD, (h + 1) * RET_HD)
            lgf = _log_decay(af_ref, h, (CH, 1))
            lgb = _log_decay(ab_ref, h, (CH, 1))
            q = _rot256(q_ref[:, sl], c, s)
            k = _rot256(k_ref[:, sl], c, s) * (RET_HD ** -0.5)
            _, dm = _decay_mask(lgf[:1], lgb[:1])
            p = _dot(q, k, NT) * dm
            o = (_dot(p, v_ref[:, sl], NN)
                 + _dot(q * jnp.exp(lgf * (j + 1.0)), sf_ref[h], NN)
                 + _dot(q * jnp.exp(lgb * (CH - j)), sb_ref[h], NN))
            o_ref[:, sl] = o
            on = o * lax.rsqrt(jnp.mean(o * o, axis=-1, keepdims=True) + RMS_EPS)
            g = g_ref[:, sl]
            u = on * (g * _sigmoid(g))
            u_ref[:, sl] = u.astype(ACT)
            ut_ref[sl, :] = u.T.astype(ACT)

    def zc(col):
        return pl.BlockSpec((CH, W), lambda c: (c, (col * RET_HD) // W))

    tab = pl.BlockSpec((CH, 128), lambda c: (c, 0))
    stt = pl.BlockSpec((H, None, RET_HD, RET_HD), lambda c: (0, c, 0, 0))
    out = pl.BlockSpec((CH, W), lambda c: (c, 0))
    return pl.pallas_call(
        body, grid=(nC,),
        in_specs=[zc(rq), zc(rk), zc(rv), zc(rg), tab, tab, stt, stt, SMEM_SPEC, SMEM_SPEC],
        out_specs=[out, out, pl.BlockSpec((W, CH), lambda c: (0, c))],
        out_shape=[jax.ShapeDtypeStruct((S, W), F32), jax.ShapeDtypeStruct((S, W), ACT),
                   jax.ShapeDtypeStruct((W, S), ACT)],
        name="ret_fwd", compiler_params=_params("parallel"))(
            z, z, z, z, cos, sin, sf, sb, af, ab)


def _ret_gate_bwd(du, o_pre, z, rg):
    S, W = du.shape
    H = RET_HEADS
    tm = min(512, S)
    assert S % tm == 0

    def body(du_ref, o_ref, g_ref, do_ref, dg_ref):
        o = o_ref[...]
        r = lax.rsqrt(jnp.mean(o * o, axis=-1, keepdims=True) + RMS_EPS)
        on = o * r
        g = g_ref[...]
        sg = _sigmoid(g)
        duv = du_ref[...]
        don = duv * (g * sg)
        dg_ref[...] = (duv * on * (sg * (1.0 + g * (1.0 - sg)))).astype(ACT)
        do_ref[...] = r * (don - on * jnp.mean(don * on, axis=-1, keepdims=True))

    blk = pl.BlockSpec((tm, RET_HD), lambda i, h: (i, h))
    return pl.pallas_call(
        body, grid=(S // tm, H),
        in_specs=[blk, blk, pl.BlockSpec((tm, RET_HD), lambda i, h: (i, rg + h))],
        out_specs=[blk, blk],
        out_shape=[jax.ShapeDtypeStruct((S, W), F32), jax.ShapeDtypeStruct((S, W), ACT)],
        name="ret_gate_bwd", compiler_params=_params("parallel", "parallel"))(du, o_pre, z)


def _ret_bwd(z, do, cos, sin, sf, sb, ef, eb, af, ab, cols):
    S = z.shape[0]
    nC = S // CH
    H = RET_HEADS
    rq, rk, rv, _ = cols
    W = H * RET_HD

    def body(q_ref, k_ref, v_ref, do_ref, c_ref, s_ref, sf_ref, sb_ref, ef_ref, eb_ref,
             af_ref, ab_ref, dq_ref, dk_ref, dv_ref, da_ref):
        @pl.when(pl.program_id(0) == 0)
        def _():
            da_ref[...] = jnp.zeros_like(da_ref)

        j = _fiota((CH, 1), 0)
        c, s = c_ref[...], s_ref[...]
        scale = RET_HD ** -0.5
        row = lax.broadcasted_iota(jnp.int32, (8, 128), 0)
        lane = lax.broadcasted_iota(jnp.int32, (8, 128), 1)
        for h in range(H):
            sl = slice(h * RET_HD, (h + 1) * RET_HD)
            lgf = _log_decay(af_ref, h, (CH, 1))
            lgb = _log_decay(ab_ref, h, (CH, 1))
            q = _rot256(q_ref[:, sl], c, s)
            k = _rot256(k_ref[:, sl], c, s) * scale
            v = v_ref[:, sl]
            do = do_ref[:, sl]
            sf_, sb_, ef_, eb_ = sf_ref[h], sb_ref[h], ef_ref[h], eb_ref[h]
            a_w = jnp.exp(lgf * (j + 1.0))
            b_w = jnp.exp(lgb * (CH - j))
            wf = jnp.exp(lgf * (CH - 1.0 - j))
            wb = jnp.exp(lgb * j)
            lag, dm = _decay_mask(lgf[:1], lgb[:1])
            sc = _dot(q, k, NT)
            gg = _dot(do, v, NT)
            dg = gg * dm
            x1 = _dot(do, sf_, NT) * a_w
            x2 = _dot(do, sb_, NT) * b_w
            y1 = _dot(v, ef_, NT) * wf
            y2 = _dot(v, eb_, NT) * wb
            dq = _dot(dg, k, NN) + x1 + x2
            dk = _dot(dg, q, TN) + y1 + y2
            dv = _dot(sc * dm, do, TN) + _dot(k * wf, ef_, NN) + _dot(k * wb, eb_, NN)
            dq_ref[:, sl] = _rot256_t(dq, c, s).astype(ACT)
            dk_ref[:, sl] = (_rot256_t(dk, c, s) * scale).astype(ACT)
            dv_ref[:, sl] = dv.astype(ACT)
            t = dm * gg * sc
            qx1 = jnp.sum(q * x1, axis=-1, keepdims=True)
            qx2 = jnp.sum(q * x2, axis=-1, keepdims=True)
            ky1 = jnp.sum(k * y1, axis=-1, keepdims=True)
            ky2 = jnp.sum(k * y2, axis=-1, keepdims=True)
            dlf = (_sum_all(jnp.where(lag > 0, lag * t, 0.0))
                   + _sum_all((j + 1.0) * qx1 + (CH - 1.0 - j) * ky1)
                   + CH * jnp.exp(lgf[:1] * CH) * _sum_all(ef_.astype(F32) * sf_.astype(F32)))
            dlb = (_sum_all(jnp.where(lag < 0, -lag * t, 0.0))
                   + _sum_all((CH - j) * qx2 + j * ky2)
                   + CH * jnp.exp(lgb[:1] * CH) * _sum_all(eb_.astype(F32) * sb_.astype(F32)))
            da_ref[h] += jnp.where((row == 0) & (lane == 0), dlf * lgf[:1],
                                   jnp.where((row == 0) & (lane == 1), dlb * lgb[:1], 0.0))

    def zc(col):
        return pl.BlockSpec((CH, W), lambda c: (c, (col * RET_HD) // W))

    tab = pl.BlockSpec((CH, 128), lambda c: (c, 0))
    stt = pl.BlockSpec((H, None, RET_HD, RET_HD), lambda c: (0, c, 0, 0))
    out = pl.BlockSpec((CH, W), lambda c: (c, 0))
    dz = jax.ShapeDtypeStruct((S, W), ACT)
    return pl.pallas_call(
        body, grid=(nC,),
        in_specs=[zc(rq), zc(rk), zc(rv), out, tab, tab, stt, stt, stt, stt, SMEM_SPEC, SMEM_SPEC],
        out_specs=[out, out, out, pl.BlockSpec((H, 8, 128), lambda c: (0, 0, 0))],
        out_shape=[dz, dz, dz, jax.ShapeDtypeStruct((H, 8, 128), F32)],
        name="ret_bwd", compiler_params=_params("arbitrary"))(
            z, z, z, do, cos, sin, sf, sb, ef, eb, af, ab)


def _fill_padded(pad_ref, src_ref, S):
    zeros = jnp.zeros((POOL_PAD, POOL_GD), F32)
    pad_ref[pl.ds(0, POOL_PAD), :] = zeros
    pad_ref[pl.ds(POOL_PAD, S), :] = src_ref[...]
    pad_ref[pl.ds(S + POOL_PAD, POOL_PAD), :] = zeros


def _window_sum(ext, T, lo, hi):
    n = T + 2 * POOL_PAD
    acc = None
    for k in range(lo, hi):
        sh = ext if k == 0 else pltpu.roll(ext, (-k) % n, 0)
        piece = sh[POOL_PAD:POOL_PAD + T]
        acc = piece if acc is None else acc + piece
    return acc


def _window_count(pos, w, S):
    lo = jnp.maximum(pos - w // 2, 0)
    hi = jnp.minimum(pos + w // 2, S)
    return (hi - lo).astype(F32)


def _pool_fwd(z, pw, scale, pv, pg):
    S = z.shape[0]
    G = len(POOL_WINDOWS)
    T = min(512, S)
    W = G * POOL_GD

    def body(x_ref, g_ref, pw_ref, sc_ref, y_ref, u_ref, ut_ref, pad, p_scr):
        grp = pl.program_id(0)
        i = pl.program_id(1)

        @pl.when(i == 0)
        def _():
            _fill_padded(pad, x_ref, S)

        r0 = pl.multiple_of(i * T, T)
        ext = pad[pl.ds(r0, T + 2 * POOL_PAD), :]
        pos = r0 + lax.broadcasted_iota(jnp.int32, (T, 1), 0)
        for gi, w in enumerate(POOL_WINDOWS):
            @pl.when(grp == gi)
            def _(w=w):
                acc = _window_sum(ext, T, -(w // 2), w // 2)
                p_scr[...] = acc / _window_count(pos, w, S) - ext[POOL_PAD:POOL_PAD + T]

        y = _dot(p_scr[...], pw_ref[...], NN)
        y_ref[...] = y
        g = g_ref[...]
        u = y * sc_ref[...] * (g * _sigmoid(g))
        u_ref[...] = u.astype(ACT)
        ut_ref[...] = u.T.astype(ACT)

    blk = pl.BlockSpec((T, POOL_GD), lambda g, i: (i, g))
    return pl.pallas_call(
        body, grid=(G, S // T),
        in_specs=[pl.BlockSpec((S, POOL_GD), lambda g, i: (0, pv + g)),
                  pl.BlockSpec((T, POOL_GD), lambda g, i: (i, pg + g)),
                  pl.BlockSpec((None, POOL_GD, POOL_GD), lambda g, i: (g, 0, 0)),
                  pl.BlockSpec((1, POOL_GD), lambda g, i: (0, g))],
        out_specs=[blk, blk, pl.BlockSpec((POOL_GD, T), lambda g, i: (g, i))],
        out_shape=[jax.ShapeDtypeStruct((S, W), F32), jax.ShapeDtypeStruct((S, W), ACT),
                   jax.ShapeDtypeStruct((W, S), ACT)],
        scratch_shapes=[pltpu.VMEM((S + 2 * POOL_PAD, POOL_GD), F32), pltpu.VMEM((T, POOL_GD), F32)],
        name="pool_fwd", compiler_params=_params("parallel", "arbitrary"))(z, z, pw, scale)


def _pool_bwd_a(z, pw, scale, y_raw, du, pv, pg):
    S = z.shape[0]
    G = len(POOL_WINDOWS)
    T = min(512, S)
    W = G * POOL_GD

    def body(x_ref, g_ref, pw_ref, sc_ref, y_ref, du_ref, dpc_ref, dg_ref, dsc_ref, dpw_ref,
             pad, p_scr, c_scr):
        grp = pl.program_id(0)
        i = pl.program_id(1)

        @pl.when(i == 0)
        def _():
            _fill_padded(pad, x_ref, S)
            dsc_ref[...] = jnp.zeros_like(dsc_ref)
            dpw_ref[...] = jnp.zeros_like(dpw_ref)

        r0 = pl.multiple_of(i * T, T)
        ext = pad[pl.ds(r0, T + 2 * POOL_PAD), :]
        pos = r0 + lax.broadcasted_iota(jnp.int32, (T, 1), 0)
        for gi, w in enumerate(POOL_WINDOWS):
            @pl.when(grp == gi)
            def _(w=w):
                cnt = _window_count(pos, w, S)
                acc = _window_sum(ext, T, -(w // 2), w // 2)
                p_scr[...] = acc / cnt - ext[POOL_PAD:POOL_PAD + T]
                c_scr[...] = jnp.broadcast_to(cnt, (T, 128))

        g = g_ref[...]
        sg = _sigmoid(g)
        duv = du_ref[...]
        y = y_ref[...]
        scl = sc_ref[...]
        dy = duv * (scl * (g * sg))
        dg_ref[...] = (duv * y * scl * (sg * (1.0 + g * (1.0 - sg)))).astype(ACT)
        dsc_ref[...] += jnp.sum(duv * y * (g * sg), axis=0, keepdims=True)
        dpw_ref[...] += _dot(p_scr[...], dy, TN)
        dpc_ref[...] = _dot(dy, pw_ref[...], NT) / c_scr[:, :1]

    blk = pl.BlockSpec((T, POOL_GD), lambda g, i: (i, g))
    return pl.pallas_call(
        body, grid=(G, S // T),
        in_specs=[pl.BlockSpec((S, POOL_GD), lambda g, i: (0, pv + g)),
                  pl.BlockSpec((T, POOL_GD), lambda g, i: (i, pg + g)),
                  pl.BlockSpec((None, POOL_GD, POOL_GD), lambda g, i: (g, 0, 0)),
                  pl.BlockSpec((1, POOL_GD), lambda g, i: (0, g)), blk, blk],
        out_specs=[blk, blk, pl.BlockSpec((1, POOL_GD), lambda g, i: (0, g)),
                   pl.BlockSpec((None, POOL_GD, POOL_GD), lambda g, i: (g, 0, 0))],
        out_shape=[jax.ShapeDtypeStruct((S, W), F32), jax.ShapeDtypeStruct((S, W), ACT),
                   jax.ShapeDtypeStruct((1, W), F32), jax.ShapeDtypeStruct((G, POOL_GD, POOL_GD), F32)],
        scratch_shapes=[pltpu.VMEM((S + 2 * POOL_PAD, POOL_GD), F32), pltpu.VMEM((T, POOL_GD), F32),
                        pltpu.VMEM((T, 128), F32)],
        name="pool_bwd_a", compiler_params=_params("parallel", "arbitrary"))(z, z, pw, scale, y_raw, du)


def _pool_bwd_b(dpc):
    S, W = dpc.shape
    G = len(POOL_WINDOWS)
    T = min(512, S)

    def body(x_ref, o_ref, pad, acc_scr):
        grp = pl.program_id(0)
        i = pl.program_id(1)

        @pl.when(i == 0)
        def _():
            _fill_padded(pad, x_ref, S)

        r0 = pl.multiple_of(i * T, T)
        ext = pad[pl.ds(r0, T + 2 * POOL_PAD), :]
        pos = r0 + lax.broadcasted_iota(jnp.int32, (T, 1), 0)
        for gi, w in enumerate(POOL_WINDOWS):
            @pl.when(grp == gi)
            def _(w=w):
                acc = _window_sum(ext, T, -(w // 2) + 1, w // 2 + 1)
                acc_scr[...] = acc - ext[POOL_PAD:POOL_PAD + T] * _window_count(pos, w, S)

        o_ref[...] = acc_scr[...].astype(ACT)

    blk = pl.BlockSpec((T, POOL_GD), lambda g, i: (i, g))
    return pl.pallas_call(
        body, grid=(G, S // T),
        in_specs=[pl.BlockSpec((S, POOL_GD), lambda g, i: (0, g))], out_specs=blk,
        out_shape=jax.ShapeDtypeStruct((S, W), ACT),
        scratch_shapes=[pltpu.VMEM((S + 2 * POOL_PAD, POOL_GD), F32), pltpu.VMEM((T, POOL_GD), F32)],
        name="pool_bwd_b", compiler_params=_params("parallel", "arbitrary"))(dpc)


def _rope128(x, cf, sa, sb):
    return x * cf + pltpu.roll(x, ROPE_HALF, 1) * sa + pltpu.roll(x, ATT_HD - ROPE_HALF, 1) * sb


def _rope128_t(g, cf, sa, sb):
    return g * cf + pltpu.roll(g * sa, ATT_HD - ROPE_HALF, 1) + pltpu.roll(g * sb, ROPE_HALF, 1)


def _attn_prep(z, qgain, kgain, cf, sa, sb, aq, ak, av):
    S = z.shape[0]
    T = min(512, S)
    QW, KW = ATT_Q * ATT_HD, ATT_KV * ATT_HD

    def body(q_ref, k_ref, v_ref, qg_ref, kg_ref, cf_ref, sa_ref, sb_ref, qn_ref, kn_ref, vb_ref):
        cfv, sav, sbv = cf_ref[...], sa_ref[...], sb_ref[...]

        def prep(x, gain):
            r = lax.rsqrt(jnp.mean(x * x, axis=-1, keepdims=True) + RMS_EPS)
            return _rope128(x * r * gain, cfv, sav, sbv)

        for hh in range(ATT_Q):
            sl = slice(hh * ATT_HD, (hh + 1) * ATT_HD)
            qn_ref[:, sl] = prep(q_ref[:, sl], qg_ref[...]).astype(ACT)
        for hh in range(ATT_KV):
            sl = slice(hh * ATT_HD, (hh + 1) * ATT_HD)
            kn_ref[:, sl] = prep(k_ref[:, sl], kg_ref[...]).astype(ACT)
        vb_ref[...] = v_ref[...].astype(ACT)

    tab = pl.BlockSpec((T, ATT_HD), lambda i: (i, 0))
    gain = pl.BlockSpec((1, ATT_HD), lambda i: (0, 0))
    return pl.pallas_call(
        body, grid=(S // T,),
        in_specs=[pl.BlockSpec((T, QW), lambda i: (i, aq)), pl.BlockSpec((T, KW), lambda i: (i, ak)),
                  pl.BlockSpec((T, KW), lambda i: (i, av)), gain, gain, tab, tab, tab],
        out_specs=[pl.BlockSpec((T, QW), lambda i: (i, 0)), pl.BlockSpec((T, KW), lambda i: (i, 0)),
                   pl.BlockSpec((T, KW), lambda i: (i, 0))],
        out_shape=[jax.ShapeDtypeStruct((S, QW), ACT), jax.ShapeDtypeStruct((S, KW), ACT),
                   jax.ShapeDtypeStruct((S, KW), ACT)],
        name="attn_prep", compiler_params=_params("parallel"))(z, z, z, qgain, kgain, cf, sa, sb)


def _attn_window(i, S):
    start = jnp.clip(i * ATT_BLK - ATT_BLK, 0, S - ATT_SPAN)
    start = pl.multiple_of(start, ATT_BLK)
    qpos = i * ATT_BLK + lax.broadcasted_iota(jnp.int32, (ATT_BLK, ATT_SPAN), 0)
    kpos = start + lax.broadcasted_iota(jnp.int32, (ATT_BLK, ATT_SPAN), 1)
    return start, jnp.abs(kpos - qpos) <= ATT_WIN


def _attn_probs(q, kw, valid, sink):
    s = _dot(q, kw, NT) * (ATT_HD ** -0.5)
    s = jnp.where(valid, s, NEG_BIG)
    m = jnp.maximum(jnp.max(s, axis=-1, keepdims=True), sink)
    p = jnp.exp(s - m)
    es = jnp.exp(sink - m)
    den = jnp.sum(p, axis=-1, keepdims=True) + es
    return p / den, es / den


def _attn_fwd(qn, kn, vb, z, sink, ag):
    S = qn.shape[0]
    nB = S // ATT_BLK
    assert S >= ATT_SPAN
    QW = ATT_Q * ATT_HD

    GW = ATT_G * ATT_HD

    def body(q_ref, k_ref, v_ref, g_ref, sink_ref, o_ref, u_ref, ut_ref):
        kvh = pl.program_id(0)
        i = pl.program_id(1)
        start, valid = _attn_window(i, S)
        kw = k_ref[pl.ds(start, ATT_SPAN), :]
        vw = v_ref[pl.ds(start, ATT_SPAN), :]
        for gi in range(ATT_G):
            sl = slice(gi * ATT_HD, (gi + 1) * ATT_HD)
            sk = jnp.full((ATT_BLK, 1), sink_ref[kvh * ATT_G + gi], F32)
            pn, _ = _attn_probs(q_ref[:, sl], kw, valid, sk)
            o = _dot(pn, vw, NN)
            o_ref[:, sl] = o
            g = g_ref[:, sl]
            u = o * (g * _sigmoid(g))
            u_ref[:, sl] = u.astype(ACT)
            ut_ref[sl, :] = u.T.astype(ACT)

    blk = pl.BlockSpec((ATT_BLK, GW), lambda k, i: (i, k))
    kv = pl.BlockSpec((S, ATT_HD), lambda k, i: (0, k))
    return pl.pallas_call(
        body, grid=(ATT_KV, nB),
        in_specs=[blk, kv, kv, pl.BlockSpec((ATT_BLK, GW), lambda k, i: (i, ag // ATT_G + k)), SMEM_SPEC],
        out_specs=[blk, blk, pl.BlockSpec((GW, ATT_BLK), lambda k, i: (k, i))],
        out_shape=[jax.ShapeDtypeStruct((S, QW), F32), jax.ShapeDtypeStruct((S, QW), ACT),
                   jax.ShapeDtypeStruct((QW, S), ACT)],
        name="attn_fwd", compiler_params=_params("parallel", "parallel"))(qn, kn, vb, z, sink)


def _attn_bwd(qn, kn, vb, o, du, z, sink, ag):
    S = qn.shape[0]
    nB = S // ATT_BLK
    QW, KW = ATT_Q * ATT_HD, ATT_KV * ATT_HD
    GW = ATT_G * ATT_HD

    def body(q_ref, k_ref, v_ref, o_ref, du_ref, g_ref, sink_ref,
             dq_ref, dk_ref, dv_ref, dg_ref, ds_ref):
        kvh = pl.program_id(0)
        i = pl.program_id(1)

        @pl.when(i == 0)
        def _():
            dk_ref[...] = jnp.zeros_like(dk_ref)
            dv_ref[...] = jnp.zeros_like(dv_ref)
            ds_ref[...] = jnp.zeros_like(ds_ref)

        start, valid = _attn_window(i, S)
        kw = k_ref[pl.ds(start, ATT_SPAN), :]
        vw = v_ref[pl.ds(start, ATT_SPAN), :]
        row = lax.broadcasted_iota(jnp.int32, (8, 128), 0)
        lane = lax.broadcasted_iota(jnp.int32, (8, 128), 1)
        dsink = jnp.zeros((8, 128), F32)
        dk_acc = jnp.zeros((ATT_SPAN, ATT_HD), F32)
        dv_acc = jnp.zeros((ATT_SPAN, ATT_HD), F32)
        for gi in range(ATT_G):
            sl = slice(gi * ATT_HD, (gi + 1) * ATT_HD)
            q = q_ref[:, sl]
            ov = o_ref[:, sl]
            g = g_ref[:, sl]
            duv = du_ref[:, sl]
            sg = _sigmoid(g)
            do = duv * (g * sg)
            dg_ref[:, sl] = (duv * ov * (sg * (1.0 + g * (1.0 - sg)))).astype(ACT)
            sk = jnp.full((ATT_BLK, 1), sink_ref[kvh * ATT_G + gi], F32)
            pn, psink = _attn_probs(q, kw, valid, sk)
            delta = jnp.sum(do * ov, axis=-1, keepdims=True)
            dsc = pn * (_dot(do, vw, NT) - delta) * (ATT_HD ** -0.5)
            dq_ref[:, sl] = _dot(dsc, kw, NN)
            dk_acc = dk_acc + _dot(dsc, q, TN)
            dv_acc = dv_acc + _dot(pn, do, TN)
            dsink = dsink + jnp.where((row == 0) & (lane == gi), -_sum_all(psink * delta), 0.0)
        dk_ref[pl.ds(start, ATT_SPAN), :] += dk_acc
        dv_ref[pl.ds(start, ATT_SPAN), :] += dv_acc
        ds_ref[...] += dsink

    grp = pl.BlockSpec((ATT_BLK, GW), lambda k, i: (i, k))
    kv = pl.BlockSpec((S, ATT_HD), lambda k, i: (0, k))
    return pl.pallas_call(
        body, grid=(ATT_KV, nB),
        in_specs=[grp, kv, kv, grp, grp,
                  pl.BlockSpec((ATT_BLK, GW), lambda k, i: (i, ag // ATT_G + k)), SMEM_SPEC],
        out_specs=[grp, kv, kv, grp, pl.BlockSpec((None, 8, 128), lambda k, i: (k, 0, 0))],
        out_shape=[jax.ShapeDtypeStruct((S, QW), F32), jax.ShapeDtypeStruct((S, KW), F32),
                   jax.ShapeDtypeStruct((S, KW), F32), jax.ShapeDtypeStruct((S, QW), ACT),
                   jax.ShapeDtypeStruct((ATT_KV, 8, 128), F32)],
        name="attn_bwd", compiler_params=_params("parallel", "arbitrary"))(qn, kn, vb, o, du, z, sink)


def _attn_prep_bwd(z, dqn, dkn, dv, qgain, kgain, cf, sa, sb, aq, ak):
    S = z.shape[0]
    T = min(512, S)
    QW, KW = ATT_Q * ATT_HD, ATT_KV * ATT_HD

    def body(q_ref, k_ref, dqn_ref, dkn_ref, dv_ref, qg_ref, kg_ref, cf_ref, sa_ref, sb_ref,
             dq_ref, dk_ref, dvb_ref, dqg_ref, dkg_ref):
        cfv, sav, sbv = cf_ref[...], sa_ref[...], sb_ref[...]

        @pl.when(pl.program_id(0) == 0)
        def _():
            dqg_ref[...] = jnp.zeros_like(dqg_ref)
            dkg_ref[...] = jnp.zeros_like(dkg_ref)

        def back(x, gn, gain):
            r = lax.rsqrt(jnp.mean(x * x, axis=-1, keepdims=True) + RMS_EPS)
            xh = x * r
            dy = _rope128_t(gn, cfv, sav, sbv)
            dxh = dy * gain
            dx = r * (dxh - xh * jnp.mean(dxh * xh, axis=-1, keepdims=True))
            return dx, jnp.sum(dy * xh, axis=0, keepdims=True)

        dqg = jnp.zeros((1, ATT_HD), F32)
        for hh in range(ATT_Q):
            sl = slice(hh * ATT_HD, (hh + 1) * ATT_HD)
            dx, dgn = back(q_ref[:, sl], dqn_ref[:, sl], qg_ref[...])
            dq_ref[:, sl] = dx.astype(ACT)
            dqg = dqg + dgn
        dkg = jnp.zeros((1, ATT_HD), F32)
        for hh in range(ATT_KV):
            sl = slice(hh * ATT_HD, (hh + 1) * ATT_HD)
            dx, dgn = back(k_ref[:, sl], dkn_ref[:, sl], kg_ref[...])
            dk_ref[:, sl] = dx.astype(ACT)
            dkg = dkg + dgn
        dvb_ref[...] = dv_ref[...].astype(ACT)
        dqg_ref[...] += dqg
        dkg_ref[...] += dkg

    tab = pl.BlockSpec((T, ATT_HD), lambda i: (i, 0))
    gain = pl.BlockSpec((1, ATT_HD), lambda i: (0, 0))
    qb = pl.BlockSpec((T, QW), lambda i: (i, 0))
    kb = pl.BlockSpec((T, KW), lambda i: (i, 0))
    return pl.pallas_call(
        body, grid=(S // T,),
        in_specs=[pl.BlockSpec((T, QW), lambda i: (i, aq)), pl.BlockSpec((T, KW), lambda i: (i, ak)),
                  qb, kb, kb, gain, gain, tab, tab, tab],
        out_specs=[qb, kb, kb, gain, gain],
        out_shape=[jax.ShapeDtypeStruct((S, QW), ACT), jax.ShapeDtypeStruct((S, KW), ACT),
                   jax.ShapeDtypeStruct((S, KW), ACT), jax.ShapeDtypeStruct((1, ATT_HD), F32),
                   jax.ShapeDtypeStruct((1, ATT_HD), F32)],
        name="attn_prep_bwd", compiler_params=_params("arbitrary"))(
            z, z, dqn, dkn, dv, qgain, kgain, cf, sa, sb)


def _branch_merge(ua, ub, uc, wr, wp, wa, z, mg):
    S, W = ua.shape
    D = wr.shape[1]
    tm, tn = min(512, S), min(512, D)
    nb = D // tn

    def body(ua_ref, ub_ref, uc_ref, wr_ref, wp_ref, wa_ref, g0_ref, g1_ref, g2_ref,
             ya_ref, yb_ref, yc_ref, m_ref, mt_ref):
        ya = _dot(ua_ref[...], wr_ref[...], NN)
        yb = _dot(ub_ref[...], wp_ref[...], NN)
        yc = _dot(uc_ref[...], wa_ref[...], NN)
        ya_ref[...] = ya.astype(ACT)
        yb_ref[...] = yb.astype(ACT)
        yc_ref[...] = yc.astype(ACT)
        m = _sigmoid(g0_ref[...]) * ya + _sigmoid(g1_ref[...]) * yb + _sigmoid(g2_ref[...]) * yc
        m_ref[...] = m.astype(ACT)
        mt_ref[...] = m.T.astype(ACT)

    u = pl.BlockSpec((tm, W), lambda i, j: (i, 0))
    w = pl.BlockSpec((W, tn), lambda i, j: (0, j))
    o = pl.BlockSpec((tm, tn), lambda i, j: (i, j))

    assert (mg * POOL_GD) % tn == 0
    base = (mg * POOL_GD) // tn

    def gate(k):
        return pl.BlockSpec((tm, tn), lambda i, j: (i, base + k * nb + j))

    sd = jax.ShapeDtypeStruct((S, D), ACT)
    return pl.pallas_call(
        body, grid=(S // tm, nb), in_specs=[u, u, u, w, w, w, gate(0), gate(1), gate(2)],
        out_specs=[o, o, o, o, pl.BlockSpec((tn, tm), lambda i, j: (j, i))],
        out_shape=[sd, sd, sd, sd, jax.ShapeDtypeStruct((D, S), ACT)], name="branch_merge",
        compiler_params=_params("parallel", "parallel"))(ua, ub, uc, wr, wp, wa, z, z, z)


def _merge_bwd(dxb, wo, ya, yb, yc, z, mg):
    S, D = dxb.shape
    tm, tn = min(512, S), min(512, D)
    nb = D // tn
    base = (mg * POOL_GD) // tn

    def body(dx_ref, wo_ref, ya_ref, yb_ref, yc_ref, g0_ref, g1_ref, g2_ref,
             da_ref, db_ref, dc_ref, dg0_ref, dg1_ref, dg2_ref):
        dm = _dot(dx_ref[...], wo_ref[...], NT)
        for y_ref, g_ref, dy_ref, dg_ref in ((ya_ref, g0_ref, da_ref, dg0_ref),
                                             (yb_ref, g1_ref, db_ref, dg1_ref),
                                             (yc_ref, g2_ref, dc_ref, dg2_ref)):
            sg = _sigmoid(g_ref[...])
            dy_ref[...] = (sg * dm).astype(ACT)
            dg_ref[...] = (dm * y_ref[...].astype(F32) * (sg * (1.0 - sg))).astype(ACT)

    o = pl.BlockSpec((tm, tn), lambda i, j: (i, j))

    def gate(k):
        return pl.BlockSpec((tm, tn), lambda i, j: (i, base + k * nb + j))

    sd = jax.ShapeDtypeStruct((S, D), ACT)
    return pl.pallas_call(
        body, grid=(S // tm, nb),
        in_specs=[pl.BlockSpec((tm, D), lambda i, j: (i, 0)), pl.BlockSpec((tn, D), lambda i, j: (j, 0)),
                  o, o, o, gate(0), gate(1), gate(2)],
        out_specs=[o] * 6, out_shape=[sd] * 6, name="merge_bwd",
        compiler_params=_params("parallel", "parallel"))(dxb, wo, ya, yb, yc, z, z, z)


def _loss_head(y, t):
    S, D = y.shape
    tm = min(256, S)

    def body(y_ref, t_ref, dy_ref, dyb_ref, l_ref):
        e = y_ref[...] - t_ref[...]
        dy = e * (1.0 / D)
        dy_ref[...] = dy
        dyb_ref[...] = dy.astype(ACT)

        @pl.when(pl.program_id(0) == 0)
        def _():
            l_ref[...] = jnp.zeros_like(l_ref)

        l_ref[...] += jnp.sum(jnp.mean(e * e, axis=-1, keepdims=True), axis=0, keepdims=True)

    row = pl.BlockSpec((tm, D), lambda i: (i, 0))
    return pl.pallas_call(
        body, grid=(S // tm,), in_specs=[row, row],
        out_specs=[row, row, pl.BlockSpec((1, 1), lambda i: (0, 0))],
        out_shape=[jax.ShapeDtypeStruct((S, D), F32), jax.ShapeDtypeStruct((S, D), ACT),
                   jax.ShapeDtypeStruct((1, 1), F32)],
        name="loss_head", compiler_params=_params("arbitrary"))(y, t)


def _adamw(w, layer, parts, m, v, name):
    L = w.shape[0]
    shape = w.shape[1:]
    C = shape[-1]
    R = int(np.prod(shape[:-1]))
    w3, m3, v3 = (a.reshape(L, R, C) for a in (w, m, v))
    p3 = parts.reshape(N_DEV, R, C)
    tr = next((c for c in (1024, 512, 256, 128, 64) if R % c == 0 and c * C * 4 <= (1 << 20)), min(64, R))
    assert R % tr == 0
    c1 = 1.0 / (1.0 - ADAM_B1 ** ADAM_STEP)
    c2 = 1.0 / (1.0 - ADAM_B2 ** ADAM_STEP)

    def body(w_ref, p_ref, m_ref, v_ref, g_ref, d_ref, nm_ref, nv_ref):
        g = p_ref[0].astype(F32)
        for k in range(1, N_DEV):
            g = g + p_ref[k].astype(F32)
        nm = ADAM_B1 * m_ref[...] + (1.0 - ADAM_B1) * g
        nv = ADAM_B2 * v_ref[...] + (1.0 - ADAM_B2) * (g * g)
        g_ref[...] = g
        nm_ref[...] = nm
        nv_ref[...] = nv
        d_ref[...] = -ADAM_LR * ((nm * c1) / (jnp.sqrt(nv * c2) + ADAM_EPS) + ADAM_WD * w_ref[...])

    lay = pl.BlockSpec((None, tr, C), lambda i: (layer, i, 0))
    out = pl.BlockSpec((tr, C), lambda i: (i, 0))
    sd = jax.ShapeDtypeStruct((R, C), F32)
    outs = pl.pallas_call(
        body, grid=(R // tr,),
        in_specs=[lay, pl.BlockSpec((N_DEV, tr, C), lambda i: (0, i, 0)), lay, lay],
        out_specs=[out] * 4, out_shape=[sd] * 4, name=name,
        compiler_params=_params("parallel"))(w3, p3, m3, v3)
    return [a.reshape(shape) for a in outs]


def _exchange(arrs, scatter, name, deps=()):
    n = len(arrs)
    nd = len(deps)
    out_shape = [jax.ShapeDtypeStruct(a.shape if scatter else (N_DEV,) + a.shape, a.dtype) for a in arrs]

    def body(*refs):
        ins, outs = refs[:n], refs[n + nd:2 * n + nd]
        send_sems, recv_sems, local_sems = refs[2 * n + nd:]
        x, y, c = lax.axis_index("x"), lax.axis_index("y"), lax.axis_index("c")
        me = 4 * x + 2 * y + c
        copies = []
        for a in range(n):
            src = ins[a].at[me] if scatter else ins[a]
            own = pltpu.make_async_copy(src, outs[a].at[me], local_sems.at[a])
            own.start()
            copies.append(own)
        sends, recvs = [], []
        for k in range(1, N_DEV):
            px, py, pc = x ^ (k >> 2), y ^ ((k >> 1) & 1), c ^ (k & 1)
            peer = 4 * px + 2 * py + pc
            for a in range(n):
                src = ins[a].at[peer] if scatter else ins[a]
                cp = pltpu.make_async_remote_copy(
                    src_ref=src, dst_ref=outs[a].at[me],
                    send_sem=send_sems.at[a, k - 1], recv_sem=recv_sems.at[a, k - 1],
                    device_id=(px, py, pc), device_id_type=pl.DeviceIdType.MESH)
                cp.start()
                sends.append(cp)
                recvs.append(pltpu.make_async_remote_copy(
                    src_ref=src, dst_ref=outs[a].at[peer],
                    send_sem=send_sems.at[a, k - 1], recv_sem=recv_sems.at[a, k - 1],
                    device_id=(px, py, pc), device_id_type=pl.DeviceIdType.MESH))
        for cp in recvs:
            cp.wait_recv()
        for cp in sends:
            cp.wait_send()
        for cp in copies:
            cp.wait()

    any_spec = pl.BlockSpec(memory_space=pl.ANY)
    return pl.pallas_call(
        body, in_specs=[any_spec] * (n + nd), out_specs=[any_spec] * n, out_shape=out_shape,
        scratch_shapes=[pltpu.SemaphoreType.DMA((n, N_DEV - 1)), pltpu.SemaphoreType.DMA((n, N_DEV - 1)),
                        pltpu.SemaphoreType.DMA((n,))],
        name=name)(*arrs, *deps)


HBM_SPEC = pl.BlockSpec(memory_space=pltpu.HBM)
SEM_SPEC = pl.BlockSpec(memory_space=pltpu.SEMAPHORE)
DATAFLOW = pltpu.SideEffectType.DATAFLOW_SIDE_EFFECTING


def _peer_of(k):
    x, y, c = lax.axis_index("x"), lax.axis_index("y"), lax.axis_index("c")
    return x ^ (k >> 2), y ^ ((k >> 1) & 1), c ^ (k & 1)


def _exchange_copy(k, a, src_ref, land_ref, send_sems, recv_sems, scatter, outgoing):
    px, py, pc = _peer_of(k)
    peer = 4 * px + 2 * py + pc
    me = 4 * lax.axis_index("x") + 2 * lax.axis_index("y") + lax.axis_index("c")
    idx = a * (N_DEV - 1) + k - 1
    return pltpu.make_async_remote_copy(
        src_ref=src_ref.at[peer] if scatter else src_ref, dst_ref=land_ref.at[me if outgoing else peer],
        send_sem=send_sems.at[idx], recv_sem=recv_sems.at[idx],
        device_id=(px, py, pc), device_id_type=pl.DeviceIdType.MESH)


def _exchange_start(arrs, scatter, name):
    n = len(arrs)
    land_shapes = [a.shape if scatter else (N_DEV,) + a.shape for a in arrs]

    def body(*refs):
        srcs, lands = refs[:n], refs[n:2 * n]
        send_sems, recv_sems = refs[2 * n], refs[2 * n + 1]
        token = refs[-1]
        for k in range(1, N_DEV):
            for a in range(n):
                _exchange_copy(k, a, srcs[a], lands[a], send_sems, recv_sems, scatter, True).start()
        token[...] = jnp.zeros_like(token)

    sems = pltpu.SemaphoreType.DMA((n * (N_DEV - 1),))
    out_shape = ([sems, sems] + [pltpu.HBM(a.shape, a.dtype) for a in arrs]
                 + [pltpu.HBM(s, a.dtype) for s, a in zip(land_shapes, arrs)]
                 + [jax.ShapeDtypeStruct((8, 128), F32)])
    ins = ([pltpu.with_memory_space_constraint(a, pltpu.HBM) for a in arrs]
           + [pltpu.with_memory_space_constraint(lax.empty(s, a.dtype), pltpu.HBM) for s, a in zip(land_shapes, arrs)])
    res = pl.pallas_call(
        body, name=name, out_shape=out_shape, in_specs=[HBM_SPEC] * (2 * n),
        out_specs=[SEM_SPEC, SEM_SPEC] + [HBM_SPEC] * (2 * n) + [pl.BlockSpec(memory_space=pltpu.VMEM)],
        input_output_aliases={i: 2 + i for i in range(2 * n)},
        compiler_params=pltpu.CompilerParams(has_side_effects=DATAFLOW))(*ins)
    return res[0], res[1], list(res[2:2 + n]), list(res[2 + n:2 + 2 * n]), res[-1]


def _exchange_wait(started, after, scatter, name):
    send_sems, recv_sems, srcs, lands, _ = started
    n = len(srcs)

    def body(*refs):
        src_refs, land_refs = refs[:n], refs[n:2 * n]
        s_sems, r_sems = refs[2 * n], refs[2 * n + 1]
        for k in range(1, N_DEV):
            for a in range(n):
                back = _exchange_copy(k, a, src_refs[a], land_refs[a], s_sems, r_sems, scatter, False)
                back.wait_send()
                back.wait_recv()

    out_shape = [pltpu.HBM(a.shape, a.dtype) for a in srcs] + [pltpu.HBM(a.shape, a.dtype) for a in lands]
    res = pl.pallas_call(
        body, name=name, out_shape=out_shape,
        in_specs=[HBM_SPEC] * (2 * n) + [SEM_SPEC, SEM_SPEC, pl.BlockSpec(memory_space=pl.ANY)],
        out_specs=[HBM_SPEC] * (2 * n), input_output_aliases={i: i for i in range(2 * n)},
        compiler_params=pltpu.CompilerParams(has_side_effects=DATAFLOW))(*srcs, *lands, send_sems, recv_sems, after)
    return list(res[:n]), list(res[n:])


def _own_slot(land, own, me):
    return lax.dynamic_update_index_in_dim(land, own, me, 0)


class _Cols:
    def __init__(self, D):
        Wb = D // 2
        sizes = (Wb, Wb, Wb, Wb, Wb, Wb, Wb, ATT_KV * ATT_HD, ATT_KV * ATT_HD, Wb, 3 * D)
        offs = np.concatenate([[0], np.cumsum(sizes)])
        assert all(int(o) % 256 == 0 for o in offs)
        (self.rq, self.rk, self.rv, self.rg, self.pv, self.pg,
         self.aq, self.ak, self.av, self.ag, self.mg) = (int(o) // 256 for o in offs[:-1])
        self.width = int(offs[-1])
        self.sizes = sizes


def _rope_tables(S):
    pos = jnp.arange(S, dtype=F32)[:, None]
    inv_r = 1.0 / (RET_ROPE_BASE ** jnp.linspace(0.0, 1.0, RET_HD // 2, dtype=F32))
    ang_r = pos * inv_r[None, :]
    inv_a = ROPE_THETA ** (-jnp.arange(ROPE_HALF, dtype=F32) / ROPE_HALF)
    ang_a = pos * inv_a[None, :]
    ca, sa = jnp.cos(ang_a), jnp.sin(ang_a)
    z16 = jnp.zeros((S, ROPE_HALF), F32)
    rest = ATT_HD - 2 * ROPE_HALF
    cf = jnp.concatenate([ca, ca, jnp.ones((S, rest), F32)], axis=1)
    s_up = jnp.concatenate([z16, sa, jnp.zeros((S, rest), F32)], axis=1)
    s_dn = jnp.concatenate([-sa, z16, jnp.zeros((S, rest), F32)], axis=1)
    return jnp.cos(ang_r), jnp.sin(ang_r), cf, s_up, s_dn


def _layer_fwd(x, p, tabs, cols, deps=()):
    cos_r, sin_r, cf, s_up, s_dn = tabs
    S, D = x.shape
    h, ht = _rmsnorm_fwd(x, p["norm_g"], deps)
    z = _matmul(h, p["w_in_t"], "nt", F32, 2048, 512, D, "in_proj")
    if "rest" in p:
        p = {**{k: v for k, v in p.items() if k != "rest"}, **p["rest"](z)}
    rcols = (cols.rq, cols.rk, cols.rv, cols.rg)
    sf, sb = _ret_state(z, cols.rk, RET_HD ** -0.5, z, cols.rv, cos_r, sin_r, p["af"], p["ab"], "fwd",
                        "ret_state_fwd")
    o_ret, ua, uat = _ret_fwd(z, cos_r, sin_r, sf, sb, p["af"], p["ab"], rcols)
    y_pool, ub, ubt = _pool_fwd(z, p["pool_w"], p["pool_scale"], cols.pv, cols.pg)
    qn, kn, vb = _attn_prep(z, p["q_gain"], p["k_gain"], cf, s_up, s_dn,
                            cols.aq // 4, cols.ak, cols.av)
    o_att, uc, uct = _attn_fwd(qn, kn, vb, z, p["sink"], cols.ag * 2)
    ya, yb, yc, merged, mergedt = _branch_merge(ua, ub, uc, p["w_ret"], p["w_pool"], p["w_att"], z, cols.mg)
    out = _matmul(merged, p["w_out"], "nn", F32, 1024, 512, D, "out_proj", res=x)
    saved = dict(x=x, ht=ht, z=z, sf=sf, sb=sb, o_ret=o_ret, uat=uat, y_pool=y_pool, ubt=ubt,
                 qn=qn, kn=kn, vb=vb, o_att=o_att, uct=uct, ya=ya, yb=yb, yc=yc, mergedt=mergedt)
    return out, saved, p


def _layer_bwd(dx, dxb, p, sv, tabs, cols, on_grads):
    cos_r, sin_r, cf, s_up, s_dn = tabs
    z = sv["z"]
    S, D = dx.shape
    dya, dyb, dyc, dmg0, dmg1, dmg2 = _merge_bwd(dxb, p["w_out"], sv["ya"], sv["yb"], sv["yc"], z, cols.mg)
    grads = {"w_out": _matmul(sv["mergedt"], dxb, "nn", ACT, 1024, 512, S, "dw_out")}
    dus = {}
    for nm, ut, dy in (("ret", sv["uat"], dya), ("pool", sv["ubt"], dyb), ("att", sv["uct"], dyc)):
        dus[nm] = _matmul(dy, p["w_" + nm], "nt", F32, 1024, 512, D, "du_" + nm)
        grads["w_" + nm] = _matmul(ut, dy, "nn", ACT, 1024, 512, S, "dw_" + nm)
    rcols = (cols.rq, cols.rk, cols.rv, cols.rg)
    do_ret, d_rg = _ret_gate_bwd(dus["ret"], sv["o_ret"], z, cols.rg)
    eb, ef = _ret_state(z, cols.rq, 1.0, do_ret, 0, cos_r, sin_r, p["af"], p["ab"], "bwd", "ret_state_bwd")
    d_rq, d_rk, d_rv, d_decay = _ret_bwd(z, do_ret, cos_r, sin_r, sv["sf"], sv["sb"], ef, eb,
                                         p["af"], p["ab"], rcols)
    dpc, d_pg, d_pscale, g_pool_w = _pool_bwd_a(z, p["pool_w"], p["pool_scale"], sv["y_pool"], dus["pool"],
                                                cols.pv, cols.pg)
    d_pv = _pool_bwd_b(dpc)
    grads["pool_w"] = g_pool_w.astype(ACT)
    dqn, dkn, dvv, d_ag, d_sink = _attn_bwd(sv["qn"], sv["kn"], sv["vb"], sv["o_att"], dus["att"], z,
                                            p["sink"], cols.ag * 2)
    d_aq, d_ak, d_av, d_qg, d_kg = _attn_prep_bwd(z, dqn, dkn, dvv, p["q_gain"], p["k_gain"], cf, s_up, s_dn,
                                                  cols.aq // 4, cols.ak)
    dz = jnp.concatenate([d_rq, d_rk, d_rv, d_rg, d_pv, d_pg, d_aq, d_ak, d_av, d_ag, dmg0, dmg1, dmg2],
                         axis=1)
    grads["w_in_t"] = _matmul(sv["ht"], dz, "nn", ACT, 1024, 512, S, "dw_in", transpose_out=True)
    tok = on_grads(grads)
    dh = _matmul(dz, p["w_in_t"], "nn", F32, 1024, 1024, 29 * 128, "dh", deps=() if tok is None else (tok,))
    dx_in, dxb_in, d_norm_g = _rmsnorm_bwd(sv["x"], p["norm_g"], dh, dx)
    misc = jnp.concatenate([d_decay[:, 0, 0], d_decay[:, 0, 1], d_sink[:, 0, :ATT_G].reshape(-1)])
    misc = jnp.pad(misc, (0, 128 - misc.shape[0]))[None, :]
    small = jnp.concatenate([d_norm_g, d_pscale, d_qg, d_kg, misc], axis=1)
    return dx_in, dxb_in, small


def _pack_small(norm_g, pool_scale, q_gain, k_gain, af, ab, sink):
    L = norm_g.shape[0]
    misc = jnp.concatenate([af, ab, sink], axis=1)
    misc = jnp.pad(misc, ((0, 0), (0, 128 - misc.shape[1])))
    return jnp.concatenate([norm_g, pool_scale, q_gain, k_gain, misc], axis=1)


def _unpack_small(a, D):
    Wb = D // 2
    o = np.cumsum([0, D, Wb, ATT_HD, ATT_HD])
    misc = a[:, o[4]:]
    return (a[:, o[0]:o[1]], a[:, o[1]:o[2]], a[:, o[2]:o[3]], a[:, o[3]:o[4]],
            misc[:, :RET_HEADS], misc[:, RET_HEADS:2 * RET_HEADS],
            misc[:, 2 * RET_HEADS:2 * RET_HEADS + ATT_Q])


def _local_step(x, t, n_layers, get_layer, on_grads, tabs, cols, first_dep=None):
    saved, layers = [], []
    after = first_dep
    for l in range(n_layers):
        p = get_layer(l, after)
        x, sv, p = _layer_fwd(x, p, tabs, cols, (first_dep,) if (l == 0 and first_dep is not None) else ())
        after = x
        layers.append(p)
        saved.append(sv)
    dx, dxb, lsum = _loss_head(x, t)
    smalls = []
    for l in reversed(range(n_layers)):
        dx, dxb, sm = _layer_bwd(dx, dxb, layers[l], saved[l], tabs, cols, functools.partial(on_grads, l))
        smalls.append(sm)
    return 0.5 * lsum[0, 0], dx, jnp.concatenate(smalls[::-1], axis=0)


WEIGHT_KEYS = ("w_in", "w_ret", "w_pool", "w_att", "w_out", "pool_w")


def kernel(x, norm_g, w_in, ret_decay_fwd, ret_decay_bwd, pool_w, pool_scale, attn_q_gain, attn_k_gain, attn_sink, w_ret, w_pool, w_att, w_out, loss_target, m_norm_g, m_w_in, m_ret_decay_fwd, m_ret_decay_bwd, m_pool_w, m_pool_scale, m_attn_q_gain, m_attn_k_gain, m_attn_sink, m_w_ret, m_w_pool, m_w_att, m_w_out, v_norm_g, v_w_in, v_ret_decay_fwd, v_ret_decay_bwd, v_pool_w, v_pool_scale, v_attn_q_gain, v_attn_k_gain, v_attn_sink, v_w_ret, v_w_pool, v_w_att, v_w_out):
    L = norm_g.shape[0]
    _, S, D = x.shape
    Wb = D // 2
    G = len(POOL_WINDOWS)
    cols = _Cols(D)
    tabs = _rope_tables(S)
    me = 4 * lax.axis_index("x") + 2 * lax.axis_index("y") + lax.axis_index("c")
    def tr(a):
        return jnp.transpose(a, (0, 2, 1))

    weights = dict(w_in=tr(w_in), w_ret=w_ret, w_pool=w_pool, w_att=w_att, w_out=w_out, pool_w=pool_w)

    gathers, tok = [], None
    for l in range(L):
        started = []
        for part, keys in (("a", WEIGHT_KEYS[:1]), ("b", WEIGHT_KEYS[1:])):
            shards = []
            for k in keys:
                w = weights[k][l]
                if tok is not None:
                    w = w + tok[0, 0]
                shards.append(w.astype(MXU))
            st = _exchange_start(shards, False, f"gather_start_{l}{part}")
            started.append(st)
            tok = st[-1]
        gathers.append(started)

    def cols_full(g, rows):
        return jnp.transpose(g, (1, 0, 2)).reshape(rows, -1)

    def get_layer(l, after):
        srcs, lands = _exchange_wait(gathers[l][0], after, False, f"gather_wait_{l}a")
        g_in = _own_slot(lands[0], srcs[0], me)

        def rest(z):
            srcs, lands = _exchange_wait(gathers[l][1], z, False, f"gather_wait_{l}b")
            g_ret, g_pool, g_att, g_out, g_pw = [_own_slot(ld, sr, me) for ld, sr in zip(lands, srcs)]
            return dict(
                w_ret=cols_full(g_ret, Wb), w_pool=cols_full(g_pool, Wb), w_att=cols_full(g_att, Wb),
                w_out=g_out.reshape(D, D),
                pool_w=jnp.transpose(g_pw, (1, 0, 2, 3)).reshape(G, POOL_GD, POOL_GD))

        return dict(
            norm_g=norm_g[l][None, :], w_in_t=g_in.reshape(-1, D), rest=rest,
            pool_scale=pool_scale[l][None, :], q_gain=attn_q_gain[l][None, :], k_gain=attn_k_gain[l][None, :],
            af=ret_decay_fwd[l], ab=ret_decay_bwd[l], sink=attn_sink[l])

    def col_slots(g, rows):
        return jnp.transpose(g.reshape(rows, N_DEV, -1), (1, 0, 2))

    scatters = {}

    def on_grads(l, g):
        slots = [g["w_in_t"].reshape(N_DEV, -1, D),
                 col_slots(g["w_ret"], Wb), col_slots(g["w_pool"], Wb),
                 col_slots(g["w_att"], Wb), g["w_out"].reshape(N_DEV, D // N_DEV, D),
                 jnp.transpose(g["pool_w"].reshape(G, N_DEV, POOL_GD // N_DEV, POOL_GD), (1, 0, 2, 3))]
        scatters[l] = _exchange_start(slots, True, f"scatter_start_{l}")
        return scatters[l][-1]

    loss_local, grad_x, small = _local_step(x[0], loss_target[0], L, get_layer, on_grads, tabs, cols, tok)
    small = small.at[0, -1].set(loss_local)

    moments = dict(w_in=(tr(m_w_in), tr(v_w_in)), w_ret=(m_w_ret, v_w_ret), w_pool=(m_w_pool, v_w_pool),
                   w_att=(m_w_att, v_w_att), w_out=(m_w_out, v_w_out), pool_w=(m_pool_w, v_pool_w))
    res = {k: [jnp.zeros(weights[k].shape, F32) for _ in range(4)] for k in WEIGHT_KEYS}
    after = grad_x
    small_all = None
    for l in reversed(range(L)):
        if l == 0:
            small_all, = _exchange([small], False, "gather_small_grads", deps=(after,))
            after = small_all
        srcs, lands = _exchange_wait(scatters[l], after, True, f"scatter_wait_{l}")
        firsts = []
        for k, ld, sr in zip(WEIGHT_KEYS, lands, srcs):
            parts = _own_slot(ld, lax.dynamic_index_in_dim(sr, me, 0, keepdims=False), me)
            m, v = moments[k]
            outs = _adamw(weights[k], l, parts, m, v, "adamw_" + k)
            res[k] = [lax.dynamic_update_index_in_dim(r, o, l, 0) for r, o in zip(res[k], outs)]
            firsts.append(outs[1].reshape(-1)[:1])
        after = jnp.concatenate(firsts)

    sw = _pack_small(norm_g, pool_scale, attn_q_gain, attn_k_gain, ret_decay_fwd, ret_decay_bwd, attn_sink)
    sm = _pack_small(m_norm_g, m_pool_scale, m_attn_q_gain, m_attn_k_gain, m_ret_decay_fwd, m_ret_decay_bwd,
                     m_attn_sink)
    sv_ = _pack_small(v_norm_g, v_pool_scale, v_attn_q_gain, v_attn_k_gain, v_ret_decay_fwd, v_ret_decay_bwd,
                      v_attn_sink)
    small_out = _adamw(sw[None], 0, small_all, sm[None], sv_[None], "adamw_small")
    loss = small_out[0][0, -1]
    small_res = [_unpack_small(a, D) for a in small_out]

    def ordered(i):
        ng, ps, qg, kg, af, ab, sk = small_res[i]
        return (ng, tr(res["w_in"][i]), af, ab, res["pool_w"][i], ps, qg, kg, sk,
                res["w_ret"][i], res["w_pool"][i], res["w_att"][i], res["w_out"][i])

    return (loss, grad_x[None], *ordered(0), *ordered(1), *ordered(2), *ordered(3))
```

```python
import functools

import numpy as np
import jax
import jax.numpy as jnp
from jax import lax
from jax.experimental import pallas as pl
from jax.experimental.pallas import tpu as pltpu

F32 = jnp.float32
MXU = jnp.bfloat16
ACT = jnp.bfloat16

N_DEV = 8
RMS_EPS = 1e-6
NEG_BIG = -1e30
RET_HEADS = 4
RET_HD = 256
CH = 128
RET_ROPE_BASE = 10000.0
POOL_WINDOWS = (2, 4, 8, 16)
POOL_GD = 256
POOL_PAD = 8
ATT_HD = 128
ATT_Q = 8
ATT_KV = 2
ATT_G = ATT_Q // ATT_KV
ATT_WIN = 128
ATT_BLK = 128
ATT_SPAN = 3 * ATT_BLK
ROPE_THETA = 500000.0
ROPE_HALF = 16

ADAM_LR = 0.001
ADAM_B1 = 0.9
ADAM_B2 = 0.999
ADAM_EPS = 1e-08
ADAM_WD = 0.01
ADAM_STEP = 10

VMEM_LIMIT = 48 * 1024 * 1024

NN = ((1,), (0,))
NT = ((1,), (1,))
TN = ((0,), (0,))


def _dot(a, b, dims):
    return lax.dot_general(a.astype(MXU), b.astype(MXU), (dims, ((), ())),
                           preferred_element_type=F32)


def _sigmoid(x):
    return 1.0 / (1.0 + jnp.exp(-x))


def _params(*sem):
    return pltpu.CompilerParams(dimension_semantics=sem, vmem_limit_bytes=VMEM_LIMIT)


def _sum_all(x):
    return jnp.sum(jnp.sum(x, axis=1, keepdims=True), axis=0, keepdims=True)


def _fiota(shape, dim):
    return lax.broadcasted_iota(jnp.int32, shape, dim).astype(F32)


SMEM_SPEC = pl.BlockSpec(memory_space=pltpu.SMEM)


def _matmul(a, b, mode, out_dtype, tm, tn, tk, name, res=None, deps=(), transpose_out=False):
    if mode == "tn":
        K, M = a.shape
    else:
        M, K = a.shape
    N = b.shape[0] if mode == "nt" else b.shape[1]
    tm, tn, tk = min(tm, M), min(tn, N), min(tk, K)
    assert M % tm == 0 and N % tn == 0 and K % tk == 0, (name, M, N, K, tm, tn, tk)
    nk = K // tk
    dims = {"nn": NN, "nt": NT, "tn": TN}[mode]
    a_spec = (pl.BlockSpec((tk, tm), lambda i, j, k: (k, i)) if mode == "tn"
              else pl.BlockSpec((tm, tk), lambda i, j, k: (i, k)))
    b_spec = (pl.BlockSpec((tn, tk), lambda i, j, k: (j, k)) if mode == "nt"
              else pl.BlockSpec((tk, tn), lambda i, j, k: (k, j)))
    o_spec = pl.BlockSpec((tm, tn), lambda i, j, k: (i, j))
    has_res = res is not None
    assert not (has_res and transpose_out)
    n_in = 2 + has_res + len(deps)

    def body(*refs):
        a_ref, b_ref = refs[:2]
        r_ref = refs[2] if has_res else None
        o_ref = refs[n_in]

        def finish(out):
            if has_res:
                out = out + r_ref[...]
            o_ref[...] = (out.T if transpose_out else out).astype(out_dtype)

        if nk == 1:
            finish(_dot(a_ref[...], b_ref[...], dims))
            return
        acc = refs[n_in + 1]
        k = pl.program_id(2)

        @pl.when(k == 0)
        def _():
            acc[...] = jnp.zeros_like(acc)

        acc[...] += _dot(a_ref[...], b_ref[...], dims)

        @pl.when(k == nk - 1)
        def _():
            finish(acc[...])

    ins = [a, b] + ([res] if has_res else []) + list(deps)
    in_specs = ([a_spec, b_spec] + ([o_spec] if has_res else [])
                + [pl.BlockSpec((8, 128), lambda i, j, k: (0, 0))] * len(deps))
    return pl.pallas_call(
        body, grid=(M // tm, N // tn, nk), in_specs=in_specs,
        out_specs=pl.BlockSpec((tn, tm), lambda i, j, k: (j, i)) if transpose_out else o_spec,
        out_shape=jax.ShapeDtypeStruct((N, M) if transpose_out else (M, N), out_dtype),
        scratch_shapes=[pltpu.VMEM((tm, tn), F32)] if nk > 1 else [], name=name,
        compiler_params=_params("parallel", "parallel", "arbitrary"))(*ins)


DEP_SPEC1 = pl.BlockSpec((8, 128), lambda i: (0, 0))


def _rmsnorm_fwd(x, g, deps=()):
    S, D = x.shape
    tm = min(512, S)
    assert S % tm == 0

    def body(x_ref, g_ref, *rest):
        h_ref, ht_ref = rest[-2:]
        xv = x_ref[...]
        r = lax.rsqrt(jnp.mean(xv * xv, axis=-1, keepdims=True) + RMS_EPS)
        hv = xv * r * g_ref[...]
        h_ref[...] = hv.astype(ACT)
        ht_ref[...] = hv.T.astype(ACT)

    row = pl.BlockSpec((tm, D), lambda i: (i, 0))
    return pl.pallas_call(
        body, grid=(S // tm,), in_specs=[row, pl.BlockSpec((1, D), lambda i: (0, 0))] + [DEP_SPEC1] * len(deps),
        out_specs=[row, pl.BlockSpec((D, tm), lambda i: (0, i))],
        out_shape=[jax.ShapeDtypeStruct((S, D), ACT), jax.ShapeDtypeStruct((D, S), ACT)], name="rmsnorm_fwd",
        compiler_params=_params("parallel"))(x, g, *deps)


def _rmsnorm_bwd(x, g, dh, dres):
    S, D = x.shape
    tm = min(256, S)

    def body(x_ref, g_ref, dh_ref, dr_ref, dx_ref, dxb_ref, dg_ref):
        xv = x_ref[...]
        r = lax.rsqrt(jnp.mean(xv * xv, axis=-1, keepdims=True) + RMS_EPS)
        xh = xv * r
        dhv = dh_ref[...]
        dxh = dhv * g_ref[...]
        dx = r * (dxh - xh * jnp.mean(dxh * xh, axis=-1, keepdims=True)) + dr_ref[...]
        dx_ref[...] = dx
        dxb_ref[...] = dx.astype(ACT)

        @pl.when(pl.program_id(0) == 0)
        def _():
            dg_ref[...] = jnp.zeros_like(dg_ref)

        dg_ref[...] += jnp.sum(dhv * xh, axis=0, keepdims=True)

    row = pl.BlockSpec((tm, D), lambda i: (i, 0))
    vec = pl.BlockSpec((1, D), lambda i: (0, 0))
    return pl.pallas_call(
        body, grid=(S // tm,), in_specs=[row, vec, row, row], out_specs=[row, row, vec],
        out_shape=[jax.ShapeDtypeStruct((S, D), F32), jax.ShapeDtypeStruct((S, D), ACT),
                   jax.ShapeDtypeStruct((1, D), F32)],
        name="rmsnorm_bwd", compiler_params=_params("arbitrary"))(x, g, dh, dres)


def _rot256(x, c, s):
    x1, x2 = x[:, :128], x[:, 128:]
    return jnp.concatenate([x1 * c - x2 * s, x2 * c + x1 * s], axis=1)


def _rot256_t(g, c, s):
    g1, g2 = g[:, :128], g[:, 128:]
    return jnp.concatenate([g1 * c + g2 * s, g2 * c - g1 * s], axis=1)


def _log_decay(a_ref, h, shape):
    return -jnp.exp(jnp.full(shape, a_ref[h], F32))


def _ret_state(xsrc, xbase, xscale, ysrc, ybase, cos, sin, af, ab, mode, name):
    S = xsrc.shape[0]
    nC = S // CH
    H = RET_HEADS
    W = H * RET_HD
    assert (xbase * RET_HD) % W == 0 and (ybase * RET_HD) % W == 0

    def body(x1_ref, y1_ref, c1_ref, s1_ref, x2_ref, y2_ref, c2_ref, s2_ref, af_ref, ab_ref,
             st1_ref, st2_ref, acc1, acc2):
        @pl.when(pl.program_id(0) == 0)
        def _():
            acc1[...] = jnp.zeros_like(acc1)
            acc2[...] = jnp.zeros_like(acc2)

        j = _fiota((CH, 1), 0)
        ca, sa, cb, sb_ = c1_ref[...], s1_ref[...], c2_ref[...], s2_ref[...]
        for h in range(H):
            sl = slice(h * RET_HD, (h + 1) * RET_HD)
            lgf = _log_decay(af_ref, h, (CH, 1))
            lgb = _log_decay(ab_ref, h, (CH, 1))
            if mode == "fwd":
                w1, d1 = jnp.exp(lgf * (CH - 1.0 - j)), jnp.exp(lgf[:1] * CH)
                w2, d2 = jnp.exp(lgb * j), jnp.exp(lgb[:1] * CH)
            else:
                w1, d1 = jnp.exp(lgb * (CH - j)), jnp.exp(lgb[:1] * CH)
                w2, d2 = jnp.exp(lgf * (j + 1.0)), jnp.exp(lgf[:1] * CH)
            xa = _rot256(x1_ref[:, sl], ca, sa) * xscale
            st1_ref[h] = acc1[h].astype(ACT)
            acc1[h] = d1 * acc1[h] + _dot(xa * w1, y1_ref[:, sl], TN)
            xb = _rot256(x2_ref[:, sl], cb, sb_) * xscale
            st2_ref[h] = acc2[h].astype(ACT)
            acc2[h] = d2 * acc2[h] + _dot(xb * w2, y2_ref[:, sl], TN)

    xcol, ycol = (xbase * RET_HD) // W, (ybase * RET_HD) // W
    in_specs = [
        pl.BlockSpec((CH, W), lambda c: (c, xcol)), pl.BlockSpec((CH, W), lambda c: (c, ycol)),
        pl.BlockSpec((CH, 128), lambda c: (c, 0)), pl.BlockSpec((CH, 128), lambda c: (c, 0)),
        pl.BlockSpec((CH, W), lambda c: (nC - 1 - c, xcol)), pl.BlockSpec((CH, W), lambda c: (nC - 1 - c, ycol)),
        pl.BlockSpec((CH, 128), lambda c: (nC - 1 - c, 0)),
        pl.BlockSpec((CH, 128), lambda c: (nC - 1 - c, 0)),
        SMEM_SPEC, SMEM_SPEC]
    out_specs = [pl.BlockSpec((H, None, RET_HD, RET_HD), lambda c: (0, c, 0, 0)),
                 pl.BlockSpec((H, None, RET_HD, RET_HD), lambda c: (0, nC - 1 - c, 0, 0))]
    st = jax.ShapeDtypeStruct((H, nC, RET_HD, RET_HD), ACT)
    return pl.pallas_call(
        body, grid=(nC,), in_specs=in_specs, out_specs=out_specs, out_shape=[st, st],
        scratch_shapes=[pltpu.VMEM((H, RET_HD, RET_HD), F32), pltpu.VMEM((H, RET_HD, RET_HD), F32)],
        name=name, compiler_params=_params("arbitrary"))(
            xsrc, ysrc, cos, sin, xsrc, ysrc, cos, sin, af, ab)


def _decay_mask(lgf1, lgb1):
    lag = _fiota((CH, CH), 0) - _fiota((CH, CH), 1)
    alag = jnp.abs(lag)
    return lag, jnp.where(lag >= 0, jnp.exp(lgf1 * alag), jnp.exp(lgb1 * alag))


def _ret_fwd(z, cos, sin, sf, sb, af, ab, cols):
    S = z.shape[0]
    nC = S // CH
    H = RET_HEADS
    rq, rk, rv, rg = cols
    W = H * RET_HD
    assert all((c * RET_HD) % W == 0 for c in cols)

    def body(q_ref, k_ref, v_ref, g_ref, c_ref, s_ref, sf_ref, sb_ref, af_ref, ab_ref,
             o_ref, u_ref, ut_ref):
        j = _fiota((CH, 1), 0)
        c, s = c_ref[...], s_ref[...]
        for h in range(H):
            sl = slice(h * RET_HD, (h + 1) * RET_HD)
            lgf = _log_decay(af_ref, h, (CH, 1))
            lgb = _log_decay(ab_ref, h, (CH, 1))
            q = _rot256(q_ref[:, sl], c, s)
            k = _rot256(k_ref[:, sl], c, s) * (RET_HD ** -0.5)
            _, dm = _decay_mask(lgf[:1], lgb[:1])
            p = _dot(q, k, NT) * dm
            o = (_dot(p, v_ref[:, sl], NN)
                 + _dot(q * jnp.exp(lgf * (j + 1.0)), sf_ref[h], NN)
                 + _dot(q * jnp.exp(lgb * (CH - j)), sb_ref[h], NN))
            o_ref[:, sl] = o
            on = o * lax.rsqrt(jnp.mean(o * o, axis=-1, keepdims=True) + RMS_EPS)
            g = g_ref[:, sl]
            u = on * (g * _sigmoid(g))
            u_ref[:, sl] = u.astype(ACT)
            ut_ref[sl, :] = u.T.astype(ACT)

    def zc(col):
        return pl.BlockSpec((CH, W), lambda c: (c, (col * RET_HD) // W))

    tab = pl.BlockSpec((CH, 128), lambda c: (c, 0))
    stt = pl.BlockSpec((H, None, RET_HD, RET_HD), lambda c: (0, c, 0, 0))
    out = pl.BlockSpec((CH, W), lambda c: (c, 0))
    return pl.pallas_call(
        body, grid=(nC,),
        in_specs=[zc(rq), zc(rk), zc(rv), zc(rg), tab, tab, stt, stt, SMEM_SPEC, SMEM_SPEC],
        out_specs=[out, out, pl.BlockSpec((W, CH), lambda c: (0, c))],
        out_shape=[jax.ShapeDtypeStruct((S, W), F32), jax.ShapeDtypeStruct((S, W), ACT),
                   jax.ShapeDtypeStruct((W, S), ACT)],
        name="ret_fwd", compiler_params=_params("parallel"))(
            z, z, z, z, cos, sin, sf, sb, af, ab)


def _ret_gate_bwd(du, o_pre, z, rg):
    S, W = du.shape
    H = RET_HEADS
    tm = min(512, S)
    assert S % tm == 0

    def body(du_ref, o_ref, g_ref, do_ref, dg_ref):
        o = o_ref[...]
        r = lax.rsqrt(jnp.mean(o * o, axis=-1, keepdims=True) + RMS_EPS)
        on = o * r
        g = g_ref[...]
        sg = _sigmoid(g)
        duv = du_ref[...]
        don = duv * (g * sg)
        dg_ref[...] = (duv * on * (sg * (1.0 + g * (1.0 - sg)))).astype(ACT)
        do_ref[...] = r * (don - on * jnp.mean(don * on, axis=-1, keepdims=True))

    blk = pl.BlockSpec((tm, RET_HD), lambda i, h: (i, h))
    return pl.pallas_call(
        body, grid=(S // tm, H),
        in_specs=[blk, blk, pl.BlockSpec((tm, RET_HD), lambda i, h: (i, rg + h))],
        out_specs=[blk, blk],
        out_shape=[jax.ShapeDtypeStruct((S, W), F32), jax.ShapeDtypeStruct((S, W), ACT)],
        name="ret_gate_bwd", compiler_params=_params("parallel", "parallel"))(du, o_pre, z)


def _ret_bwd(z, do, cos, sin, sf, sb, ef, eb, af, ab, cols):
    S = z.shape[0]
    nC = S // CH
    H = RET_HEADS
    rq, rk, rv, _ = cols
    W = H * RET_HD

    def body(q_ref, k_ref, v_ref, do_ref, c_ref, s_ref, sf_ref, sb_ref, ef_ref, eb_ref,
             af_ref, ab_ref, dq_ref, dk_ref, dv_ref, da_ref):
        @pl.when(pl.program_id(0) == 0)
        def _():
            da_ref[...] = jnp.zeros_like(da_ref)

        j = _fiota((CH, 1), 0)
        c, s = c_ref[...], s_ref[...]
        scale = RET_HD ** -0.5
        row = lax.broadcasted_iota(jnp.int32, (8, 128), 0)
        lane = lax.broadcasted_iota(jnp.int32, (8, 128), 1)
        for h in range(H):
            sl = slice(h * RET_HD, (h + 1) * RET_HD)
            lgf = _log_decay(af_ref, h, (CH, 1))
            lgb = _log_decay(ab_ref, h, (CH, 1))
            q = _rot256(q_ref[:, sl], c, s)
            k = _rot256(k_ref[:, sl], c, s) * scale
            v = v_ref[:, sl]
            do = do_ref[:, sl]
            sf_, sb_, ef_, eb_ = sf_ref[h], sb_ref[h], ef_ref[h], eb_ref[h]
            a_w = jnp.exp(lgf * (j + 1.0))
            b_w = jnp.exp(lgb * (CH - j))
            wf = jnp.exp(lgf * (CH - 1.0 - j))
            wb = jnp.exp(lgb * j)
            lag, dm = _decay_mask(lgf[:1], lgb[:1])
            sc = _dot(q, k, NT)
            gg = _dot(do, v, NT)
            dg = gg * dm
            x1 = _dot(do, sf_, NT) * a_w
            x2 = _dot(do, sb_, NT) * b_w
            y1 = _dot(v, ef_, NT) * wf
            y2 = _dot(v, eb_, NT) * wb
            dq = _dot(dg, k, NN) + x1 + x2
            dk = _dot(dg, q, TN) + y1 + y2
            dv = _dot(sc * dm, do, TN) + _dot(k * wf, ef_, NN) + _dot(k * wb, eb_, NN)
            dq_ref[:, sl] = _rot256_t(dq, c, s).astype(ACT)
            dk_ref[:, sl] = (_rot256_t(dk, c, s) * scale).astype(ACT)
            dv_ref[:, sl] = dv.astype(ACT)
            t = dm * gg * sc
            qx1 = jnp.sum(q * x1, axis=-1, keepdims=True)
            qx2 = jnp.sum(q * x2, axis=-1, keepdims=True)
            ky1 = jnp.sum(k * y1, axis=-1, keepdims=True)
            ky2 = jnp.sum(k * y2, axis=-1, keepdims=True)
            dlf = (_sum_all(jnp.where(lag > 0, lag * t, 0.0))
                   + _sum_all((j + 1.0) * qx1 + (CH - 1.0 - j) * ky1)
                   + CH * jnp.exp(lgf[:1] * CH) * _sum_all(ef_.astype(F32) * sf_.astype(F32)))
            dlb = (_sum_all(jnp.where(lag < 0, -lag * t, 0.0))
                   + _sum_all((CH - j) * qx2 + j * ky2)
                   + CH * jnp.exp(lgb[:1] * CH) * _sum_all(eb_.astype(F32) * sb_.astype(F32)))
            da_ref[h] += jnp.where((row == 0) & (lane == 0), dlf * lgf[:1],
                                   jnp.where((row == 0) & (lane == 1), dlb * lgb[:1], 0.0))

    def zc(col):
        return pl.BlockSpec((CH, W), lambda c: (c, (col * RET_HD) // W))

    tab = pl.BlockSpec((CH, 128), lambda c: (c, 0))
    stt = pl.BlockSpec((H, None, RET_HD, RET_HD), lambda c: (0, c, 0, 0))
    out = pl.BlockSpec((CH, W), lambda c: (c, 0))
    dz = jax.ShapeDtypeStruct((S, W), ACT)
    return pl.pallas_call(
        body, grid=(nC,),
        in_specs=[zc(rq), zc(rk), zc(rv), out, tab, tab, stt, stt, stt, stt, SMEM_SPEC, SMEM_SPEC],
        out_specs=[out, out, out, pl.BlockSpec((H, 8, 128), lambda c: (0, 0, 0))],
        out_shape=[dz, dz, dz, jax.ShapeDtypeStruct((H, 8, 128), F32)],
        name="ret_bwd", compiler_params=_params("arbitrary"))(
            z, z, z, do, cos, sin, sf, sb, ef, eb, af, ab)


def _fill_padded(pad_ref, src_ref, S):
    zeros = jnp.zeros((POOL_PAD, POOL_GD), F32)
    pad_ref[pl.ds(0, POOL_PAD), :] = zeros
    pad_ref[pl.ds(POOL_PAD, S), :] = src_ref[...]
    pad_ref[pl.ds(S + POOL_PAD, POOL_PAD), :] = zeros


def _window_sum(ext, T, lo, hi):
    n = T + 2 * POOL_PAD
    acc = None
    for k in range(lo, hi):
        sh = ext if k == 0 else pltpu.roll(ext, (-k) % n, 0)
        piece = sh[POOL_PAD:POOL_PAD + T]
        acc = piece if acc is None else acc + piece
    return acc


def _window_count(pos, w, S):
    lo = jnp.maximum(pos - w // 2, 0)
    hi = jnp.minimum(pos + w // 2, S)
    return (hi - lo).astype(F32)


def _pool_fwd(z, pw, scale, pv, pg):
    S = z.shape[0]
    G = len(POOL_WINDOWS)
    T = min(512, S)
    W = G * POOL_GD

    def body(x_ref, g_ref, pw_ref, sc_ref, y_ref, u_ref, ut_ref, pad, p_scr):
        grp = pl.program_id(0)
        i = pl.program_id(1)

        @pl.when(i == 0)
        def _():
            _fill_padded(pad, x_ref, S)

        r0 = pl.multiple_of(i * T, T)
        ext = pad[pl.ds(r0, T + 2 * POOL_PAD), :]
        pos = r0 + lax.broadcasted_iota(jnp.int32, (T, 1), 0)
        for gi, w in enumerate(POOL_WINDOWS):
            @pl.when(grp == gi)
            def _(w=w):
                acc = _window_sum(ext, T, -(w // 2), w // 2)
                p_scr[...] = acc / _window_count(pos, w, S) - ext[POOL_PAD:POOL_PAD + T]

        y = _dot(p_scr[...], pw_ref[...], NN)
        y_ref[...] = y
        g = g_ref[...]
        u = y * sc_ref[...] * (g * _sigmoid(g))
        u_ref[...] = u.astype(ACT)
        ut_ref[...] = u.T.astype(ACT)

    blk = pl.BlockSpec((T, POOL_GD), lambda g, i: (i, g))
    return pl.pallas_call(
        body, grid=(G, S // T),
        in_specs=[pl.BlockSpec((S, POOL_GD), lambda g, i: (0, pv + g)),
                  pl.BlockSpec((T, POOL_GD), lambda g, i: (i, pg + g)),
                  pl.BlockSpec((None, POOL_GD, POOL_GD), lambda g, i: (g, 0, 0)),
                  pl.BlockSpec((1, POOL_GD), lambda g, i: (0, g))],
        out_specs=[blk, blk, pl.BlockSpec((POOL_GD, T), lambda g, i: (g, i))],
        out_shape=[jax.ShapeDtypeStruct((S, W), F32), jax.ShapeDtypeStruct((S, W), ACT),
                   jax.ShapeDtypeStruct((W, S), ACT)],
        scratch_shapes=[pltpu.VMEM((S + 2 * POOL_PAD, POOL_GD), F32), pltpu.VMEM((T, POOL_GD), F32)],
        name="pool_fwd", compiler_params=_params("parallel", "arbitrary"))(z, z, pw, scale)


def _pool_bwd_a(z, pw, scale, y_raw, du, pv, pg):
    S = z.shape[0]
    G = len(POOL_WINDOWS)
    T = min(512, S)
    W = G * POOL_GD

    def body(x_ref, g_ref, pw_ref, sc_ref, y_ref, du_ref, dpc_ref, dg_ref, dsc_ref, dpw_ref,
             pad, p_scr, c_scr):
        grp = pl.program_id(0)
        i = pl.program_id(1)

        @pl.when(i == 0)
        def _():
            _fill_padded(pad, x_ref, S)
            dsc_ref[...] = jnp.zeros_like(dsc_ref)
            dpw_ref[...] = jnp.zeros_like(dpw_ref)

        r0 = pl.multiple_of(i * T, T)
        ext = pad[pl.ds(r0, T + 2 * POOL_PAD), :]
        pos = r0 + lax.broadcasted_iota(jnp.int32, (T, 1), 0)
        for gi, w in enumerate(POOL_WINDOWS):
            @pl.when(grp == gi)
            def _(w=w):
                cnt = _window_count(pos, w, S)
                acc = _window_sum(ext, T, -(w // 2), w // 2)
                p_scr[...] = acc / cnt - ext[POOL_PAD:POOL_PAD + T]
                c_scr[...] = jnp.broadcast_to(cnt, (T, 128))

        g = g_ref[...]
        sg = _sigmoid(g)
        duv = du_ref[...]
        y = y_ref[...]
        scl = sc_ref[...]
        dy = duv * (scl * (g * sg))
        dg_ref[...] = (duv * y * scl * (sg * (1.0 + g * (1.0 - sg)))).astype(ACT)
        dsc_ref[...] += jnp.sum(duv * y * (g * sg), axis=0, keepdims=True)
        dpw_ref[...] += _dot(p_scr[...], dy, TN)
        dpc_ref[...] = _dot(dy, pw_ref[...], NT) / c_scr[:, :1]

    blk = pl.BlockSpec((T, POOL_GD), lambda g, i: (i, g))
    return pl.pallas_call(
        body, grid=(G, S // T),
        in_specs=[pl.BlockSpec((S, POOL_GD), lambda g, i: (0, pv + g)),
                  pl.BlockSpec((T, POOL_GD), lambda g, i: (i, pg + g)),
                  pl.BlockSpec((None, POOL_GD, POOL_GD), lambda g, i: (g, 0, 0)),
                  pl.BlockSpec((1, POOL_GD), lambda g, i: (0, g)), blk, blk],
        out_specs=[blk, blk, pl.BlockSpec((1, POOL_GD), lambda g, i: (0, g)),
                   pl.BlockSpec((None, POOL_GD, POOL_GD), lambda g, i: (g, 0, 0))],
        out_shape=[jax.ShapeDtypeStruct((S, W), F32), jax.ShapeDtypeStruct((S, W), ACT),
                   jax.ShapeDtypeStruct((1, W), F32), jax.ShapeDtypeStruct((G, POOL_GD, POOL_GD), F32)],
        scratch_shapes=[pltpu.VMEM((S + 2 * POOL_PAD, POOL_GD), F32), pltpu.VMEM((T, POOL_GD), F32),
                        pltpu.VMEM((T, 128), F32)],
        name="pool_bwd_a", compiler_params=_params("parallel", "arbitrary"))(z, z, pw, scale, y_raw, du)


def _pool_bwd_b(dpc):
    S, W = dpc.shape
    G = len(POOL_WINDOWS)
    T = min(512, S)

    def body(x_ref, o_ref, pad, acc_scr):
        grp = pl.program_id(0)
        i = pl.program_id(1)

        @pl.when(i == 0)
        def _():
            _fill_padded(pad, x_ref, S)

        r0 = pl.multiple_of(i * T, T)
        ext = pad[pl.ds(r0, T + 2 * POOL_PAD), :]
        pos = r0 + lax.broadcasted_iota(jnp.int32, (T, 1), 0)
        for gi, w in enumerate(POOL_WINDOWS):
            @pl.when(grp == gi)
            def _(w=w):
                acc = _window_sum(ext, T, -(w // 2) + 1, w // 2 + 1)
                acc_scr[...] = acc - ext[POOL_PAD:POOL_PAD + T] * _window_count(pos, w, S)

        o_ref[...] = acc_scr[...].astype(ACT)

    blk = pl.BlockSpec((T, POOL_GD), lambda g, i: (i, g))
    return pl.pallas_call(
        body, grid=(G, S // T),
        in_specs=[pl.BlockSpec((S, POOL_GD), lambda g, i: (0, g))], out_specs=blk,
        out_shape=jax.ShapeDtypeStruct((S, W), ACT),
        scratch_shapes=[pltpu.VMEM((S + 2 * POOL_PAD, POOL_GD), F32), pltpu.VMEM((T, POOL_GD), F32)],
        name="pool_bwd_b", compiler_params=_params("parallel", "arbitrary"))(dpc)


def _rope128(x, cf, sa, sb):
    return x * cf + pltpu.roll(x, ROPE_HALF, 1) * sa + pltpu.roll(x, ATT_HD - ROPE_HALF, 1) * sb


def _rope128_t(g, cf, sa, sb):
    return g * cf + pltpu.roll(g * sa, ATT_HD - ROPE_HALF, 1) + pltpu.roll(g * sb, ROPE_HALF, 1)


def _attn_prep(z, qgain, kgain, cf, sa, sb, aq, ak, av):
    S = z.shape[0]
    T = min(512, S)
    QW, KW = ATT_Q * ATT_HD, ATT_KV * ATT_HD

    def body(q_ref, k_ref, v_ref, qg_ref, kg_ref, cf_ref, sa_ref, sb_ref, qn_ref, kn_ref, vb_ref):
        cfv, sav, sbv = cf_ref[...], sa_ref[...], sb_ref[...]

        def prep(x, gain):
            r = lax.rsqrt(jnp.mean(x * x, axis=-1, keepdims=True) + RMS_EPS)
            return _rope128(x * r * gain, cfv, sav, sbv)

        for hh in range(ATT_Q):
            sl = slice(hh * ATT_HD, (hh + 1) * ATT_HD)
            qn_ref[:, sl] = prep(q_ref[:, sl], qg_ref[...]).astype(ACT)
        for hh in range(ATT_KV):
            sl = slice(hh * ATT_HD, (hh + 1) * ATT_HD)
            kn_ref[:, sl] = prep(k_ref[:, sl], kg_ref[...]).astype(ACT)
        vb_ref[...] = v_ref[...].astype(ACT)

    tab = pl.BlockSpec((T, ATT_HD), lambda i: (i, 0))
    gain = pl.BlockSpec((1, ATT_HD), lambda i: (0, 0))
    return pl.pallas_call(
        body, grid=(S // T,),
        in_specs=[pl.BlockSpec((T, QW), lambda i: (i, aq)), pl.BlockSpec((T, KW), lambda i: (i, ak)),
                  pl.BlockSpec((T, KW), lambda i: (i, av)), gain, gain, tab, tab, tab],
        out_specs=[pl.BlockSpec((T, QW), lambda i: (i, 0)), pl.BlockSpec((T, KW), lambda i: (i, 0)),
                   pl.BlockSpec((T, KW), lambda i: (i, 0))],
        out_shape=[jax.ShapeDtypeStruct((S, QW), ACT), jax.ShapeDtypeStruct((S, KW), ACT),
                   jax.ShapeDtypeStruct((S, KW), ACT)],
        name="attn_prep", compiler_params=_params("parallel"))(z, z, z, qgain, kgain, cf, sa, sb)


def _attn_window(i, S):
    start = jnp.clip(i * ATT_BLK - ATT_BLK, 0, S - ATT_SPAN)
    start = pl.multiple_of(start, ATT_BLK)
    qpos = i * ATT_BLK + lax.broadcasted_iota(jnp.int32, (ATT_BLK, ATT_SPAN), 0)
    kpos = start + lax.broadcasted_iota(jnp.int32, (ATT_BLK, ATT_SPAN), 1)
    return start, jnp.abs(kpos - qpos) <= ATT_WIN


def _attn_probs(q, kw, valid, sink):
    s = _dot(q, kw, NT) * (ATT_HD ** -0.5)
    s = jnp.where(valid, s, NEG_BIG)
    m = jnp.maximum(jnp.max(s, axis=-1, keepdims=True), sink)
    p = jnp.exp(s - m)
    es = jnp.exp(sink - m)
    den = jnp.sum(p, axis=-1, keepdims=True) + es
    return p / den, es / den


def _attn_fwd(qn, kn, vb, z, sink, ag):
    S = qn.shape[0]
    nB = S // ATT_BLK
    assert S >= ATT_SPAN
    QW = ATT_Q * ATT_HD

    GW = ATT_G * ATT_HD

    def body(q_ref, k_ref, v_ref, g_ref, sink_ref, o_ref, u_ref, ut_ref):
        kvh = pl.program_id(0)
        i = pl.program_id(1)
        start, valid = _attn_window(i, S)
        kw = k_ref[pl.ds(start, ATT_SPAN), :]
        vw = v_ref[pl.ds(start, ATT_SPAN), :]
        for gi in range(ATT_G):
            sl = slice(gi * ATT_HD, (gi + 1) * ATT_HD)
            sk = jnp.full((ATT_BLK, 1), sink_ref[kvh * ATT_G + gi], F32)
            pn, _ = _attn_probs(q_ref[:, sl], kw, valid, sk)
            o = _dot(pn, vw, NN)
            o_ref[:, sl] = o
            g = g_ref[:, sl]
            u = o * (g * _sigmoid(g))
            u_ref[:, sl] = u.astype(ACT)
            ut_ref[sl, :] = u.T.astype(ACT)

    blk = pl.BlockSpec((ATT_BLK, GW), lambda k, i: (i, k))
    kv = pl.BlockSpec((S, ATT_HD), lambda k, i: (0, k))
    return pl.pallas_call(
        body, grid=(ATT_KV, nB),
        in_specs=[blk, kv, kv, pl.BlockSpec((ATT_BLK, GW), lambda k, i: (i, ag // ATT_G + k)), SMEM_SPEC],
        out_specs=[blk, blk, pl.BlockSpec((GW, ATT_BLK), lambda k, i: (k, i))],
        out_shape=[jax.ShapeDtypeStruct((S, QW), F32), jax.ShapeDtypeStruct((S, QW), ACT),
                   jax.ShapeDtypeStruct((QW, S), ACT)],
        name="attn_fwd", compiler_params=_params("parallel", "parallel"))(qn, kn, vb, z, sink)


def _attn_bwd(qn, kn, vb, o, du, z, sink, ag):
    S = qn.shape[0]
    nB = S // ATT_BLK
    QW, KW = ATT_Q * ATT_HD, ATT_KV * ATT_HD
    GW = ATT_G * ATT_HD

    def body(q_ref, k_ref, v_ref, o_ref, du_ref, g_ref, sink_ref,
             dq_ref, dk_ref, dv_ref, dg_ref, ds_ref):
        kvh = pl.program_id(0)
        i = pl.program_id(1)

        @pl.when(i == 0)
        def _():
            dk_ref[...] = jnp.zeros_like(dk_ref)
            dv_ref[...] = jnp.zeros_like(dv_ref)
            ds_ref[...] = jnp.zeros_like(ds_ref)

        start, valid = _attn_window(i, S)
        kw = k_ref[pl.ds(start, ATT_SPAN), :]
        vw = v_ref[pl.ds(start, ATT_SPAN), :]
        row = lax.broadcasted_iota(jnp.int32, (8, 128), 0)
        lane = lax.broadcasted_iota(jnp.int32, (8, 128), 1)
        dsink = jnp.zeros((8, 128), F32)
        dk_acc = jnp.zeros((ATT_SPAN, ATT_HD), F32)
        dv_acc = jnp.zeros((ATT_SPAN, ATT_HD), F32)
        for gi in range(ATT_G):
            sl = slice(gi * ATT_HD, (gi + 1) * ATT_HD)
            q = q_ref[:, sl]
            ov = o_ref[:, sl]
            g = g_ref[:, sl]
            duv = du_ref[:, sl]
            sg = _sigmoid(g)
            do = duv * (g * sg)
            dg_ref[:, sl] = (duv * ov * (sg * (1.0 + g * (1.0 - sg)))).astype(ACT)
            sk = jnp.full((ATT_BLK, 1), sink_ref[kvh * ATT_G + gi], F32)
            pn, psink = _attn_probs(q, kw, valid, sk)
            delta = jnp.sum(do * ov, axis=-1, keepdims=True)
            dsc = pn * (_dot(do, vw, NT) - delta) * (ATT_HD ** -0.5)
            dq_ref[:, sl] = _dot(dsc, kw, NN)
            dk_acc = dk_acc + _dot(dsc, q, TN)
            dv_acc = dv_acc + _dot(pn, do, TN)
            dsink = dsink + jnp.where((row == 0) & (lane == gi), -_sum_all(psink * delta), 0.0)
        dk_ref[pl.ds(start, ATT_SPAN), :] += dk_acc
        dv_ref[pl.ds(start, ATT_SPAN), :] += dv_acc
        ds_ref[...] += dsink

    grp = pl.BlockSpec((ATT_BLK, GW), lambda k, i: (i, k))
    kv = pl.BlockSpec((S, ATT_HD), lambda k, i: (0, k))
    return pl.pallas_call(
        body, grid=(ATT_KV, nB),
        in_specs=[grp, kv, kv, grp, grp,
                  pl.BlockSpec((ATT_BLK, GW), lambda k, i: (i, ag // ATT_G + k)), SMEM_SPEC],
        out_specs=[grp, kv, kv, grp, pl.BlockSpec((None, 8, 128), lambda k, i: (k, 0, 0))],
        out_shape=[jax.ShapeDtypeStruct((S, QW), F32), jax.ShapeDtypeStruct((S, KW), F32),
                   jax.ShapeDtypeStruct((S, KW), F32), jax.ShapeDtypeStruct((S, QW), ACT),
                   jax.ShapeDtypeStruct((ATT_KV, 8, 128), F32)],
        name="attn_bwd", compiler_params=_params("parallel", "arbitrary"))(qn, kn, vb, o, du, z, sink)


def _attn_prep_bwd(z, dqn, dkn, dv, qgain, kgain, cf, sa, sb, aq, ak):
    S = z.shape[0]
    T = min(512, S)
    QW, KW = ATT_Q * ATT_HD, ATT_KV * ATT_HD

    def body(q_ref, k_ref, dqn_ref, dkn_ref, dv_ref, qg_ref, kg_ref, cf_ref, sa_ref, sb_ref,
             dq_ref, dk_ref, dvb_ref, dqg_ref, dkg_ref):
        cfv, sav, sbv = cf_ref[...], sa_ref[...], sb_ref[...]

        @pl.when(pl.program_id(0) == 0)
        def _():
            dqg_ref[...] = jnp.zeros_like(dqg_ref)
            dkg_ref[...] = jnp.zeros_like(dkg_ref)

        def back(x, gn, gain):
            r = lax.rsqrt(jnp.mean(x * x, axis=-1, keepdims=True) + RMS_EPS)
            xh = x * r
            dy = _rope128_t(gn, cfv, sav, sbv)
            dxh = dy * gain
            dx = r * (dxh - xh * jnp.mean(dxh * xh, axis=-1, keepdims=True))
            return dx, jnp.sum(dy * xh, axis=0, keepdims=True)

        dqg = jnp.zeros((1, ATT_HD), F32)
        for hh in range(ATT_Q):
            sl = slice(hh * ATT_HD, (hh + 1) * ATT_HD)
            dx, dgn = back(q_ref[:, sl], dqn_ref[:, sl], qg_ref[...])
            dq_ref[:, sl] = dx.astype(ACT)
            dqg = dqg + dgn
        dkg = jnp.zeros((1, ATT_HD), F32)
        for hh in range(ATT_KV):
            sl = slice(hh * ATT_HD, (hh + 1) * ATT_HD)
            dx, dgn = back(k_ref[:, sl], dkn_ref[:, sl], kg_ref[...])
            dk_ref[:, sl] = dx.astype(ACT)
            dkg = dkg + dgn
        dvb_ref[...] = dv_ref[...].astype(ACT)
        dqg_ref[...] += dqg
        dkg_ref[...] += dkg

    tab = pl.BlockSpec((T, ATT_HD), lambda i: (i, 0))
    gain = pl.BlockSpec((1, ATT_HD), lambda i: (0, 0))
    qb = pl.BlockSpec((T, QW), lambda i: (i, 0))
    kb = pl.BlockSpec((T, KW), lambda i: (i, 0))
    return pl.pallas_call(
        body, grid=(S // T,),
        in_specs=[pl.BlockSpec((T, QW), lambda i: (i, aq)), pl.BlockSpec((T, KW), lambda i: (i, ak)),
                  qb, kb, kb, gain, gain, tab, tab, tab],
        out_specs=[qb, kb, kb, gain, gain],
        out_shape=[jax.ShapeDtypeStruct((S, QW), ACT), jax.ShapeDtypeStruct((S, KW), ACT),
                   jax.ShapeDtypeStruct((S, KW), ACT), jax.ShapeDtypeStruct((1, ATT_HD), F32),
                   jax.ShapeDtypeStruct((1, ATT_HD), F32)],
        name="attn_prep_bwd", compiler_params=_params("arbitrary"))(
            z, z, dqn, dkn, dv, qgain, kgain, cf, sa, sb)


def _branch_merge(ua, ub, uc, wr, wp, wa, z, mg):
    S, W = ua.shape
    D = wr.shape[1]
    tm, tn = min(512, S), min(512, D)
    nb = D // tn

    def body(ua_ref, ub_ref, uc_ref, wr_ref, wp_ref, wa_ref, g0_ref, g1_ref, g2_ref,
             ya_ref, yb_ref, yc_ref, m_ref, mt_ref):
        ya = _dot(ua_ref[...], wr_ref[...], NN)
        yb = _dot(ub_ref[...], wp_ref[...], NN)
        yc = _dot(uc_ref[...], wa_ref[...], NN)
        ya_ref[...] = ya.astype(ACT)
        yb_ref[...] = yb.astype(ACT)
        yc_ref[...] = yc.astype(ACT)
        m = _sigmoid(g0_ref[...]) * ya + _sigmoid(g1_ref[...]) * yb + _sigmoid(g2_ref[...]) * yc
        m_ref[...] = m.astype(ACT)
        mt_ref[...] = m.T.astype(ACT)

    u = pl.BlockSpec((tm, W), lambda i, j: (i, 0))
    w = pl.BlockSpec((W, tn), lambda i, j: (0, j))
    o = pl.BlockSpec((tm, tn), lambda i, j: (i, j))

    assert (mg * POOL_GD) % tn == 0
    base = (mg * POOL_GD) // tn

    def gate(k):
        return pl.BlockSpec((tm, tn), lambda i, j: (i, base + k * nb + j))

    sd = jax.ShapeDtypeStruct((S, D), ACT)
    return pl.pallas_call(
        body, grid=(S // tm, nb), in_specs=[u, u, u, w, w, w, gate(0), gate(1), gate(2)],
        out_specs=[o, o, o, o, pl.BlockSpec((tn, tm), lambda i, j: (j, i))],
        out_shape=[sd, sd, sd, sd, jax.ShapeDtypeStruct((D, S), ACT)], name="branch_merge",
        compiler_params=_params("parallel", "parallel"))(ua, ub, uc, wr, wp, wa, z, z, z)


def _merge_bwd(dxb, wo, ya, yb, yc, z, mg):
    S, D = dxb.shape
    tm, tn = min(512, S), min(512, D)
    nb = D // tn
    base = (mg * POOL_GD) // tn

    def body(dx_ref, wo_ref, ya_ref, yb_ref, yc_ref, g0_ref, g1_ref, g2_ref,
             da_ref, db_ref, dc_ref, dg0_ref, dg1_ref, dg2_ref):
        dm = _dot(dx_ref[...], wo_ref[...], NT)
        for y_ref, g_ref, dy_ref, dg_ref in ((ya_ref, g0_ref, da_ref, dg0_ref),
                                             (yb_ref, g1_ref, db_ref, dg1_ref),
                                             (yc_ref, g2_ref, dc_ref, dg2_ref)):
            sg = _sigmoid(g_ref[...])
            dy_ref[...] = (sg * dm).astype(ACT)
            dg_ref[...] = (dm * y_ref[...].astype(F32) * (sg * (1.0 - sg))).astype(ACT)

    o = pl.BlockSpec((tm, tn), lambda i, j: (i, j))

    def gate(k):
        return pl.BlockSpec((tm, tn), lambda i, j: (i, base + k * nb + j))

    sd = jax.ShapeDtypeStruct((S, D), ACT)
    return pl.pallas_call(
        body, grid=(S // tm, nb),
        in_specs=[pl.BlockSpec((tm, D), lambda i, j: (i, 0)), pl.BlockSpec((tn, D), lambda i, j: (j, 0)),
                  o, o, o, gate(0), gate(1), gate(2)],
        out_specs=[o] * 6, out_shape=[sd] * 6, name="merge_bwd",
        compiler_params=_params("parallel", "parallel"))(dxb, wo, ya, yb, yc, z, z, z)


def _loss_head(y, t):
    S, D = y.shape
    tm = min(256, S)

    def body(y_ref, t_ref, dy_ref, dyb_ref, l_ref):
        e = y_ref[...] - t_ref[...]
        dy = e * (1.0 / D)
        dy_ref[...] = dy
        dyb_ref[...] = dy.astype(ACT)

        @pl.when(pl.program_id(0) == 0)
        def _():
            l_ref[...] = jnp.zeros_like(l_ref)

        l_ref[...] += jnp.sum(jnp.mean(e * e, axis=-1, keepdims=True), axis=0, keepdims=True)

    row = pl.BlockSpec((tm, D), lambda i: (i, 0))
    return pl.pallas_call(
        body, grid=(S // tm,), in_specs=[row, row],
        out_specs=[row, row, pl.BlockSpec((1, 1), lambda i: (0, 0))],
        out_shape=[jax.ShapeDtypeStruct((S, D), F32), jax.ShapeDtypeStruct((S, D), ACT),
                   jax.ShapeDtypeStruct((1, 1), F32)],
        name="loss_head", compiler_params=_params("arbitrary"))(y, t)


def _adamw(w, layer, parts, m, v, name):
    L = w.shape[0]
    shape = w.shape[1:]
    C = shape[-1]
    R = int(np.prod(shape[:-1]))
    w3, m3, v3 = (a.reshape(L, R, C) for a in (w, m, v))
    p3 = parts.reshape(N_DEV, R, C)
    tr = next((c for c in (1024, 512, 256, 128, 64) if R % c == 0 and c * C * 4 <= (1 << 20)), min(64, R))
    assert R % tr == 0
    c1 = 1.0 / (1.0 - ADAM_B1 ** ADAM_STEP)
    c2 = 1.0 / (1.0 - ADAM_B2 ** ADAM_STEP)

    def body(w_ref, p_ref, m_ref, v_ref, g_ref, d_ref, nm_ref, nv_ref):
        g = p_ref[0].astype(F32)
        for k in range(1, N_DEV):
            g = g + p_ref[k].astype(F32)
        nm = ADAM_B1 * m_ref[...] + (1.0 - ADAM_B1) * g
        nv = ADAM_B2 * v_ref[...] + (1.0 - ADAM_B2) * (g * g)
        g_ref[...] = g
        nm_ref[...] = nm
        nv_ref[...] = nv
        d_ref[...] = -ADAM_LR * ((nm * c1) / (jnp.sqrt(nv * c2) + ADAM_EPS) + ADAM_WD * w_ref[...])

    lay = pl.BlockSpec((None, tr, C), lambda i: (layer, i, 0))
    out = pl.BlockSpec((tr, C), lambda i: (i, 0))
    sd = jax.ShapeDtypeStruct((R, C), F32)
    outs = pl.pallas_call(
        body, grid=(R // tr,),
        in_specs=[lay, pl.BlockSpec((N_DEV, tr, C), lambda i: (0, i, 0)), lay, lay],
        out_specs=[out] * 4, out_shape=[sd] * 4, name=name,
        compiler_params=_params("parallel"))(w3, p3, m3, v3)
    return [a.reshape(shape) for a in outs]


def _exchange(arrs, scatter, name, deps=()):
    n = len(arrs)
    nd = len(deps)
    out_shape = [jax.ShapeDtypeStruct(a.shape if scatter else (N_DEV,) + a.shape, a.dtype) for a in arrs]

    def body(*refs):
        ins, outs = refs[:n], refs[n + nd:2 * n + nd]
        send_sems, recv_sems, local_sems = refs[2 * n + nd:]
        x, y, c = lax.axis_index("x"), lax.axis_index("y"), lax.axis_index("c")
        me = 4 * x + 2 * y + c
        copies = []
        for a in range(n):
            src = ins[a].at[me] if scatter else ins[a]
            own = pltpu.make_async_copy(src, outs[a].at[me], local_sems.at[a])
            own.start()
            copies.append(own)
        sends, recvs = [], []
        for k in range(1, N_DEV):
            px, py, pc = x ^ (k >> 2), y ^ ((k >> 1) & 1), c ^ (k & 1)
            peer = 4 * px + 2 * py + pc
            for a in range(n):
                src = ins[a].at[peer] if scatter else ins[a]
                cp = pltpu.make_async_remote_copy(
                    src_ref=src, dst_ref=outs[a].at[me],
                    send_sem=send_sems.at[a, k - 1], recv_sem=recv_sems.at[a, k - 1],
                    device_id=(px, py, pc), device_id_type=pl.DeviceIdType.MESH)
                cp.start()
                sends.append(cp)
                recvs.append(pltpu.make_async_remote_copy(
                    src_ref=src, dst_ref=outs[a].at[peer],
                    send_sem=send_sems.at[a, k - 1], recv_sem=recv_sems.at[a, k - 1],
                    device_id=(px, py, pc), device_id_type=pl.DeviceIdType.MESH))
        for cp in recvs:
            cp.wait_recv()
        for cp in sends:
            cp.wait_send()
        for cp in copies:
            cp.wait()

    any_spec = pl.BlockSpec(memory_space=pl.ANY)
    return pl.pallas_call(
        body, in_specs=[any_spec] * (n + nd), out_specs=[any_spec] * n, out_shape=out_shape,
        scratch_shapes=[pltpu.SemaphoreType.DMA((n, N_DEV - 1)), pltpu.SemaphoreType.DMA((n, N_DEV - 1)),
                        pltpu.SemaphoreType.DMA((n,))],
        name=name)(*arrs, *deps)


HBM_SPEC = pl.BlockSpec(memory_space=pltpu.HBM)
SEM_SPEC = pl.BlockSpec(memory_space=pltpu.SEMAPHORE)
DATAFLOW = pltpu.SideEffectType.DATAFLOW_SIDE_EFFECTING


def _peer_of(k):
    x, y, c = lax.axis_index("x"), lax.axis_index("y"), lax.axis_index("c")
    return x ^ (k >> 2), y ^ ((k >> 1) & 1), c ^ (k & 1)


def _exchange_copy(k, a, src_ref, land_ref, send_sems, recv_sems, scatter, outgoing):
    px, py, pc = _peer_of(k)
    peer = 4 * px + 2 * py + pc
    me = 4 * lax.axis_index("x") + 2 * lax.axis_index("y") + lax.axis_index("c")
    idx = a * (N_DEV - 1) + k - 1
    return pltpu.make_async_remote_copy(
        src_ref=src_ref.at[peer] if scatter else src_ref, dst_ref=land_ref.at[me if outgoing else peer],
        send_sem=send_sems.at[idx], recv_sem=recv_sems.at[idx],
        device_id=(px, py, pc), device_id_type=pl.DeviceIdType.MESH)


def _exchange_start(arrs, scatter, name):
    n = len(arrs)
    land_shapes = [a.shape if scatter else (N_DEV,) + a.shape for a in arrs]

    def body(*refs):
        srcs, lands = refs[:n], refs[n:2 * n]
        send_sems, recv_sems = refs[2 * n], refs[2 * n + 1]
        token = refs[-1]
        for k in range(1, N_DEV):
            for a in range(n):
                _exchange_copy(k, a, srcs[a], lands[a], send_sems, recv_sems, scatter, True).start()
        token[...] = jnp.zeros_like(token)

    sems = pltpu.SemaphoreType.DMA((n * (N_DEV - 1),))
    out_shape = ([sems, sems] + [pltpu.HBM(a.shape, a.dtype) for a in arrs]
                 + [pltpu.HBM(s, a.dtype) for s, a in zip(land_shapes, arrs)]
                 + [jax.ShapeDtypeStruct((8, 128), F32)])
    ins = ([pltpu.with_memory_space_constraint(a, pltpu.HBM) for a in arrs]
           + [pltpu.with_memory_space_constraint(lax.empty(s, a.dtype), pltpu.HBM) for s, a in zip(land_shapes, arrs)])
    res = pl.pallas_call(
        body, name=name, out_shape=out_shape, in_specs=[HBM_SPEC] * (2 * n),
        out_specs=[SEM_SPEC, SEM_SPEC] + [HBM_SPEC] * (2 * n) + [pl.BlockSpec(memory_space=pltpu.VMEM)],
        input_output_aliases={i: 2 + i for i in range(2 * n)},
        compiler_params=pltpu.CompilerParams(has_side_effects=DATAFLOW))(*ins)
    return res[0], res[1], list(res[2:2 + n]), list(res[2 + n:2 + 2 * n]), res[-1]


def _exchange_wait(started, after, scatter, name):
    send_sems, recv_sems, srcs, lands, _ = started
    n = len(srcs)

    def body(*refs):
        src_refs, land_refs = refs[:n], refs[n:2 * n]
        s_sems, r_sems = refs[2 * n], refs[2 * n + 1]
        for k in range(1, N_DEV):
            for a in range(n):
                back = _exchange_copy(k, a, src_refs[a], land_refs[a], s_sems, r_sems, scatter, False)
                back.wait_send()
                back.wait_recv()

    out_shape = [pltpu.HBM(a.shape, a.dtype) for a in srcs] + [pltpu.HBM(a.shape, a.dtype) for a in lands]
    res = pl.pallas_call(
        body, name=name, out_shape=out_shape,
        in_specs=[HBM_SPEC] * (2 * n) + [SEM_SPEC, SEM_SPEC, pl.BlockSpec(memory_space=pl.ANY)],
        out_specs=[HBM_SPEC] * (2 * n), input_output_aliases={i: i for i in range(2 * n)},
        compiler_params=pltpu.CompilerParams(has_side_effects=DATAFLOW))(*srcs, *lands, send_sems, recv_sems, after)
    return list(res[:n]), list(res[n:])


def _own_slot(land, own, me):
    return lax.dynamic_update_index_in_dim(land, own, me, 0)


class _Cols:
    def __init__(self, D):
        Wb = D // 2
        sizes = (Wb, Wb, Wb, Wb, Wb, Wb, Wb, ATT_KV * ATT_HD, ATT_KV * ATT_HD, Wb, 3 * D)
        offs = np.concatenate([[0], np.cumsum(sizes)])
        assert all(int(o) % 256 == 0 for o in offs)
        (self.rq, self.rk, self.rv, self.rg, self.pv, self.pg,
         self.aq, self.ak, self.av, self.ag, self.mg) = (int(o) // 256 for o in offs[:-1])
        self.width = int(offs[-1])
        self.sizes = sizes


def _rope_tables(S):
    pos = jnp.arange(S, dtype=F32)[:, None]
    inv_r = 1.0 / (RET_ROPE_BASE ** jnp.linspace(0.0, 1.0, RET_HD // 2, dtype=F32))
    ang_r = pos * inv_r[None, :]
    inv_a = ROPE_THETA ** (-jnp.arange(ROPE_HALF, dtype=F32) / ROPE_HALF)
    ang_a = pos * inv_a[None, :]
    ca, sa = jnp.cos(ang_a), jnp.sin(ang_a)
    z16 = jnp.zeros((S, ROPE_HALF), F32)
    rest = ATT_HD - 2 * ROPE_HALF
    cf = jnp.concatenate([ca, ca, jnp.ones((S, rest), F32)], axis=1)
    s_up = jnp.concatenate([z16, sa, jnp.zeros((S, rest), F32)], axis=1)
    s_dn = jnp.concatenate([-sa, z16, jnp.zeros((S, rest), F32)], axis=1)
    return jnp.cos(ang_r), jnp.sin(ang_r), cf, s_up, s_dn


def _layer_fwd(x, p, tabs, cols, deps=()):
    cos_r, sin_r, cf, s_up, s_dn = tabs
    S, D = x.shape
    h, ht = _rmsnorm_fwd(x, p["norm_g"], deps)
    z = _matmul(h, p["w_in_t"], "nt", F32, 2048, 512, D, "in_proj")
    if "rest" in p:
        p = {**{k: v for k, v in p.items() if k != "rest"}, **p["rest"](z)}
    rcols = (cols.rq, cols.rk, cols.rv, cols.rg)
    sf, sb = _ret_state(z, cols.rk, RET_HD ** -0.5, z, cols.rv, cos_r, sin_r, p["af"], p["ab"], "fwd",
                        "ret_state_fwd")
    o_ret, ua, uat = _ret_fwd(z, cos_r, sin_r, sf, sb, p["af"], p["ab"], rcols)
    y_pool, ub, ubt = _pool_fwd(z, p["pool_w"], p["pool_scale"], cols.pv, cols.pg)
    qn, kn, vb = _attn_prep(z, p["q_gain"], p["k_gain"], cf, s_up, s_dn,
                            cols.aq // 4, cols.ak, cols.av)
    o_att, uc, uct = _attn_fwd(qn, kn, vb, z, p["sink"], cols.ag * 2)
    ya, yb, yc, merged, mergedt = _branch_merge(ua, ub, uc, p["w_ret"], p["w_pool"], p["w_att"], z, cols.mg)
    out = _matmul(merged, p["w_out"], "nn", F32, 1024, 512, D, "out_proj", res=x)
    saved = dict(x=x, ht=ht, z=z, sf=sf, sb=sb, o_ret=o_ret, uat=uat, y_pool=y_pool, ubt=ubt,
                 qn=qn, kn=kn, vb=vb, o_att=o_att, uct=uct, ya=ya, yb=yb, yc=yc, mergedt=mergedt)
    return out, saved, p


def _layer_bwd(dx, dxb, p, sv, tabs, cols, on_grads):
    cos_r, sin_r, cf, s_up, s_dn = tabs
    z = sv["z"]
    S, D = dx.shape
    dya, dyb, dyc, dmg0, dmg1, dmg2 = _merge_bwd(dxb, p["w_out"], sv["ya"], sv["yb"], sv["yc"], z, cols.mg)
    grads = {"w_out": _matmul(sv["mergedt"], dxb, "nn", ACT, 1024, 512, S, "dw_out")}
    dus = {}
    for nm, ut, dy in (("ret", sv["uat"], dya), ("pool", sv["ubt"], dyb), ("att", sv["uct"], dyc)):
        dus[nm] = _matmul(dy, p["w_" + nm], "nt", ACT, 1024, 512, D, "du_" + nm)
        grads["w_" + nm] = _matmul(ut, dy, "nn", ACT, 1024, 512, S, "dw_" + nm)
    rcols = (cols.rq, cols.rk, cols.rv, cols.rg)
    do_ret, d_rg = _ret_gate_bwd(dus["ret"], sv["o_ret"], z, cols.rg)
    eb, ef = _ret_state(z, cols.rq, 1.0, do_ret, 0, cos_r, sin_r, p["af"], p["ab"], "bwd", "ret_state_bwd")
    d_rq, d_rk, d_rv, d_decay = _ret_bwd(z, do_ret, cos_r, sin_r, sv["sf"], sv["sb"], ef, eb,
                                         p["af"], p["ab"], rcols)
    dpc, d_pg, d_pscale, g_pool_w = _pool_bwd_a(z, p["pool_w"], p["pool_scale"], sv["y_pool"], dus["pool"],
                                                cols.pv, cols.pg)
    d_pv = _pool_bwd_b(dpc)
    grads["pool_w"] = g_pool_w.astype(ACT)
    dqn, dkn, dvv, d_ag, d_sink = _attn_bwd(sv["qn"], sv["kn"], sv["vb"], sv["o_att"], dus["att"], z,
                                            p["sink"], cols.ag * 2)
    d_aq, d_ak, d_av, d_qg, d_kg = _attn_prep_bwd(z, dqn, dkn, dvv, p["q_gain"], p["k_gain"], cf, s_up, s_dn,
                                                  cols.aq // 4, cols.ak)
    dz = jnp.concatenate([d_rq, d_rk, d_rv, d_rg, d_pv, d_pg, d_aq, d_ak, d_av, d_ag, dmg0, dmg1, dmg2],
                         axis=1)
    grads["w_in_t"] = _matmul(sv["ht"], dz, "nn", ACT, 1024, 512, S, "dw_in", transpose_out=True)
    tok = on_grads(grads)
    dh = _matmul(dz, p["w_in_t"], "nn", F32, 1024, 1024, 29 * 128, "dh", deps=() if tok is None else (tok,))
    dx_in, dxb_in, d_norm_g = _rmsnorm_bwd(sv["x"], p["norm_g"], dh, dx)
    misc = jnp.concatenate([d_decay[:, 0, 0], d_decay[:, 0, 1], d_sink[:, 0, :ATT_G].reshape(-1)])
    misc = jnp.pad(misc, (0, 128 - misc.shape[0]))[None, :]
    small = jnp.concatenate([d_norm_g, d_pscale, d_qg, d_kg, misc], axis=1)
    return dx_in, dxb_in, small


def _pack_small(norm_g, pool_scale, q_gain, k_gain, af, ab, sink):
    L = norm_g.shape[0]
    misc = jnp.concatenate([af, ab, sink], axis=1)
    misc = jnp.pad(misc, ((0, 0), (0, 128 - misc.shape[1])))
    return jnp.concatenate([norm_g, pool_scale, q_gain, k_gain, misc], axis=1)


def _unpack_small(a, D):
    Wb = D // 2
    o = np.cumsum([0, D, Wb, ATT_HD, ATT_HD])
    misc = a[:, o[4]:]
    return (a[:, o[0]:o[1]], a[:, o[1]:o[2]], a[:, o[2]:o[3]], a[:, o[3]:o[4]],
            misc[:, :RET_HEADS], misc[:, RET_HEADS:2 * RET_HEADS],
            misc[:, 2 * RET_HEADS:2 * RET_HEADS + ATT_Q])


def _local_step(x, t, n_layers, get_layer, on_grads, tabs, cols, first_dep=None):
    saved, layers = [], []
    after = first_dep
    for l in range(n_layers):
        p = get_layer(l, after)
        x, sv, p = _layer_fwd(x, p, tabs, cols, (first_dep,) if (l == 0 and first_dep is not None) else ())
        after = x
        layers.append(p)
        saved.append(sv)
    dx, dxb, lsum = _loss_head(x, t)
    smalls = []
    for l in reversed(range(n_layers)):
        dx, dxb, sm = _layer_bwd(dx, dxb, layers[l], saved[l], tabs, cols, functools.partial(on_grads, l))
        smalls.append(sm)
    return 0.5 * lsum[0, 0], dx, jnp.concatenate(smalls[::-1], axis=0)


WEIGHT_KEYS = ("w_in", "w_ret", "w_pool", "w_att", "w_out", "pool_w")


def kernel(x, norm_g, w_in, ret_decay_fwd, ret_decay_bwd, pool_w, pool_scale, attn_q_gain, attn_k_gain, attn_sink, w_ret, w_pool, w_att, w_out, loss_target, m_norm_g, m_w_in, m_ret_decay_fwd, m_ret_decay_bwd, m_pool_w, m_pool_scale, m_attn_q_gain, m_attn_k_gain, m_attn_sink, m_w_ret, m_w_pool, m_w_att, m_w_out, v_norm_g, v_w_in, v_ret_decay_fwd, v_ret_decay_bwd, v_pool_w, v_pool_scale, v_attn_q_gain, v_attn_k_gain, v_attn_sink, v_w_ret, v_w_pool, v_w_att, v_w_out):
    L = norm_g.shape[0]
    _, S, D = x.shape
    Wb = D // 2
    G = len(POOL_WINDOWS)
    cols = _Cols(D)
    tabs = _rope_tables(S)
    me = 4 * lax.axis_index("x") + 2 * lax.axis_index("y") + lax.axis_index("c")
    def tr(a):
        return jnp.transpose(a, (0, 2, 1))

    weights = dict(w_in=tr(w_in), w_ret=w_ret, w_pool=w_pool, w_att=w_att, w_out=w_out, pool_w=pool_w)

    gathers, tok = [], None
    for l in range(L):
        started = []
        for part, keys in (("a", WEIGHT_KEYS[:1]), ("b", WEIGHT_KEYS[1:])):
            shards = []
            for k in keys:
                w = weights[k][l]
                if tok is not None:
                    w = w + tok[0, 0]
                shards.append(w.astype(MXU))
            st = _exchange_start(shards, False, f"gather_start_{l}{part}")
            started.append(st)
            tok = st[-1]
        gathers.append(started)

    def cols_full(g, rows):
        return jnp.transpose(g, (1, 0, 2)).reshape(rows, -1)

    def get_layer(l, after):
        srcs, lands = _exchange_wait(gathers[l][0], after, False, f"gather_wait_{l}a")
        g_in = _own_slot(lands[0], srcs[0], me)

        def rest(z):
            srcs, lands = _exchange_wait(gathers[l][1], z, False, f"gather_wait_{l}b")
            g_ret, g_pool, g_att, g_out, g_pw = [_own_slot(ld, sr, me) for ld, sr in zip(lands, srcs)]
            return dict(
                w_ret=cols_full(g_ret, Wb), w_pool=cols_full(g_pool, Wb), w_att=cols_full(g_att, Wb),
                w_out=g_out.reshape(D, D),
                pool_w=jnp.transpose(g_pw, (1, 0, 2, 3)).reshape(G, POOL_GD, POOL_GD))

        return dict(
            norm_g=norm_g[l][None, :], w_in_t=g_in.reshape(-1, D), rest=rest,
            pool_scale=pool_scale[l][None, :], q_gain=attn_q_gain[l][None, :], k_gain=attn_k_gain[l][None, :],
            af=ret_decay_fwd[l], ab=ret_decay_bwd[l], sink=attn_sink[l])

    def col_slots(g, rows):
        return jnp.transpose(g.reshape(rows, N_DEV, -1), (1, 0, 2))

    scatters = {}

    def on_grads(l, g):
        slots = [g["w_in_t"].reshape(N_DEV, -1, D),
                 col_slots(g["w_ret"], Wb), col_slots(g["w_pool"], Wb),
                 col_slots(g["w_att"], Wb), g["w_out"].reshape(N_DEV, D // N_DEV, D),
                 jnp.transpose(g["pool_w"].reshape(G, N_DEV, POOL_GD // N_DEV, POOL_GD), (1, 0, 2, 3))]
        scatters[l] = _exchange_start(slots, True, f"scatter_start_{l}")
        return scatters[l][-1]

    loss_local, grad_x, small = _local_step(x[0], loss_target[0], L, get_layer, on_grads, tabs, cols, tok)
    small = small.at[0, -1].set(loss_local)

    moments = dict(w_in=(tr(m_w_in), tr(v_w_in)), w_ret=(m_w_ret, v_w_ret), w_pool=(m_w_pool, v_w_pool),
                   w_att=(m_w_att, v_w_att), w_out=(m_w_out, v_w_out), pool_w=(m_pool_w, v_pool_w))
    res = {k: [jnp.zeros(weights[k].shape, F32) for _ in range(4)] for k in WEIGHT_KEYS}
    after = grad_x
    small_all = None
    for l in reversed(range(L)):
        if l == 0:
            small_all, = _exchange([small], False, "gather_small_grads", deps=(after,))
            after = small_all
        srcs, lands = _exchange_wait(scatters[l], after, True, f"scatter_wait_{l}")
        firsts = []
        for k, ld, sr in zip(WEIGHT_KEYS, lands, srcs):
            parts = _own_slot(ld, lax.dynamic_index_in_dim(sr, me, 0, keepdims=False), me)
            m, v = moments[k]
            outs = _adamw(weights[k], l, parts, m, v, "adamw_" + k)
            res[k] = [lax.dynamic_update_index_in_dim(r, o, l, 0) for r, o in zip(res[k], outs)]
            firsts.append(outs[1].reshape(-1)[:1])
        after = jnp.concatenate(firsts)

    sw = _pack_small(norm_g, pool_scale, attn_q_gain, attn_k_gain, ret_decay_fwd, ret_decay_bwd, attn_sink)
    sm = _pack_small(m_norm_g, m_pool_scale, m_attn_q_gain, m_attn_k_gain, m_ret_decay_fwd, m_ret_decay_bwd,
                     m_attn_sink)
    sv_ = _pack_small(v_norm_g, v_pool_scale, v_attn_q_gain, v_attn_k_gain, v_ret_decay_fwd, v_ret_decay_bwd,
                      v_attn_sink)
    small_out = _adamw(sw[None], 0, small_all, sm[None], sv_[None], "adamw_small")
    loss = small_out[0][0, -1]
    small_res = [_unpack_small(a, D) for a in small_out]

    def ordered(i):
        ng, ps, qg, kg, af, ab, sk = small_res[i]
        return (ng, tr(res["w_in"][i]), af, ab, res["pool_w"][i], ps, qg, kg, sk,
                res["w_ret"][i], res["w_pool"][i], res["w_att"][i], res["w_out"][i])

    return (loss, grad_x[None], *ordered(0), *ordered(1), *ordered(2), *ordered(3))
```
